```python
import jax, jax.numpy as jnp
from jax import lax
import numpy as np

D_MODEL = 1024
BATCH = 8
SEQ = 2048
DEPTH = 1
DEC_BATCH = 128
DEC_SEQ = 1
PAST_LEN = 16384
PAGE_SIZE = 128

MIX_WIDTH = D_MODEL
HGRN_WIDTH = MIX_WIDTH // 2
HGRN_HEADS = 4
HGRN_DK = HGRN_WIDTH // HGRN_HEADS
HGRN_DV = HGRN_WIDTH // HGRN_HEADS
HGRN_CHUNK = 64
POOL_WIDTH = MIX_WIDTH - HGRN_WIDTH
POOL_WINDOWS = (2, 4, 8, 16)
POOL_GROUPS = len(POOL_WINDOWS)
POOL_GROUP_DIM = POOL_WIDTH // POOL_GROUPS
POOL_STATE = max(POOL_WINDOWS) - 1
IN_PROJ = 4 * HGRN_WIDTH + POOL_WIDTH
MEM_LEN = 256
X_HEADS = 4
X_HEAD_DIM = D_MODEL // X_HEADS
D_FF = -(-8 * D_MODEL // (3 * 256)) * 256
EPS = 1e-6

kernel_name = "hgrn2_pool_hybrid_decode_step"


def rmsnorm(x, g):
    xf = x.astype(jnp.float32)
    y = xf * lax.rsqrt(jnp.mean(xf * xf, axis=-1, keepdims=True) + EPS)
    return (y * g.astype(jnp.float32)).astype(x.dtype)


def hgrn2_chunked(q, k, v, logf, s0):
    B, L, H, DK = q.shape
    DV = v.shape[-1]
    C = HGRN_CHUNK if L % HGRN_CHUNK == 0 else L
    n = L // C

    def chunks(a):
        return a.reshape(B, n, C, H, a.shape[-1]).transpose(1, 0, 3, 2, 4)

    causal = jnp.tril(jnp.ones((C, C), dtype=bool))
    mid = (C - 1) // 2

    def step(S, inp):
        qc, kc, vc, lc = inp
        b = jnp.cumsum(lc, axis=2)
        m = b[:, :, mid:mid + 1, :]
        inter = jnp.einsum('bhck,bhkv->bhcv', qc * jnp.exp(b), S)
        a = jnp.einsum('bhck,bhsk->bhcs', qc * jnp.exp(b - m), kc * jnp.exp(m - b))
        intra = jnp.einsum('bhcs,bhsv->bhcv', jnp.where(causal, a, 0.0), vc)
        b_end = b[:, :, -1:, :]
        S = jnp.exp(b_end[:, :, 0, :])[..., None] * S + jnp.einsum(
            'bhck,bhcv->bhkv', kc * jnp.exp(b_end - b), vc)
        return S, inter + intra

    S, o = lax.scan(step, s0, (chunks(q), chunks(k), chunks(v), chunks(logf)))
    return o.transpose(1, 0, 3, 2, 4).reshape(B, L, H, DV), S


def multiscale_pool(u, past, start_pos):
    B, L, _ = u.shape
    P = POOL_STATE
    ext = jnp.concatenate([past.astype(jnp.float32), u.astype(jnp.float32)], axis=1)
    cs = jnp.concatenate([jnp.zeros((B, 1, POOL_WIDTH), jnp.float32),
                          jnp.cumsum(ext, axis=1)], axis=1)
    pos = start_pos + jnp.arange(L)
    outs = []
    for g, w in enumerate(POOL_WINDOWS):
        sl = slice(g * POOL_GROUP_DIM, (g + 1) * POOL_GROUP_DIM)
        hi = cs[:, P + 1:P + 1 + L, sl]
        lo = cs[:, P + 1 - w:P + 1 - w + L, sl]
        cnt = jnp.minimum(pos + 1, w).astype(jnp.float32)[None, :, None]
        outs.append((hi - lo) / cnt)
    return jnp.concatenate(outs, axis=-1) - u.astype(jnp.float32)


def mem_kv(mem, g, w_kv):
    B, M, _ = mem.shape
    kv = rmsnorm(mem, g) @ w_kv
    k, v = jnp.split(kv, 2, axis=-1)
    return (k.reshape(B, M, X_HEADS, X_HEAD_DIM), v.reshape(B, M, X_HEADS, X_HEAD_DIM))


def cross_attend(h, mk, mv, w_q, w_o):
    B, L, _ = h.shape
    q = (h @ w_q).reshape(B, L, X_HEADS, X_HEAD_DIM)
    s = jnp.einsum('blhd,bmhd->bhlm', q.astype(jnp.float32), mk.astype(jnp.float32)) * (X_HEAD_DIM ** -0.5)
    p = jax.nn.softmax(s, axis=-1)
    o = jnp.einsum('bhlm,bmhd->blhd', p, mv.astype(jnp.float32))
    return o.reshape(B, L, D_MODEL).astype(h.dtype) @ w_o


def run_trunk(x, s_hgrn, s_pool, mem_k, mem_v, start_pos, lb_all, g_mix, w_in, hgrn_norm,
              pool_mix, pool_scale, w_out, g_cross, w_cq, w_co, g_ffn, w_ffn_in, w_ffn_out, g_final):
    B, L, _ = x.shape
    new_h, new_p = [], []
    for l in range(DEPTH):
        h = rmsnorm(x, g_mix[l])
        proj = h @ w_in[l]
        qr, fr, ir, gr, u = jnp.split(
            proj, [HGRN_WIDTH, 2 * HGRN_WIDTH, 3 * HGRN_WIDTH, 4 * HGRN_WIDTH], axis=-1)
        lb = lb_all[l]
        zf = fr.astype(jnp.float32)
        logf = jnp.log(lb + (1.0 - lb) * jax.nn.sigmoid(zf))
        kk = (1.0 - lb) * jax.nn.sigmoid(-zf)
        qq = jax.nn.silu(qr.astype(jnp.float32))
        shp = (B, L, HGRN_HEADS, HGRN_DK)
        o, S = hgrn2_chunked(qq.reshape(shp), kk.reshape(shp),
                             ir.astype(jnp.float32).reshape(B, L, HGRN_HEADS, HGRN_DV),
                             logf.reshape(shp), s_hgrn[l].astype(jnp.float32))
        o = o * lax.rsqrt(jnp.mean(o * o, axis=-1, keepdims=True) + EPS)
        o = o * hgrn_norm[l].astype(jnp.float32).reshape(HGRN_HEADS, HGRN_DV)
        y_a = o.reshape(B, L, HGRN_WIDTH) * jax.nn.sigmoid(gr.astype(jnp.float32))
        new_h.append(S.astype(x.dtype))
        past = s_pool[l].astype(u.dtype)
        pooled = multiscale_pool(u, past, start_pos)
        y_b = jnp.einsum('blgc,gcd->blgd',
                         pooled.reshape(B, L, POOL_GROUPS, POOL_GROUP_DIM),
                         pool_mix[l].astype(jnp.float32)).reshape(B, L, POOL_WIDTH)
        y_b = y_b * pool_scale[l].astype(jnp.float32)
        new_p.append(jnp.concatenate([past, u], axis=1)[:, -POOL_STATE:].astype(x.dtype))
        x = x + jnp.concatenate([y_a, y_b], axis=-1).astype(x.dtype) @ w_out[l]
        x = x + cross_attend(rmsnorm(x, g_cross[l]), mem_k[l], mem_v[l], w_cq[l], w_co[l])
        a, bgate = jnp.split(rmsnorm(x, g_ffn[l]) @ w_ffn_in[l], 2, axis=-1)
        x = x + (jax.nn.silu(a) * bgate) @ w_ffn_out[l]
    return rmsnorm(x, g_final), jnp.stack(new_h), jnp.stack(new_p)


def setup_inputs(seed: int = 0) -> dict:
    key = jax.random.key(seed)
    ks = jax.random.split(key, 24)
    f32 = jnp.float32
    nrm = lambda k, s, sc: jax.random.normal(k, s, f32) * sc
    gain = lambda k, s: 1.0 + 0.02 * jax.random.normal(k, s, f32)
    return {
        "x_prompt": nrm(ks[0], (BATCH, SEQ, D_MODEL), 1.0),
        "x_sample": nrm(ks[1], (DEC_BATCH, DEC_SEQ, D_MODEL), 1.0),
        "mem_prompt": nrm(ks[2], (BATCH, MEM_LEN, D_MODEL), 1.0),
        "state_hgrn": nrm(ks[3], (DEPTH, DEC_BATCH, HGRN_HEADS, HGRN_DK, HGRN_DV), 0.3),
        "state_pool": nrm(ks[4], (DEPTH, DEC_BATCH, POOL_STATE, POOL_WIDTH), 1.0),
        "cache_mem_k": nrm(ks[5], (DEPTH, DEC_BATCH, MEM_LEN, X_HEADS, X_HEAD_DIM), 1.0),
        "cache_mem_v": nrm(ks[6], (DEPTH, DEC_BATCH, MEM_LEN, X_HEADS, X_HEAD_DIM), 1.0),
        "g_mix": gain(ks[7], (DEPTH, D_MODEL)),
        "w_in": nrm(ks[8], (DEPTH, D_MODEL, IN_PROJ), D_MODEL ** -0.5),
        "hgrn_lb": nrm(ks[9], (DEPTH + 1, HGRN_WIDTH), 0.1),
        "hgrn_norm": gain(ks[10], (DEPTH, HGRN_WIDTH)),
        "pool_mix": nrm(ks[11], (DEPTH, POOL_GROUPS, POOL_GROUP_DIM, POOL_GROUP_DIM), POOL_GROUP_DIM ** -0.5),
        "pool_scale": gain(ks[12], (DEPTH, POOL_WIDTH)),
        "w_out": nrm(ks[13], (DEPTH, MIX_WIDTH, D_MODEL), MIX_WIDTH ** -0.5),
        "g_mem": gain(ks[14], (DEPTH, D_MODEL)),
        "w_mem_kv": nrm(ks[15], (DEPTH, D_MODEL, 2 * D_MODEL), D_MODEL ** -0.5),
        "g_cross": gain(ks[16], (DEPTH, D_MODEL)),
        "w_cq": nrm(ks[17], (DEPTH, D_MODEL, D_MODEL), D_MODEL ** -0.5),
        "w_co": nrm(ks[18], (DEPTH, D_MODEL, D_MODEL), D_MODEL ** -0.5),
        "g_ffn": gain(ks[19], (DEPTH, D_MODEL)),
        "w_ffn_in": nrm(ks[20], (DEPTH, D_MODEL, 2 * D_FF), D_MODEL ** -0.5),
        "w_ffn_out": nrm(ks[21], (DEPTH, D_FF, D_MODEL), D_FF ** -0.5),
        "g_final": gain(ks[22], (D_MODEL,)),
    }


def reference(x_prompt, x_sample, mem_prompt, state_hgrn, state_pool, cache_mem_k, cache_mem_v,
              g_mix, w_in, hgrn_lb, hgrn_norm, pool_mix, pool_scale, w_out, g_mem, w_mem_kv,
              g_cross, w_cq, w_co, g_ffn, w_ffn_in, w_ffn_out, g_final):
    lb_all = jnp.cumsum(jax.nn.softmax(hgrn_lb.astype(jnp.float32), axis=0), axis=0)
    kv = [mem_kv(mem_prompt, g_mem[l], w_mem_kv[l]) for l in range(DEPTH)]
    new_mem_k = jnp.stack([p[0] for p in kv])
    new_mem_v = jnp.stack([p[1] for p in kv])
    s0_hgrn = jnp.zeros((DEPTH, BATCH, HGRN_HEADS, HGRN_DK, HGRN_DV), x_prompt.dtype)
    s0_pool = jnp.zeros((DEPTH, BATCH, POOL_STATE, POOL_WIDTH), x_prompt.dtype)
    y_prompt, hgrn_p, pool_p = run_trunk(
        x_prompt, s0_hgrn, s0_pool, new_mem_k, new_mem_v, 0, lb_all, g_mix, w_in, hgrn_norm,
        pool_mix, pool_scale, w_out, g_cross, w_cq, w_co, g_ffn, w_ffn_in, w_ffn_out, g_final)
    y_sample, hgrn_s, pool_s = run_trunk(
        x_sample, state_hgrn, state_pool, cache_mem_k, cache_mem_v, PAST_LEN, lb_all, g_mix, w_in,
        hgrn_norm, pool_mix, pool_scale, w_out, g_cross, w_cq, w_co, g_ffn, w_ffn_in, w_ffn_out, g_final)
    return (y_prompt, y_sample, hgrn_p, pool_p, new_mem_k, new_mem_v, hgrn_s, pool_s)
```

```python
import functools

import jax
import jax.numpy as jnp
from jax import lax
from jax.experimental import pallas as pl
from jax.experimental.pallas import tpu as pltpu

F32 = jnp.float32
BF16 = jnp.bfloat16

D_MODEL = 1024
HGRN_WIDTH = 512
HEADS = 4
DK = 128
CHUNK = 64
POOL_WIDTH = 512
POOL_WINDOWS = (2, 4, 8, 16)
POOL_GROUP = 128
POOL_STATE = 15
IN_PROJ = 4 * HGRN_WIDTH + POOL_WIDTH
MEM_LEN = 256
X_HEADS = 4
X_HEAD_DIM = 256
D_FF = 2816
EPS = 1e-6
ATTN_SCALE = X_HEAD_DIM ** -0.5

VMEM_LIMIT_BYTES = 56 * 1024 * 1024

MIX_ROWS = 256
ATTN_ROWS = 512
FFN_ROWS = 512
STATE_TOKENS = 8
ATTN_TOKENS = 4


def _dot(a, b):
    return jnp.dot(a, b, preferred_element_type=F32)


def _dot_nt(a, b):
    return lax.dot_general(a, b, (((1,), (1,)), ((), ())), preferred_element_type=F32)


def _dot_tn(a, b):
    return lax.dot_general(a, b, (((0,), (0,)), ((), ())), preferred_element_type=F32)


def _rms(x, g):
    ms = jnp.mean(x * x, axis=-1, keepdims=True)
    return x * lax.rsqrt(ms + EPS) * g


def _sigmoid(x):
    return 1.0 / (1.0 + jnp.exp(-x))


def _lower_bound(lb_ref):
    t = lb_ref[...]
    m = jnp.max(t, axis=0, keepdims=True)
    e = jnp.exp(t - m)
    return e[0:1, :] / jnp.sum(e, axis=0, keepdims=True)


def _gates(proj_q, proj_f, lb):
    qq = proj_q * _sigmoid(proj_q)
    fgate = lb + (1.0 - lb) * _sigmoid(proj_f)
    kk = (1.0 - lb) * _sigmoid(-proj_f)
    return qq, fgate, kk


def _split3(x):
    hi = x.astype(BF16)
    r1 = x - hi.astype(F32)
    mid = r1.astype(BF16)
    lo = (r1 - mid.astype(F32)).astype(BF16)
    return hi, mid, lo


def _memkv_kernel(mem_ref, g_ref, w_ref, k_ref, v_ref, kt_ref, vb_ref):
    h = _rms(mem_ref[0], g_ref[...]).astype(BF16)
    kv = _dot(h, w_ref[...])
    k = kv[:, :D_MODEL]
    v = kv[:, D_MODEL:]
    k_ref[0] = k
    v_ref[0] = v
    kt_ref[0] = k.T.astype(BF16)
    vb_ref[0] = v.astype(BF16)


def _mem_kv(mem, g_mem, w_kv):
    nb = mem.shape[0]
    full = lambda shape: pl.BlockSpec(shape, lambda b: (0,) * len(shape))
    return pl.pallas_call(
        _memkv_kernel,
        grid=(nb,),
        in_specs=[pl.BlockSpec((1, MEM_LEN, D_MODEL), lambda b: (b, 0, 0)),
                  full((1, D_MODEL)), full((D_MODEL, 2 * D_MODEL))],
        out_specs=[pl.BlockSpec((1, MEM_LEN, D_MODEL), lambda b: (b, 0, 0)),
                   pl.BlockSpec((1, MEM_LEN, D_MODEL), lambda b: (b, 0, 0)),
                   pl.BlockSpec((1, D_MODEL, MEM_LEN), lambda b: (b, 0, 0)),
                   pl.BlockSpec((1, MEM_LEN, D_MODEL), lambda b: (b, 0, 0))],
        out_shape=[jax.ShapeDtypeStruct((nb, MEM_LEN, D_MODEL), F32),
                   jax.ShapeDtypeStruct((nb, MEM_LEN, D_MODEL), F32),
                   jax.ShapeDtypeStruct((nb, D_MODEL, MEM_LEN), BF16),
                   jax.ShapeDtypeStruct((nb, MEM_LEN, D_MODEL), BF16)],
        compiler_params=pltpu.CompilerParams(
            dimension_semantics=("arbitrary",), vmem_limit_bytes=VMEM_LIMIT_BYTES),
        name="mem_kv",
    )(mem, g_mem, w_kv)


def _mixer_kernel(x_ref, g_ref, win_ref, lb_ref, hn_ref, pmix_ref, ps_ref, wout_ref,
                  x1_ref, hst_ref, pst_ref,
                  st_ref, ext_ref, proj_ref, qq_ref, kk_ref, b_ref, mrg_ref):
    T = MIX_ROWS
    l = pl.program_id(1)

    @pl.when(l == 0)
    def _():
        st_ref[...] = jnp.zeros_like(st_ref)
        ext_ref[0:16, :] = jnp.zeros((16, POOL_WIDTH), F32)

    x = x_ref[0]
    hb = _rms(x, g_ref[...]).astype(BF16)
    proj_ref[...] = _dot(hb, win_ref[...])

    lb = _lower_bound(lb_ref)
    qq, fgate, kk = _gates(proj_ref[:, 0:512], proj_ref[:, 512:1024], lb)
    qq_ref[...] = qq
    kk_ref[...] = kk

    r = lax.broadcasted_iota(jnp.int32, (T, T), 0)
    c = lax.broadcasted_iota(jnp.int32, (T, T), 1)
    tri = jnp.where((c <= r) & (c >= (r & -CHUNK)), 1.0, 0.0).astype(BF16)
    hi, mid, lo = _split3(jnp.log(fgate))
    b_ref[...] = _dot(tri, hi) + _dot(tri, mid) + _dot(tri, lo)

    cr = lax.broadcasted_iota(jnp.int32, (CHUNK, CHUNK), 0)
    cc = lax.broadcasted_iota(jnp.int32, (CHUNK, CHUNK), 1)
    causal = cc <= cr
    mid_row = (CHUNK - 1) // 2

    for ci in range(T // CHUNK):
        r0 = ci * CHUNK
        rows = slice(r0, r0 + CHUNK)
        for h in range(HEADS):
            cols = slice(h * DK, (h + 1) * DK)
            b = b_ref[rows, cols]
            m = b_ref[r0 + mid_row:r0 + mid_row + 1, cols]
            b_end = b_ref[r0 + CHUNK - 1:r0 + CHUNK, cols]
            e1 = jnp.exp(b - m)
            e2 = jnp.exp(m - b)
            q1 = qq_ref[rows, cols] * e1
            q0 = q1 * jnp.exp(m)
            ks = kk_ref[rows, cols] * e2
            k2 = ks * jnp.exp(b_end - m)
            v = proj_ref[rows, 1024 + h * DK:1024 + (h + 1) * DK].astype(BF16)
            st = st_ref[h]
            inter = _dot_nt(q0.astype(BF16), st.astype(BF16))
            a = _dot_nt(q1.astype(BF16), ks.astype(BF16))
            a = jnp.where(causal, a, 0.0)
            o = inter + _dot(a.astype(BF16), v)
            st_ref[h] = st * jnp.exp(b_end) + _dot_tn(v, k2.astype(BF16))
            o = o * lax.rsqrt(jnp.mean(o * o, axis=-1, keepdims=True) + EPS)
            o = o * hn_ref[:, cols]
            g = proj_ref[rows, 1536 + h * DK:1536 + (h + 1) * DK]
            mrg_ref[rows, cols] = (o * _sigmoid(g)).astype(BF16)

    ext_ref[16:16 + T, :] = proj_ref[:, 2048:2560]
    pos = l * T + lax.broadcasted_iota(jnp.int32, (T, POOL_GROUP), 0)
    for gi, w in enumerate(POOL_WINDOWS):
        cols = slice(gi * POOL_GROUP, (gi + 1) * POOL_GROUP)
        u = ext_ref[16:16 + T, cols]
        acc = u
        for j in range(1, w):
            acc = acc + ext_ref[16 - j:16 - j + T, cols]
        cnt = jnp.minimum(pos + 1, w).astype(F32)
        pooled = acc / cnt - u
        yb = _dot(pooled.astype(BF16), pmix_ref[gi]) * ps_ref[:, cols]
        mrg_ref[:, HGRN_WIDTH + gi * POOL_GROUP:HGRN_WIDTH + (gi + 1) * POOL_GROUP] = yb.astype(BF16)

    @pl.when(l == pl.num_programs(1) - 1)
    def _():
        for h in range(HEADS):
            hst_ref[0, h] = st_ref[h].T
        pst_ref[0] = ext_ref[T + 1:T + 16, :]

    ext_ref[0:16, :] = ext_ref[T:T + 16, :]
    x1_ref[0] = x + _dot(mrg_ref[...], wout_ref[...])


def _mixer_prompt(x, g_mix, w_in, hgrn_lb, hgrn_norm, pool_mix, pool_scale, w_out):
    nb, L, _ = x.shape
    T = MIX_ROWS
    full = lambda shape: pl.BlockSpec(shape, lambda b, l: (0,) * len(shape))
    return pl.pallas_call(
        _mixer_kernel,
        grid=(nb, L // T),
        in_specs=[pl.BlockSpec((1, T, D_MODEL), lambda b, l: (b, l, 0)),
                  full((1, D_MODEL)), full((D_MODEL, IN_PROJ)), full((2, HGRN_WIDTH)),
                  full((1, HGRN_WIDTH)), full((4, POOL_GROUP, POOL_GROUP)),
                  full((1, POOL_WIDTH)), full((D_MODEL, D_MODEL))],
        out_specs=[pl.BlockSpec((1, T, D_MODEL), lambda b, l: (b, l, 0)),
                   pl.BlockSpec((1, HEADS, DK, DK), lambda b, l: (b, 0, 0, 0)),
                   pl.BlockSpec((1, POOL_STATE, POOL_WIDTH), lambda b, l: (b, 0, 0))],
        out_shape=[jax.ShapeDtypeStruct((nb, L, D_MODEL), F32),
                   jax.ShapeDtypeStruct((nb, HEADS, DK, DK), F32),
                   jax.ShapeDtypeStruct((nb, POOL_STATE, POOL_WIDTH), F32)],
        scratch_shapes=[pltpu.VMEM((HEADS, DK, DK), F32),
                        pltpu.VMEM((16 + T, POOL_WIDTH), F32),
                        pltpu.VMEM((T, IN_PROJ), F32),
                        pltpu.VMEM((T, HGRN_WIDTH), F32),
                        pltpu.VMEM((T, HGRN_WIDTH), F32),
                        pltpu.VMEM((T, HGRN_WIDTH), F32),
                        pltpu.VMEM((T, D_MODEL), BF16)],
        compiler_params=pltpu.CompilerParams(
            dimension_semantics=("arbitrary", "arbitrary"), vmem_limit_bytes=VMEM_LIMIT_BYTES),
        name="mixer_prompt",
    )(x, g_mix, w_in, hgrn_lb, hgrn_norm, pool_mix, pool_scale, w_out)


def _attn_kernel(x_ref, g_ref, wq_ref, kt_ref, vb_ref, wo_ref, o_ref, att_ref):
    x = x_ref[0]
    hb = _rms(x, g_ref[...]).astype(BF16)
    q = (_dot(hb, wq_ref[...]) * ATTN_SCALE).astype(BF16)
    for h in range(X_HEADS):
        cols = slice(h * X_HEAD_DIM, (h + 1) * X_HEAD_DIM)
        s = _dot(q[:, cols], kt_ref[0, cols, :])
        e = jnp.exp(s - jnp.max(s, axis=-1, keepdims=True))
        den = jnp.sum(e, axis=-1, keepdims=True)
        o = _dot(e.astype(BF16), vb_ref[0, :, cols]) / den
        att_ref[:, cols] = o.astype(BF16)
    o_ref[0] = x + _dot(att_ref[...], wo_ref[...])


def _attn_prompt(x, g_cross, w_cq, kt, vb, w_co):
    nb, L, _ = x.shape
    T = ATTN_ROWS
    full = lambda shape: pl.BlockSpec(shape, lambda b, l: (0,) * len(shape))
    return pl.pallas_call(
        _attn_kernel,
        grid=(nb, L // T),
        in_specs=[pl.BlockSpec((1, T, D_MODEL), lambda b, l: (b, l, 0)),
                  full((1, D_MODEL)), full((D_MODEL, D_MODEL)),
                  pl.BlockSpec((1, D_MODEL, MEM_LEN), lambda b, l: (b, 0, 0)),
                  pl.BlockSpec((1, MEM_LEN, D_MODEL), lambda b, l: (b, 0, 0)),
                  full((D_MODEL, D_MODEL))],
        out_specs=pl.BlockSpec((1, T, D_MODEL), lambda b, l: (b, l, 0)),
        out_shape=jax.ShapeDtypeStruct((nb, L, D_MODEL), F32),
        scratch_shapes=[pltpu.VMEM((T, D_MODEL), BF16)],
        compiler_params=pltpu.CompilerParams(
            dimension_semantics=("arbitrary", "arbitrary"), vmem_limit_bytes=VMEM_LIMIT_BYTES),
        name="attn_prompt",
    )(x, g_cross, w_cq, kt, vb, w_co)


def _ffn_body(x, g_ref, w1_ref, w2_ref, gf_ref):
    hb = _rms(x, g_ref[...]).astype(BF16)
    a = _dot(hb, w1_ref[:, :D_FF])
    bg = _dot(hb, w1_ref[:, D_FF:])
    act = (a * _sigmoid(a) * bg).astype(BF16)
    y = x + _dot(act, w2_ref[...])
    return _rms(y, gf_ref[...])


def _ffn_kernel(x_ref, g_ref, w1_ref, w2_ref, gf_ref, o_ref):
    o_ref[...] = _ffn_body(x_ref[...], g_ref, w1_ref, w2_ref, gf_ref)


def _ffn_prompt(x, g_ffn, w1, w2, g_final):
    n = x.shape[0]
    T = FFN_ROWS
    full = lambda shape: pl.BlockSpec(shape, lambda i: (0,) * len(shape))
    return pl.pallas_call(
        _ffn_kernel,
        grid=(n // T,),
        in_specs=[pl.BlockSpec((T, D_MODEL), lambda i: (i, 0)),
                  full((1, D_MODEL)), full((D_MODEL, 2 * D_FF)), full((D_FF, D_MODEL)),
                  full((1, D_MODEL))],
        out_specs=pl.BlockSpec((T, D_MODEL), lambda i: (i, 0)),
        out_shape=jax.ShapeDtypeStruct((n, D_MODEL), F32),
        compiler_params=pltpu.CompilerParams(
            dimension_semantics=("arbitrary",), vmem_limit_bytes=VMEM_LIMIT_BYTES),
        name="ffn_prompt",
    )(x, g_ffn, w1, w2, g_final)


def _sample_gates_kernel(x_ref, g_ref, win_ref, lb_ref, o_ref):
    hb = _rms(x_ref[...], g_ref[...]).astype(BF16)
    proj = _dot(hb, win_ref[...])
    lb = _lower_bound(lb_ref)
    qq, fgate, kk = _gates(proj[:, 0:512], proj[:, 512:1024], lb)
    o_ref[:, 0:512] = qq
    o_ref[:, 512:1024] = fgate
    o_ref[:, 1024:1536] = kk
    o_ref[:, 1536:2048] = proj[:, 1024:1536]
    o_ref[:, 2048:2560] = _sigmoid(proj[:, 1536:2048])
    o_ref[:, 2560:3072] = proj[:, 2048:2560]


def _sample_gates(x, g_mix, w_in, hgrn_lb):
    n = x.shape[0]
    full = lambda shape: pl.BlockSpec(shape, lambda i: (0,) * len(shape))
    return pl.pallas_call(
        _sample_gates_kernel,
        grid=(1,),
        in_specs=[full((n, D_MODEL)), full((1, D_MODEL)), full((D_MODEL, IN_PROJ)),
                  full((2, HGRN_WIDTH))],
        out_specs=full((n, 6 * 512)),
        out_shape=jax.ShapeDtypeStruct((n, 6 * 512), F32),
        compiler_params=pltpu.CompilerParams(
            dimension_semantics=("arbitrary",), vmem_limit_bytes=VMEM_LIMIT_BYTES),
        name="sample_gates",
    )(x, g_mix, w_in, hgrn_lb)


def _sample_state_kernel(q_ref, f_ref, k_ref, v_ref, s_ref, so_ref, o_ref):
    G = STATE_TOKENS
    pad = jnp.zeros((DK - G, DK), F32)
    for h in range(HEADS):
        cols = slice(h * DK, (h + 1) * DK)
        qt = jnp.concatenate([q_ref[:, cols], pad], axis=0).T
        ft = jnp.concatenate([f_ref[:, cols], pad], axis=0).T
        kt = jnp.concatenate([k_ref[:, cols], pad], axis=0).T
        for j in range(G):
            s_new = ft[:, j:j + 1] * s_ref[j, h] + kt[:, j:j + 1] * v_ref[j:j + 1, cols]
            so_ref[j, h] = s_new
            o_ref[j:j + 1, cols] = jnp.sum(qt[:, j:j + 1] * s_new, axis=0, keepdims=True)


def _sample_state(gates, state):
    n = gates.shape[0]
    G = STATE_TOKENS
    col = lambda c: pl.BlockSpec((G, 512), lambda i: (i, c))
    return pl.pallas_call(
        _sample_state_kernel,
        grid=(n // G,),
        in_specs=[col(0), col(1), col(2), col(3),
                  pl.BlockSpec((G, HEADS, DK, DK), lambda i: (i, 0, 0, 0))],
        out_specs=[pl.BlockSpec((G, HEADS, DK, DK), lambda i: (i, 0, 0, 0)),
                   pl.BlockSpec((G, HGRN_WIDTH), lambda i: (i, 0))],
        out_shape=[jax.ShapeDtypeStruct((n, HEADS, DK, DK), F32),
                   jax.ShapeDtypeStruct((n, HGRN_WIDTH), F32)],
        compiler_params=pltpu.CompilerParams(
            dimension_semantics=("arbitrary",), vmem_limit_bytes=VMEM_LIMIT_BYTES),
        name="sample_state",
    )(gates, gates, gates, gates, state)


def _sample_mix_kernel(x_ref, gates_ref, o_ref, past_ref, hn_ref, pmix_ref, ps_ref, wout_ref,
                       gc_ref, wq_ref, x1_ref, pool_ref, q_ref, mrg_ref):
    for h in range(HEADS):
        cols = slice(h * DK, (h + 1) * DK)
        o = o_ref[:, cols]
        o = o * lax.rsqrt(jnp.mean(o * o, axis=-1, keepdims=True) + EPS) * hn_ref[:, cols]
        mrg_ref[:, cols] = (o * gates_ref[:, 2048 + h * DK:2048 + (h + 1) * DK]).astype(BF16)
    for gi, w in enumerate(POOL_WINDOWS):
        cols = slice(gi * POOL_GROUP, (gi + 1) * POOL_GROUP)
        u = gates_ref[:, 2560 + gi * POOL_GROUP:2560 + (gi + 1) * POOL_GROUP]
        acc = u
        for j in range(1, w):
            r = POOL_STATE - j
            acc = acc + past_ref[:, r * POOL_WIDTH + gi * POOL_GROUP:r * POOL_WIDTH + (gi + 1) * POOL_GROUP]
        pooled = acc / float(w) - u
        yb = _dot(pooled.astype(BF16), pmix_ref[gi]) * ps_ref[:, cols]
        mrg_ref[:, HGRN_WIDTH + gi * POOL_GROUP:HGRN_WIDTH + (gi + 1) * POOL_GROUP] = yb.astype(BF16)
    pool_ref[:, 0:(POOL_STATE - 1) * POOL_WIDTH] = past_ref[:, POOL_WIDTH:POOL_STATE * POOL_WIDTH]
    pool_ref[:, (POOL_STATE - 1) * POOL_WIDTH:] = gates_ref[:, 2560:3072]
    x1 = x_ref[...] + _dot(mrg_ref[...], wout_ref[...])
    x1_ref[...] = x1
    hb = _rms(x1, gc_ref[...]).astype(BF16)
    q_ref[...] = _dot(hb, wq_ref[...]) * ATTN_SCALE


def _sample_mix(x, gates, o, past, hgrn_norm, pool_mix, pool_scale, w_out, g_cross, w_cq):
    n = x.shape[0]
    full = lambda shape: pl.BlockSpec(shape, lambda i: (0,) * len(shape))
    return pl.pallas_call(
        _sample_mix_kernel,
        grid=(1,),
        in_specs=[full((n, D_MODEL)), full((n, 6 * 512)), full((n, HGRN_WIDTH)),
                  full((n, POOL_STATE * POOL_WIDTH)), full((1, HGRN_WIDTH)),
                  full((4, POOL_GROUP, POOL_GROUP)), full((1, POOL_WIDTH)),
                  full((D_MODEL, D_MODEL)), full((1, D_MODEL)), full((D_MODEL, D_MODEL))],
        out_specs=[full((n, D_MODEL)), full((n, POOL_STATE * POOL_WIDTH)), full((n, D_MODEL))],
        out_shape=[jax.ShapeDtypeStruct((n, D_MODEL), F32),
                   jax.ShapeDtypeStruct((n, POOL_STATE * POOL_WIDTH), F32),
                   jax.ShapeDtypeStruct((n, D_MODEL), F32)],
        scratch_shapes=[pltpu.VMEM((n, D_MODEL), BF16)],
        compiler_params=pltpu.CompilerParams(
            dimension_semantics=("arbitrary",), vmem_limit_bytes=VMEM_LIMIT_BYTES),
        name="sample_mix",
    )(x, gates, o, past, hgrn_norm, pool_mix, pool_scale, w_out, g_cross, w_cq)


def _sample_attn_kernel(q_ref, k_ref, v_ref, o_ref):
    for j in range(ATTN_TOKENS):
        q = q_ref[j]
        s = jnp.sum(k_ref[j] * q[None], axis=-1, keepdims=True)
        e = jnp.exp(s - jnp.max(s, axis=0, keepdims=True))
        den = jnp.sum(e, axis=0)
        o_ref[j] = jnp.sum(e * v_ref[j], axis=0) / den


def _sample_attn(q, cache_k, cache_v):
    n = q.shape[0]
    G = ATTN_TOKENS
    kv_spec = pl.BlockSpec((G, MEM_LEN, X_HEADS, X_HEAD_DIM), lambda i: (i, 0, 0, 0))
    q_spec = pl.BlockSpec((G, X_HEADS, X_HEAD_DIM), lambda i: (i, 0, 0))
    return pl.pallas_call(
        _sample_attn_kernel,
        grid=(n // G,),
        in_specs=[q_spec, kv_spec, kv_spec],
        out_specs=q_spec,
        out_shape=jax.ShapeDtypeStruct((n, X_HEADS, X_HEAD_DIM), F32),
        compiler_params=pltpu.CompilerParams(
            dimension_semantics=("arbitrary",), vmem_limit_bytes=VMEM_LIMIT_BYTES),
        name="sample_attn",
    )(q, cache_k, cache_v)


def _sample_ffn_kernel(x_ref, att_ref, wo_ref, g_ref, w1_ref, w2_ref, gf_ref, o_ref):
    x2 = x_ref[...] + _dot(att_ref[...].astype(BF16), wo_ref[...])
    o_ref[...] = _ffn_body(x2, g_ref, w1_ref, w2_ref, gf_ref)


def _sample_ffn(x1, att, w_co, g_ffn, w1, w2, g_final):
    n = x1.shape[0]
    full = lambda shape: pl.BlockSpec(shape, lambda i: (0,) * len(shape))
    return pl.pallas_call(
        _sample_ffn_kernel,
        grid=(1,),
        in_specs=[full((n, D_MODEL)), full((n, D_MODEL)), full((D_MODEL, D_MODEL)),
                  full((1, D_MODEL)), full((D_MODEL, 2 * D_FF)), full((D_FF, D_MODEL)),
                  full((1, D_MODEL))],
        out_specs=full((n, D_MODEL)),
        out_shape=jax.ShapeDtypeStruct((n, D_MODEL), F32),
        compiler_params=pltpu.CompilerParams(
            dimension_semantics=("arbitrary",), vmem_limit_bytes=VMEM_LIMIT_BYTES),
        name="sample_ffn",
    )(x1, att, w_co, g_ffn, w1, w2, g_final)


def kernel(x_prompt, x_sample, mem_prompt, state_hgrn, state_pool, cache_mem_k, cache_mem_v,
           g_mix, w_in, hgrn_lb, hgrn_norm, pool_mix, pool_scale, w_out, g_mem, w_mem_kv,
           g_cross, w_cq, w_co, g_ffn, w_ffn_in, w_ffn_out, g_final):
    nb, L, _ = x_prompt.shape
    ns = x_sample.shape[0]

    w_in_b = w_in[0].astype(BF16)
    w_out_b = w_out[0].astype(BF16)
    w_kv_b = w_mem_kv[0].astype(BF16)
    w_cq_b = w_cq[0].astype(BF16)
    w_co_b = w_co[0].astype(BF16)
    w1_b = w_ffn_in[0].astype(BF16)
    w2_b = w_ffn_out[0].astype(BF16)
    pmix_b = pool_mix[0].astype(BF16)
    g_final2 = g_final.reshape(1, D_MODEL)

    mem_k, mem_v, kt, vb = _mem_kv(mem_prompt, g_mem, w_kv_b)
    x1, hgrn_p, pool_p = _mixer_prompt(x_prompt, g_mix, w_in_b, hgrn_lb, hgrn_norm, pmix_b,
                                       pool_scale, w_out_b)
    x2 = _attn_prompt(x1, g_cross, w_cq_b, kt, vb, w_co_b)
    y_prompt = _ffn_prompt(x2.reshape(nb * L, D_MODEL), g_ffn, w1_b, w2_b, g_final2)

    xs = x_sample.reshape(ns, D_MODEL)
    gates = _sample_gates(xs, g_mix, w_in_b, hgrn_lb)
    hgrn_s, o_s = _sample_state(gates, state_hgrn[0])
    x1s, pool_s, qs = _sample_mix(xs, gates, o_s, state_pool[0].reshape(ns, POOL_STATE * POOL_WIDTH),
                                  hgrn_norm, pmix_b, pool_scale, w_out_b, g_cross, w_cq_b)
    att = _sample_attn(qs.reshape(ns, X_HEADS, X_HEAD_DIM), cache_mem_k[0], cache_mem_v[0])
    y_sample = _sample_ffn(x1s, att.reshape(ns, D_MODEL), w_co_b, g_ffn, w1_b, w2_b, g_final2)

    return (y_prompt.reshape(nb, L, D_MODEL),
            y_sample.reshape(ns, 1, D_MODEL),
            hgrn_p[None],
            pool_p[None],
            mem_k.reshape(1, nb, MEM_LEN, X_HEADS, X_HEAD_DIM),
            mem_v.reshape(1, nb, MEM_LEN, X_HEADS, X_HEAD_DIM),
            hgrn_s[None],
            pool_s.reshape(1, ns, POOL_STATE, POOL_WIDTH))
```

```python
import functools

import jax
import jax.numpy as jnp
from jax import lax
from jax.experimental import pallas as pl
from jax.experimental.pallas import tpu as pltpu

F32 = jnp.float32
BF16 = jnp.bfloat16

D_MODEL = 1024
HGRN_WIDTH = 512
HEADS = 4
DK = 128
CHUNK = 64
POOL_WIDTH = 512
POOL_WINDOWS = (2, 4, 8, 16)
POOL_GROUP = 128
POOL_STATE = 15
IN_PROJ = 4 * HGRN_WIDTH + POOL_WIDTH
MEM_LEN = 256
X_HEADS = 4
X_HEAD_DIM = 256
D_FF = 2816
EPS = 1e-6
ATTN_SCALE = X_HEAD_DIM ** -0.5

VMEM_LIMIT_BYTES = 56 * 1024 * 1024

MIX_ROWS = 512
TRI_ROWS = 256
ATTN_ROWS = 512
FFN_ROWS = 512
STATE_TOKENS = 8
ATTN_TOKENS = 4


def _dot(a, b):
    return jnp.dot(a, b, preferred_element_type=F32)


def _dot_nt(a, b):
    return lax.dot_general(a, b, (((1,), (1,)), ((), ())), preferred_element_type=F32)


def _dot_tn(a, b):
    return lax.dot_general(a, b, (((0,), (0,)), ((), ())), preferred_element_type=F32)


def _rms(x, g):
    ms = jnp.mean(x * x, axis=-1, keepdims=True)
    return x * lax.rsqrt(ms + EPS) * g


def _sigmoid(x):
    return 1.0 / (1.0 + jnp.exp(-x))


def _lower_bound(lb_ref):
    t = lb_ref[...]
    m = jnp.max(t, axis=0, keepdims=True)
    e = jnp.exp(t - m)
    return e[0:1, :] / jnp.sum(e, axis=0, keepdims=True)


def _gates(proj_q, proj_f, lb):
    qq = proj_q * _sigmoid(proj_q)
    sig = _sigmoid(proj_f)
    fgate = lb + (1.0 - lb) * sig
    kk = (1.0 - lb) * (1.0 - sig)
    return qq, fgate, kk


def _split3(x):
    hi = x.astype(BF16)
    r1 = x - hi.astype(F32)
    mid = r1.astype(BF16)
    lo = (r1 - mid.astype(F32)).astype(BF16)
    return hi, mid, lo


def _memkv_kernel(mem_ref, g_ref, w_ref, k_ref, v_ref, kt_ref, vb_ref):
    h = _rms(mem_ref[0], g_ref[...]).astype(BF16)
    kv = _dot(h, w_ref[...])
    k = kv[:, :D_MODEL]
    v = kv[:, D_MODEL:]
    k_ref[0] = k
    v_ref[0] = v
    kt_ref[0] = k.T.astype(BF16)
    vb_ref[0] = v.astype(BF16)


def _mem_kv(mem, g_mem, w_kv):
    nb = mem.shape[0]
    full = lambda shape: pl.BlockSpec(shape, lambda b: (0,) * len(shape))
    return pl.pallas_call(
        _memkv_kernel,
        grid=(nb,),
        in_specs=[pl.BlockSpec((1, MEM_LEN, D_MODEL), lambda b: (b, 0, 0)),
                  full((1, D_MODEL)), full((D_MODEL, 2 * D_MODEL))],
        out_specs=[pl.BlockSpec((1, MEM_LEN, D_MODEL), lambda b: (b, 0, 0)),
                   pl.BlockSpec((1, MEM_LEN, D_MODEL), lambda b: (b, 0, 0)),
                   pl.BlockSpec((1, D_MODEL, MEM_LEN), lambda b: (b, 0, 0)),
                   pl.BlockSpec((1, MEM_LEN, D_MODEL), lambda b: (b, 0, 0))],
        out_shape=[jax.ShapeDtypeStruct((nb, MEM_LEN, D_MODEL), F32),
                   jax.ShapeDtypeStruct((nb, MEM_LEN, D_MODEL), F32),
                   jax.ShapeDtypeStruct((nb, D_MODEL, MEM_LEN), BF16),
                   jax.ShapeDtypeStruct((nb, MEM_LEN, D_MODEL), BF16)],
        compiler_params=pltpu.CompilerParams(
            dimension_semantics=("arbitrary",), vmem_limit_bytes=VMEM_LIMIT_BYTES),
        name="mem_kv",
    )(mem, g_mem, w_kv)


def _mixer_kernel(x_ref, g_ref, win_ref, lb_ref, hn_ref, pmix_ref, ps_ref, wout_ref,
                  x1_ref, hst_ref, pst_ref,
                  st_ref, ext_ref, proj_ref, qq_ref, kk_ref, b_ref, mrg_ref):
    T = MIX_ROWS
    l = pl.program_id(1)

    @pl.when(l == 0)
    def _():
        st_ref[...] = jnp.zeros_like(st_ref)
        ext_ref[0:16, :] = jnp.zeros((16, POOL_WIDTH), F32)

    x = x_ref[0]
    hb = _rms(x, g_ref[...]).astype(BF16)
    proj_ref[...] = _dot(hb, win_ref[...])

    lb = _lower_bound(lb_ref)
    qq, fgate, kk = _gates(proj_ref[:, 0:512], proj_ref[:, 512:1024], lb)
    qq_ref[...] = qq
    kk_ref[...] = kk

    r = lax.broadcasted_iota(jnp.int32, (TRI_ROWS, TRI_ROWS), 0)
    c = lax.broadcasted_iota(jnp.int32, (TRI_ROWS, TRI_ROWS), 1)
    tri = jnp.where((c <= r) & (c >= (r & -CHUNK)), 1.0, 0.0).astype(BF16)
    logf = jnp.log(fgate)
    for blk in range(T // TRI_ROWS):
        rows = slice(blk * TRI_ROWS, (blk + 1) * TRI_ROWS)
        hi, mid, lo = _split3(logf[rows])
        b_ref[rows, :] = _dot(tri, hi) + _dot(tri, mid) + _dot(tri, lo)

    cr = lax.broadcasted_iota(jnp.int32, (CHUNK, CHUNK), 0)
    cc = lax.broadcasted_iota(jnp.int32, (CHUNK, CHUNK), 1)
    causal = cc <= cr
    mid_row = (CHUNK - 1) // 2

    for ci in range(T // CHUNK):
        r0 = ci * CHUNK
        rows = slice(r0, r0 + CHUNK)
        for h in range(HEADS):
            cols = slice(h * DK, (h + 1) * DK)
            b = b_ref[rows, cols]
            m = b_ref[r0 + mid_row:r0 + mid_row + 1, cols]
            b_end = b_ref[r0 + CHUNK - 1:r0 + CHUNK, cols]
            e1 = jnp.exp(b - m)
            e2 = jnp.exp(m - b)
            q1 = qq_ref[rows, cols] * e1
            q0 = q1 * jnp.exp(m)
            ks = kk_ref[rows, cols] * e2
            k2 = ks * jnp.exp(b_end - m)
            v = proj_ref[rows, 1024 + h * DK:1024 + (h + 1) * DK].astype(BF16)
            st = st_ref[h]
            inter = _dot_nt(q0.astype(BF16), st.astype(BF16))
            a = _dot_nt(q1.astype(BF16), ks.astype(BF16))
            a = jnp.where(causal, a, 0.0)
            o = inter + _dot(a.astype(BF16), v)
            st_ref[h] = st * jnp.exp(b_end) + _dot_tn(v, k2.astype(BF16))
            o = o * lax.rsqrt(jnp.mean(o * o, axis=-1, keepdims=True) + EPS)
            o = o * hn_ref[:, cols]
            g = proj_ref[rows, 1536 + h * DK:1536 + (h + 1) * DK]
            mrg_ref[rows, cols] = (o * _sigmoid(g)).astype(BF16)

    ext_ref[16:16 + T, :] = proj_ref[:, 2048:2560]
    pos = l * T + lax.broadcasted_iota(jnp.int32, (T, POOL_GROUP), 0)
    for gi, w in enumerate(POOL_WINDOWS):
        cols = slice(gi * POOL_GROUP, (gi + 1) * POOL_GROUP)
        s = ext_ref[:, cols]
        u = s[16:]
        shift = 1
        while shift < w:
            s = s + pltpu.roll(s, shift, axis=0)
            shift *= 2
        cnt = jnp.minimum(pos + 1, w).astype(F32)
        pooled = s[16:] / cnt - u
        yb = _dot(pooled.astype(BF16), pmix_ref[gi]) * ps_ref[:, cols]
        mrg_ref[:, HGRN_WIDTH + gi * POOL_GROUP:HGRN_WIDTH + (gi + 1) * POOL_GROUP] = yb.astype(BF16)

    @pl.when(l == pl.num_programs(1) - 1)
    def _():
        for h in range(HEADS):
            hst_ref[0, h] = st_ref[h].T
        pst_ref[0] = ext_ref[T + 1:T + 16, :]

    ext_ref[0:16, :] = ext_ref[T:T + 16, :]
    x1_ref[0] = x + _dot(mrg_ref[...], wout_ref[...])


def _mixer_prompt(x, g_mix, w_in, hgrn_lb, hgrn_norm, pool_mix, pool_scale, w_out):
    nb, L, _ = x.shape
    T = MIX_ROWS
    full = lambda shape: pl.BlockSpec(shape, lambda b, l: (0,) * len(shape))
    return pl.pallas_call(
        _mixer_kernel,
        grid=(nb, L // T),
        in_specs=[pl.BlockSpec((1, T, D_MODEL), lambda b, l: (b, l, 0)),
                  full((1, D_MODEL)), full((D_MODEL, IN_PROJ)), full((2, HGRN_WIDTH)),
                  full((1, HGRN_WIDTH)), full((4, POOL_GROUP, POOL_GROUP)),
                  full((1, POOL_WIDTH)), full((D_MODEL, D_MODEL))],
        out_specs=[pl.BlockSpec((1, T, D_MODEL), lambda b, l: (b, l, 0)),
                   pl.BlockSpec((1, HEADS, DK, DK), lambda b, l: (b, 0, 0, 0)),
                   pl.BlockSpec((1, POOL_STATE, POOL_WIDTH), lambda b, l: (b, 0, 0))],
        out_shape=[jax.ShapeDtypeStruct((nb, L, D_MODEL), F32),
                   jax.ShapeDtypeStruct((nb, HEADS, DK, DK), F32),
                   jax.ShapeDtypeStruct((nb, POOL_STATE, POOL_WIDTH), F32)],
        scratch_shapes=[pltpu.VMEM((HEADS, DK, DK), F32),
                        pltpu.VMEM((16 + T, POOL_WIDTH), F32),
                        pltpu.VMEM((T, IN_PROJ), F32),
                        pltpu.VMEM((T, HGRN_WIDTH), F32),
                        pltpu.VMEM((T, HGRN_WIDTH), F32),
                        pltpu.VMEM((T, HGRN_WIDTH), F32),
                        pltpu.VMEM((T, D_MODEL), BF16)],
        compiler_params=pltpu.CompilerParams(
            dimension_semantics=("arbitrary", "arbitrary"), vmem_limit_bytes=VMEM_LIMIT_BYTES),
        name="mixer_prompt",
    )(x, g_mix, w_in, hgrn_lb, hgrn_norm, pool_mix, pool_scale, w_out)


def _attn_kernel(x_ref, g_ref, wq_ref, kt_ref, vb_ref, wo_ref, o_ref, att_ref):
    x = x_ref[0]
    hb = _rms(x, g_ref[...]).astype(BF16)
    q = (_dot(hb, wq_ref[...]) * ATTN_SCALE).astype(BF16)
    for h in range(X_HEADS):
        cols = slice(h * X_HEAD_DIM, (h + 1) * X_HEAD_DIM)
        s = _dot(q[:, cols], kt_ref[0, cols, :])
        e = jnp.exp(s - jnp.max(s, axis=-1, keepdims=True))
        den = jnp.sum(e, axis=-1, keepdims=True)
        o = _dot(e.astype(BF16), vb_ref[0, :, cols]) / den
        att_ref[:, cols] = o.astype(BF16)
    o_ref[0] = x + _dot(att_ref[...], wo_ref[...])


def _attn_prompt(x, g_cross, w_cq, kt, vb, w_co):
    nb, L, _ = x.shape
    T = ATTN_ROWS
    full = lambda shape: pl.BlockSpec(shape, lambda b, l: (0,) * len(shape))
    return pl.pallas_call(
        _attn_kernel,
        grid=(nb, L // T),
        in_specs=[pl.BlockSpec((1, T, D_MODEL), lambda b, l: (b, l, 0)),
                  full((1, D_MODEL)), full((D_MODEL, D_MODEL)),
                  pl.BlockSpec((1, D_MODEL, MEM_LEN), lambda b, l: (b, 0, 0)),
                  pl.BlockSpec((1, MEM_LEN, D_MODEL), lambda b, l: (b, 0, 0)),
                  full((D_MODEL, D_MODEL))],
        out_specs=pl.BlockSpec((1, T, D_MODEL), lambda b, l: (b, l, 0)),
        out_shape=jax.ShapeDtypeStruct((nb, L, D_MODEL), F32),
        scratch_shapes=[pltpu.VMEM((T, D_MODEL), BF16)],
        compiler_params=pltpu.CompilerParams(
            dimension_semantics=("arbitrary", "arbitrary"), vmem_limit_bytes=VMEM_LIMIT_BYTES),
        name="attn_prompt",
    )(x, g_cross, w_cq, kt, vb, w_co)


def _ffn_body(x, g_ref, w1_ref, w2_ref, gf_ref):
    hb = _rms(x, g_ref[...]).astype(BF16)
    a = _dot(hb, w1_ref[:, :D_FF])
    bg = _dot(hb, w1_ref[:, D_FF:])
    act = (a * _sigmoid(a) * bg).astype(BF16)
    y = x + _dot(act, w2_ref[...])
    return _rms(y, gf_ref[...])


def _ffn_kernel(x_ref, g_ref, w1_ref, w2_ref, gf_ref, o_ref):
    o_ref[...] = _ffn_body(x_ref[...], g_ref, w1_ref, w2_ref, gf_ref)


def _ffn_prompt(x, g_ffn, w1, w2, g_final):
    n = x.shape[0]
    T = FFN_ROWS
    full = lambda shape: pl.BlockSpec(shape, lambda i: (0,) * len(shape))
    return pl.pallas_call(
        _ffn_kernel,
        grid=(n // T,),
        in_specs=[pl.BlockSpec((T, D_MODEL), lambda i: (i, 0)),
                  full((1, D_MODEL)), full((D_MODEL, 2 * D_FF)), full((D_FF, D_MODEL)),
                  full((1, D_MODEL))],
        out_specs=pl.BlockSpec((T, D_MODEL), lambda i: (i, 0)),
        out_shape=jax.ShapeDtypeStruct((n, D_MODEL), F32),
        compiler_params=pltpu.CompilerParams(
            dimension_semantics=("arbitrary",), vmem_limit_bytes=VMEM_LIMIT_BYTES),
        name="ffn_prompt",
    )(x, g_ffn, w1, w2, g_final)


def _sample_gates_kernel(x_ref, g_ref, win_ref, lb_ref, o_ref):
    hb = _rms(x_ref[...], g_ref[...]).astype(BF16)
    proj = _dot(hb, win_ref[...])
    lb = _lower_bound(lb_ref)
    qq, fgate, kk = _gates(proj[:, 0:512], proj[:, 512:1024], lb)
    o_ref[:, 0:512] = qq
    o_ref[:, 512:1024] = fgate
    o_ref[:, 1024:1536] = kk
    o_ref[:, 1536:2048] = proj[:, 1024:1536]
    o_ref[:, 2048:2560] = _sigmoid(proj[:, 1536:2048])
    o_ref[:, 2560:3072] = proj[:, 2048:2560]


def _sample_gates(x, g_mix, w_in, hgrn_lb):
    n = x.shape[0]
    full = lambda shape: pl.BlockSpec(shape, lambda i: (0,) * len(shape))
    return pl.pallas_call(
        _sample_gates_kernel,
        grid=(1,),
        in_specs=[full((n, D_MODEL)), full((1, D_MODEL)), full((D_MODEL, IN_PROJ)),
                  full((2, HGRN_WIDTH))],
        out_specs=full((n, 6 * 512)),
        out_shape=jax.ShapeDtypeStruct((n, 6 * 512), F32),
        compiler_params=pltpu.CompilerParams(
            dimension_semantics=("arbitrary",), vmem_limit_bytes=VMEM_LIMIT_BYTES),
        name="sample_gates",
    )(x, g_mix, w_in, hgrn_lb)


def _sample_state_kernel(q_ref, f_ref, k_ref, v_ref, s_ref, so_ref, o_ref):
    G = STATE_TOKENS
    pad = jnp.zeros((DK - G, DK), F32)
    for h in range(HEADS):
        cols = slice(h * DK, (h + 1) * DK)
        qt = jnp.concatenate([q_ref[:, cols], pad], axis=0).T
        ft = jnp.concatenate([f_ref[:, cols], pad], axis=0).T
        kt = jnp.concatenate([k_ref[:, cols], pad], axis=0).T
        for j in range(G):
            s_new = ft[:, j:j + 1] * s_ref[j, h] + kt[:, j:j + 1] * v_ref[j:j + 1, cols]
            so_ref[j, h] = s_new
            o_ref[j:j + 1, cols] = jnp.sum(qt[:, j:j + 1] * s_new, axis=0, keepdims=True)


def _sample_state(gates, state):
    n = gates.shape[0]
    G = STATE_TOKENS
    col = lambda c: pl.BlockSpec((G, 512), lambda i: (i, c))
    return pl.pallas_call(
        _sample_state_kernel,
        grid=(n // G,),
        in_specs=[col(0), col(1), col(2), col(3),
                  pl.BlockSpec((G, HEADS, DK, DK), lambda i: (i, 0, 0, 0))],
        out_specs=[pl.BlockSpec((G, HEADS, DK, DK), lambda i: (i, 0, 0, 0)),
                   pl.BlockSpec((G, HGRN_WIDTH), lambda i: (i, 0))],
        out_shape=[jax.ShapeDtypeStruct((n, HEADS, DK, DK), F32),
                   jax.ShapeDtypeStruct((n, HGRN_WIDTH), F32)],
        compiler_params=pltpu.CompilerParams(
            dimension_semantics=("arbitrary",), vmem_limit_bytes=VMEM_LIMIT_BYTES),
        name="sample_state",
    )(gates, gates, gates, gates, state)


def _sample_mix_kernel(x_ref, gates_ref, o_ref, past_ref, hn_ref, pmix_ref, ps_ref, wout_ref,
                       gc_ref, wq_ref, x1_ref, pool_ref, q_ref, mrg_ref):
    for h in range(HEADS):
        cols = slice(h * DK, (h + 1) * DK)
        o = o_ref[:, cols]
        o = o * lax.rsqrt(jnp.mean(o * o, axis=-1, keepdims=True) + EPS) * hn_ref[:, cols]
        mrg_ref[:, cols] = (o * gates_ref[:, 2048 + h * DK:2048 + (h + 1) * DK]).astype(BF16)
    for gi, w in enumerate(POOL_WINDOWS):
        cols = slice(gi * POOL_GROUP, (gi + 1) * POOL_GROUP)
        u = gates_ref[:, 2560 + gi * POOL_GROUP:2560 + (gi + 1) * POOL_GROUP]
        acc = u
        for j in range(1, w):
            r = POOL_STATE - j
            acc = acc + past_ref[:, r * POOL_WIDTH + gi * POOL_GROUP:r * POOL_WIDTH + (gi + 1) * POOL_GROUP]
        pooled = acc / float(w) - u
        yb = _dot(pooled.astype(BF16), pmix_ref[gi]) * ps_ref[:, cols]
        mrg_ref[:, HGRN_WIDTH + gi * POOL_GROUP:HGRN_WIDTH + (gi + 1) * POOL_GROUP] = yb.astype(BF16)
    pool_ref[:, 0:(POOL_STATE - 1) * POOL_WIDTH] = past_ref[:, POOL_WIDTH:POOL_STATE * POOL_WIDTH]
    pool_ref[:, (POOL_STATE - 1) * POOL_WIDTH:] = gates_ref[:, 2560:3072]
    x1 = x_ref[...] + _dot(mrg_ref[...], wout_ref[...])
    x1_ref[...] = x1
    hb = _rms(x1, gc_ref[...]).astype(BF16)
    q_ref[...] = _dot(hb, wq_ref[...]) * ATTN_SCALE


def _sample_mix(x, gates, o, past, hgrn_norm, pool_mix, pool_scale, w_out, g_cross, w_cq):
    n = x.shape[0]
    full = lambda shape: pl.BlockSpec(shape, lambda i: (0,) * len(shape))
    return pl.pallas_call(
        _sample_mix_kernel,
        grid=(1,),
        in_specs=[full((n, D_MODEL)), full((n, 6 * 512)), full((n, HGRN_WIDTH)),
                  full((n, POOL_STATE * POOL_WIDTH)), full((1, HGRN_WIDTH)),
                  full((4, POOL_GROUP, POOL_GROUP)), full((1, POOL_WIDTH)),
                  full((D_MODEL, D_MODEL)), full((1, D_MODEL)), full((D_MODEL, D_MODEL))],
        out_specs=[full((n, D_MODEL)), full((n, POOL_STATE * POOL_WIDTH)), full((n, D_MODEL))],
        out_shape=[jax.ShapeDtypeStruct((n, D_MODEL), F32),
                   jax.ShapeDtypeStruct((n, POOL_STATE * POOL_WIDTH), F32),
                   jax.ShapeDtypeStruct((n, D_MODEL), F32)],
        scratch_shapes=[pltpu.VMEM((n, D_MODEL), BF16)],
        compiler_params=pltpu.CompilerParams(
            dimension_semantics=("arbitrary",), vmem_limit_bytes=VMEM_LIMIT_BYTES),
        name="sample_mix",
    )(x, gates, o, past, hgrn_norm, pool_mix, pool_scale, w_out, g_cross, w_cq)


def _split_heads(x):
    lead = x.shape[:-2]
    x = x.reshape(lead + (X_HEADS, 2, 128))
    x = jnp.swapaxes(x, -3, -2)
    return x.reshape(lead + (2 * X_HEADS, 128))


def _merge_heads(x):
    lead = x.shape[:-2]
    x = x.reshape(lead + (2, X_HEADS, 128))
    x = jnp.swapaxes(x, -3, -2)
    return x.reshape(lead + (X_HEADS, X_HEAD_DIM))


def _sample_attn_kernel(q_ref, k_ref, v_ref, o_ref):
    for j in range(ATTN_TOKENS):
        part = jnp.sum(k_ref[j] * q_ref[j][None], axis=-1, keepdims=True)
        s = part + pltpu.roll(part, X_HEADS, axis=1)
        e = jnp.exp(s - jnp.max(s, axis=0, keepdims=True))
        den = jnp.sum(e, axis=0)
        o_ref[j] = jnp.sum(e * v_ref[j], axis=0) / den


def _sample_attn(q, cache_k, cache_v):
    n = q.shape[0]
    G = ATTN_TOKENS
    kv_spec = pl.BlockSpec((G, MEM_LEN, 2 * X_HEADS, 128), lambda i: (i, 0, 0, 0))
    q_spec = pl.BlockSpec((G, 2 * X_HEADS, 128), lambda i: (i, 0, 0))
    return pl.pallas_call(
        _sample_attn_kernel,
        grid=(n // G,),
        in_specs=[q_spec, kv_spec, kv_spec],
        out_specs=q_spec,
        out_shape=jax.ShapeDtypeStruct((n, 2 * X_HEADS, 128), F32),
        compiler_params=pltpu.CompilerParams(
            dimension_semantics=("arbitrary",), vmem_limit_bytes=VMEM_LIMIT_BYTES),
        name="sample_attn",
    )(q, cache_k, cache_v)


def _sample_ffn_kernel(x_ref, att_ref, wo_ref, g_ref, w1_ref, w2_ref, gf_ref, o_ref):
    x2 = x_ref[...] + _dot(att_ref[...].astype(BF16), wo_ref[...])
    o_ref[...] = _ffn_body(x2, g_ref, w1_ref, w2_ref, gf_ref)


def _sample_ffn(x1, att, w_co, g_ffn, w1, w2, g_final):
    n = x1.shape[0]
    full = lambda shape: pl.BlockSpec(shape, lambda i: (0,) * len(shape))
    return pl.pallas_call(
        _sample_ffn_kernel,
        grid=(1,),
        in_specs=[full((n, D_MODEL)), full((n, D_MODEL)), full((D_MODEL, D_MODEL)),
                  full((1, D_MODEL)), full((D_MODEL, 2 * D_FF)), full((D_FF, D_MODEL)),
                  full((1, D_MODEL))],
        out_specs=full((n, D_MODEL)),
        out_shape=jax.ShapeDtypeStruct((n, D_MODEL), F32),
        compiler_params=pltpu.CompilerParams(
            dimension_semantics=("arbitrary",), vmem_limit_bytes=VMEM_LIMIT_BYTES),
        name="sample_ffn",
    )(x1, att, w_co, g_ffn, w1, w2, g_final)


def kernel(x_prompt, x_sample, mem_prompt, state_hgrn, state_pool, cache_mem_k, cache_mem_v,
           g_mix, w_in, hgrn_lb, hgrn_norm, pool_mix, pool_scale, w_out, g_mem, w_mem_kv,
           g_cross, w_cq, w_co, g_ffn, w_ffn_in, w_ffn_out, g_final):
    nb, L, _ = x_prompt.shape
    ns = x_sample.shape[0]

    w_in_b = w_in[0].astype(BF16)
    w_out_b = w_out[0].astype(BF16)
    w_kv_b = w_mem_kv[0].astype(BF16)
    w_cq_b = w_cq[0].astype(BF16)
    w_co_b = w_co[0].astype(BF16)
    w1_b = w_ffn_in[0].astype(BF16)
    w2_b = w_ffn_out[0].astype(BF16)
    pmix_b = pool_mix[0].astype(BF16)
    g_final2 = g_final.reshape(1, D_MODEL)

    mem_k, mem_v, kt, vb = _mem_kv(mem_prompt, g_mem, w_kv_b)
    x1, hgrn_p, pool_p = _mixer_prompt(x_prompt, g_mix, w_in_b, hgrn_lb, hgrn_norm, pmix_b,
                                       pool_scale, w_out_b)
    x2 = _attn_prompt(x1, g_cross, w_cq_b, kt, vb, w_co_b)
    y_prompt = _ffn_prompt(x2.reshape(nb * L, D_MODEL), g_ffn, w1_b, w2_b, g_final2)

    xs = x_sample.reshape(ns, D_MODEL)
    gates = _sample_gates(xs, g_mix, w_in_b, hgrn_lb)
    hgrn_s, o_s = _sample_state(gates, state_hgrn[0])
    x1s, pool_s, qs = _sample_mix(xs, gates, o_s, state_pool[0].reshape(ns, POOL_STATE * POOL_WIDTH),
                                  hgrn_norm, pmix_b, pool_scale, w_out_b, g_cross, w_cq_b)
    att = _sample_attn(_split_heads(qs.reshape(ns, X_HEADS, X_HEAD_DIM)),
                       _split_heads(cache_mem_k[0]), _split_heads(cache_mem_v[0]))
    y_sample = _sample_ffn(x1s, _merge_heads(att).reshape(ns, D_MODEL), w_co_b, g_ffn, w1_b, w2_b,
                           g_final2)

    return (y_prompt.reshape(nb, L, D_MODEL),
            y_sample.reshape(ns, 1, D_MODEL),
            hgrn_p[None],
            pool_p[None],
            mem_k.reshape(1, nb, MEM_LEN, X_HEADS, X_HEAD_DIM),
            mem_v.reshape(1, nb, MEM_LEN, X_HEADS, X_HEAD_DIM),
            hgrn_s[None],
            pool_s.reshape(1, ns, POOL_STATE, POOL_WIDTH))
```

```python
import functools

import jax
import jax.numpy as jnp
from jax import lax
from jax.experimental import pallas as pl
from jax.experimental.pallas import tpu as pltpu

F32 = jnp.float32
BF16 = jnp.bfloat16

D_MODEL = 1024
HGRN_WIDTH = 512
HEADS = 4
DK = 128
CHUNK = 64
POOL_WIDTH = 512
POOL_WINDOWS = (2, 4, 8, 16)
POOL_GROUP = 128
POOL_STATE = 15
IN_PROJ = 4 * HGRN_WIDTH + POOL_WIDTH
MEM_LEN = 256
X_HEADS = 4
X_HEAD_DIM = 256
D_FF = 2816
EPS = 1e-6
ATTN_SCALE = X_HEAD_DIM ** -0.5

VMEM_LIMIT_BYTES = 56 * 1024 * 1024

MIX_ROWS = 512
TRI_ROWS = 256
ATTN_ROWS = 512
FFN_ROWS = 512
STATE_TOKENS = 8
ATTN_TOKENS = 4
FFN_COL_CHUNKS = ((0, 1024), (1024, 2048), (2048, D_FF))


def _dot(a, b):
    return jnp.dot(a, b, preferred_element_type=F32)


def _dot_nt(a, b):
    return lax.dot_general(a, b, (((1,), (1,)), ((), ())), preferred_element_type=F32)


def _dot_tn(a, b):
    return lax.dot_general(a, b, (((0,), (0,)), ((), ())), preferred_element_type=F32)


def _rms(x, g):
    ms = jnp.mean(x * x, axis=-1, keepdims=True)
    return x * lax.rsqrt(ms + EPS) * g


def _sigmoid(x):
    return 1.0 / (1.0 + jnp.exp(-x))


def _lower_bound(lb_ref):
    t = lb_ref[...]
    m = jnp.max(t, axis=0, keepdims=True)
    e = jnp.exp(t - m)
    return e[0:1, :] / jnp.sum(e, axis=0, keepdims=True)


def _gates(proj_q, proj_f, lb):
    qq = proj_q * _sigmoid(proj_q)
    sig = _sigmoid(proj_f)
    fgate = lb + (1.0 - lb) * sig
    kk = (1.0 - lb) * (1.0 - sig)
    return qq, fgate, kk


def _split3(x):
    hi = x.astype(BF16)
    r1 = x - hi.astype(F32)
    mid = r1.astype(BF16)
    lo = (r1 - mid.astype(F32)).astype(BF16)
    return hi, mid, lo


def _memkv_kernel(mem_ref, g_ref, w_ref, k_ref, v_ref, kt_ref, vb_ref):
    h = _rms(mem_ref[0], g_ref[...]).astype(BF16)
    kv = _dot(h, w_ref[...])
    k = kv[:, :D_MODEL]
    v = kv[:, D_MODEL:]
    k_ref[0] = k
    v_ref[0] = v
    kt_ref[0] = k.T.astype(BF16)
    vb_ref[0] = v.astype(BF16)


def _mem_kv(mem, g_mem, w_kv):
    nb = mem.shape[0]
    full = lambda shape: pl.BlockSpec(shape, lambda b: (0,) * len(shape))
    return pl.pallas_call(
        _memkv_kernel,
        grid=(nb,),
        in_specs=[pl.BlockSpec((1, MEM_LEN, D_MODEL), lambda b: (b, 0, 0)),
                  full((1, D_MODEL)), full((D_MODEL, 2 * D_MODEL))],
        out_specs=[pl.BlockSpec((1, MEM_LEN, D_MODEL), lambda b: (b, 0, 0)),
                   pl.BlockSpec((1, MEM_LEN, D_MODEL), lambda b: (b, 0, 0)),
                   pl.BlockSpec((1, D_MODEL, MEM_LEN), lambda b: (b, 0, 0)),
                   pl.BlockSpec((1, MEM_LEN, D_MODEL), lambda b: (b, 0, 0))],
        out_shape=[jax.ShapeDtypeStruct((nb, MEM_LEN, D_MODEL), F32),
                   jax.ShapeDtypeStruct((nb, MEM_LEN, D_MODEL), F32),
                   jax.ShapeDtypeStruct((nb, D_MODEL, MEM_LEN), BF16),
                   jax.ShapeDtypeStruct((nb, MEM_LEN, D_MODEL), BF16)],
        compiler_params=pltpu.CompilerParams(
            dimension_semantics=("arbitrary",), vmem_limit_bytes=VMEM_LIMIT_BYTES),
        name="mem_kv",
    )(mem, g_mem, w_kv)


def _mixer_kernel(x_ref, g_ref, win_ref, lb_ref, hn_ref, pmix_ref, ps_ref, wout_ref,
                  x1_ref, hst_ref, pst_ref,
                  st_ref, ext_ref, proj_ref, qq_ref, kk_ref, b_ref, mrg_ref):
    T = MIX_ROWS
    l = pl.program_id(1)

    @pl.when(l == 0)
    def _():
        st_ref[...] = jnp.zeros_like(st_ref)
        ext_ref[0:16, :] = jnp.zeros((16, POOL_WIDTH), F32)

    x = x_ref[0]
    hb = _rms(x, g_ref[...]).astype(BF16)
    proj_ref[...] = _dot(hb, win_ref[...])

    lb = _lower_bound(lb_ref)
    qq, fgate, kk = _gates(proj_ref[:, 0:512], proj_ref[:, 512:1024], lb)
    qq_ref[...] = qq
    kk_ref[...] = kk

    r = lax.broadcasted_iota(jnp.int32, (TRI_ROWS, TRI_ROWS), 0)
    c = lax.broadcasted_iota(jnp.int32, (TRI_ROWS, TRI_ROWS), 1)
    tri = jnp.where((c <= r) & (c >= (r & -CHUNK)), 1.0, 0.0).astype(BF16)
    logf = jnp.log(fgate)
    for blk in range(T // TRI_ROWS):
        rows = slice(blk * TRI_ROWS, (blk + 1) * TRI_ROWS)
        hi, mid, lo = _split3(logf[rows])
        b_ref[rows, :] = _dot(tri, hi) + _dot(tri, mid) + _dot(tri, lo)

    cr = lax.broadcasted_iota(jnp.int32, (CHUNK, CHUNK), 0)
    cc = lax.broadcasted_iota(jnp.int32, (CHUNK, CHUNK), 1)
    causal = cc <= cr
    mid_row = (CHUNK - 1) // 2

    for ci in range(T // CHUNK):
        r0 = ci * CHUNK
        rows = slice(r0, r0 + CHUNK)
        for h in range(HEADS):
            cols = slice(h * DK, (h + 1) * DK)
            b = b_ref[rows, cols]
            m = b_ref[r0 + mid_row:r0 + mid_row + 1, cols]
            b_end = b_ref[r0 + CHUNK - 1:r0 + CHUNK, cols]
            e1 = jnp.exp(b - m)
            e2 = jnp.exp(m - b)
            q1 = qq_ref[rows, cols] * e1
            q0 = q1 * jnp.exp(m)
            ks = kk_ref[rows, cols] * e2
            k2 = ks * jnp.exp(b_end - m)
            v = proj_ref[rows, 1024 + h * DK:1024 + (h + 1) * DK].astype(BF16)
            st = st_ref[h]
            inter = _dot_nt(q0.astype(BF16), st.astype(BF16))
            a = _dot_nt(q1.astype(BF16), ks.astype(BF16))
            a = jnp.where(causal, a, 0.0)
            o = inter + _dot(a.astype(BF16), v)
            st_ref[h] = st * jnp.exp(b_end) + _dot_tn(v, k2.astype(BF16))
            o = o * lax.rsqrt(jnp.mean(o * o, axis=-1, keepdims=True) + EPS)
            o = o * hn_ref[:, cols]
            g = proj_ref[rows, 1536 + h * DK:1536 + (h + 1) * DK]
            mrg_ref[rows, cols] = (o * _sigmoid(g)).astype(BF16)

    ext_ref[16:16 + T, :] = proj_ref[:, 2048:2560]
    pos = l * T + lax.broadcasted_iota(jnp.int32, (T, POOL_GROUP), 0)
    for gi, w in enumerate(POOL_WINDOWS):
        cols = slice(gi * POOL_GROUP, (gi + 1) * POOL_GROUP)
        s = ext_ref[:, cols]
        u = s[16:]
        shift = 1
        while shift < w:
            s = s + pltpu.roll(s, shift, axis=0)
            shift *= 2
        cnt = jnp.minimum(pos + 1, w).astype(F32)
        pooled = s[16:] / cnt - u
        yb = _dot(pooled.astype(BF16), pmix_ref[gi]) * ps_ref[:, cols]
        mrg_ref[:, HGRN_WIDTH + gi * POOL_GROUP:HGRN_WIDTH + (gi + 1) * POOL_GROUP] = yb.astype(BF16)

    @pl.when(l == pl.num_programs(1) - 1)
    def _():
        for h in range(HEADS):
            hst_ref[0, h] = st_ref[h].T
        pst_ref[0] = ext_ref[T + 1:T + 16, :]

    ext_ref[0:16, :] = ext_ref[T:T + 16, :]
    x1_ref[0] = x + _dot(mrg_ref[...], wout_ref[...])


def _mixer_prompt(x, g_mix, w_in, hgrn_lb, hgrn_norm, pool_mix, pool_scale, w_out):
    nb, L, _ = x.shape
    T = MIX_ROWS
    full = lambda shape: pl.BlockSpec(shape, lambda b, l: (0,) * len(shape))
    return pl.pallas_call(
        _mixer_kernel,
        grid=(nb, L // T),
        in_specs=[pl.BlockSpec((1, T, D_MODEL), lambda b, l: (b, l, 0)),
                  full((1, D_MODEL)), full((D_MODEL, IN_PROJ)), full((2, HGRN_WIDTH)),
                  full((1, HGRN_WIDTH)), full((4, POOL_GROUP, POOL_GROUP)),
                  full((1, POOL_WIDTH)), full((D_MODEL, D_MODEL))],
        out_specs=[pl.BlockSpec((1, T, D_MODEL), lambda b, l: (b, l, 0)),
                   pl.BlockSpec((1, HEADS, DK, DK), lambda b, l: (b, 0, 0, 0)),
                   pl.BlockSpec((1, POOL_STATE, POOL_WIDTH), lambda b, l: (b, 0, 0))],
        out_shape=[jax.ShapeDtypeStruct((nb, L, D_MODEL), F32),
                   jax.ShapeDtypeStruct((nb, HEADS, DK, DK), F32),
                   jax.ShapeDtypeStruct((nb, POOL_STATE, POOL_WIDTH), F32)],
        scratch_shapes=[pltpu.VMEM((HEADS, DK, DK), F32),
                        pltpu.VMEM((16 + T, POOL_WIDTH), F32),
                        pltpu.VMEM((T, IN_PROJ), F32),
                        pltpu.VMEM((T, HGRN_WIDTH), F32),
                        pltpu.VMEM((T, HGRN_WIDTH), F32),
                        pltpu.VMEM((T, HGRN_WIDTH), F32),
                        pltpu.VMEM((T, D_MODEL), BF16)],
        compiler_params=pltpu.CompilerParams(
            dimension_semantics=("arbitrary", "arbitrary"), vmem_limit_bytes=VMEM_LIMIT_BYTES),
        name="mixer_prompt",
    )(x, g_mix, w_in, hgrn_lb, hgrn_norm, pool_mix, pool_scale, w_out)


def _attn_kernel(x_ref, g_ref, wq_ref, kt_ref, vb_ref, wo_ref, o_ref, att_ref):
    x = x_ref[0]
    hb = _rms(x, g_ref[...]).astype(BF16)
    q = (_dot(hb, wq_ref[...]) * ATTN_SCALE).astype(BF16)
    for h in range(X_HEADS):
        cols = slice(h * X_HEAD_DIM, (h + 1) * X_HEAD_DIM)
        s = _dot(q[:, cols], kt_ref[0, cols, :])
        e = jnp.exp(s - jnp.max(s, axis=-1, keepdims=True))
        den = jnp.sum(e, axis=-1, keepdims=True)
        o = _dot(e.astype(BF16), vb_ref[0, :, cols]) / den
        att_ref[:, cols] = o.astype(BF16)
    o_ref[0] = x + _dot(att_ref[...], wo_ref[...])


def _attn_prompt(x, g_cross, w_cq, kt, vb, w_co):
    nb, L, _ = x.shape
    T = ATTN_ROWS
    full = lambda shape: pl.BlockSpec(shape, lambda b, l: (0,) * len(shape))
    return pl.pallas_call(
        _attn_kernel,
        grid=(nb, L // T),
        in_specs=[pl.BlockSpec((1, T, D_MODEL), lambda b, l: (b, l, 0)),
                  full((1, D_MODEL)), full((D_MODEL, D_MODEL)),
                  pl.BlockSpec((1, D_MODEL, MEM_LEN), lambda b, l: (b, 0, 0)),
                  pl.BlockSpec((1, MEM_LEN, D_MODEL), lambda b, l: (b, 0, 0)),
                  full((D_MODEL, D_MODEL))],
        out_specs=pl.BlockSpec((1, T, D_MODEL), lambda b, l: (b, l, 0)),
        out_shape=jax.ShapeDtypeStruct((nb, L, D_MODEL), F32),
        scratch_shapes=[pltpu.VMEM((T, D_MODEL), BF16)],
        compiler_params=pltpu.CompilerParams(
            dimension_semantics=("arbitrary", "arbitrary"), vmem_limit_bytes=VMEM_LIMIT_BYTES),
        name="attn_prompt",
    )(x, g_cross, w_cq, kt, vb, w_co)


def _ffn_body(x, g_ref, w1_ref, w2_ref, gf_ref):
    hb = _rms(x, g_ref[...]).astype(BF16)
    y = x
    for c0, c1 in FFN_COL_CHUNKS:
        a = _dot(hb, w1_ref[:, c0:c1])
        bg = _dot(hb, w1_ref[:, D_FF + c0:D_FF + c1])
        act = (a * _sigmoid(a) * bg).astype(BF16)
        y = y + _dot(act, w2_ref[c0:c1, :])
    return _rms(y, gf_ref[...])


def _split_heads(x):
    lead = x.shape[:-2]
    x = x.reshape(lead + (X_HEADS, 2, 128))
    x = jnp.swapaxes(x, -3, -2)
    return x.reshape(lead + (2 * X_HEADS, 128))


def _merge_heads(x):
    lead = x.shape[:-2]
    x = x.reshape(lead + (2, X_HEADS, 128))
    x = jnp.swapaxes(x, -3, -2)
    return x.reshape(lead + (X_HEADS, X_HEAD_DIM))


def _memory_attention(q_ref, k_ref, v_ref, o_ref):
    for j in range(ATTN_TOKENS):
        part = jnp.sum(k_ref[j] * q_ref[j][None], axis=-1, keepdims=True)
        s = part + pltpu.roll(part, X_HEADS, axis=1)
        e = jnp.exp(s - jnp.max(s, axis=0, keepdims=True))
        den = jnp.sum(e, axis=0)
        o_ref[j] = jnp.sum(e * v_ref[j], axis=0) / den


def _ffn_attn_kernel(x_ref, g_ref, w1_ref, w2_ref, gf_ref, q_ref, k_ref, v_ref, o_ref, att_ref):
    o_ref[...] = _ffn_body(x_ref[...], g_ref, w1_ref, w2_ref, gf_ref)
    _memory_attention(q_ref, k_ref, v_ref, att_ref)


def _ffn_prompt_attn_sample(x, g_ffn, w1, w2, g_final, q, cache_k, cache_v):
    n = x.shape[0]
    T = FFN_ROWS
    G = ATTN_TOKENS
    assert n // T == q.shape[0] // G
    full = lambda shape: pl.BlockSpec(shape, lambda i: (0,) * len(shape))
    kv_spec = pl.BlockSpec((G, MEM_LEN, 2 * X_HEADS, 128), lambda i: (i, 0, 0, 0))
    q_spec = pl.BlockSpec((G, 2 * X_HEADS, 128), lambda i: (i, 0, 0))
    return pl.pallas_call(
        _ffn_attn_kernel,
        grid=(n // T,),
        in_specs=[pl.BlockSpec((T, D_MODEL), lambda i: (i, 0)),
                  full((1, D_MODEL)), full((D_MODEL, 2 * D_FF)), full((D_FF, D_MODEL)),
                  full((1, D_MODEL)), q_spec, kv_spec, kv_spec],
        out_specs=[pl.BlockSpec((T, D_MODEL), lambda i: (i, 0)), q_spec],
        out_shape=[jax.ShapeDtypeStruct((n, D_MODEL), F32),
                   jax.ShapeDtypeStruct((q.shape[0], 2 * X_HEADS, 128), F32)],
        compiler_params=pltpu.CompilerParams(
            dimension_semantics=("arbitrary",), vmem_limit_bytes=VMEM_LIMIT_BYTES),
        name="ffn_prompt_attn_sample",
    )(x, g_ffn, w1, w2, g_final, q, cache_k, cache_v)


def _sample_gates_kernel(x_ref, g_ref, win_ref, lb_ref, o_ref):
    hb = _rms(x_ref[...], g_ref[...]).astype(BF16)
    proj = _dot(hb, win_ref[...])
    lb = _lower_bound(lb_ref)
    qq, fgate, kk = _gates(proj[:, 0:512], proj[:, 512:1024], lb)
    o_ref[:, 0:512] = qq
    o_ref[:, 512:1024] = fgate
    o_ref[:, 1024:1536] = kk
    o_ref[:, 1536:2048] = proj[:, 1024:1536]
    o_ref[:, 2048:2560] = _sigmoid(proj[:, 1536:2048])
    o_ref[:, 2560:3072] = proj[:, 2048:2560]


def _sample_gates(x, g_mix, w_in, hgrn_lb):
    n = x.shape[0]
    full = lambda shape: pl.BlockSpec(shape, lambda i: (0,) * len(shape))
    return pl.pallas_call(
        _sample_gates_kernel,
        grid=(1,),
        in_specs=[full((n, D_MODEL)), full((1, D_MODEL)), full((D_MODEL, IN_PROJ)),
                  full((2, HGRN_WIDTH))],
        out_specs=full((n, 6 * 512)),
        out_shape=jax.ShapeDtypeStruct((n, 6 * 512), F32),
        compiler_params=pltpu.CompilerParams(
            dimension_semantics=("arbitrary",), vmem_limit_bytes=VMEM_LIMIT_BYTES),
        name="sample_gates",
    )(x, g_mix, w_in, hgrn_lb)


def _sample_state_kernel(q_ref, f_ref, k_ref, v_ref, s_ref, so_ref, o_ref):
    G = STATE_TOKENS
    pad = jnp.zeros((DK - G, DK), F32)
    for h in range(HEADS):
        cols = slice(h * DK, (h + 1) * DK)
        qt = jnp.concatenate([q_ref[:, cols], pad], axis=0).T
        ft = jnp.concatenate([f_ref[:, cols], pad], axis=0).T
        kt = jnp.concatenate([k_ref[:, cols], pad], axis=0).T
        for j in range(G):
            s_new = ft[:, j:j + 1] * s_ref[j, h] + kt[:, j:j + 1] * v_ref[j:j + 1, cols]
            so_ref[j, h] = s_new
            o_ref[j:j + 1, cols] = jnp.sum(qt[:, j:j + 1] * s_new, axis=0, keepdims=True)


def _sample_state(gates, state):
    n = gates.shape[0]
    G = STATE_TOKENS
    col = lambda c: pl.BlockSpec((G, 512), lambda i: (i, c))
    return pl.pallas_call(
        _sample_state_kernel,
        grid=(n // G,),
        in_specs=[col(0), col(1), col(2), col(3),
                  pl.BlockSpec((G, HEADS, DK, DK), lambda i: (i, 0, 0, 0))],
        out_specs=[pl.BlockSpec((G, HEADS, DK, DK), lambda i: (i, 0, 0, 0)),
                   pl.BlockSpec((G, HGRN_WIDTH), lambda i: (i, 0))],
        out_shape=[jax.ShapeDtypeStruct((n, HEADS, DK, DK), F32),
                   jax.ShapeDtypeStruct((n, HGRN_WIDTH), F32)],
        compiler_params=pltpu.CompilerParams(
            dimension_semantics=("arbitrary",), vmem_limit_bytes=VMEM_LIMIT_BYTES),
        name="sample_state",
    )(gates, gates, gates, gates, state)


def _sample_mix_kernel(x_ref, gates_ref, o_ref, past_ref, hn_ref, pmix_ref, ps_ref, wout_ref,
                       gc_ref, wq_ref, x1_ref, pool_ref, q_ref, mrg_ref):
    for h in range(HEADS):
        cols = slice(h * DK, (h + 1) * DK)
        o = o_ref[:, cols]
        o = o * lax.rsqrt(jnp.mean(o * o, axis=-1, keepdims=True) + EPS) * hn_ref[:, cols]
        mrg_ref[:, cols] = (o * gates_ref[:, 2048 + h * DK:2048 + (h + 1) * DK]).astype(BF16)
    for gi, w in enumerate(POOL_WINDOWS):
        cols = slice(gi * POOL_GROUP, (gi + 1) * POOL_GROUP)
        u = gates_ref[:, 2560 + gi * POOL_GROUP:2560 + (gi + 1) * POOL_GROUP]
        acc = u
        for j in range(1, w):
            r = POOL_STATE - j
            acc = acc + past_ref[:, r * POOL_WIDTH + gi * POOL_GROUP:r * POOL_WIDTH + (gi + 1) * POOL_GROUP]
        pooled = acc / float(w) - u
        yb = _dot(pooled.astype(BF16), pmix_ref[gi]) * ps_ref[:, cols]
        mrg_ref[:, HGRN_WIDTH + gi * POOL_GROUP:HGRN_WIDTH + (gi + 1) * POOL_GROUP] = yb.astype(BF16)
    pool_ref[:, 0:(POOL_STATE - 1) * POOL_WIDTH] = past_ref[:, POOL_WIDTH:POOL_STATE * POOL_WIDTH]
    pool_ref[:, (POOL_STATE - 1) * POOL_WIDTH:] = gates_ref[:, 2560:3072]
    x1 = x_ref[...] + _dot(mrg_ref[...], wout_ref[...])
    x1_ref[...] = x1
    hb = _rms(x1, gc_ref[...]).astype(BF16)
    q_ref[...] = _dot(hb, wq_ref[...]) * ATTN_SCALE


def _sample_mix(x, gates, o, past, hgrn_norm, pool_mix, pool_scale, w_out, g_cross, w_cq):
    n = x.shape[0]
    full = lambda shape: pl.BlockSpec(shape, lambda i: (0,) * len(shape))
    return pl.pallas_call(
        _sample_mix_kernel,
        grid=(1,),
        in_specs=[full((n, D_MODEL)), full((n, 6 * 512)), full((n, HGRN_WIDTH)),
                  full((n, POOL_STATE * POOL_WIDTH)), full((1, HGRN_WIDTH)),
                  full((4, POOL_GROUP, POOL_GROUP)), full((1, POOL_WIDTH)),
                  full((D_MODEL, D_MODEL)), full((1, D_MODEL)), full((D_MODEL, D_MODEL))],
        out_specs=[full((n, D_MODEL)), full((n, POOL_STATE * POOL_WIDTH)), full((n, D_MODEL))],
        out_shape=[jax.ShapeDtypeStruct((n, D_MODEL), F32),
                   jax.ShapeDtypeStruct((n, POOL_STATE * POOL_WIDTH), F32),
                   jax.ShapeDtypeStruct((n, D_MODEL), F32)],
        scratch_shapes=[pltpu.VMEM((n, D_MODEL), BF16)],
        compiler_params=pltpu.CompilerParams(
            dimension_semantics=("arbitrary",), vmem_limit_bytes=VMEM_LIMIT_BYTES),
        name="sample_mix",
    )(x, gates, o, past, hgrn_norm, pool_mix, pool_scale, w_out, g_cross, w_cq)


def _sample_ffn_kernel(x_ref, att_ref, wo_ref, g_ref, w1_ref, w2_ref, gf_ref, o_ref):
    x2 = x_ref[...] + _dot(att_ref[...].astype(BF16), wo_ref[...])
    o_ref[...] = _ffn_body(x2, g_ref, w1_ref, w2_ref, gf_ref)


def _sample_ffn(x1, att, w_co, g_ffn, w1, w2, g_final):
    n = x1.shape[0]
    full = lambda shape: pl.BlockSpec(shape, lambda i: (0,) * len(shape))
    return pl.pallas_call(
        _sample_ffn_kernel,
        grid=(1,),
        in_specs=[full((n, D_MODEL)), full((n, D_MODEL)), full((D_MODEL, D_MODEL)),
                  full((1, D_MODEL)), full((D_MODEL, 2 * D_FF)), full((D_FF, D_MODEL)),
                  full((1, D_MODEL))],
        out_specs=full((n, D_MODEL)),
        out_shape=jax.ShapeDtypeStruct((n, D_MODEL), F32),
        compiler_params=pltpu.CompilerParams(
            dimension_semantics=("arbitrary",), vmem_limit_bytes=VMEM_LIMIT_BYTES),
        name="sample_ffn",
    )(x1, att, w_co, g_ffn, w1, w2, g_final)


def kernel(x_prompt, x_sample, mem_prompt, state_hgrn, state_pool, cache_mem_k, cache_mem_v,
           g_mix, w_in, hgrn_lb, hgrn_norm, pool_mix, pool_scale, w_out, g_mem, w_mem_kv,
           g_cross, w_cq, w_co, g_ffn, w_ffn_in, w_ffn_out, g_final):
    nb, L, _ = x_prompt.shape
    ns = x_sample.shape[0]

    w_in_b = w_in[0].astype(BF16)
    w_out_b = w_out[0].astype(BF16)
    w_kv_b = w_mem_kv[0].astype(BF16)
    w_cq_b = w_cq[0].astype(BF16)
    w_co_b = w_co[0].astype(BF16)
    w1_b = w_ffn_in[0].astype(BF16)
    w2_b = w_ffn_out[0].astype(BF16)
    pmix_b = pool_mix[0].astype(BF16)
    g_final2 = g_final.reshape(1, D_MODEL)

    xs = x_sample.reshape(ns, D_MODEL)
    gates = _sample_gates(xs, g_mix, w_in_b, hgrn_lb)
    hgrn_s, o_s = _sample_state(gates, state_hgrn[0])
    x1s, pool_s, qs = _sample_mix(xs, gates, o_s, state_pool[0].reshape(ns, POOL_STATE * POOL_WIDTH),
                                  hgrn_norm, pmix_b, pool_scale, w_out_b, g_cross, w_cq_b)

    mem_k, mem_v, kt, vb = _mem_kv(mem_prompt, g_mem, w_kv_b)
    x1, hgrn_p, pool_p = _mixer_prompt(x_prompt, g_mix, w_in_b, hgrn_lb, hgrn_norm, pmix_b,
                                       pool_scale, w_out_b)
    x2 = _attn_prompt(x1, g_cross, w_cq_b, kt, vb, w_co_b)
    y_prompt, att = _ffn_prompt_attn_sample(
        x2.reshape(nb * L, D_MODEL), g_ffn, w1_b, w2_b, g_final2,
        _split_heads(qs.reshape(ns, X_HEADS, X_HEAD_DIM)),
        _split_heads(cache_mem_k[0]), _split_heads(cache_mem_v[0]))

    y_sample = _sample_ffn(x1s, _merge_heads(att).reshape(ns, D_MODEL), w_co_b, g_ffn, w1_b, w2_b,
                           g_final2)

    return (y_prompt.reshape(nb, L, D_MODEL),
            y_sample.reshape(ns, 1, D_MODEL),
            hgrn_p[None],
            pool_p[None],
            mem_k.reshape(1, nb, MEM_LEN, X_HEADS, X_HEAD_DIM),
            mem_v.reshape(1, nb, MEM_LEN, X_HEADS, X_HEAD_DIM),
            hgrn_s[None],
            pool_s.reshape(1, ns, POOL_STATE, POOL_WIDTH))
```

```python
import functools

import jax
import jax.numpy as jnp
from jax import lax
from jax.experimental import pallas as pl
from jax.experimental.pallas import tpu as pltpu

F32 = jnp.float32
BF16 = jnp.bfloat16

D_MODEL = 1024
HGRN_WIDTH = 512
HEADS = 4
DK = 128
CHUNK = 64
POOL_WIDTH = 512
POOL_WINDOWS = (2, 4, 8, 16)
POOL_GROUP = 128
POOL_STATE = 15
IN_PROJ = 4 * HGRN_WIDTH + POOL_WIDTH
MEM_LEN = 256
X_HEADS = 4
X_HEAD_DIM = 256
D_FF = 2816
EPS = 1e-6
ATTN_SCALE = X_HEAD_DIM ** -0.5

VMEM_LIMIT_BYTES = 56 * 1024 * 1024

MIX_ROWS = 512
MIX_BLOCKS_PER_SEQ = 4
IN_PROJ_PIECE = 256
TRI_ROWS = 256
ATTN_ROWS = 512
FFN_ROWS = 512
STATE_TOKENS = 8
ATTN_TOKENS = 4
FFN_COL_CHUNKS = ((0, 1024), (1024, 2048), (2048, D_FF))


def _dot(a, b):
    return jnp.dot(a, b, preferred_element_type=F32)


def _dot_nt(a, b):
    return lax.dot_general(a, b, (((1,), (1,)), ((), ())), preferred_element_type=F32)


def _dot_tn(a, b):
    return lax.dot_general(a, b, (((0,), (0,)), ((), ())), preferred_element_type=F32)


def _rms(x, g):
    ms = jnp.mean(x * x, axis=-1, keepdims=True)
    return x * lax.rsqrt(ms + EPS) * g


def _sigmoid(x):
    return 1.0 / (1.0 + jnp.exp(-x))


def _lower_bound(lb_ref):
    t = lb_ref[...]
    m = jnp.max(t, axis=0, keepdims=True)
    e = jnp.exp(t - m)
    return e[0:1, :] / jnp.sum(e, axis=0, keepdims=True)


def _gates(proj_q, proj_f, lb):
    qq = proj_q * _sigmoid(proj_q)
    sig = _sigmoid(proj_f)
    fgate = lb + (1.0 - lb) * sig
    kk = (1.0 - lb) * (1.0 - sig)
    return qq, fgate, kk


def _split3(x):
    hi = x.astype(BF16)
    r1 = x - hi.astype(F32)
    mid = r1.astype(BF16)
    lo = (r1 - mid.astype(F32)).astype(BF16)
    return hi, mid, lo


def _memkv_kernel(mem_ref, g_ref, w_ref, k_ref, v_ref, kt_ref, vb_ref):
    h = _rms(mem_ref[0], g_ref[...]).astype(BF16)
    kv = _dot(h, w_ref[...])
    k = kv[:, :D_MODEL]
    v = kv[:, D_MODEL:]
    k_ref[0] = k
    v_ref[0] = v
    kt_ref[0] = k.T.astype(BF16)
    vb_ref[0] = v.astype(BF16)


def _mem_kv(mem, g_mem, w_kv):
    nb = mem.shape[0]
    full = lambda shape: pl.BlockSpec(shape, lambda b: (0,) * len(shape))
    return pl.pallas_call(
        _memkv_kernel,
        grid=(nb,),
        in_specs=[pl.BlockSpec((1, MEM_LEN, D_MODEL), lambda b: (b, 0, 0)),
                  full((1, D_MODEL)), full((D_MODEL, 2 * D_MODEL))],
        out_specs=[pl.BlockSpec((1, MEM_LEN, D_MODEL), lambda b: (b, 0, 0)),
                   pl.BlockSpec((1, MEM_LEN, D_MODEL), lambda b: (b, 0, 0)),
                   pl.BlockSpec((1, D_MODEL, MEM_LEN), lambda b: (b, 0, 0)),
                   pl.BlockSpec((1, MEM_LEN, D_MODEL), lambda b: (b, 0, 0))],
        out_shape=[jax.ShapeDtypeStruct((nb, MEM_LEN, D_MODEL), F32),
                   jax.ShapeDtypeStruct((nb, MEM_LEN, D_MODEL), F32),
                   jax.ShapeDtypeStruct((nb, D_MODEL, MEM_LEN), BF16),
                   jax.ShapeDtypeStruct((nb, MEM_LEN, D_MODEL), BF16)],
        compiler_params=pltpu.CompilerParams(
            dimension_semantics=("arbitrary",), vmem_limit_bytes=VMEM_LIMIT_BYTES),
        name="mem_kv",
    )(mem, g_mem, w_kv)


def _mix_block(n, x, proj_ref, out_ref, fillers, lb_ref, hn_ref, pmix_ref, ps_ref, wout_ref,
               st_ref, ext_ref, qq_ref, kk_ref, b_ref, mrg_ref):
    T = MIX_ROWS
    fillers = list(fillers)
    assert len(fillers) == 3 + T // CHUNK
    l = n % MIX_BLOCKS_PER_SEQ
    first = l == 0
    for h in range(HEADS):
        st_ref[h] = jnp.where(first, 0.0, st_ref[h])
    ext_ref[0:16, :] = jnp.where(first, 0.0, ext_ref[0:16, :])

    fillers.pop(0)()
    fillers.pop(0)()
    fillers.pop(0)()
    lb = _lower_bound(lb_ref)
    qq, fgate, kk = _gates(proj_ref[:, 0:512], proj_ref[:, 512:1024], lb)
    qq_ref[...] = qq
    kk_ref[...] = kk

    r = lax.broadcasted_iota(jnp.int32, (TRI_ROWS, TRI_ROWS), 0)
    c = lax.broadcasted_iota(jnp.int32, (TRI_ROWS, TRI_ROWS), 1)
    tri = jnp.where((c <= r) & (c >= (r & -CHUNK)), 1.0, 0.0).astype(BF16)
    logf = jnp.log(fgate)
    for blk in range(T // TRI_ROWS):
        rows = slice(blk * TRI_ROWS, (blk + 1) * TRI_ROWS)
        hi, mid, lo = _split3(logf[rows])
        b_ref[rows, :] = _dot(tri, hi) + _dot(tri, mid) + _dot(tri, lo)

    cr = lax.broadcasted_iota(jnp.int32, (CHUNK, CHUNK), 0)
    cc = lax.broadcasted_iota(jnp.int32, (CHUNK, CHUNK), 1)
    causal = cc <= cr
    mid_row = (CHUNK - 1) // 2

    for ci in range(T // CHUNK):
        r0 = ci * CHUNK
        rows = slice(r0, r0 + CHUNK)
        scores, inters, vals = [], [], []
        for h in range(HEADS):
            cols = slice(h * DK, (h + 1) * DK)
            b = b_ref[rows, cols]
            m = b_ref[r0 + mid_row:r0 + mid_row + 1, cols]
            b_end = b_ref[r0 + CHUNK - 1:r0 + CHUNK, cols]
            e1 = jnp.exp(b - m)
            e2 = jnp.exp(m - b)
            q1 = qq_ref[rows, cols] * e1
            q0 = q1 * jnp.exp(m)
            ks = kk_ref[rows, cols] * e2
            k2 = ks * jnp.exp(b_end - m)
            v = proj_ref[rows, 1024 + h * DK:1024 + (h + 1) * DK].astype(BF16)
            st = st_ref[h]
            scores.append(_dot_nt(q1.astype(BF16), ks.astype(BF16)))
            inters.append(_dot_nt(q0.astype(BF16), st.astype(BF16)))
            st_ref[h] = st * jnp.exp(b_end) + _dot_tn(v, k2.astype(BF16))
            vals.append(v)
        fillers.pop(0)()
        for h in range(HEADS):
            cols = slice(h * DK, (h + 1) * DK)
            a = jnp.where(causal, scores[h], 0.0)
            o = inters[h] + _dot(a.astype(BF16), vals[h])
            o = o * lax.rsqrt(jnp.mean(o * o, axis=-1, keepdims=True) + EPS)
            o = o * hn_ref[:, cols]
            g = proj_ref[rows, 1536 + h * DK:1536 + (h + 1) * DK]
            mrg_ref[rows, cols] = (o * _sigmoid(g)).astype(BF16)

    ext_ref[16:16 + T, :] = proj_ref[:, 2048:2560]
    pos = l * T + lax.broadcasted_iota(jnp.int32, (T, POOL_GROUP), 0)
    for gi, w in enumerate(POOL_WINDOWS):
        cols = slice(gi * POOL_GROUP, (gi + 1) * POOL_GROUP)
        s = ext_ref[:, cols]
        u = s[16:]
        shift = 1
        while shift < w:
            s = s + pltpu.roll(s, shift, axis=0)
            shift *= 2
        cnt = jnp.minimum(pos + 1, w).astype(F32)
        pooled = s[16:] / cnt - u
        yb = _dot(pooled.astype(BF16), pmix_ref[gi]) * ps_ref[:, cols]
        mrg_ref[:, HGRN_WIDTH + gi * POOL_GROUP:HGRN_WIDTH + (gi + 1) * POOL_GROUP] = yb.astype(BF16)

    ext_ref[0:16, :] = ext_ref[T:T + 16, :]
    out_ref[...] = x + _dot(mrg_ref[...], wout_ref[...])


def _mixer_kernel(xa_ref, xn_ref, g_ref, win_ref, lb_ref, hn_ref, pmix_ref, ps_ref, wout_ref,
                  x1_ref, hst_ref, pst_ref,
                  p0_ref, p1_ref, hb_ref, st_ref, ext_ref, qq_ref, kk_ref, b_ref, mrg_ref):
    T = MIX_ROWS
    g = pl.program_id(0)
    rest = (lb_ref, hn_ref, pmix_ref, ps_ref, wout_ref, st_ref, ext_ref, qq_ref, kk_ref, b_ref, mrg_ref)

    def in_proj_pieces(x_ref, rows, p_ref):
        def prep():
            hb_ref[...] = _rms(x_ref[rows, :], g_ref[...]).astype(BF16)

        def piece(k):
            cols = slice(k * IN_PROJ_PIECE, (k + 1) * IN_PROJ_PIECE)
            p_ref[:, cols] = _dot(hb_ref[...], win_ref[:, cols])

        return [prep] + [functools.partial(piece, k) for k in range(IN_PROJ // IN_PROJ_PIECE)]

    @pl.when(g == 0)
    def _():
        st_ref[...] = jnp.zeros_like(st_ref)
        ext_ref[0:16, :] = jnp.zeros((16, POOL_WIDTH), F32)
        for f in in_proj_pieces(xa_ref, slice(0, T), p0_ref):
            f()

    _mix_block(2 * g, xa_ref[0:T, :], p0_ref, x1_ref.at[0:T, :],
               in_proj_pieces(xa_ref, slice(T, 2 * T), p1_ref), *rest)
    _mix_block(2 * g + 1, xa_ref[T:2 * T, :], p1_ref, x1_ref.at[T:2 * T, :],
               in_proj_pieces(xn_ref, slice(0, T), p0_ref), *rest)

    @pl.when(g % (MIX_BLOCKS_PER_SEQ // 2) == MIX_BLOCKS_PER_SEQ // 2 - 1)
    def _():
        for h in range(HEADS):
            hst_ref[0, h] = st_ref[h].T
        pst_ref[0] = ext_ref[T + 1:T + 16, :]


def _mixer_prompt(x, g_mix, w_in, hgrn_lb, hgrn_norm, pool_mix, pool_scale, w_out):
    nb, L, _ = x.shape
    T = MIX_ROWS
    assert L // T == MIX_BLOCKS_PER_SEQ and MIX_BLOCKS_PER_SEQ % 2 == 0
    n_blocks = nb * MIX_BLOCKS_PER_SEQ
    steps_per_seq = MIX_BLOCKS_PER_SEQ // 2
    x2d = x.reshape(nb * L, D_MODEL)
    full = lambda shape: pl.BlockSpec(shape, lambda g: (0,) * len(shape))
    x1, hst, pst = pl.pallas_call(
        _mixer_kernel,
        grid=(n_blocks // 2,),
        in_specs=[pl.BlockSpec((2 * T, D_MODEL), lambda g: (g, 0)),
                  pl.BlockSpec((T, D_MODEL), lambda g: (jnp.minimum(2 * g + 2, n_blocks - 1), 0)),
                  full((1, D_MODEL)), full((D_MODEL, IN_PROJ)), full((2, HGRN_WIDTH)),
                  full((1, HGRN_WIDTH)), full((4, POOL_GROUP, POOL_GROUP)),
                  full((1, POOL_WIDTH)), full((D_MODEL, D_MODEL))],
        out_specs=[pl.BlockSpec((2 * T, D_MODEL), lambda g: (g, 0)),
                   pl.BlockSpec((1, HEADS, DK, DK), lambda g: (g // steps_per_seq, 0, 0, 0)),
                   pl.BlockSpec((1, POOL_STATE, POOL_WIDTH), lambda g: (g // steps_per_seq, 0, 0))],
        out_shape=[jax.ShapeDtypeStruct((nb * L, D_MODEL), F32),
                   jax.ShapeDtypeStruct((nb, HEADS, DK, DK), F32),
                   jax.ShapeDtypeStruct((nb, POOL_STATE, POOL_WIDTH), F32)],
        scratch_shapes=[pltpu.VMEM((T, IN_PROJ), F32),
                        pltpu.VMEM((T, IN_PROJ), F32),
                        pltpu.VMEM((T, D_MODEL), BF16),
                        pltpu.VMEM((HEADS, DK, DK), F32),
                        pltpu.VMEM((16 + T, POOL_WIDTH), F32),
                        pltpu.VMEM((T, HGRN_WIDTH), F32),
                        pltpu.VMEM((T, HGRN_WIDTH), F32),
                        pltpu.VMEM((T, HGRN_WIDTH), F32),
                        pltpu.VMEM((T, D_MODEL), BF16)],
        compiler_params=pltpu.CompilerParams(
            dimension_semantics=("arbitrary",), vmem_limit_bytes=VMEM_LIMIT_BYTES),
        name="mixer_prompt",
    )(x2d, x2d, g_mix, w_in, hgrn_lb, hgrn_norm, pool_mix, pool_scale, w_out)
    return x1.reshape(nb, L, D_MODEL), hst, pst


def _attn_kernel(x_ref, g_ref, wq_ref, kt_ref, vb_ref, wo_ref, o_ref, att_ref):
    x = x_ref[0]
    hb = _rms(x, g_ref[...]).astype(BF16)
    q = (_dot(hb, wq_ref[...]) * ATTN_SCALE).astype(BF16)
    for h in range(X_HEADS):
        cols = slice(h * X_HEAD_DIM, (h + 1) * X_HEAD_DIM)
        s = _dot(q[:, cols], kt_ref[0, cols, :])
        e = jnp.exp(s - jnp.max(s, axis=-1, keepdims=True))
        den = jnp.sum(e, axis=-1, keepdims=True)
        o = _dot(e.astype(BF16), vb_ref[0, :, cols]) / den
        att_ref[:, cols] = o.astype(BF16)
    o_ref[0] = x + _dot(att_ref[...], wo_ref[...])


def _attn_prompt(x, g_cross, w_cq, kt, vb, w_co):
    nb, L, _ = x.shape
    T = ATTN_ROWS
    full = lambda shape: pl.BlockSpec(shape, lambda b, l: (0,) * len(shape))
    return pl.pallas_call(
        _attn_kernel,
        grid=(nb, L // T),
        in_specs=[pl.BlockSpec((1, T, D_MODEL), lambda b, l: (b, l, 0)),
                  full((1, D_MODEL)), full((D_MODEL, D_MODEL)),
                  pl.BlockSpec((1, D_MODEL, MEM_LEN), lambda b, l: (b, 0, 0)),
                  pl.BlockSpec((1, MEM_LEN, D_MODEL), lambda b, l: (b, 0, 0)),
                  full((D_MODEL, D_MODEL))],
        out_specs=pl.BlockSpec((1, T, D_MODEL), lambda b, l: (b, l, 0)),
        out_shape=jax.ShapeDtypeStruct((nb, L, D_MODEL), F32),
        scratch_shapes=[pltpu.VMEM((T, D_MODEL), BF16)],
        compiler_params=pltpu.CompilerParams(
            dimension_semantics=("arbitrary", "arbitrary"), vmem_limit_bytes=VMEM_LIMIT_BYTES),
        name="attn_prompt",
    )(x, g_cross, w_cq, kt, vb, w_co)


def _ffn_body(x, g_ref, w1_ref, w2_ref, gf_ref):
    hb = _rms(x, g_ref[...]).astype(BF16)
    y = x
    for c0, c1 in FFN_COL_CHUNKS:
        a = _dot(hb, w1_ref[:, c0:c1])
        bg = _dot(hb, w1_ref[:, D_FF + c0:D_FF + c1])
        act = (a * _sigmoid(a) * bg).astype(BF16)
        y = y + _dot(act, w2_ref[c0:c1, :])
    return _rms(y, gf_ref[...])


def _split_heads(x):
    lead = x.shape[:-2]
    x = x.reshape(lead + (X_HEADS, 2, 128))
    x = jnp.swapaxes(x, -3, -2)
    return x.reshape(lead + (2 * X_HEADS, 128))


def _merge_heads(x):
    lead = x.shape[:-2]
    x = x.reshape(lead + (2, X_HEADS, 128))
    x = jnp.swapaxes(x, -3, -2)
    return x.reshape(lead + (X_HEADS, X_HEAD_DIM))


def _memory_attention(q_ref, k_ref, v_ref, o_ref):
    for j in range(ATTN_TOKENS):
        part = jnp.sum(k_ref[j] * q_ref[j][None], axis=-1, keepdims=True)
        s = part + pltpu.roll(part, X_HEADS, axis=1)
        e = jnp.exp(s - jnp.max(s, axis=0, keepdims=True))
        den = jnp.sum(e, axis=0)
        o_ref[j] = jnp.sum(e * v_ref[j], axis=0) / den


def _ffn_attn_kernel(x_ref, g_ref, w1_ref, w2_ref, gf_ref, q_ref, k_ref, v_ref, o_ref, att_ref):
    o_ref[...] = _ffn_body(x_ref[...], g_ref, w1_ref, w2_ref, gf_ref)
    _memory_attention(q_ref, k_ref, v_ref, att_ref)


def _ffn_prompt_attn_sample(x, g_ffn, w1, w2, g_final, q, cache_k, cache_v):
    n = x.shape[0]
    T = FFN_ROWS
    G = ATTN_TOKENS
    assert n // T == q.shape[0] // G
    full = lambda shape: pl.BlockSpec(shape, lambda i: (0,) * len(shape))
    kv_spec = pl.BlockSpec((G, MEM_LEN, 2 * X_HEADS, 128), lambda i: (i, 0, 0, 0))
    q_spec = pl.BlockSpec((G, 2 * X_HEADS, 128), lambda i: (i, 0, 0))
    return pl.pallas_call(
        _ffn_attn_kernel,
        grid=(n // T,),
        in_specs=[pl.BlockSpec((T, D_MODEL), lambda i: (i, 0)),
                  full((1, D_MODEL)), full((D_MODEL, 2 * D_FF)), full((D_FF, D_MODEL)),
                  full((1, D_MODEL)), q_spec, kv_spec, kv_spec],
        out_specs=[pl.BlockSpec((T, D_MODEL), lambda i: (i, 0)), q_spec],
        out_shape=[jax.ShapeDtypeStruct((n, D_MODEL), F32),
                   jax.ShapeDtypeStruct((q.shape[0], 2 * X_HEADS, 128), F32)],
        compiler_params=pltpu.CompilerParams(
            dimension_semantics=("arbitrary",), vmem_limit_bytes=VMEM_LIMIT_BYTES),
        name="ffn_prompt_attn_sample",
    )(x, g_ffn, w1, w2, g_final, q, cache_k, cache_v)


def _sample_gates_kernel(x_ref, g_ref, win_ref, lb_ref, o_ref):
    hb = _rms(x_ref[...], g_ref[...]).astype(BF16)
    proj = _dot(hb, win_ref[...])
    lb = _lower_bound(lb_ref)
    qq, fgate, kk = _gates(proj[:, 0:512], proj[:, 512:1024], lb)
    o_ref[:, 0:512] = qq
    o_ref[:, 512:1024] = fgate
    o_ref[:, 1024:1536] = kk
    o_ref[:, 1536:2048] = proj[:, 1024:1536]
    o_ref[:, 2048:2560] = _sigmoid(proj[:, 1536:2048])
    o_ref[:, 2560:3072] = proj[:, 2048:2560]


def _sample_gates(x, g_mix, w_in, hgrn_lb):
    n = x.shape[0]
    full = lambda shape: pl.BlockSpec(shape, lambda i: (0,) * len(shape))
    return pl.pallas_call(
        _sample_gates_kernel,
        grid=(1,),
        in_specs=[full((n, D_MODEL)), full((1, D_MODEL)), full((D_MODEL, IN_PROJ)),
                  full((2, HGRN_WIDTH))],
        out_specs=full((n, 6 * 512)),
        out_shape=jax.ShapeDtypeStruct((n, 6 * 512), F32),
        compiler_params=pltpu.CompilerParams(
            dimension_semantics=("arbitrary",), vmem_limit_bytes=VMEM_LIMIT_BYTES),
        name="sample_gates",
    )(x, g_mix, w_in, hgrn_lb)


def _sample_state_kernel(q_ref, f_ref, k_ref, v_ref, s_ref, so_ref, o_ref):
    G = STATE_TOKENS
    pad = jnp.zeros((DK - G, DK), F32)
    for h in range(HEADS):
        cols = slice(h * DK, (h + 1) * DK)
        qt = jnp.concatenate([q_ref[:, cols], pad], axis=0).T
        ft = jnp.concatenate([f_ref[:, cols], pad], axis=0).T
        kt = jnp.concatenate([k_ref[:, cols], pad], axis=0).T
        for j in range(G):
            s_new = ft[:, j:j + 1] * s_ref[j, h] + kt[:, j:j + 1] * v_ref[j:j + 1, cols]
            so_ref[j, h] = s_new
            o_ref[j:j + 1, cols] = jnp.sum(qt[:, j:j + 1] * s_new, axis=0, keepdims=True)


def _sample_state(gates, state):
    n = gates.shape[0]
    G = STATE_TOKENS
    col = lambda c: pl.BlockSpec((G, 512), lambda i: (i, c))
    return pl.pallas_call(
        _sample_state_kernel,
        grid=(n // G,),
        in_specs=[col(0), col(1), col(2), col(3),
                  pl.BlockSpec((G, HEADS, DK, DK), lambda i: (i, 0, 0, 0))],
        out_specs=[pl.BlockSpec((G, HEADS, DK, DK), lambda i: (i, 0, 0, 0)),
                   pl.BlockSpec((G, HGRN_WIDTH), lambda i: (i, 0))],
        out_shape=[jax.ShapeDtypeStruct((n, HEADS, DK, DK), F32),
                   jax.ShapeDtypeStruct((n, HGRN_WIDTH), F32)],
        compiler_params=pltpu.CompilerParams(
            dimension_semantics=("arbitrary",), vmem_limit_bytes=VMEM_LIMIT_BYTES),
        name="sample_state",
    )(gates, gates, gates, gates, state)


def _sample_mix_kernel(x_ref, gates_ref, o_ref, past_ref, hn_ref, pmix_ref, ps_ref, wout_ref,
                       gc_ref, wq_ref, x1_ref, pool_ref, q_ref, mrg_ref):
    for h in range(HEADS):
        cols = slice(h * DK, (h + 1) * DK)
        o = o_ref[:, cols]
        o = o * lax.rsqrt(jnp.mean(o * o, axis=-1, keepdims=True) + EPS) * hn_ref[:, cols]
        mrg_ref[:, cols] = (o * gates_ref[:, 2048 + h * DK:2048 + (h + 1) * DK]).astype(BF16)
    for gi, w in enumerate(POOL_WINDOWS):
        cols = slice(gi * POOL_GROUP, (gi + 1) * POOL_GROUP)
        u = gates_ref[:, 2560 + gi * POOL_GROUP:2560 + (gi + 1) * POOL_GROUP]
        acc = u
        for j in range(1, w):
            r = POOL_STATE - j
            acc = acc + past_ref[:, r * POOL_WIDTH + gi * POOL_GROUP:r * POOL_WIDTH + (gi + 1) * POOL_GROUP]
        pooled = acc / float(w) - u
        yb = _dot(pooled.astype(BF16), pmix_ref[gi]) * ps_ref[:, cols]
        mrg_ref[:, HGRN_WIDTH + gi * POOL_GROUP:HGRN_WIDTH + (gi + 1) * POOL_GROUP] = yb.astype(BF16)
    pool_ref[:, 0:(POOL_STATE - 1) * POOL_WIDTH] = past_ref[:, POOL_WIDTH:POOL_STATE * POOL_WIDTH]
    pool_ref[:, (POOL_STATE - 1) * POOL_WIDTH:] = gates_ref[:, 2560:3072]
    x1 = x_ref[...] + _dot(mrg_ref[...], wout_ref[...])
    x1_ref[...] = x1
    hb = _rms(x1, gc_ref[...]).astype(BF16)
    q_ref[...] = _dot(hb, wq_ref[...]) * ATTN_SCALE


def _sample_mix(x, gates, o, past, hgrn_norm, pool_mix, pool_scale, w_out, g_cross, w_cq):
    n = x.shape[0]
    full = lambda shape: pl.BlockSpec(shape, lambda i: (0,) * len(shape))
    return pl.pallas_call(
        _sample_mix_kernel,
        grid=(1,),
        in_specs=[full((n, D_MODEL)), full((n, 6 * 512)), full((n, HGRN_WIDTH)),
                  full((n, POOL_STATE * POOL_WIDTH)), full((1, HGRN_WIDTH)),
                  full((4, POOL_GROUP, POOL_GROUP)), full((1, POOL_WIDTH)),
                  full((D_MODEL, D_MODEL)), full((1, D_MODEL)), full((D_MODEL, D_MODEL))],
        out_specs=[full((n, D_MODEL)), full((n, POOL_STATE * POOL_WIDTH)), full((n, D_MODEL))],
        out_shape=[jax.ShapeDtypeStruct((n, D_MODEL), F32),
                   jax.ShapeDtypeStruct((n, POOL_STATE * POOL_WIDTH), F32),
                   jax.ShapeDtypeStruct((n, D_MODEL), F32)],
        scratch_shapes=[pltpu.VMEM((n, D_MODEL), BF16)],
        compiler_params=pltpu.CompilerParams(
            dimension_semantics=("arbitrary",), vmem_limit_bytes=VMEM_LIMIT_BYTES),
        name="sample_mix",
    )(x, gates, o, past, hgrn_norm, pool_mix, pool_scale, w_out, g_cross, w_cq)


def _sample_ffn_kernel(x_ref, att_ref, wo_ref, g_ref, w1_ref, w2_ref, gf_ref, o_ref):
    x2 = x_ref[...] + _dot(att_ref[...].astype(BF16), wo_ref[...])
    o_ref[...] = _ffn_body(x2, g_ref, w1_ref, w2_ref, gf_ref)


def _sample_ffn(x1, att, w_co, g_ffn, w1, w2, g_final):
    n = x1.shape[0]
    full = lambda shape: pl.BlockSpec(shape, lambda i: (0,) * len(shape))
    return pl.pallas_call(
        _sample_ffn_kernel,
        grid=(1,),
        in_specs=[full((n, D_MODEL)), full((n, D_MODEL)), full((D_MODEL, D_MODEL)),
                  full((1, D_MODEL)), full((D_MODEL, 2 * D_FF)), full((D_FF, D_MODEL)),
                  full((1, D_MODEL))],
        out_specs=full((n, D_MODEL)),
        out_shape=jax.ShapeDtypeStruct((n, D_MODEL), F32),
        compiler_params=pltpu.CompilerParams(
            dimension_semantics=("arbitrary",), vmem_limit_bytes=VMEM_LIMIT_BYTES),
        name="sample_ffn",
    )(x1, att, w_co, g_ffn, w1, w2, g_final)


def kernel(x_prompt, x_sample, mem_prompt, state_hgrn, state_pool, cache_mem_k, cache_mem_v,
           g_mix, w_in, hgrn_lb, hgrn_norm, pool_mix, pool_scale, w_out, g_mem, w_mem_kv,
           g_cross, w_cq, w_co, g_ffn, w_ffn_in, w_ffn_out, g_final):
    nb, L, _ = x_prompt.shape
    ns = x_sample.shape[0]

    w_in_b = w_in[0].astype(BF16)
    w_out_b = w_out[0].astype(BF16)
    w_kv_b = w_mem_kv[0].astype(BF16)
    w_cq_b = w_cq[0].astype(BF16)
    w_co_b = w_co[0].astype(BF16)
    w1_b = w_ffn_in[0].astype(BF16)
    w2_b = w_ffn_out[0].astype(BF16)
    pmix_b = pool_mix[0].astype(BF16)
    g_final2 = g_final.reshape(1, D_MODEL)

    xs = x_sample.reshape(ns, D_MODEL)
    gates = _sample_gates(xs, g_mix, w_in_b, hgrn_lb)
    hgrn_s, o_s = _sample_state(gates, state_hgrn[0])
    x1s, pool_s, qs = _sample_mix(xs, gates, o_s, state_pool[0].reshape(ns, POOL_STATE * POOL_WIDTH),
                                  hgrn_norm, pmix_b, pool_scale, w_out_b, g_cross, w_cq_b)

    mem_k, mem_v, kt, vb = _mem_kv(mem_prompt, g_mem, w_kv_b)
    x1, hgrn_p, pool_p = _mixer_prompt(x_prompt, g_mix, w_in_b, hgrn_lb, hgrn_norm, pmix_b,
                                       pool_scale, w_out_b)
    x2 = _attn_prompt(x1, g_cross, w_cq_b, kt, vb, w_co_b)
    y_prompt, att = _ffn_prompt_attn_sample(
        x2.reshape(nb * L, D_MODEL), g_ffn, w1_b, w2_b, g_final2,
        _split_heads(qs.reshape(ns, X_HEADS, X_HEAD_DIM)),
        _split_heads(cache_mem_k[0]), _split_heads(cache_mem_v[0]))

    y_sample = _sample_ffn(x1s, _merge_heads(att).reshape(ns, D_MODEL), w_co_b, g_ffn, w1_b, w2_b,
                           g_final2)

    return (y_prompt.reshape(nb, L, D_MODEL),
            y_sample.reshape(ns, 1, D_MODEL),
            hgrn_p[None],
            pool_p[None],
            mem_k.reshape(1, nb, MEM_LEN, X_HEADS, X_HEAD_DIM),
            mem_v.reshape(1, nb, MEM_LEN, X_HEADS, X_HEAD_DIM),
            hgrn_s[None],
            pool_s.reshape(1, ns, POOL_STATE, POOL_WIDTH))
```

```python
import functools

import jax
import jax.numpy as jnp
from jax import lax
from jax.experimental import pallas as pl
from jax.experimental.pallas import tpu as pltpu

F32 = jnp.float32
BF16 = jnp.bfloat16

D_MODEL = 1024
HGRN_WIDTH = 512
HEADS = 4
DK = 128
CHUNK = 64
POOL_WIDTH = 512
POOL_WINDOWS = (2, 4, 8, 16)
POOL_GROUP = 128
POOL_STATE = 15
IN_PROJ = 4 * HGRN_WIDTH + POOL_WIDTH
MEM_LEN = 256
X_HEADS = 4
X_HEAD_DIM = 256
D_FF = 2816
EPS = 1e-6
ATTN_SCALE = X_HEAD_DIM ** -0.5

VMEM_LIMIT_BYTES = 56 * 1024 * 1024

MIX_ROWS = 512
MIX_BLOCKS_PER_SEQ = 4
IN_PROJ_PIECE = 256
TRI_ROWS = 256
ATTN_ROWS = 1024
FFN_ROWS = 512
STATE_TOKENS = 8
ATTN_TOKENS = 4
FFN_COL_CHUNKS = tuple((c, min(c + 512, D_FF)) for c in range(0, D_FF, 512))


def _dot(a, b):
    return jnp.dot(a, b, preferred_element_type=F32)


def _dot_nt(a, b):
    return lax.dot_general(a, b, (((1,), (1,)), ((), ())), preferred_element_type=F32)


def _dot_tn(a, b):
    return lax.dot_general(a, b, (((0,), (0,)), ((), ())), preferred_element_type=F32)


def _rms(x, g):
    ms = jnp.mean(x * x, axis=-1, keepdims=True)
    return x * lax.rsqrt(ms + EPS) * g


def _sigmoid(x):
    return 1.0 / (1.0 + jnp.exp(-x))


def _lower_bound(lb_ref):
    t = lb_ref[...]
    m = jnp.max(t, axis=0, keepdims=True)
    e = jnp.exp(t - m)
    return e[0:1, :] / jnp.sum(e, axis=0, keepdims=True)


def _gates(proj_q, proj_f, lb):
    qq = proj_q * _sigmoid(proj_q)
    sig = _sigmoid(proj_f)
    fgate = lb + (1.0 - lb) * sig
    kk = (1.0 - lb) * (1.0 - sig)
    return qq, fgate, kk


def _split3(x):
    hi = x.astype(BF16)
    r1 = x - hi.astype(F32)
    mid = r1.astype(BF16)
    lo = (r1 - mid.astype(F32)).astype(BF16)
    return hi, mid, lo


def _memkv_kernel(mem_ref, g_ref, w_ref, k_ref, v_ref, kt_ref, vb_ref):
    h = _rms(mem_ref[0], g_ref[...]).astype(BF16)
    kv = _dot(h, w_ref[...])
    k = kv[:, :D_MODEL]
    v = kv[:, D_MODEL:]
    k_ref[0] = k
    v_ref[0] = v
    kt_ref[0] = k.T.astype(BF16)
    vb_ref[0] = v.astype(BF16)


def _mem_kv(mem, g_mem, w_kv):
    nb = mem.shape[0]
    full = lambda shape: pl.BlockSpec(shape, lambda b: (0,) * len(shape))
    return pl.pallas_call(
        _memkv_kernel,
        grid=(nb,),
        in_specs=[pl.BlockSpec((1, MEM_LEN, D_MODEL), lambda b: (b, 0, 0)),
                  full((1, D_MODEL)), full((D_MODEL, 2 * D_MODEL))],
        out_specs=[pl.BlockSpec((1, MEM_LEN, D_MODEL), lambda b: (b, 0, 0)),
                   pl.BlockSpec((1, MEM_LEN, D_MODEL), lambda b: (b, 0, 0)),
                   pl.BlockSpec((1, D_MODEL, MEM_LEN), lambda b: (b, 0, 0)),
                   pl.BlockSpec((1, MEM_LEN, D_MODEL), lambda b: (b, 0, 0))],
        out_shape=[jax.ShapeDtypeStruct((nb, MEM_LEN, D_MODEL), F32),
                   jax.ShapeDtypeStruct((nb, MEM_LEN, D_MODEL), F32),
                   jax.ShapeDtypeStruct((nb, D_MODEL, MEM_LEN), BF16),
                   jax.ShapeDtypeStruct((nb, MEM_LEN, D_MODEL), BF16)],
        compiler_params=pltpu.CompilerParams(
            dimension_semantics=("arbitrary",), vmem_limit_bytes=VMEM_LIMIT_BYTES),
        name="mem_kv",
    )(mem, g_mem, w_kv)


def _mix_block(n, x, proj_ref, out_ref, fillers, lb_ref, hn_ref, pmix_ref, ps_ref, wout_ref,
               st_ref, ext_ref, qq_ref, kk_ref, b_ref, mrg_ref):
    T = MIX_ROWS
    fillers = list(fillers)
    assert len(fillers) == 3 + T // CHUNK
    l = n % MIX_BLOCKS_PER_SEQ
    first = l == 0
    for h in range(HEADS):
        st_ref[h] = jnp.where(first, 0.0, st_ref[h])
    ext_ref[0:16, :] = jnp.where(first, 0.0, ext_ref[0:16, :])

    fillers.pop(0)()
    fillers.pop(0)()
    fillers.pop(0)()
    lb = _lower_bound(lb_ref)
    qq, fgate, kk = _gates(proj_ref[:, 0:512], proj_ref[:, 512:1024], lb)
    qq_ref[...] = qq
    kk_ref[...] = kk

    r = lax.broadcasted_iota(jnp.int32, (TRI_ROWS, TRI_ROWS), 0)
    c = lax.broadcasted_iota(jnp.int32, (TRI_ROWS, TRI_ROWS), 1)
    tri = jnp.where((c <= r) & (c >= (r & -CHUNK)), 1.0, 0.0).astype(BF16)
    logf = jnp.log(fgate)
    for blk in range(T // TRI_ROWS):
        rows = slice(blk * TRI_ROWS, (blk + 1) * TRI_ROWS)
        hi, mid, lo = _split3(logf[rows])
        b_ref[rows, :] = _dot(tri, hi) + _dot(tri, mid) + _dot(tri, lo)

    cr = lax.broadcasted_iota(jnp.int32, (CHUNK, CHUNK), 0)
    cc = lax.broadcasted_iota(jnp.int32, (CHUNK, CHUNK), 1)
    causal = cc <= cr
    mid_row = (CHUNK - 1) // 2

    for ci in range(T // CHUNK):
        r0 = ci * CHUNK
        rows = slice(r0, r0 + CHUNK)
        scores, inters, vals = [], [], []
        for h in range(HEADS):
            cols = slice(h * DK, (h + 1) * DK)
            b = b_ref[rows, cols]
            m = b_ref[r0 + mid_row:r0 + mid_row + 1, cols]
            b_end = b_ref[r0 + CHUNK - 1:r0 + CHUNK, cols]
            e1 = jnp.exp(b - m)
            e2 = jnp.exp(m - b)
            q1 = qq_ref[rows, cols] * e1
            q0 = q1 * jnp.exp(m)
            ks = kk_ref[rows, cols] * e2
            k2 = ks * jnp.exp(b_end - m)
            v = proj_ref[rows, 1024 + h * DK:1024 + (h + 1) * DK].astype(BF16)
            st = st_ref[h]
            scores.append(_dot_nt(q1.astype(BF16), ks.astype(BF16)))
            inters.append(_dot_nt(q0.astype(BF16), st.astype(BF16)))
            st_ref[h] = st * jnp.exp(b_end) + _dot_tn(v, k2.astype(BF16))
            vals.append(v)
        fillers.pop(0)()
        for h in range(HEADS):
            cols = slice(h * DK, (h + 1) * DK)
            a = jnp.where(causal, scores[h], 0.0)
            o = inters[h] + _dot(a.astype(BF16), vals[h])
            o = o * lax.rsqrt(jnp.mean(o * o, axis=-1, keepdims=True) + EPS)
            o = o * hn_ref[:, cols]
            g = proj_ref[rows, 1536 + h * DK:1536 + (h + 1) * DK]
            mrg_ref[rows, cols] = (o * _sigmoid(g)).astype(BF16)

    ext_ref[16:16 + T, :] = proj_ref[:, 2048:2560]
    pos = l * T + lax.broadcasted_iota(jnp.int32, (T, POOL_GROUP), 0)
    for gi, w in enumerate(POOL_WINDOWS):
        cols = slice(gi * POOL_GROUP, (gi + 1) * POOL_GROUP)
        s = ext_ref[:, cols]
        u = s[16:]
        shift = 1
        while shift < w:
            s = s + pltpu.roll(s, shift, axis=0)
            shift *= 2
        cnt = jnp.minimum(pos + 1, w).astype(F32)
        pooled = s[16:] / cnt - u
        yb = _dot(pooled.astype(BF16), pmix_ref[gi]) * ps_ref[:, cols]
        mrg_ref[:, HGRN_WIDTH + gi * POOL_GROUP:HGRN_WIDTH + (gi + 1) * POOL_GROUP] = yb.astype(BF16)

    ext_ref[0:16, :] = ext_ref[T:T + 16, :]
    out_ref[...] = x + _dot(mrg_ref[...], wout_ref[...])


def _mixer_kernel(xa_ref, xn_ref, g_ref, win_ref, lb_ref, hn_ref, pmix_ref, ps_ref, wout_ref,
                  x1_ref, hst_ref, pst_ref,
                  p0_ref, p1_ref, hb_ref, st_ref, ext_ref, qq_ref, kk_ref, b_ref, mrg_ref):
    T = MIX_ROWS
    g = pl.program_id(0)
    rest = (lb_ref, hn_ref, pmix_ref, ps_ref, wout_ref, st_ref, ext_ref, qq_ref, kk_ref, b_ref, mrg_ref)

    def in_proj_pieces(x_ref, rows, p_ref):
        def prep():
            hb_ref[...] = _rms(x_ref[rows, :], g_ref[...]).astype(BF16)

        def piece(k):
            cols = slice(k * IN_PROJ_PIECE, (k + 1) * IN_PROJ_PIECE)
            p_ref[:, cols] = _dot(hb_ref[...], win_ref[:, cols])

        return [prep] + [functools.partial(piece, k) for k in range(IN_PROJ // IN_PROJ_PIECE)]

    @pl.when(g == 0)
    def _():
        st_ref[...] = jnp.zeros_like(st_ref)
        ext_ref[0:16, :] = jnp.zeros((16, POOL_WIDTH), F32)
        for f in in_proj_pieces(xa_ref, slice(0, T), p0_ref):
            f()

    _mix_block(2 * g, xa_ref[0:T, :], p0_ref, x1_ref.at[0:T, :],
               in_proj_pieces(xa_ref, slice(T, 2 * T), p1_ref), *rest)
    _mix_block(2 * g + 1, xa_ref[T:2 * T, :], p1_ref, x1_ref.at[T:2 * T, :],
               in_proj_pieces(xn_ref, slice(0, T), p0_ref), *rest)

    @pl.when(g % (MIX_BLOCKS_PER_SEQ // 2) == MIX_BLOCKS_PER_SEQ // 2 - 1)
    def _():
        for h in range(HEADS):
            hst_ref[0, h] = st_ref[h].T
        pst_ref[0] = ext_ref[T + 1:T + 16, :]


def _mixer_prompt(x, g_mix, w_in, hgrn_lb, hgrn_norm, pool_mix, pool_scale, w_out):
    nb, L, _ = x.shape
    T = MIX_ROWS
    assert L // T == MIX_BLOCKS_PER_SEQ and MIX_BLOCKS_PER_SEQ % 2 == 0
    n_blocks = nb * MIX_BLOCKS_PER_SEQ
    steps_per_seq = MIX_BLOCKS_PER_SEQ // 2
    x2d = x.reshape(nb * L, D_MODEL)
    full = lambda shape: pl.BlockSpec(shape, lambda g: (0,) * len(shape))
    x1, hst, pst = pl.pallas_call(
        _mixer_kernel,
        grid=(n_blocks // 2,),
        in_specs=[pl.BlockSpec((2 * T, D_MODEL), lambda g: (g, 0)),
                  pl.BlockSpec((T, D_MODEL), lambda g: (jnp.minimum(2 * g + 2, n_blocks - 1), 0)),
                  full((1, D_MODEL)), full((D_MODEL, IN_PROJ)), full((2, HGRN_WIDTH)),
                  full((1, HGRN_WIDTH)), full((4, POOL_GROUP, POOL_GROUP)),
                  full((1, POOL_WIDTH)), full((D_MODEL, D_MODEL))],
        out_specs=[pl.BlockSpec((2 * T, D_MODEL), lambda g: (g, 0)),
                   pl.BlockSpec((1, HEADS, DK, DK), lambda g: (g // steps_per_seq, 0, 0, 0)),
                   pl.BlockSpec((1, POOL_STATE, POOL_WIDTH), lambda g: (g // steps_per_seq, 0, 0))],
        out_shape=[jax.ShapeDtypeStruct((nb * L, D_MODEL), F32),
                   jax.ShapeDtypeStruct((nb, HEADS, DK, DK), F32),
                   jax.ShapeDtypeStruct((nb, POOL_STATE, POOL_WIDTH), F32)],
        scratch_shapes=[pltpu.VMEM((T, IN_PROJ), F32),
                        pltpu.VMEM((T, IN_PROJ), F32),
                        pltpu.VMEM((T, D_MODEL), BF16),
                        pltpu.VMEM((HEADS, DK, DK), F32),
                        pltpu.VMEM((16 + T, POOL_WIDTH), F32),
                        pltpu.VMEM((T, HGRN_WIDTH), F32),
                        pltpu.VMEM((T, HGRN_WIDTH), F32),
                        pltpu.VMEM((T, HGRN_WIDTH), F32),
                        pltpu.VMEM((T, D_MODEL), BF16)],
        compiler_params=pltpu.CompilerParams(
            dimension_semantics=("arbitrary",), vmem_limit_bytes=VMEM_LIMIT_BYTES),
        name="mixer_prompt",
    )(x2d, x2d, g_mix, w_in, hgrn_lb, hgrn_norm, pool_mix, pool_scale, w_out)
    return x1.reshape(nb, L, D_MODEL), hst, pst


def _state_step(q_ref, f_ref, k_ref, v_ref, s_ref, so_ref, o_ref):
    G = STATE_TOKENS
    pad = jnp.zeros((DK - G, DK), F32)
    for h in range(HEADS):
        cols = slice(h * DK, (h + 1) * DK)
        qt = jnp.concatenate([q_ref[:, cols], pad], axis=0).T
        ft = jnp.concatenate([f_ref[:, cols], pad], axis=0).T
        kt = jnp.concatenate([k_ref[:, cols], pad], axis=0).T
        for j in range(G):
            s_new = ft[:, j:j + 1] * s_ref[j, h] + kt[:, j:j + 1] * v_ref[j:j + 1, cols]
            so_ref[j, h] = s_new
            o_ref[j:j + 1, cols] = jnp.sum(qt[:, j:j + 1] * s_new, axis=0, keepdims=True)


def _attn_state_kernel(x_ref, g_ref, wq_ref, kt_ref, vb_ref, wo_ref,
                       sq_ref, sf_ref, sk_ref, sv_ref, s_ref,
                       o_ref, so_ref, oo_ref, att_ref):
    H = ATTN_ROWS // 2
    halves = (slice(0, H), slice(H, 2 * H))

    def query(rows):
        hb = _rms(x_ref[0, rows, :], g_ref[...]).astype(BF16)
        return (_dot(hb, wq_ref[...]) * ATTN_SCALE).astype(BF16)

    def scores(q):
        return [_dot(q[:, h * X_HEAD_DIM:(h + 1) * X_HEAD_DIM],
                     kt_ref[0, h * X_HEAD_DIM:(h + 1) * X_HEAD_DIM, :]) for h in range(X_HEADS)]

    def values(rows, ss):
        for h, s in enumerate(ss):
            cols = slice(h * X_HEAD_DIM, (h + 1) * X_HEAD_DIM)
            e = jnp.exp(s - jnp.max(s, axis=-1, keepdims=True))
            den = jnp.sum(e, axis=-1, keepdims=True)
            o = _dot(e.astype(BF16), vb_ref[0, :, cols]) / den
            att_ref[rows, cols] = o.astype(BF16)

    def project(rows):
        o_ref[0, rows, :] = x_ref[0, rows, :] + _dot(att_ref[rows, :], wo_ref[...])

    s0 = scores(query(halves[0]))
    q1 = query(halves[1])
    values(halves[0], s0)
    s1 = scores(q1)
    project(halves[0])
    values(halves[1], s1)
    project(halves[1])
    _state_step(sq_ref, sf_ref, sk_ref, sv_ref, s_ref, so_ref, oo_ref)


def _attn_prompt_state_sample(x, g_cross, w_cq, kt, vb, w_co, gates, state):
    nb, L, _ = x.shape
    T = ATTN_ROWS
    G = STATE_TOKENS
    steps_per_seq = L // T
    ns = gates.shape[0]
    assert nb * steps_per_seq == ns // G
    full = lambda shape: pl.BlockSpec(shape, lambda i: (0,) * len(shape))
    col = lambda c: pl.BlockSpec((G, 512), lambda i: (i, c))
    st_spec = pl.BlockSpec((G, HEADS, DK, DK), lambda i: (i, 0, 0, 0))
    x_spec = pl.BlockSpec((1, T, D_MODEL), lambda i: (i // steps_per_seq, i % steps_per_seq, 0))
    return pl.pallas_call(
        _attn_state_kernel,
        grid=(nb * steps_per_seq,),
        in_specs=[x_spec, full((1, D_MODEL)), full((D_MODEL, D_MODEL)),
                  pl.BlockSpec((1, D_MODEL, MEM_LEN), lambda i: (i // steps_per_seq, 0, 0)),
                  pl.BlockSpec((1, MEM_LEN, D_MODEL), lambda i: (i // steps_per_seq, 0, 0)),
                  full((D_MODEL, D_MODEL)),
                  col(0), col(1), col(2), col(3), st_spec],
        out_specs=[x_spec, st_spec, pl.BlockSpec((G, HGRN_WIDTH), lambda i: (i, 0))],
        out_shape=[jax.ShapeDtypeStruct((nb, L, D_MODEL), F32),
                   jax.ShapeDtypeStruct((ns, HEADS, DK, DK), F32),
                   jax.ShapeDtypeStruct((ns, HGRN_WIDTH), F32)],
        scratch_shapes=[pltpu.VMEM((T, D_MODEL), BF16)],
        compiler_params=pltpu.CompilerParams(
            dimension_semantics=("arbitrary",), vmem_limit_bytes=VMEM_LIMIT_BYTES),
        name="attn_prompt_state_sample",
    )(x, g_cross, w_cq, kt, vb, w_co, gates, gates, gates, gates, state)


def _zero_after(x):
    u = lax.bitcast_convert_type(x, jnp.uint32)
    z = lax.shift_right_logical(lax.shift_right_logical(u, jnp.uint32(16)), jnp.uint32(16))
    return lax.bitcast_convert_type(z, F32)


def _ffn_body(x, g_ref, w1_ref, w2_ref, gf_ref, fillers=()):
    hb = _rms(x, g_ref[...]).astype(BF16)

    def up(c0, c1):
        return _dot(hb, w1_ref[:, c0:c1]), _dot(hb, w1_ref[:, D_FF + c0:D_FF + c1])

    def add_to_first_tile(m, z):
        top = jnp.concatenate([m[0:8, 0:128] + z, m[0:8, 128:]], axis=1)
        return jnp.concatenate([top, m[8:, :]], axis=0)

    y = x
    fillers = list(fillers)
    assert len(fillers) <= len(FFN_COL_CHUNKS)
    nxt = up(*FFN_COL_CHUNKS[0])
    for i, (c0, c1) in enumerate(FFN_COL_CHUNKS):
        a, bg = nxt
        if i + 1 < len(FFN_COL_CHUNKS):
            nxt = up(*FFN_COL_CHUNKS[i + 1])
        if fillers:
            bg = add_to_first_tile(bg, _zero_after(fillers.pop(0)()))
        act = (a * _sigmoid(a) * bg).astype(BF16)
        y = y + _dot(act, w2_ref[c0:c1, :])
    return _rms(y, gf_ref[...])


def _split_heads(x):
    lead = x.shape[:-2]
    x = x.reshape(lead + (X_HEADS, 2, 128))
    x = jnp.swapaxes(x, -3, -2)
    return x.reshape(lead + (2 * X_HEADS, 128))


def _merge_heads(x):
    lead = x.shape[:-2]
    x = x.reshape(lead + (2, X_HEADS, 128))
    x = jnp.swapaxes(x, -3, -2)
    return x.reshape(lead + (X_HEADS, X_HEAD_DIM))


def _memory_attention(j, q_ref, k_ref, v_ref, o_ref):
    part = jnp.sum(k_ref[j] * q_ref[j][None], axis=-1, keepdims=True)
    s = part + pltpu.roll(part, X_HEADS, axis=1)
    e = jnp.exp(s - jnp.max(s, axis=0, keepdims=True))
    den = jnp.sum(e, axis=0)
    o = jnp.sum(e * v_ref[j], axis=0) / den
    o_ref[j] = o
    return o


def _ffn_attn_kernel(x_ref, g_ref, w1_ref, w2_ref, gf_ref, q_ref, k_ref, v_ref, o_ref, att_ref):
    fillers = [functools.partial(_memory_attention, j, q_ref, k_ref, v_ref, att_ref)
               for j in range(ATTN_TOKENS)]
    o_ref[...] = _ffn_body(x_ref[...], g_ref, w1_ref, w2_ref, gf_ref, fillers)


def _ffn_prompt_attn_sample(x, g_ffn, w1, w2, g_final, q, cache_k, cache_v):
    n = x.shape[0]
    T = FFN_ROWS
    G = ATTN_TOKENS
    assert n // T == q.shape[0] // G
    full = lambda shape: pl.BlockSpec(shape, lambda i: (0,) * len(shape))
    kv_spec = pl.BlockSpec((G, MEM_LEN, 2 * X_HEADS, 128), lambda i: (i, 0, 0, 0))
    q_spec = pl.BlockSpec((G, 2 * X_HEADS, 128), lambda i: (i, 0, 0))
    return pl.pallas_call(
        _ffn_attn_kernel,
        grid=(n // T,),
        in_specs=[pl.BlockSpec((T, D_MODEL), lambda i: (i, 0)),
                  full((1, D_MODEL)), full((D_MODEL, 2 * D_FF)), full((D_FF, D_MODEL)),
                  full((1, D_MODEL)), q_spec, kv_spec, kv_spec],
        out_specs=[pl.BlockSpec((T, D_MODEL), lambda i: (i, 0)), q_spec],
        out_shape=[jax.ShapeDtypeStruct((n, D_MODEL), F32),
                   jax.ShapeDtypeStruct((q.shape[0], 2 * X_HEADS, 128), F32)],
        compiler_params=pltpu.CompilerParams(
            dimension_semantics=("arbitrary",), vmem_limit_bytes=VMEM_LIMIT_BYTES),
        name="ffn_prompt_attn_sample",
    )(x, g_ffn, w1, w2, g_final, q, cache_k, cache_v)


def _sample_gates_kernel(x_ref, g_ref, win_ref, lb_ref, o_ref):
    hb = _rms(x_ref[...], g_ref[...]).astype(BF16)
    proj = _dot(hb, win_ref[...])
    lb = _lower_bound(lb_ref)
    qq, fgate, kk = _gates(proj[:, 0:512], proj[:, 512:1024], lb)
    o_ref[:, 0:512] = qq
    o_ref[:, 512:1024] = fgate
    o_ref[:, 1024:1536] = kk
    o_ref[:, 1536:2048] = proj[:, 1024:1536]
    o_ref[:, 2048:2560] = _sigmoid(proj[:, 1536:2048])
    o_ref[:, 2560:3072] = proj[:, 2048:2560]


def _sample_gates(x, g_mix, w_in, hgrn_lb):
    n = x.shape[0]
    full = lambda shape: pl.BlockSpec(shape, lambda i: (0,) * len(shape))
    return pl.pallas_call(
        _sample_gates_kernel,
        grid=(1,),
        in_specs=[full((n, D_MODEL)), full((1, D_MODEL)), full((D_MODEL, IN_PROJ)),
                  full((2, HGRN_WIDTH))],
        out_specs=full((n, 6 * 512)),
        out_shape=jax.ShapeDtypeStruct((n, 6 * 512), F32),
        compiler_params=pltpu.CompilerParams(
            dimension_semantics=("arbitrary",), vmem_limit_bytes=VMEM_LIMIT_BYTES),
        name="sample_gates",
    )(x, g_mix, w_in, hgrn_lb)


def _sample_mix_kernel(x_ref, gates_ref, o_ref, past_ref, hn_ref, pmix_ref, ps_ref, wout_ref,
                       gc_ref, wq_ref, x1_ref, pool_ref, q_ref, mrg_ref):
    for h in range(HEADS):
        cols = slice(h * DK, (h + 1) * DK)
        o = o_ref[:, cols]
        o = o * lax.rsqrt(jnp.mean(o * o, axis=-1, keepdims=True) + EPS) * hn_ref[:, cols]
        mrg_ref[:, cols] = (o * gates_ref[:, 2048 + h * DK:2048 + (h + 1) * DK]).astype(BF16)
    for gi, w in enumerate(POOL_WINDOWS):
        cols = slice(gi * POOL_GROUP, (gi + 1) * POOL_GROUP)
        u = gates_ref[:, 2560 + gi * POOL_GROUP:2560 + (gi + 1) * POOL_GROUP]
        acc = u
        for j in range(1, w):
            r = POOL_STATE - j
            acc = acc + past_ref[:, r * POOL_WIDTH + gi * POOL_GROUP:r * POOL_WIDTH + (gi + 1) * POOL_GROUP]
        pooled = acc / float(w) - u
        yb = _dot(pooled.astype(BF16), pmix_ref[gi]) * ps_ref[:, cols]
        mrg_ref[:, HGRN_WIDTH + gi * POOL_GROUP:HGRN_WIDTH + (gi + 1) * POOL_GROUP] = yb.astype(BF16)
    pool_ref[:, 0:(POOL_STATE - 1) * POOL_WIDTH] = past_ref[:, POOL_WIDTH:POOL_STATE * POOL_WIDTH]
    pool_ref[:, (POOL_STATE - 1) * POOL_WIDTH:] = gates_ref[:, 2560:3072]
    x1 = x_ref[...] + _dot(mrg_ref[...], wout_ref[...])
    x1_ref[...] = x1
    hb = _rms(x1, gc_ref[...]).astype(BF16)
    q_ref[...] = _dot(hb, wq_ref[...]) * ATTN_SCALE


def _sample_mix(x, gates, o, past, hgrn_norm, pool_mix, pool_scale, w_out, g_cross, w_cq):
    n = x.shape[0]
    full = lambda shape: pl.BlockSpec(shape, lambda i: (0,) * len(shape))
    return pl.pallas_call(
        _sample_mix_kernel,
        grid=(1,),
        in_specs=[full((n, D_MODEL)), full((n, 6 * 512)), full((n, HGRN_WIDTH)),
                  full((n, POOL_STATE * POOL_WIDTH)), full((1, HGRN_WIDTH)),
                  full((4, POOL_GROUP, POOL_GROUP)), full((1, POOL_WIDTH)),
                  full((D_MODEL, D_MODEL)), full((1, D_MODEL)), full((D_MODEL, D_MODEL))],
        out_specs=[full((n, D_MODEL)), full((n, POOL_STATE * POOL_WIDTH)), full((n, D_MODEL))],
        out_shape=[jax.ShapeDtypeStruct((n, D_MODEL), F32),
                   jax.ShapeDtypeStruct((n, POOL_STATE * POOL_WIDTH), F32),
                   jax.ShapeDtypeStruct((n, D_MODEL), F32)],
        scratch_shapes=[pltpu.VMEM((n, D_MODEL), BF16)],
        compiler_params=pltpu.CompilerParams(
            dimension_semantics=("arbitrary",), vmem_limit_bytes=VMEM_LIMIT_BYTES),
        name="sample_mix",
    )(x, gates, o, past, hgrn_norm, pool_mix, pool_scale, w_out, g_cross, w_cq)


def _sample_ffn_kernel(x_ref, att_ref, wo_ref, g_ref, w1_ref, w2_ref, gf_ref, o_ref):
    x2 = x_ref[...] + _dot(att_ref[...].astype(BF16), wo_ref[...])
    o_ref[...] = _ffn_body(x2, g_ref, w1_ref, w2_ref, gf_ref)


def _sample_ffn(x1, att, w_co, g_ffn, w1, w2, g_final):
    n = x1.shape[0]
    full = lambda shape: pl.BlockSpec(shape, lambda i: (0,) * len(shape))
    return pl.pallas_call(
        _sample_ffn_kernel,
        grid=(1,),
        in_specs=[full((n, D_MODEL)), full((n, D_MODEL)), full((D_MODEL, D_MODEL)),
                  full((1, D_MODEL)), full((D_MODEL, 2 * D_FF)), full((D_FF, D_MODEL)),
                  full((1, D_MODEL))],
        out_specs=full((n, D_MODEL)),
        out_shape=jax.ShapeDtypeStruct((n, D_MODEL), F32),
        compiler_params=pltpu.CompilerParams(
            dimension_semantics=("arbitrary",), vmem_limit_bytes=VMEM_LIMIT_BYTES),
        name="sample_ffn",
    )(x1, att, w_co, g_ffn, w1, w2, g_final)


def kernel(x_prompt, x_sample, mem_prompt, state_hgrn, state_pool, cache_mem_k, cache_mem_v,
           g_mix, w_in, hgrn_lb, hgrn_norm, pool_mix, pool_scale, w_out, g_mem, w_mem_kv,
           g_cross, w_cq, w_co, g_ffn, w_ffn_in, w_ffn_out, g_final):
    nb, L, _ = x_prompt.shape
    ns = x_sample.shape[0]

    w_in_b = w_in[0].astype(BF16)
    w_out_b = w_out[0].astype(BF16)
    w_kv_b = w_mem_kv[0].astype(BF16)
    w_cq_b = w_cq[0].astype(BF16)
    w_co_b = w_co[0].astype(BF16)
    w1_b = w_ffn_in[0].astype(BF16)
    w2_b = w_ffn_out[0].astype(BF16)
    pmix_b = pool_mix[0].astype(BF16)
    g_final2 = g_final.reshape(1, D_MODEL)

    xs = x_sample.reshape(ns, D_MODEL)
    gates = _sample_gates(xs, g_mix, w_in_b, hgrn_lb)
    mem_k, mem_v, kt, vb = _mem_kv(mem_prompt, g_mem, w_kv_b)
    x1, hgrn_p, pool_p = _mixer_prompt(x_prompt, g_mix, w_in_b, hgrn_lb, hgrn_norm, pmix_b,
                                       pool_scale, w_out_b)
    x2, hgrn_s, o_s = _attn_prompt_state_sample(x1, g_cross, w_cq_b, kt, vb, w_co_b, gates,
                                                state_hgrn[0])
    x1s, pool_s, qs = _sample_mix(xs, gates, o_s, state_pool[0].reshape(ns, POOL_STATE * POOL_WIDTH),
                                  hgrn_norm, pmix_b, pool_scale, w_out_b, g_cross, w_cq_b)
    y_prompt, att = _ffn_prompt_attn_sample(
        x2.reshape(nb * L, D_MODEL), g_ffn, w1_b, w2_b, g_final2,
        _split_heads(qs.reshape(ns, X_HEADS, X_HEAD_DIM)),
        _split_heads(cache_mem_k[0]), _split_heads(cache_mem_v[0]))

    y_sample = _sample_ffn(x1s, _merge_heads(att).reshape(ns, D_MODEL), w_co_b, g_ffn, w1_b, w2_b,
                           g_final2)

    return (y_prompt.reshape(nb, L, D_MODEL),
            y_sample.reshape(ns, 1, D_MODEL),
            hgrn_p[None],
            pool_p[None],
            mem_k.reshape(1, nb, MEM_LEN, X_HEADS, X_HEAD_DIM),
            mem_v.reshape(1, nb, MEM_LEN, X_HEADS, X_HEAD_DIM),
            hgrn_s[None],
            pool_s.reshape(1, ns, POOL_STATE, POOL_WIDTH))
```

```python
import functools

import jax
import jax.numpy as jnp
from jax import lax
from jax.experimental import pallas as pl
from jax.experimental.pallas import tpu as pltpu

F32 = jnp.float32
BF16 = jnp.bfloat16

D_MODEL = 1024
HGRN_WIDTH = 512
HEADS = 4
DK = 128
CHUNK = 64
POOL_WIDTH = 512
POOL_WINDOWS = (2, 4, 8, 16)
POOL_GROUP = 128
POOL_STATE = 15
IN_PROJ = 4 * HGRN_WIDTH + POOL_WIDTH
MEM_LEN = 256
X_HEADS = 4
X_HEAD_DIM = 256
D_FF = 2816
EPS = 1e-6
ATTN_SCALE = X_HEAD_DIM ** -0.5

VMEM_LIMIT_BYTES = 56 * 1024 * 1024

MIX_ROWS = 512
MIX_BLOCKS_PER_SEQ = 4
IN_PROJ_PIECE = 256
TRI_ROWS = 256
ATTN_ROWS = 1024
FFN_ROWS = 512
STATE_TOKENS = 8
ATTN_TOKENS = 4
FFN_COL_CHUNKS = tuple((c, min(c + 512, D_FF)) for c in range(0, D_FF, 512))


def _dot(a, b):
    return jnp.dot(a, b, preferred_element_type=F32)


def _dot_nt(a, b):
    return lax.dot_general(a, b, (((1,), (1,)), ((), ())), preferred_element_type=F32)


def _dot_tn(a, b):
    return lax.dot_general(a, b, (((0,), (0,)), ((), ())), preferred_element_type=F32)


def _rms(x, g):
    ms = jnp.mean(x * x, axis=-1, keepdims=True)
    return x * lax.rsqrt(ms + EPS) * g


def _sigmoid(x):
    return 1.0 / (1.0 + jnp.exp(-x))


def _lower_bound(lb_ref):
    t = lb_ref[...]
    m = jnp.max(t, axis=0, keepdims=True)
    e = jnp.exp(t - m)
    return e[0:1, :] / jnp.sum(e, axis=0, keepdims=True)


def _gates(proj_q, proj_f, lb):
    qq = proj_q * _sigmoid(proj_q)
    sig = _sigmoid(proj_f)
    fgate = lb + (1.0 - lb) * sig
    kk = (1.0 - lb) * (1.0 - sig)
    return qq, fgate, kk


def _split3(x):
    hi = x.astype(BF16)
    r1 = x - hi.astype(F32)
    mid = r1.astype(BF16)
    lo = (r1 - mid.astype(F32)).astype(BF16)
    return hi, mid, lo


def _memkv_kernel(mem_ref, g_ref, w_ref, k_ref, v_ref, kt_ref, vb_ref, wb_ref):
    @pl.when(pl.program_id(0) == 0)
    def _():
        wb_ref[...] = w_ref[...].astype(BF16)

    h = _rms(mem_ref[0], g_ref[...]).astype(BF16)
    kv = _dot(h, wb_ref[...])
    k = kv[:, :D_MODEL]
    v = kv[:, D_MODEL:]
    for r in range(2 * X_HEADS):
        half, head = divmod(r, X_HEADS)
        c0 = head * X_HEAD_DIM + half * 128
        k_ref[0, :, r, :] = k[:, c0:c0 + 128]
        v_ref[0, :, r, :] = v[:, c0:c0 + 128]
    kt_ref[0] = k.T.astype(BF16)
    vb_ref[0] = v.astype(BF16)


def _mem_kv(mem, g_mem, w_kv):
    nb = mem.shape[0]
    full = lambda shape: pl.BlockSpec(shape, lambda b: (0,) * len(shape))
    split_spec = pl.BlockSpec((1, MEM_LEN, 2 * X_HEADS, 128), lambda b: (b, 0, 0, 0))
    return pl.pallas_call(
        _memkv_kernel,
        grid=(nb,),
        in_specs=[pl.BlockSpec((1, MEM_LEN, D_MODEL), lambda b: (b, 0, 0)),
                  full((1, D_MODEL)), full((D_MODEL, 2 * D_MODEL))],
        out_specs=[split_spec, split_spec,
                   pl.BlockSpec((1, D_MODEL, MEM_LEN), lambda b: (b, 0, 0)),
                   pl.BlockSpec((1, MEM_LEN, D_MODEL), lambda b: (b, 0, 0))],
        out_shape=[jax.ShapeDtypeStruct((nb, MEM_LEN, 2 * X_HEADS, 128), F32),
                   jax.ShapeDtypeStruct((nb, MEM_LEN, 2 * X_HEADS, 128), F32),
                   jax.ShapeDtypeStruct((nb, D_MODEL, MEM_LEN), BF16),
                   jax.ShapeDtypeStruct((nb, MEM_LEN, D_MODEL), BF16)],
        scratch_shapes=[pltpu.VMEM((D_MODEL, 2 * D_MODEL), BF16)],
        compiler_params=pltpu.CompilerParams(
            dimension_semantics=("arbitrary",), vmem_limit_bytes=VMEM_LIMIT_BYTES),
        name="mem_kv",
    )(mem, g_mem, w_kv)


def _mix_block(n, x, proj_ref, out_ref, fillers, lb_ref, hn_ref, pmix_ref, ps_ref, wout_ref,
               st_ref, ext_ref, qq_ref, kk_ref, b_ref, mrg_ref):
    T = MIX_ROWS
    fillers = list(fillers)
    assert len(fillers) == 3 + T // CHUNK
    l = n % MIX_BLOCKS_PER_SEQ
    first = l == 0
    for h in range(HEADS):
        st_ref[h] = jnp.where(first, 0.0, st_ref[h])
    ext_ref[0:16, :] = jnp.where(first, 0.0, ext_ref[0:16, :])

    fillers.pop(0)()
    fillers.pop(0)()
    fillers.pop(0)()
    lb = _lower_bound(lb_ref)
    qq, fgate, kk = _gates(proj_ref[:, 0:512], proj_ref[:, 512:1024], lb)
    qq_ref[...] = qq
    kk_ref[...] = kk

    r = lax.broadcasted_iota(jnp.int32, (TRI_ROWS, TRI_ROWS), 0)
    c = lax.broadcasted_iota(jnp.int32, (TRI_ROWS, TRI_ROWS), 1)
    tri = jnp.where((c <= r) & (c >= (r & -CHUNK)), 1.0, 0.0).astype(BF16)
    logf = jnp.log(fgate)
    for blk in range(T // TRI_ROWS):
        rows = slice(blk * TRI_ROWS, (blk + 1) * TRI_ROWS)
        hi, mid, lo = _split3(logf[rows])
        b_ref[rows, :] = _dot(tri, hi) + _dot(tri, mid) + _dot(tri, lo)

    cr = lax.broadcasted_iota(jnp.int32, (CHUNK, CHUNK), 0)
    cc = lax.broadcasted_iota(jnp.int32, (CHUNK, CHUNK), 1)
    causal = cc <= cr
    mid_row = (CHUNK - 1) // 2

    for ci in range(T // CHUNK):
        r0 = ci * CHUNK
        rows = slice(r0, r0 + CHUNK)
        scores, inters, vals = [], [], []
        for h in range(HEADS):
            cols = slice(h * DK, (h + 1) * DK)
            b = b_ref[rows, cols]
            m = b_ref[r0 + mid_row:r0 + mid_row + 1, cols]
            b_end = b_ref[r0 + CHUNK - 1:r0 + CHUNK, cols]
            e1 = jnp.exp(b - m)
            e2 = jnp.exp(m - b)
            q1 = qq_ref[rows, cols] * e1
            q0 = q1 * jnp.exp(m)
            ks = kk_ref[rows, cols] * e2
            k2 = ks * jnp.exp(b_end - m)
            v = proj_ref[rows, 1024 + h * DK:1024 + (h + 1) * DK].astype(BF16)
            st = st_ref[h]
            scores.append(_dot_nt(q1.astype(BF16), ks.astype(BF16)))
            inters.append(_dot_nt(q0.astype(BF16), st.astype(BF16)))
            st_ref[h] = st * jnp.exp(b_end) + _dot_tn(v, k2.astype(BF16))
            vals.append(v)
        fillers.pop(0)()
        for h in range(HEADS):
            cols = slice(h * DK, (h + 1) * DK)
            a = jnp.where(causal, scores[h], 0.0)
            o = inters[h] + _dot(a.astype(BF16), vals[h])
            o = o * lax.rsqrt(jnp.mean(o * o, axis=-1, keepdims=True) + EPS)
            o = o * hn_ref[:, cols]
            g = proj_ref[rows, 1536 + h * DK:1536 + (h + 1) * DK]
            mrg_ref[rows, cols] = (o * _sigmoid(g)).astype(BF16)

    ext_ref[16:16 + T, :] = proj_ref[:, 2048:2560]
    pos = l * T + lax.broadcasted_iota(jnp.int32, (T, POOL_GROUP), 0)
    for gi, w in enumerate(POOL_WINDOWS):
        cols = slice(gi * POOL_GROUP, (gi + 1) * POOL_GROUP)
        s = ext_ref[:, cols]
        u = s[16:]
        shift = 1
        while shift < w:
            s = s + pltpu.roll(s, shift, axis=0)
            shift *= 2
        cnt = jnp.minimum(pos + 1, w).astype(F32)
        pooled = s[16:] / cnt - u
        yb = _dot(pooled.astype(BF16), pmix_ref[gi]) * ps_ref[:, cols]
        mrg_ref[:, HGRN_WIDTH + gi * POOL_GROUP:HGRN_WIDTH + (gi + 1) * POOL_GROUP] = yb.astype(BF16)

    ext_ref[0:16, :] = ext_ref[T:T + 16, :]
    out_ref[...] = x + _dot(mrg_ref[...], wout_ref[...])


def _mixer_kernel(xa_ref, xn_ref, g_ref, win_ref, lb_ref, hn_ref, pmix_ref, ps_ref, wout_ref,
                  x1_ref, hst_ref, pst_ref,
                  p0_ref, p1_ref, hb_ref, st_ref, ext_ref, qq_ref, kk_ref, b_ref, mrg_ref):
    T = MIX_ROWS
    g = pl.program_id(0)
    rest = (lb_ref, hn_ref, pmix_ref, ps_ref, wout_ref, st_ref, ext_ref, qq_ref, kk_ref, b_ref, mrg_ref)

    def in_proj_pieces(x_ref, rows, p_ref):
        def prep():
            hb_ref[...] = _rms(x_ref[rows, :], g_ref[...]).astype(BF16)

        def piece(k):
            cols = slice(k * IN_PROJ_PIECE, (k + 1) * IN_PROJ_PIECE)
            p_ref[:, cols] = _dot(hb_ref[...], win_ref[:, cols])

        return [prep] + [functools.partial(piece, k) for k in range(IN_PROJ // IN_PROJ_PIECE)]

    @pl.when(g == 0)
    def _():
        st_ref[...] = jnp.zeros_like(st_ref)
        ext_ref[0:16, :] = jnp.zeros((16, POOL_WIDTH), F32)
        for f in in_proj_pieces(xa_ref, slice(0, T), p0_ref):
            f()

    _mix_block(2 * g, xa_ref[0:T, :], p0_ref, x1_ref.at[0:T, :],
               in_proj_pieces(xa_ref, slice(T, 2 * T), p1_ref), *rest)
    _mix_block(2 * g + 1, xa_ref[T:2 * T, :], p1_ref, x1_ref.at[T:2 * T, :],
               in_proj_pieces(xn_ref, slice(0, T), p0_ref), *rest)

    @pl.when(g % (MIX_BLOCKS_PER_SEQ // 2) == MIX_BLOCKS_PER_SEQ // 2 - 1)
    def _():
        for h in range(HEADS):
            hst_ref[0, h] = st_ref[h].T
        pst_ref[0] = ext_ref[T + 1:T + 16, :]


def _mixer_prompt(x, g_mix, w_in, hgrn_lb, hgrn_norm, pool_mix, pool_scale, w_out):
    nb, L, _ = x.shape
    T = MIX_ROWS
    assert L // T == MIX_BLOCKS_PER_SEQ and MIX_BLOCKS_PER_SEQ % 2 == 0
    n_blocks = nb * MIX_BLOCKS_PER_SEQ
    steps_per_seq = MIX_BLOCKS_PER_SEQ // 2
    x2d = x.reshape(nb * L, D_MODEL)
    full = lambda shape: pl.BlockSpec(shape, lambda g: (0,) * len(shape))
    x1, hst, pst = pl.pallas_call(
        _mixer_kernel,
        grid=(n_blocks // 2,),
        in_specs=[pl.BlockSpec((2 * T, D_MODEL), lambda g: (g, 0)),
                  pl.BlockSpec((T, D_MODEL), lambda g: (jnp.minimum(2 * g + 2, n_blocks - 1), 0)),
                  full((1, D_MODEL)), full((D_MODEL, IN_PROJ)), full((2, HGRN_WIDTH)),
                  full((1, HGRN_WIDTH)), full((4, POOL_GROUP, POOL_GROUP)),
                  full((1, POOL_WIDTH)), full((D_MODEL, D_MODEL))],
        out_specs=[pl.BlockSpec((2 * T, D_MODEL), lambda g: (g, 0)),
                   pl.BlockSpec((1, HEADS, DK, DK), lambda g: (g // steps_per_seq, 0, 0, 0)),
                   pl.BlockSpec((1, POOL_STATE, POOL_WIDTH), lambda g: (g // steps_per_seq, 0, 0))],
        out_shape=[jax.ShapeDtypeStruct((nb * L, D_MODEL), F32),
                   jax.ShapeDtypeStruct((nb, HEADS, DK, DK), F32),
                   jax.ShapeDtypeStruct((nb, POOL_STATE, POOL_WIDTH), F32)],
        scratch_shapes=[pltpu.VMEM((T, IN_PROJ), F32),
                        pltpu.VMEM((T, IN_PROJ), F32),
                        pltpu.VMEM((T, D_MODEL), BF16),
                        pltpu.VMEM((HEADS, DK, DK), F32),
                        pltpu.VMEM((16 + T, POOL_WIDTH), F32),
                        pltpu.VMEM((T, HGRN_WIDTH), F32),
                        pltpu.VMEM((T, HGRN_WIDTH), F32),
                        pltpu.VMEM((T, HGRN_WIDTH), F32),
                        pltpu.VMEM((T, D_MODEL), BF16)],
        compiler_params=pltpu.CompilerParams(
            dimension_semantics=("arbitrary",), vmem_limit_bytes=VMEM_LIMIT_BYTES),
        name="mixer_prompt",
    )(x2d, x2d, g_mix, w_in, hgrn_lb, hgrn_norm, pool_mix, pool_scale, w_out)
    return x1.reshape(nb, L, D_MODEL), hst, pst


def _state_step(q_ref, f_ref, k_ref, v_ref, s_ref, so_ref, o_ref):
    G = STATE_TOKENS
    pad = jnp.zeros((DK - G, DK), F32)
    for h in range(HEADS):
        cols = slice(h * DK, (h + 1) * DK)
        qt = jnp.concatenate([q_ref[:, cols], pad], axis=0).T
        ft = jnp.concatenate([f_ref[:, cols], pad], axis=0).T
        kt = jnp.concatenate([k_ref[:, cols], pad], axis=0).T
        for j in range(G):
            s_new = ft[:, j:j + 1] * s_ref[j, h] + kt[:, j:j + 1] * v_ref[j:j + 1, cols]
            so_ref[j, h] = s_new
            o_ref[j:j + 1, cols] = jnp.sum(qt[:, j:j + 1] * s_new, axis=0, keepdims=True)


def _attn_state_kernel(x_ref, g_ref, wq_ref, kt_ref, vb_ref, wo_ref,
                       sq_ref, sf_ref, sk_ref, sv_ref, s_ref,
                       o_ref, so_ref, oo_ref, att_ref):
    H = ATTN_ROWS // 2
    halves = (slice(0, H), slice(H, 2 * H))

    def query(rows):
        hb = _rms(x_ref[0, rows, :], g_ref[...]).astype(BF16)
        return (_dot(hb, wq_ref[...]) * ATTN_SCALE).astype(BF16)

    def scores(q):
        return [_dot(q[:, h * X_HEAD_DIM:(h + 1) * X_HEAD_DIM],
                     kt_ref[0, h * X_HEAD_DIM:(h + 1) * X_HEAD_DIM, :]) for h in range(X_HEADS)]

    def values(rows, ss):
        for h, s in enumerate(ss):
            cols = slice(h * X_HEAD_DIM, (h + 1) * X_HEAD_DIM)
            e = jnp.exp(s - jnp.max(s, axis=-1, keepdims=True))
            den = jnp.sum(e, axis=-1, keepdims=True)
            o = _dot(e.astype(BF16), vb_ref[0, :, cols]) / den
            att_ref[rows, cols] = o.astype(BF16)

    def project(rows):
        o_ref[0, rows, :] = x_ref[0, rows, :] + _dot(att_ref[rows, :], wo_ref[...])

    s0 = scores(query(halves[0]))
    q1 = query(halves[1])
    values(halves[0], s0)
    s1 = scores(q1)
    project(halves[0])
    values(halves[1], s1)
    project(halves[1])
    _state_step(sq_ref, sf_ref, sk_ref, sv_ref, s_ref, so_ref, oo_ref)


def _attn_prompt_state_sample(x, g_cross, w_cq, kt, vb, w_co, gates, state):
    nb, L, _ = x.shape
    T = ATTN_ROWS
    G = STATE_TOKENS
    steps_per_seq = L // T
    ns = gates.shape[0]
    assert nb * steps_per_seq == ns // G
    full = lambda shape: pl.BlockSpec(shape, lambda i: (0,) * len(shape))
    col = lambda c: pl.BlockSpec((G, 512), lambda i: (i, c))
    st_spec = pl.BlockSpec((G, HEADS, DK, DK), lambda i: (i, 0, 0, 0))
    x_spec = pl.BlockSpec((1, T, D_MODEL), lambda i: (i // steps_per_seq, i % steps_per_seq, 0))
    return pl.pallas_call(
        _attn_state_kernel,
        grid=(nb * steps_per_seq,),
        in_specs=[x_spec, full((1, D_MODEL)), full((D_MODEL, D_MODEL)),
                  pl.BlockSpec((1, D_MODEL, MEM_LEN), lambda i: (i // steps_per_seq, 0, 0)),
                  pl.BlockSpec((1, MEM_LEN, D_MODEL), lambda i: (i // steps_per_seq, 0, 0)),
                  full((D_MODEL, D_MODEL)),
                  col(0), col(1), col(2), col(3), st_spec],
        out_specs=[x_spec, st_spec, pl.BlockSpec((G, HGRN_WIDTH), lambda i: (i, 0))],
        out_shape=[jax.ShapeDtypeStruct((nb, L, D_MODEL), F32),
                   jax.ShapeDtypeStruct((ns, HEADS, DK, DK), F32),
                   jax.ShapeDtypeStruct((ns, HGRN_WIDTH), F32)],
        scratch_shapes=[pltpu.VMEM((T, D_MODEL), BF16)],
        compiler_params=pltpu.CompilerParams(
            dimension_semantics=("arbitrary",), vmem_limit_bytes=VMEM_LIMIT_BYTES),
        name="attn_prompt_state_sample",
    )(x, g_cross, w_cq, kt, vb, w_co, gates, gates, gates, gates, state)


def _zero_after(x):
    u = lax.bitcast_convert_type(x, jnp.uint32)
    z = lax.shift_right_logical(lax.shift_right_logical(u, jnp.uint32(16)), jnp.uint32(16))
    return lax.bitcast_convert_type(z, F32)


def _ffn_body(x, g_ref, w1_ref, w2_ref, gf_ref, fillers=()):
    hb = _rms(x, g_ref[...]).astype(BF16)

    def up(c0, c1):
        return _dot(hb, w1_ref[:, c0:c1]), _dot(hb, w1_ref[:, D_FF + c0:D_FF + c1])

    def add_to_first_tile(m, z):
        top = jnp.concatenate([m[0:8, 0:128] + z, m[0:8, 128:]], axis=1)
        return jnp.concatenate([top, m[8:, :]], axis=0)

    y = x
    fillers = list(fillers)
    assert len(fillers) <= len(FFN_COL_CHUNKS)
    nxt = up(*FFN_COL_CHUNKS[0])
    for i, (c0, c1) in enumerate(FFN_COL_CHUNKS):
        a, bg = nxt
        if i + 1 < len(FFN_COL_CHUNKS):
            nxt = up(*FFN_COL_CHUNKS[i + 1])
        if fillers:
            bg = add_to_first_tile(bg, _zero_after(fillers.pop(0)()))
        act = (a * _sigmoid(a) * bg).astype(BF16)
        y = y + _dot(act, w2_ref[c0:c1, :])
    return _rms(y, gf_ref[...])


def _split_heads(x):
    lead = x.shape[:-2]
    x = x.reshape(lead + (X_HEADS, 2, 128))
    x = jnp.swapaxes(x, -3, -2)
    return x.reshape(lead + (2 * X_HEADS, 128))


def _merge_heads(x):
    lead = x.shape[:-2]
    x = x.reshape(lead + (2, X_HEADS, 128))
    x = jnp.swapaxes(x, -3, -2)
    return x.reshape(lead + (X_HEADS, X_HEAD_DIM))


def _memory_attention(j, q_ref, k_ref, v_ref, o_ref):
    part = jnp.sum(k_ref[j] * q_ref[j][None], axis=-1, keepdims=True)
    s = part + pltpu.roll(part, X_HEADS, axis=1)
    e = jnp.exp(s - jnp.max(s, axis=0, keepdims=True))
    den = jnp.sum(e, axis=0)
    o = jnp.sum(e * v_ref[j], axis=0) / den
    o_ref[j] = o
    return o


def _ffn_attn_kernel(x_ref, g_ref, w1_ref, w2_ref, gf_ref, q_ref, k_ref, v_ref, o_ref, att_ref):
    fillers = [functools.partial(_memory_attention, j, q_ref, k_ref, v_ref, att_ref)
               for j in range(ATTN_TOKENS)]
    o_ref[...] = _ffn_body(x_ref[...], g_ref, w1_ref, w2_ref, gf_ref, fillers)


def _ffn_prompt_attn_sample(x, g_ffn, w1, w2, g_final, q, cache_k, cache_v):
    n = x.shape[0]
    T = FFN_ROWS
    G = ATTN_TOKENS
    assert n // T == q.shape[0] // G
    full = lambda shape: pl.BlockSpec(shape, lambda i: (0,) * len(shape))
    kv_spec = pl.BlockSpec((G, MEM_LEN, 2 * X_HEADS, 128), lambda i: (i, 0, 0, 0))
    q_spec = pl.BlockSpec((G, 2 * X_HEADS, 128), lambda i: (i, 0, 0))
    return pl.pallas_call(
        _ffn_attn_kernel,
        grid=(n // T,),
        in_specs=[pl.BlockSpec((T, D_MODEL), lambda i: (i, 0)),
                  full((1, D_MODEL)), full((D_MODEL, 2 * D_FF)), full((D_FF, D_MODEL)),
                  full((1, D_MODEL)), q_spec, kv_spec, kv_spec],
        out_specs=[pl.BlockSpec((T, D_MODEL), lambda i: (i, 0)), q_spec],
        out_shape=[jax.ShapeDtypeStruct((n, D_MODEL), F32),
                   jax.ShapeDtypeStruct((q.shape[0], 2 * X_HEADS, 128), F32)],
        compiler_params=pltpu.CompilerParams(
            dimension_semantics=("arbitrary",), vmem_limit_bytes=VMEM_LIMIT_BYTES),
        name="ffn_prompt_attn_sample",
    )(x, g_ffn, w1, w2, g_final, q, cache_k, cache_v)


def _sample_gates_kernel(x_ref, g_ref, win_ref, lb_ref, o_ref, x2d_ref):
    x = x_ref[:, 0, :]
    x2d_ref[...] = x
    hb = _rms(x, g_ref[...]).astype(BF16)
    proj = _dot(hb, win_ref[...])
    lb = _lower_bound(lb_ref)
    qq, fgate, kk = _gates(proj[:, 0:512], proj[:, 512:1024], lb)
    o_ref[:, 0:512] = qq
    o_ref[:, 512:1024] = fgate
    o_ref[:, 1024:1536] = kk
    o_ref[:, 1536:2048] = proj[:, 1024:1536]
    o_ref[:, 2048:2560] = _sigmoid(proj[:, 1536:2048])
    o_ref[:, 2560:3072] = proj[:, 2048:2560]


def _sample_gates(x, g_mix, w_in, hgrn_lb):
    n = x.shape[0]
    full = lambda shape: pl.BlockSpec(shape, lambda i: (0,) * len(shape))
    return pl.pallas_call(
        _sample_gates_kernel,
        grid=(1,),
        in_specs=[full((n, 1, D_MODEL)), full((1, D_MODEL)), full((D_MODEL, IN_PROJ)),
                  full((2, HGRN_WIDTH))],
        out_specs=[full((n, 6 * 512)), full((n, D_MODEL))],
        out_shape=[jax.ShapeDtypeStruct((n, 6 * 512), F32),
                   jax.ShapeDtypeStruct((n, D_MODEL), F32)],
        compiler_params=pltpu.CompilerParams(
            dimension_semantics=("arbitrary",), vmem_limit_bytes=VMEM_LIMIT_BYTES),
        name="sample_gates",
    )(x, g_mix, w_in, hgrn_lb)


def _sample_mix_kernel(x_ref, gates_ref, o_ref, past_ref, hn_ref, pmix_ref, ps_ref, wout_ref,
                       gc_ref, wq_ref, x1_ref, pool_ref, q_ref, mrg_ref):
    for h in range(HEADS):
        cols = slice(h * DK, (h + 1) * DK)
        o = o_ref[:, cols]
        o = o * lax.rsqrt(jnp.mean(o * o, axis=-1, keepdims=True) + EPS) * hn_ref[:, cols]
        mrg_ref[:, cols] = (o * gates_ref[:, 2048 + h * DK:2048 + (h + 1) * DK]).astype(BF16)
    for gi, w in enumerate(POOL_WINDOWS):
        cols = slice(gi * POOL_GROUP, (gi + 1) * POOL_GROUP)
        u = gates_ref[:, 2560 + gi * POOL_GROUP:2560 + (gi + 1) * POOL_GROUP]
        acc = u
        for j in range(1, w):
            acc = acc + past_ref[0, :, POOL_STATE - j, cols]
        pooled = acc / float(w) - u
        yb = _dot(pooled.astype(BF16), pmix_ref[gi]) * ps_ref[:, cols]
        mrg_ref[:, HGRN_WIDTH + gi * POOL_GROUP:HGRN_WIDTH + (gi + 1) * POOL_GROUP] = yb.astype(BF16)
    for r in range(POOL_STATE - 1):
        pool_ref[0, :, r, :] = past_ref[0, :, r + 1, :]
    pool_ref[0, :, POOL_STATE - 1, :] = gates_ref[:, 2560:3072]
    x1 = x_ref[...] + _dot(mrg_ref[...], wout_ref[...])
    x1_ref[...] = x1
    hb = _rms(x1, gc_ref[...]).astype(BF16)
    q = _dot(hb, wq_ref[...]) * ATTN_SCALE
    for r in range(2 * X_HEADS):
        half, head = divmod(r, X_HEADS)
        c0 = head * X_HEAD_DIM + half * 128
        q_ref[:, r, :] = q[:, c0:c0 + 128]


def _sample_mix(x, gates, o, past, hgrn_norm, pool_mix, pool_scale, w_out, g_cross, w_cq):
    n = x.shape[0]
    full = lambda shape: pl.BlockSpec(shape, lambda i: (0,) * len(shape))
    return pl.pallas_call(
        _sample_mix_kernel,
        grid=(1,),
        in_specs=[full((n, D_MODEL)), full((n, 6 * 512)), full((n, HGRN_WIDTH)),
                  full((1, n, POOL_STATE, POOL_WIDTH)), full((1, HGRN_WIDTH)),
                  full((4, POOL_GROUP, POOL_GROUP)), full((1, POOL_WIDTH)),
                  full((D_MODEL, D_MODEL)), full((1, D_MODEL)), full((D_MODEL, D_MODEL))],
        out_specs=[full((n, D_MODEL)), full((1, n, POOL_STATE, POOL_WIDTH)),
                   full((n, 2 * X_HEADS, 128))],
        out_shape=[jax.ShapeDtypeStruct((n, D_MODEL), F32),
                   jax.ShapeDtypeStruct((1, n, POOL_STATE, POOL_WIDTH), F32),
                   jax.ShapeDtypeStruct((n, 2 * X_HEADS, 128), F32)],
        scratch_shapes=[pltpu.VMEM((n, D_MODEL), BF16)],
        compiler_params=pltpu.CompilerParams(
            dimension_semantics=("arbitrary",), vmem_limit_bytes=VMEM_LIMIT_BYTES),
        name="sample_mix",
    )(x, gates, o, past, hgrn_norm, pool_mix, pool_scale, w_out, g_cross, w_cq)


def _sample_ffn_kernel(x_ref, att_ref, wo_ref, g_ref, w1_ref, w2_ref, gf_ref, o_ref):
    att = jnp.concatenate([att_ref[:, half * X_HEADS + head, :]
                           for head in range(X_HEADS) for half in range(2)], axis=1)
    x2 = x_ref[...] + _dot(att.astype(BF16), wo_ref[...])
    o_ref[:, 0, :] = _ffn_body(x2, g_ref, w1_ref, w2_ref, gf_ref)


def _sample_ffn(x1, att, w_co, g_ffn, w1, w2, g_final):
    n = x1.shape[0]
    full = lambda shape: pl.BlockSpec(shape, lambda i: (0,) * len(shape))
    return pl.pallas_call(
        _sample_ffn_kernel,
        grid=(1,),
        in_specs=[full((n, D_MODEL)), full((n, 2 * X_HEADS, 128)), full((D_MODEL, D_MODEL)),
                  full((1, D_MODEL)), full((D_MODEL, 2 * D_FF)), full((D_FF, D_MODEL)),
                  full((1, D_MODEL))],
        out_specs=full((n, 1, D_MODEL)),
        out_shape=jax.ShapeDtypeStruct((n, 1, D_MODEL), F32),
        compiler_params=pltpu.CompilerParams(
            dimension_semantics=("arbitrary",), vmem_limit_bytes=VMEM_LIMIT_BYTES),
        name="sample_ffn",
    )(x1, att, w_co, g_ffn, w1, w2, g_final)


def kernel(x_prompt, x_sample, mem_prompt, state_hgrn, state_pool, cache_mem_k, cache_mem_v,
           g_mix, w_in, hgrn_lb, hgrn_norm, pool_mix, pool_scale, w_out, g_mem, w_mem_kv,
           g_cross, w_cq, w_co, g_ffn, w_ffn_in, w_ffn_out, g_final):
    nb, L, _ = x_prompt.shape
    ns = x_sample.shape[0]

    w_in_b = w_in[0].astype(BF16)
    w_out_b = w_out[0].astype(BF16)
    w_cq_b = w_cq[0].astype(BF16)
    w_co_b = w_co[0].astype(BF16)
    w1_b = w_ffn_in[0].astype(BF16)
    w2_b = w_ffn_out[0].astype(BF16)
    pmix_b = pool_mix[0].astype(BF16)
    g_final2 = g_final.reshape(1, D_MODEL)

    gates, xs = _sample_gates(x_sample, g_mix, w_in_b, hgrn_lb)
    mem_k, mem_v, kt, vb = _mem_kv(mem_prompt, g_mem, w_mem_kv[0])
    x1, hgrn_p, pool_p = _mixer_prompt(x_prompt, g_mix, w_in_b, hgrn_lb, hgrn_norm, pmix_b,
                                       pool_scale, w_out_b)
    x2, hgrn_s, o_s = _attn_prompt_state_sample(x1, g_cross, w_cq_b, kt, vb, w_co_b, gates,
                                                state_hgrn[0])
    x1s, pool_s, qs = _sample_mix(xs, gates, o_s, state_pool, hgrn_norm, pmix_b, pool_scale,
                                  w_out_b, g_cross, w_cq_b)
    y_prompt, att = _ffn_prompt_attn_sample(
        x2.reshape(nb * L, D_MODEL), g_ffn, w1_b, w2_b, g_final2, qs,
        _split_heads(cache_mem_k[0]), _split_heads(cache_mem_v[0]))
    y_sample = _sample_ffn(x1s, att, w_co_b, g_ffn, w1_b, w2_b, g_final2)

    return (y_prompt.reshape(nb, L, D_MODEL),
            y_sample,
            hgrn_p[None],
            pool_p[None],
            _merge_heads(mem_k)[None],
            _merge_heads(mem_v)[None],
            hgrn_s[None],
            pool_s)
```

```python
import functools

import jax
import jax.numpy as jnp
from jax import lax
from jax.experimental import pallas as pl
from jax.experimental.pallas import tpu as pltpu

F32 = jnp.float32
BF16 = jnp.bfloat16

D_MODEL = 1024
HGRN_WIDTH = 512
HEADS = 4
DK = 128
CHUNK = 64
POOL_WIDTH = 512
POOL_WINDOWS = (2, 4, 8, 16)
POOL_GROUP = 128
POOL_STATE = 15
IN_PROJ = 4 * HGRN_WIDTH + POOL_WIDTH
MEM_LEN = 256
X_HEADS = 4
X_HEAD_DIM = 256
D_FF = 2816
EPS = 1e-6
ATTN_SCALE = X_HEAD_DIM ** -0.5
LOG2_E = 1.4426950408889634
QUERY_SCALE = ATTN_SCALE * LOG2_E

VMEM_LIMIT_BYTES = 56 * 1024 * 1024

MIX_ROWS = 512
MIX_BLOCKS_PER_SEQ = 4
IN_PROJ_PIECE = 256
MIX_FILLER_SCHEDULE = (3, 1, 0) + (1, 1, 1, 0, 1, 1, 1, 0) + (1, 0)
TRI_ROWS = 256
ATTN_ROWS = 1024
FFN_ROWS = 512
STATE_TOKENS = 8
ATTN_TOKENS = 4
FFN_COL_CHUNKS = tuple((c, min(c + 512, D_FF)) for c in range(0, D_FF, 512))


def _dot(a, b):
    return jnp.dot(a, b, preferred_element_type=F32)


def _dot_nt(a, b):
    return lax.dot_general(a, b, (((1,), (1,)), ((), ())), preferred_element_type=F32)


def _dot_tn(a, b):
    return lax.dot_general(a, b, (((0,), (0,)), ((), ())), preferred_element_type=F32)


def _rms(x, g):
    ms = jnp.mean(x * x, axis=-1, keepdims=True)
    return x * lax.rsqrt(ms + EPS) * g


def _sigmoid(x):
    return 1.0 / (1.0 + jnp.exp(-x))


def _lower_bound(lb_ref):
    t = lb_ref[...]
    m = jnp.max(t, axis=0, keepdims=True)
    e = jnp.exp(t - m)
    return e[0:1, :] / jnp.sum(e, axis=0, keepdims=True)


def _gates(proj_q, proj_f, lb):
    qq = proj_q * _sigmoid(proj_q)
    sig = _sigmoid(proj_f)
    fgate = lb + (1.0 - lb) * sig
    kk = (1.0 - lb) * (1.0 - sig)
    return qq, fgate, kk


def _split3(x):
    hi = x.astype(BF16)
    r1 = x - hi.astype(F32)
    mid = r1.astype(BF16)
    lo = (r1 - mid.astype(F32)).astype(BF16)
    return hi, mid, lo


def _memkv_kernel(mem_ref, g_ref, w_ref, k_ref, v_ref, kt_ref, vb_ref, wb_ref):
    @pl.when(pl.program_id(0) == 0)
    def _():
        wb_ref[...] = w_ref[...].astype(BF16)

    h = _rms(mem_ref[0], g_ref[...]).astype(BF16)
    kv = _dot(h, wb_ref[...])
    k = kv[:, :D_MODEL]
    v = kv[:, D_MODEL:]
    for r in range(2 * X_HEADS):
        half, head = divmod(r, X_HEADS)
        c0 = head * X_HEAD_DIM + half * 128
        k_ref[0, :, r, :] = k[:, c0:c0 + 128]
        v_ref[0, :, r, :] = v[:, c0:c0 + 128]
    kt_ref[0] = k.T.astype(BF16)
    vb_ref[0] = v.astype(BF16)


def _mem_kv(mem, g_mem, w_kv):
    nb = mem.shape[0]
    full = lambda shape: pl.BlockSpec(shape, lambda b: (0,) * len(shape))
    split_spec = pl.BlockSpec((1, MEM_LEN, 2 * X_HEADS, 128), lambda b: (b, 0, 0, 0))
    return pl.pallas_call(
        _memkv_kernel,
        grid=(nb,),
        in_specs=[pl.BlockSpec((1, MEM_LEN, D_MODEL), lambda b: (b, 0, 0)),
                  full((1, D_MODEL)), full((D_MODEL, 2 * D_MODEL))],
        out_specs=[split_spec, split_spec,
                   pl.BlockSpec((1, D_MODEL, MEM_LEN), lambda b: (b, 0, 0)),
                   pl.BlockSpec((1, MEM_LEN, D_MODEL), lambda b: (b, 0, 0))],
        out_shape=[jax.ShapeDtypeStruct((nb, MEM_LEN, 2 * X_HEADS, 128), F32),
                   jax.ShapeDtypeStruct((nb, MEM_LEN, 2 * X_HEADS, 128), F32),
                   jax.ShapeDtypeStruct((nb, D_MODEL, MEM_LEN), BF16),
                   jax.ShapeDtypeStruct((nb, MEM_LEN, D_MODEL), BF16)],
        scratch_shapes=[pltpu.VMEM((D_MODEL, 2 * D_MODEL), BF16)],
        compiler_params=pltpu.CompilerParams(
            dimension_semantics=("arbitrary",), vmem_limit_bytes=VMEM_LIMIT_BYTES),
        name="mem_kv",
    )(mem, g_mem, w_kv)


def _mix_block(n, x, proj_ref, out_ref, fillers, lb_ref, hn_ref, pmix_ref, ps_ref, wout_ref,
               st_ref, ext_ref, qq_ref, kk_ref, b_ref, mrg_ref):
    T = MIX_ROWS
    fillers = list(fillers)
    schedule = list(MIX_FILLER_SCHEDULE)
    assert len(schedule) == 5 + T // CHUNK and sum(schedule) == len(fillers)

    def fill():
        for _ in range(schedule.pop(0)):
            fillers.pop(0)()

    l = n % MIX_BLOCKS_PER_SEQ
    first = l == 0
    for h in range(HEADS):
        st_ref[h] = jnp.where(first, 0.0, st_ref[h])
    ext_ref[0:16, :] = jnp.where(first, 0.0, ext_ref[0:16, :])

    fill()
    lb = _lower_bound(lb_ref)
    qq, fgate, kk = _gates(proj_ref[:, 0:512], proj_ref[:, 512:1024], lb)
    qq_ref[...] = qq
    kk_ref[...] = kk
    fill()

    r = lax.broadcasted_iota(jnp.int32, (TRI_ROWS, TRI_ROWS), 0)
    c = lax.broadcasted_iota(jnp.int32, (TRI_ROWS, TRI_ROWS), 1)
    tri = jnp.where((c <= r) & (c >= (r & -CHUNK)), 1.0, 0.0).astype(BF16)
    logf = jnp.log2(fgate)
    for blk in range(T // TRI_ROWS):
        rows = slice(blk * TRI_ROWS, (blk + 1) * TRI_ROWS)
        hi, mid, lo = _split3(logf[rows])
        b_ref[rows, :] = _dot(tri, hi) + _dot(tri, mid) + _dot(tri, lo)
    fill()

    cr = lax.broadcasted_iota(jnp.int32, (CHUNK, CHUNK), 0)
    cc = lax.broadcasted_iota(jnp.int32, (CHUNK, CHUNK), 1)
    causal = cc <= cr
    mid_row = (CHUNK - 1) // 2

    for ci in range(T // CHUNK):
        r0 = ci * CHUNK
        rows = slice(r0, r0 + CHUNK)
        scores, inters, vals = [], [], []
        for h in range(HEADS):
            cols = slice(h * DK, (h + 1) * DK)
            b = b_ref[rows, cols]
            m = b_ref[r0 + mid_row:r0 + mid_row + 1, cols]
            b_end = b_ref[r0 + CHUNK - 1:r0 + CHUNK, cols]
            e1 = jnp.exp2(b - m)
            e2 = jnp.exp2(m - b)
            q1 = qq_ref[rows, cols] * e1
            q0 = q1 * jnp.exp2(m)
            ks = kk_ref[rows, cols] * e2
            k2 = ks * jnp.exp2(b_end - m)
            v = proj_ref[rows, 1024 + h * DK:1024 + (h + 1) * DK].astype(BF16)
            st = st_ref[h]
            scores.append(_dot_nt(q1.astype(BF16), ks.astype(BF16)))
            inters.append(_dot_nt(q0.astype(BF16), st.astype(BF16)))
            st_ref[h] = st * jnp.exp2(b_end) + _dot_tn(v, k2.astype(BF16))
            vals.append(v)
        fill()
        for h in range(HEADS):
            cols = slice(h * DK, (h + 1) * DK)
            a = jnp.where(causal, scores[h], 0.0)
            o = inters[h] + _dot(a.astype(BF16), vals[h])
            o = o * lax.rsqrt(jnp.mean(o * o, axis=-1, keepdims=True) + EPS)
            o = o * hn_ref[:, cols]
            g = proj_ref[rows, 1536 + h * DK:1536 + (h + 1) * DK]
            mrg_ref[rows, cols] = (o * _sigmoid(g)).astype(BF16)

    ext_ref[16:16 + T, :] = proj_ref[:, 2048:2560]
    pos = l * T + lax.broadcasted_iota(jnp.int32, (16, POOL_GROUP), 0)
    for gi, w in enumerate(POOL_WINDOWS):
        if gi % 2 == 0:
            fill()
        cols = slice(gi * POOL_GROUP, (gi + 1) * POOL_GROUP)
        s = ext_ref[:, cols]
        u = s[16:]
        shift = 1
        while shift < w:
            s = s + pltpu.roll(s, shift, axis=0)
            shift *= 2
        acc = s[16:]
        cnt = jnp.minimum(pos + 1, w).astype(F32)
        pooled = jnp.concatenate([acc[0:16] / cnt, acc[16:] * (1.0 / w)], axis=0) - u
        yb = _dot(pooled.astype(BF16), pmix_ref[gi]) * ps_ref[:, cols]
        mrg_ref[:, HGRN_WIDTH + gi * POOL_GROUP:HGRN_WIDTH + (gi + 1) * POOL_GROUP] = yb.astype(BF16)

    ext_ref[0:16, :] = ext_ref[T:T + 16, :]
    out_ref[...] = x + _dot(mrg_ref[...], wout_ref[...])


def _mixer_kernel(xa_ref, xn_ref, g_ref, win_ref, lb_ref, hn_ref, pmix_ref, ps_ref, wout_ref,
                  x1_ref, hst_ref, pst_ref,
                  p0_ref, p1_ref, hb_ref, st_ref, ext_ref, qq_ref, kk_ref, b_ref, mrg_ref):
    T = MIX_ROWS
    g = pl.program_id(0)
    rest = (lb_ref, hn_ref, pmix_ref, ps_ref, wout_ref, st_ref, ext_ref, qq_ref, kk_ref, b_ref, mrg_ref)

    def in_proj_pieces(x_ref, rows, p_ref):
        def prep():
            hb_ref[...] = _rms(x_ref[rows, :], g_ref[...]).astype(BF16)

        def piece(k):
            cols = slice(k * IN_PROJ_PIECE, (k + 1) * IN_PROJ_PIECE)
            p_ref[:, cols] = _dot(hb_ref[...], win_ref[:, cols])

        return [prep] + [functools.partial(piece, k) for k in range(IN_PROJ // IN_PROJ_PIECE)]

    @pl.when(g == 0)
    def _():
        st_ref[...] = jnp.zeros_like(st_ref)
        ext_ref[0:16, :] = jnp.zeros((16, POOL_WIDTH), F32)
        for f in in_proj_pieces(xa_ref, slice(0, T), p0_ref):
            f()

    _mix_block(2 * g, xa_ref[0:T, :], p0_ref, x1_ref.at[0:T, :],
               in_proj_pieces(xa_ref, slice(T, 2 * T), p1_ref), *rest)
    _mix_block(2 * g + 1, xa_ref[T:2 * T, :], p1_ref, x1_ref.at[T:2 * T, :],
               in_proj_pieces(xn_ref, slice(0, T), p0_ref), *rest)

    @pl.when(g % (MIX_BLOCKS_PER_SEQ // 2) == MIX_BLOCKS_PER_SEQ // 2 - 1)
    def _():
        for h in range(HEADS):
            hst_ref[0, h] = st_ref[h].T
        pst_ref[0] = ext_ref[T + 1:T + 16, :]


def _mixer_prompt(x, g_mix, w_in, hgrn_lb, hgrn_norm, pool_mix, pool_scale, w_out):
    nb, L, _ = x.shape
    T = MIX_ROWS
    assert L // T == MIX_BLOCKS_PER_SEQ and MIX_BLOCKS_PER_SEQ % 2 == 0
    n_blocks = nb * MIX_BLOCKS_PER_SEQ
    steps_per_seq = MIX_BLOCKS_PER_SEQ // 2
    x2d = x.reshape(nb * L, D_MODEL)
    full = lambda shape: pl.BlockSpec(shape, lambda g: (0,) * len(shape))
    x1, hst, pst = pl.pallas_call(
        _mixer_kernel,
        grid=(n_blocks // 2,),
        in_specs=[pl.BlockSpec((2 * T, D_MODEL), lambda g: (g, 0)),
                  pl.BlockSpec((T, D_MODEL), lambda g: (jnp.minimum(2 * g + 2, n_blocks - 1), 0)),
                  full((1, D_MODEL)), full((D_MODEL, IN_PROJ)), full((2, HGRN_WIDTH)),
                  full((1, HGRN_WIDTH)), full((4, POOL_GROUP, POOL_GROUP)),
                  full((1, POOL_WIDTH)), full((D_MODEL, D_MODEL))],
        out_specs=[pl.BlockSpec((2 * T, D_MODEL), lambda g: (g, 0)),
                   pl.BlockSpec((1, HEADS, DK, DK), lambda g: (g // steps_per_seq, 0, 0, 0)),
                   pl.BlockSpec((1, POOL_STATE, POOL_WIDTH), lambda g: (g // steps_per_seq, 0, 0))],
        out_shape=[jax.ShapeDtypeStruct((nb * L, D_MODEL), F32),
                   jax.ShapeDtypeStruct((nb, HEADS, DK, DK), F32),
                   jax.ShapeDtypeStruct((nb, POOL_STATE, POOL_WIDTH), F32)],
        scratch_shapes=[pltpu.VMEM((T, IN_PROJ), F32),
                        pltpu.VMEM((T, IN_PROJ), F32),
                        pltpu.VMEM((T, D_MODEL), BF16),
                        pltpu.VMEM((HEADS, DK, DK), F32),
                        pltpu.VMEM((16 + T, POOL_WIDTH), F32),
                        pltpu.VMEM((T, HGRN_WIDTH), F32),
                        pltpu.VMEM((T, HGRN_WIDTH), F32),
                        pltpu.VMEM((T, HGRN_WIDTH), F32),
                        pltpu.VMEM((T, D_MODEL), BF16)],
        compiler_params=pltpu.CompilerParams(
            dimension_semantics=("arbitrary",), vmem_limit_bytes=VMEM_LIMIT_BYTES),
        name="mixer_prompt",
    )(x2d, x2d, g_mix, w_in, hgrn_lb, hgrn_norm, pool_mix, pool_scale, w_out)
    return x1.reshape(nb, L, D_MODEL), hst, pst


def _state_step(q_ref, f_ref, k_ref, v_ref, s_ref, so_ref, o_ref):
    G = STATE_TOKENS
    pad = jnp.zeros((DK - G, DK), F32)
    for h in range(HEADS):
        cols = slice(h * DK, (h + 1) * DK)
        qt = jnp.concatenate([q_ref[:, cols], pad], axis=0).T
        ft = jnp.concatenate([f_ref[:, cols], pad], axis=0).T
        kt = jnp.concatenate([k_ref[:, cols], pad], axis=0).T
        for j in range(G):
            s_new = ft[:, j:j + 1] * s_ref[j, h] + kt[:, j:j + 1] * v_ref[j:j + 1, cols]
            so_ref[j, h] = s_new
            o_ref[j:j + 1, cols] = jnp.sum(qt[:, j:j + 1] * s_new, axis=0, keepdims=True)


def _attn_state_kernel(x_ref, g_ref, wq_ref, kt_ref, vb_ref, wo_ref,
                       sq_ref, sf_ref, sk_ref, sv_ref, s_ref,
                       o_ref, so_ref, oo_ref, att_ref):
    H = ATTN_ROWS // 2
    halves = (slice(0, H), slice(H, 2 * H))

    def query(rows):
        hb = _rms(x_ref[0, rows, :], g_ref[...]).astype(BF16)
        return (_dot(hb, wq_ref[...]) * QUERY_SCALE).astype(BF16)

    def scores(q):
        return [_dot(q[:, h * X_HEAD_DIM:(h + 1) * X_HEAD_DIM],
                     kt_ref[0, h * X_HEAD_DIM:(h + 1) * X_HEAD_DIM, :]) for h in range(X_HEADS)]

    def values(rows, ss):
        for h, s in enumerate(ss):
            cols = slice(h * X_HEAD_DIM, (h + 1) * X_HEAD_DIM)
            e = jnp.exp2(s - jnp.max(s, axis=-1, keepdims=True))
            den = jnp.sum(e, axis=-1, keepdims=True)
            o = _dot(e.astype(BF16), vb_ref[0, :, cols]) / den
            att_ref[rows, cols] = o.astype(BF16)

    def project(rows):
        o_ref[0, rows, :] = x_ref[0, rows, :] + _dot(att_ref[rows, :], wo_ref[...])

    s0 = scores(query(halves[0]))
    q1 = query(halves[1])
    values(halves[0], s0)
    s1 = scores(q1)
    project(halves[0])
    values(halves[1], s1)
    project(halves[1])
    _state_step(sq_ref, sf_ref, sk_ref, sv_ref, s_ref, so_ref, oo_ref)


def _attn_prompt_state_sample(x, g_cross, w_cq, kt, vb, w_co, gates, state):
    nb, L, _ = x.shape
    T = ATTN_ROWS
    G = STATE_TOKENS
    steps_per_seq = L // T
    ns = gates.shape[0]
    assert nb * steps_per_seq == ns // G
    full = lambda shape: pl.BlockSpec(shape, lambda i: (0,) * len(shape))
    col = lambda c: pl.BlockSpec((G, 512), lambda i: (i, c))
    st_spec = pl.BlockSpec((G, HEADS, DK, DK), lambda i: (i, 0, 0, 0))
    x_spec = pl.BlockSpec((1, T, D_MODEL), lambda i: (i // steps_per_seq, i % steps_per_seq, 0))
    return pl.pallas_call(
        _attn_state_kernel,
        grid=(nb * steps_per_seq,),
        in_specs=[x_spec, full((1, D_MODEL)), full((D_MODEL, D_MODEL)),
                  pl.BlockSpec((1, D_MODEL, MEM_LEN), lambda i: (i // steps_per_seq, 0, 0)),
                  pl.BlockSpec((1, MEM_LEN, D_MODEL), lambda i: (i // steps_per_seq, 0, 0)),
                  full((D_MODEL, D_MODEL)),
                  col(0), col(1), col(2), col(3), st_spec],
        out_specs=[x_spec, st_spec, pl.BlockSpec((G, HGRN_WIDTH), lambda i: (i, 0))],
        out_shape=[jax.ShapeDtypeStruct((nb, L, D_MODEL), F32),
                   jax.ShapeDtypeStruct((ns, HEADS, DK, DK), F32),
                   jax.ShapeDtypeStruct((ns, HGRN_WIDTH), F32)],
        scratch_shapes=[pltpu.VMEM((T, D_MODEL), BF16)],
        compiler_params=pltpu.CompilerParams(
            dimension_semantics=("arbitrary",), vmem_limit_bytes=VMEM_LIMIT_BYTES),
        name="attn_prompt_state_sample",
    )(x, g_cross, w_cq, kt, vb, w_co, gates, gates, gates, gates, state)


def _zero_after(x):
    u = lax.bitcast_convert_type(x, jnp.uint32)
    z = lax.shift_right_logical(lax.shift_right_logical(u, jnp.uint32(16)), jnp.uint32(16))
    return lax.bitcast_convert_type(z, F32)


def _ffn_body(x, g_ref, w1_ref, w2_ref, gf_ref, fillers=()):
    hb = _rms(x, g_ref[...]).astype(BF16)

    def up(c0, c1):
        return _dot(hb, w1_ref[:, c0:c1]), _dot(hb, w1_ref[:, D_FF + c0:D_FF + c1])

    def add_to_first_tile(m, z):
        top = jnp.concatenate([m[0:8, 0:128] + z, m[0:8, 128:]], axis=1)
        return jnp.concatenate([top, m[8:, :]], axis=0)

    y = x
    fillers = list(fillers)
    assert len(fillers) <= len(FFN_COL_CHUNKS)
    nxt = up(*FFN_COL_CHUNKS[0])
    for i, (c0, c1) in enumerate(FFN_COL_CHUNKS):
        a, bg = nxt
        if i + 1 < len(FFN_COL_CHUNKS):
            nxt = up(*FFN_COL_CHUNKS[i + 1])
        if fillers:
            bg = add_to_first_tile(bg, _zero_after(fillers.pop(0)()))
        act = (a * _sigmoid(a) * bg).astype(BF16)
        y = y + _dot(act, w2_ref[c0:c1, :])
    return _rms(y, gf_ref[...])


def _split_heads(x):
    lead = x.shape[:-2]
    x = x.reshape(lead + (X_HEADS, 2, 128))
    x = jnp.swapaxes(x, -3, -2)
    return x.reshape(lead + (2 * X_HEADS, 128))


def _merge_heads(x):
    lead = x.shape[:-2]
    x = x.reshape(lead + (2, X_HEADS, 128))
    x = jnp.swapaxes(x, -3, -2)
    return x.reshape(lead + (X_HEADS, X_HEAD_DIM))


def _memory_attention(j, q_ref, k_ref, v_ref, o_ref):
    prod = k_ref[j] * q_ref[j][None]
    s = jnp.sum(prod + pltpu.roll(prod, X_HEADS, axis=1), axis=-1, keepdims=True)
    e = jnp.exp2(s - jnp.max(s, axis=0, keepdims=True))
    den = jnp.sum(e, axis=0)
    o = jnp.sum(e * v_ref[j], axis=0) / den
    o_ref[j] = o
    return o


def _ffn_attn_kernel(x_ref, g_ref, w1_ref, w2_ref, gf_ref, q_ref, k_ref, v_ref, o_ref, att_ref):
    fillers = [functools.partial(_memory_attention, j, q_ref, k_ref, v_ref, att_ref)
               for j in range(ATTN_TOKENS)]
    o_ref[...] = _ffn_body(x_ref[...], g_ref, w1_ref, w2_ref, gf_ref, fillers)


def _ffn_prompt_attn_sample(x, g_ffn, w1, w2, g_final, q, cache_k, cache_v):
    n = x.shape[0]
    T = FFN_ROWS
    G = ATTN_TOKENS
    assert n // T == q.shape[0] // G
    full = lambda shape: pl.BlockSpec(shape, lambda i: (0,) * len(shape))
    kv_spec = pl.BlockSpec((G, MEM_LEN, 2 * X_HEADS, 128), lambda i: (i, 0, 0, 0))
    q_spec = pl.BlockSpec((G, 2 * X_HEADS, 128), lambda i: (i, 0, 0))
    return pl.pallas_call(
        _ffn_attn_kernel,
        grid=(n // T,),
        in_specs=[pl.BlockSpec((T, D_MODEL), lambda i: (i, 0)),
                  full((1, D_MODEL)), full((D_MODEL, 2 * D_FF)), full((D_FF, D_MODEL)),
                  full((1, D_MODEL)), q_spec, kv_spec, kv_spec],
        out_specs=[pl.BlockSpec((T, D_MODEL), lambda i: (i, 0)), q_spec],
        out_shape=[jax.ShapeDtypeStruct((n, D_MODEL), F32),
                   jax.ShapeDtypeStruct((q.shape[0], 2 * X_HEADS, 128), F32)],
        compiler_params=pltpu.CompilerParams(
            dimension_semantics=("arbitrary",), vmem_limit_bytes=VMEM_LIMIT_BYTES),
        name="ffn_prompt_attn_sample",
    )(x, g_ffn, w1, w2, g_final, q, cache_k, cache_v)


def _sample_gates_kernel(x_ref, g_ref, win_ref, lb_ref, o_ref, x2d_ref):
    x = x_ref[:, 0, :]
    x2d_ref[...] = x
    hb = _rms(x, g_ref[...]).astype(BF16)
    proj = _dot(hb, win_ref[...])
    lb = _lower_bound(lb_ref)
    qq, fgate, kk = _gates(proj[:, 0:512], proj[:, 512:1024], lb)
    o_ref[:, 0:512] = qq
    o_ref[:, 512:1024] = fgate
    o_ref[:, 1024:1536] = kk
    o_ref[:, 1536:2048] = proj[:, 1024:1536]
    o_ref[:, 2048:2560] = _sigmoid(proj[:, 1536:2048])
    o_ref[:, 2560:3072] = proj[:, 2048:2560]


def _sample_gates(x, g_mix, w_in, hgrn_lb):
    n = x.shape[0]
    full = lambda shape: pl.BlockSpec(shape, lambda i: (0,) * len(shape))
    return pl.pallas_call(
        _sample_gates_kernel,
        grid=(1,),
        in_specs=[full((n, 1, D_MODEL)), full((1, D_MODEL)), full((D_MODEL, IN_PROJ)),
                  full((2, HGRN_WIDTH))],
        out_specs=[full((n, 6 * 512)), full((n, D_MODEL))],
        out_shape=[jax.ShapeDtypeStruct((n, 6 * 512), F32),
                   jax.ShapeDtypeStruct((n, D_MODEL), F32)],
        compiler_params=pltpu.CompilerParams(
            dimension_semantics=("arbitrary",), vmem_limit_bytes=VMEM_LIMIT_BYTES),
        name="sample_gates",
    )(x, g_mix, w_in, hgrn_lb)


def _sample_mix_kernel(x_ref, gates_ref, o_ref, past_ref, hn_ref, pmix_ref, ps_ref, wout_ref,
                       gc_ref, wq_ref, x1_ref, pool_ref, q_ref, mrg_ref):
    for h in range(HEADS):
        cols = slice(h * DK, (h + 1) * DK)
        o = o_ref[:, cols]
        o = o * lax.rsqrt(jnp.mean(o * o, axis=-1, keepdims=True) + EPS) * hn_ref[:, cols]
        mrg_ref[:, cols] = (o * gates_ref[:, 2048 + h * DK:2048 + (h + 1) * DK]).astype(BF16)
    for gi, w in enumerate(POOL_WINDOWS):
        cols = slice(gi * POOL_GROUP, (gi + 1) * POOL_GROUP)
        u = gates_ref[:, 2560 + gi * POOL_GROUP:2560 + (gi + 1) * POOL_GROUP]
        acc = u
        for j in range(1, w):
            acc = acc + past_ref[POOL_STATE - j, :, cols]
        pooled = acc / float(w) - u
        yb = _dot(pooled.astype(BF16), pmix_ref[gi]) * ps_ref[:, cols]
        mrg_ref[:, HGRN_WIDTH + gi * POOL_GROUP:HGRN_WIDTH + (gi + 1) * POOL_GROUP] = yb.astype(BF16)
    pool_ref[0:POOL_STATE - 1] = past_ref[1:POOL_STATE]
    pool_ref[POOL_STATE - 1] = gates_ref[:, 2560:3072]
    x1 = x_ref[...] + _dot(mrg_ref[...], wout_ref[...])
    x1_ref[...] = x1
    hb = _rms(x1, gc_ref[...]).astype(BF16)
    q = _dot(hb, wq_ref[...]) * QUERY_SCALE
    for r in range(2 * X_HEADS):
        half, head = divmod(r, X_HEADS)
        c0 = head * X_HEAD_DIM + half * 128
        q_ref[:, r, :] = q[:, c0:c0 + 128]


def _sample_mix(x, gates, o, past, hgrn_norm, pool_mix, pool_scale, w_out, g_cross, w_cq):
    n = x.shape[0]
    full = lambda shape: pl.BlockSpec(shape, lambda i: (0,) * len(shape))
    return pl.pallas_call(
        _sample_mix_kernel,
        grid=(1,),
        in_specs=[full((n, D_MODEL)), full((n, 6 * 512)), full((n, HGRN_WIDTH)),
                  full((POOL_STATE, n, POOL_WIDTH)), full((1, HGRN_WIDTH)),
                  full((4, POOL_GROUP, POOL_GROUP)), full((1, POOL_WIDTH)),
                  full((D_MODEL, D_MODEL)), full((1, D_MODEL)), full((D_MODEL, D_MODEL))],
        out_specs=[full((n, D_MODEL)), full((POOL_STATE, n, POOL_WIDTH)),
                   full((n, 2 * X_HEADS, 128))],
        out_shape=[jax.ShapeDtypeStruct((n, D_MODEL), F32),
                   jax.ShapeDtypeStruct((POOL_STATE, n, POOL_WIDTH), F32),
                   jax.ShapeDtypeStruct((n, 2 * X_HEADS, 128), F32)],
        scratch_shapes=[pltpu.VMEM((n, D_MODEL), BF16)],
        compiler_params=pltpu.CompilerParams(
            dimension_semantics=("arbitrary",), vmem_limit_bytes=VMEM_LIMIT_BYTES),
        name="sample_mix",
    )(x, gates, o, past, hgrn_norm, pool_mix, pool_scale, w_out, g_cross, w_cq)


def _sample_ffn_kernel(x_ref, att_ref, wo_ref, g_ref, w1_ref, w2_ref, gf_ref, o_ref):
    att = jnp.concatenate([att_ref[:, half * X_HEADS + head, :]
                           for head in range(X_HEADS) for half in range(2)], axis=1)
    x2 = x_ref[...] + _dot(att.astype(BF16), wo_ref[...])
    o_ref[:, 0, :] = _ffn_body(x2, g_ref, w1_ref, w2_ref, gf_ref)


def _sample_ffn(x1, att, w_co, g_ffn, w1, w2, g_final):
    n = x1.shape[0]
    full = lambda shape: pl.BlockSpec(shape, lambda i: (0,) * len(shape))
    return pl.pallas_call(
        _sample_ffn_kernel,
        grid=(1,),
        in_specs=[full((n, D_MODEL)), full((n, 2 * X_HEADS, 128)), full((D_MODEL, D_MODEL)),
                  full((1, D_MODEL)), full((D_MODEL, 2 * D_FF)), full((D_FF, D_MODEL)),
                  full((1, D_MODEL))],
        out_specs=full((n, 1, D_MODEL)),
        out_shape=jax.ShapeDtypeStruct((n, 1, D_MODEL), F32),
        compiler_params=pltpu.CompilerParams(
            dimension_semantics=("arbitrary",), vmem_limit_bytes=VMEM_LIMIT_BYTES),
        name="sample_ffn",
    )(x1, att, w_co, g_ffn, w1, w2, g_final)


def kernel(x_prompt, x_sample, mem_prompt, state_hgrn, state_pool, cache_mem_k, cache_mem_v,
           g_mix, w_in, hgrn_lb, hgrn_norm, pool_mix, pool_scale, w_out, g_mem, w_mem_kv,
           g_cross, w_cq, w_co, g_ffn, w_ffn_in, w_ffn_out, g_final):
    nb, L, _ = x_prompt.shape
    ns = x_sample.shape[0]

    w_in_b = w_in[0].astype(BF16)
    w_out_b = w_out[0].astype(BF16)
    w_cq_b = w_cq[0].astype(BF16)
    w_co_b = w_co[0].astype(BF16)
    w1_b = w_ffn_in[0].astype(BF16)
    w2_b = w_ffn_out[0].astype(BF16)
    pmix_b = pool_mix[0].astype(BF16)
    g_final2 = g_final.reshape(1, D_MODEL)

    gates, xs = _sample_gates(x_sample, g_mix, w_in_b, hgrn_lb)
    mem_k, mem_v, kt, vb = _mem_kv(mem_prompt, g_mem, w_mem_kv[0])
    x1, hgrn_p, pool_p = _mixer_prompt(x_prompt, g_mix, w_in_b, hgrn_lb, hgrn_norm, pmix_b,
                                       pool_scale, w_out_b)
    x2, hgrn_s, o_s = _attn_prompt_state_sample(x1, g_cross, w_cq_b, kt, vb, w_co_b, gates,
                                                state_hgrn[0])
    x1s, pool_s, qs = _sample_mix(xs, gates, o_s, jnp.swapaxes(state_pool[0], 0, 1), hgrn_norm,
                                  pmix_b, pool_scale, w_out_b, g_cross, w_cq_b)
    y_prompt, att = _ffn_prompt_attn_sample(
        x2.reshape(nb * L, D_MODEL), g_ffn, w1_b, w2_b, g_final2, qs,
        _split_heads(cache_mem_k[0]), _split_heads(cache_mem_v[0]))
    y_sample = _sample_ffn(x1s, att, w_co_b, g_ffn, w1_b, w2_b, g_final2)

    return (y_prompt.reshape(nb, L, D_MODEL),
            y_sample,
            hgrn_p[None],
            pool_p[None],
            _merge_heads(mem_k)[None],
            _merge_heads(mem_v)[None],
            hgrn_s[None],
            jnp.swapaxes(pool_s, 0, 1)[None])
```

```python
import functools

import jax
import jax.numpy as jnp
from jax import lax
from jax.experimental import pallas as pl
from jax.experimental.pallas import tpu as pltpu

F32 = jnp.float32
BF16 = jnp.bfloat16

D_MODEL = 1024
HGRN_WIDTH = 512
HEADS = 4
DK = 128
CHUNK = 64
POOL_WIDTH = 512
POOL_WINDOWS = (2, 4, 8, 16)
POOL_GROUP = 128
POOL_STATE = 15
IN_PROJ = 4 * HGRN_WIDTH + POOL_WIDTH
MEM_LEN = 256
X_HEADS = 4
X_HEAD_DIM = 256
D_FF = 2816
EPS = 1e-6
ATTN_SCALE = X_HEAD_DIM ** -0.5
LOG2_E = 1.4426950408889634
QUERY_SCALE = ATTN_SCALE * LOG2_E

VMEM_LIMIT_BYTES = 56 * 1024 * 1024

MIX_ROWS = 512
MIX_BLOCKS_PER_SEQ = 4
IN_PROJ_PIECE = 256
MIX_FILLER_SCHEDULE = (3, 1, 0) + (1, 1, 1, 0, 1, 1, 1, 0) + (1, 0)
TRI_ROWS = 256
ATTN_ROWS = 1024
SPLIT_PITCH = MEM_LEN + 4
FFN_ROWS = 512
STATE_TOKENS = 8
ATTN_TOKENS = 4
FFN_COL_CHUNKS = tuple((c, min(c + 512, D_FF)) for c in range(0, D_FF, 512))


def _dot(a, b):
    return jnp.dot(a, b, preferred_element_type=F32)


def _dot_nt(a, b):
    return lax.dot_general(a, b, (((1,), (1,)), ((), ())), preferred_element_type=F32)


def _dot_tn(a, b):
    return lax.dot_general(a, b, (((0,), (0,)), ((), ())), preferred_element_type=F32)


def _rms(x, g):
    ms = jnp.mean(x * x, axis=-1, keepdims=True)
    return x * lax.rsqrt(ms + EPS) * g


def _sigmoid(x):
    return 1.0 / (1.0 + jnp.exp(-x))


def _lower_bound(lb_ref):
    t = lb_ref[...]
    m = jnp.max(t, axis=0, keepdims=True)
    e = jnp.exp(t - m)
    return e[0:1, :] / jnp.sum(e, axis=0, keepdims=True)


def _gates(proj_q, proj_f, lb):
    qq = proj_q * _sigmoid(proj_q)
    sig = _sigmoid(proj_f)
    fgate = lb + (1.0 - lb) * sig
    kk = (1.0 - lb) * (1.0 - sig)
    return qq, fgate, kk


def _split3(x):
    hi = x.astype(BF16)
    r1 = x - hi.astype(F32)
    mid = r1.astype(BF16)
    lo = (r1 - mid.astype(F32)).astype(BF16)
    return hi, mid, lo


def _memkv_kernel(mem_ref, g_ref, w_ref, win_ref, wout_ref,
                  k_ref, v_ref, kt_ref, vb_ref, winb_ref, woutb_ref, wb_ref, rows_ref):
    @pl.when(pl.program_id(0) == 0)
    def _():
        wb_ref[...] = w_ref[...].astype(BF16)

    winb_ref[...] = win_ref[...].astype(BF16)
    woutb_ref[...] = wout_ref[...].astype(BF16)

    h = _rms(mem_ref[0], g_ref[...]).astype(BF16)
    kv = _dot(h, wb_ref[...])
    k = kv[:, :D_MODEL]
    v = kv[:, D_MODEL:]
    kt_ref[0] = k.T.astype(BF16)
    vb_ref[0] = v.astype(BF16)
    for val, out_ref in ((k, k_ref), (v, v_ref)):
        for r in range(2 * X_HEADS):
            half, head = divmod(r, X_HEADS)
            c0 = head * X_HEAD_DIM + half * 128
            rows_ref[r * SPLIT_PITCH:r * SPLIT_PITCH + MEM_LEN, :] = val[:, c0:c0 + 128]
        for m in range(MEM_LEN):
            out_ref[0, m] = rows_ref[pl.ds(m, 2 * X_HEADS, stride=SPLIT_PITCH), :]


def _mem_kv(mem, g_mem, w_kv, w_in, w_out):
    nb = mem.shape[0]
    full = lambda shape: pl.BlockSpec(shape, lambda b: (0,) * len(shape))
    split_spec = pl.BlockSpec((1, MEM_LEN, 2 * X_HEADS, 128), lambda b: (b, 0, 0, 0))
    wrows = D_MODEL // nb
    win_spec = pl.BlockSpec((wrows, IN_PROJ), lambda b: (b, 0))
    wout_spec = pl.BlockSpec((wrows, D_MODEL), lambda b: (b, 0))
    return pl.pallas_call(
        _memkv_kernel,
        grid=(nb,),
        in_specs=[pl.BlockSpec((1, MEM_LEN, D_MODEL), lambda b: (b, 0, 0)),
                  full((1, D_MODEL)), full((D_MODEL, 2 * D_MODEL)), win_spec, wout_spec],
        out_specs=[split_spec, split_spec,
                   pl.BlockSpec((1, D_MODEL, MEM_LEN), lambda b: (b, 0, 0)),
                   pl.BlockSpec((1, MEM_LEN, D_MODEL), lambda b: (b, 0, 0)),
                   win_spec, wout_spec],
        out_shape=[jax.ShapeDtypeStruct((nb, MEM_LEN, 2 * X_HEADS, 128), F32),
                   jax.ShapeDtypeStruct((nb, MEM_LEN, 2 * X_HEADS, 128), F32),
                   jax.ShapeDtypeStruct((nb, D_MODEL, MEM_LEN), BF16),
                   jax.ShapeDtypeStruct((nb, MEM_LEN, D_MODEL), BF16),
                   jax.ShapeDtypeStruct((D_MODEL, IN_PROJ), BF16),
                   jax.ShapeDtypeStruct((D_MODEL, D_MODEL), BF16)],
        scratch_shapes=[pltpu.VMEM((D_MODEL, 2 * D_MODEL), BF16),
                        pltpu.VMEM((2 * X_HEADS * SPLIT_PITCH, 128), F32)],
        compiler_params=pltpu.CompilerParams(
            dimension_semantics=("arbitrary",), vmem_limit_bytes=VMEM_LIMIT_BYTES),
        name="mem_kv",
    )(mem, g_mem, w_kv, w_in, w_out)


def _mix_block(n, x, proj_ref, out_ref, fillers, lb_ref, hn_ref, pmix_ref, ps_ref, wout_ref,
               st_ref, ext_ref, qq_ref, kk_ref, b_ref, mrg_ref):
    T = MIX_ROWS
    fillers = list(fillers)
    schedule = list(MIX_FILLER_SCHEDULE)
    assert len(schedule) == 5 + T // CHUNK and sum(schedule) == len(fillers)

    def fill():
        for _ in range(schedule.pop(0)):
            fillers.pop(0)()

    l = n % MIX_BLOCKS_PER_SEQ
    first = l == 0
    for h in range(HEADS):
        st_ref[h] = jnp.where(first, 0.0, st_ref[h])
    ext_ref[0:16, :] = jnp.where(first, 0.0, ext_ref[0:16, :])

    fill()
    lb = _lower_bound(lb_ref)
    qq, fgate, kk = _gates(proj_ref[:, 0:512], proj_ref[:, 512:1024], lb)
    qq_ref[...] = qq
    kk_ref[...] = kk
    fill()

    r = lax.broadcasted_iota(jnp.int32, (TRI_ROWS, TRI_ROWS), 0)
    c = lax.broadcasted_iota(jnp.int32, (TRI_ROWS, TRI_ROWS), 1)
    tri = jnp.where((c <= r) & (c >= (r & -CHUNK)), 1.0, 0.0).astype(BF16)
    logf = jnp.log2(fgate)
    for blk in range(T // TRI_ROWS):
        rows = slice(blk * TRI_ROWS, (blk + 1) * TRI_ROWS)
        hi, mid, lo = _split3(logf[rows])
        b_ref[rows, :] = _dot(tri, hi) + _dot(tri, mid) + _dot(tri, lo)
    fill()

    cr = lax.broadcasted_iota(jnp.int32, (CHUNK, CHUNK), 0)
    cc = lax.broadcasted_iota(jnp.int32, (CHUNK, CHUNK), 1)
    causal = cc <= cr
    mid_row = (CHUNK - 1) // 2

    for ci in range(T // CHUNK):
        r0 = ci * CHUNK
        rows = slice(r0, r0 + CHUNK)
        scores, inters, vals = [], [], []
        for h in range(HEADS):
            cols = slice(h * DK, (h + 1) * DK)
            b = b_ref[rows, cols]
            m = b_ref[r0 + mid_row:r0 + mid_row + 1, cols]
            b_end = b_ref[r0 + CHUNK - 1:r0 + CHUNK, cols]
            e1 = jnp.exp2(b - m)
            e2 = jnp.exp2(m - b)
            q1 = qq_ref[rows, cols] * e1
            q0 = q1 * jnp.exp2(m)
            ks = kk_ref[rows, cols] * e2
            k2 = ks * jnp.exp2(b_end - m)
            v = proj_ref[rows, 1024 + h * DK:1024 + (h + 1) * DK].astype(BF16)
            st = st_ref[h]
            scores.append(_dot_nt(q1.astype(BF16), ks.astype(BF16)))
            inters.append(_dot_nt(q0.astype(BF16), st.astype(BF16)))
            st_ref[h] = st * jnp.exp2(b_end) + _dot_tn(v, k2.astype(BF16))
            vals.append(v)
        fill()
        for h in range(HEADS):
            cols = slice(h * DK, (h + 1) * DK)
            a = jnp.where(causal, scores[h], 0.0)
            o = inters[h] + _dot(a.astype(BF16), vals[h])
            o = o * lax.rsqrt(jnp.mean(o * o, axis=-1, keepdims=True) + EPS)
            o = o * hn_ref[:, cols]
            g = proj_ref[rows, 1536 + h * DK:1536 + (h + 1) * DK]
            mrg_ref[rows, cols] = (o * _sigmoid(g)).astype(BF16)

    ext_ref[16:16 + T, :] = proj_ref[:, 2048:2560]
    pos = l * T + lax.broadcasted_iota(jnp.int32, (16, POOL_GROUP), 0)
    for gi, w in enumerate(POOL_WINDOWS):
        if gi % 2 == 0:
            fill()
        cols = slice(gi * POOL_GROUP, (gi + 1) * POOL_GROUP)
        s = ext_ref[:, cols]
        u = s[16:]
        shift = 1
        while shift < w:
            s = s + pltpu.roll(s, shift, axis=0)
            shift *= 2
        acc = s[16:]
        cnt = jnp.minimum(pos + 1, w).astype(F32)
        pooled = jnp.concatenate([acc[0:16] / cnt, acc[16:] * (1.0 / w)], axis=0) - u
        yb = _dot(pooled.astype(BF16), pmix_ref[gi].astype(BF16)) * ps_ref[:, cols]
        mrg_ref[:, HGRN_WIDTH + gi * POOL_GROUP:HGRN_WIDTH + (gi + 1) * POOL_GROUP] = yb.astype(BF16)

    ext_ref[0:16, :] = ext_ref[T:T + 16, :]
    out_ref[...] = x + _dot(mrg_ref[...], wout_ref[...])


def _mixer_kernel(xa_ref, xn_ref, g_ref, win_ref, lb_ref, hn_ref, pmix_ref, ps_ref, wout_ref,
                  w1f_ref, w2f_ref, wcqf_ref, wcof_ref,
                  x1_ref, hst_ref, pst_ref, w1b_ref, w2b_ref, wcqb_ref, wcob_ref,
                  p0_ref, p1_ref, hb_ref, st_ref, ext_ref, qq_ref, kk_ref, b_ref, mrg_ref):
    T = MIX_ROWS
    g = pl.program_id(0)
    for src, dst in ((w1f_ref, w1b_ref), (w2f_ref, w2b_ref), (wcqf_ref, wcqb_ref), (wcof_ref, wcob_ref)):
        dst[...] = src[...].astype(BF16)
    rest = (lb_ref, hn_ref, pmix_ref, ps_ref, wout_ref, st_ref, ext_ref, qq_ref, kk_ref, b_ref, mrg_ref)

    def in_proj_pieces(x_ref, rows, p_ref):
        def prep():
            hb_ref[...] = _rms(x_ref[rows, :], g_ref[...]).astype(BF16)

        def piece(k):
            cols = slice(k * IN_PROJ_PIECE, (k + 1) * IN_PROJ_PIECE)
            p_ref[:, cols] = _dot(hb_ref[...], win_ref[:, cols])

        return [prep] + [functools.partial(piece, k) for k in range(IN_PROJ // IN_PROJ_PIECE)]

    @pl.when(g == 0)
    def _():
        st_ref[...] = jnp.zeros_like(st_ref)
        ext_ref[0:16, :] = jnp.zeros((16, POOL_WIDTH), F32)
        for f in in_proj_pieces(xa_ref, slice(0, T), p0_ref):
            f()

    _mix_block(2 * g, xa_ref[0:T, :], p0_ref, x1_ref.at[0:T, :],
               in_proj_pieces(xa_ref, slice(T, 2 * T), p1_ref), *rest)
    _mix_block(2 * g + 1, xa_ref[T:2 * T, :], p1_ref, x1_ref.at[T:2 * T, :],
               in_proj_pieces(xn_ref, slice(0, T), p0_ref), *rest)

    @pl.when(g % (MIX_BLOCKS_PER_SEQ // 2) == MIX_BLOCKS_PER_SEQ // 2 - 1)
    def _():
        for h in range(HEADS):
            hst_ref[0, h] = st_ref[h].T
        pst_ref[0] = ext_ref[T + 1:T + 16, :]


def _mixer_prompt(x, g_mix, w_in, hgrn_lb, hgrn_norm, pool_mix, pool_scale, w_out,
                  w_ffn_in, w_ffn_out, w_cq, w_co):
    nb, L, _ = x.shape
    T = MIX_ROWS
    assert L // T == MIX_BLOCKS_PER_SEQ and MIX_BLOCKS_PER_SEQ % 2 == 0
    n_blocks = nb * MIX_BLOCKS_PER_SEQ
    steps = n_blocks // 2
    steps_per_seq = MIX_BLOCKS_PER_SEQ // 2
    x2d = x.reshape(nb * L, D_MODEL)
    full = lambda shape: pl.BlockSpec(shape, lambda g: (0,) * len(shape))
    rows = lambda w: pl.BlockSpec((w.shape[0] // steps, w.shape[1]), lambda g: (g, 0))
    side = (w_ffn_in, w_ffn_out, w_cq, w_co)
    x1, hst, pst, w1b, w2b, wcqb, wcob = pl.pallas_call(
        _mixer_kernel,
        grid=(steps,),
        in_specs=[pl.BlockSpec((2 * T, D_MODEL), lambda g: (g, 0)),
                  pl.BlockSpec((T, D_MODEL), lambda g: (jnp.minimum(2 * g + 2, n_blocks - 1), 0)),
                  full((1, D_MODEL)), full((D_MODEL, IN_PROJ)), full((2, HGRN_WIDTH)),
                  full((1, HGRN_WIDTH)), full((4, POOL_GROUP, POOL_GROUP)),
                  full((1, POOL_WIDTH)), full((D_MODEL, D_MODEL))] + [rows(w) for w in side],
        out_specs=[pl.BlockSpec((2 * T, D_MODEL), lambda g: (g, 0)),
                   pl.BlockSpec((1, HEADS, DK, DK), lambda g: (g // steps_per_seq, 0, 0, 0)),
                   pl.BlockSpec((1, POOL_STATE, POOL_WIDTH), lambda g: (g // steps_per_seq, 0, 0))]
                  + [rows(w) for w in side],
        out_shape=[jax.ShapeDtypeStruct((nb * L, D_MODEL), F32),
                   jax.ShapeDtypeStruct((nb, HEADS, DK, DK), F32),
                   jax.ShapeDtypeStruct((nb, POOL_STATE, POOL_WIDTH), F32)]
                  + [jax.ShapeDtypeStruct(w.shape, BF16) for w in side],
        scratch_shapes=[pltpu.VMEM((T, IN_PROJ), F32),
                        pltpu.VMEM((T, IN_PROJ), F32),
                        pltpu.VMEM((T, D_MODEL), BF16),
                        pltpu.VMEM((HEADS, DK, DK), F32),
                        pltpu.VMEM((16 + T, POOL_WIDTH), F32),
                        pltpu.VMEM((T, HGRN_WIDTH), F32),
                        pltpu.VMEM((T, HGRN_WIDTH), F32),
                        pltpu.VMEM((T, HGRN_WIDTH), F32),
                        pltpu.VMEM((T, D_MODEL), BF16)],
        compiler_params=pltpu.CompilerParams(
            dimension_semantics=("arbitrary",), vmem_limit_bytes=VMEM_LIMIT_BYTES),
        name="mixer_prompt",
    )(x2d, x2d, g_mix, w_in, hgrn_lb, hgrn_norm, pool_mix, pool_scale, w_out, *side)
    return x1.reshape(nb, L, D_MODEL), hst, pst, w1b, w2b, wcqb, wcob


def _state_step(q_ref, f_ref, k_ref, v_ref, s_ref, so_ref, o_ref):
    G = STATE_TOKENS
    pad = jnp.zeros((DK - G, DK), F32)
    for h in range(HEADS):
        cols = slice(h * DK, (h + 1) * DK)
        qt = jnp.concatenate([q_ref[:, cols], pad], axis=0).T
        ft = jnp.concatenate([f_ref[:, cols], pad], axis=0).T
        kt = jnp.concatenate([k_ref[:, cols], pad], axis=0).T
        for j in range(G):
            s_new = ft[:, j:j + 1] * s_ref[j, h] + kt[:, j:j + 1] * v_ref[j:j + 1, cols]
            so_ref[j, h] = s_new
            o_ref[j:j + 1, cols] = jnp.sum(qt[:, j:j + 1] * s_new, axis=0, keepdims=True)


def _attn_state_kernel(x_ref, g_ref, wq_ref, kt_ref, vb_ref, wo_ref,
                       sq_ref, sf_ref, sk_ref, sv_ref, s_ref,
                       o_ref, so_ref, oo_ref, att_ref):
    H = ATTN_ROWS // 2
    halves = (slice(0, H), slice(H, 2 * H))

    def query(rows):
        hb = _rms(x_ref[0, rows, :], g_ref[...]).astype(BF16)
        return (_dot(hb, wq_ref[...]) * QUERY_SCALE).astype(BF16)

    def scores(q):
        return [_dot(q[:, h * X_HEAD_DIM:(h + 1) * X_HEAD_DIM],
                     kt_ref[0, h * X_HEAD_DIM:(h + 1) * X_HEAD_DIM, :]) for h in range(X_HEADS)]

    def values(rows, ss):
        for h, s in enumerate(ss):
            cols = slice(h * X_HEAD_DIM, (h + 1) * X_HEAD_DIM)
            e = jnp.exp2(s - jnp.max(s, axis=-1, keepdims=True))
            den = jnp.sum(e, axis=-1, keepdims=True)
            o = _dot(e.astype(BF16), vb_ref[0, :, cols]) / den
            att_ref[rows, cols] = o.astype(BF16)

    def project(rows):
        o_ref[0, rows, :] = x_ref[0, rows, :] + _dot(att_ref[rows, :], wo_ref[...])

    s0 = scores(query(halves[0]))
    q1 = query(halves[1])
    values(halves[0], s0)
    s1 = scores(q1)
    project(halves[0])
    values(halves[1], s1)
    project(halves[1])
    _state_step(sq_ref, sf_ref, sk_ref, sv_ref, s_ref, so_ref, oo_ref)


def _attn_prompt_state_sample(x, g_cross, w_cq, kt, vb, w_co, gates, state):
    nb, L, _ = x.shape
    T = ATTN_ROWS
    G = STATE_TOKENS
    steps_per_seq = L // T
    ns = gates.shape[0]
    assert nb * steps_per_seq == ns // G
    full = lambda shape: pl.BlockSpec(shape, lambda i: (0,) * len(shape))
    col = lambda c: pl.BlockSpec((G, 512), lambda i: (i, c))
    st_spec = pl.BlockSpec((G, HEADS, DK, DK), lambda i: (i, 0, 0, 0))
    x_spec = pl.BlockSpec((1, T, D_MODEL), lambda i: (i // steps_per_seq, i % steps_per_seq, 0))
    return pl.pallas_call(
        _attn_state_kernel,
        grid=(nb * steps_per_seq,),
        in_specs=[x_spec, full((1, D_MODEL)), full((D_MODEL, D_MODEL)),
                  pl.BlockSpec((1, D_MODEL, MEM_LEN), lambda i: (i // steps_per_seq, 0, 0)),
                  pl.BlockSpec((1, MEM_LEN, D_MODEL), lambda i: (i // steps_per_seq, 0, 0)),
                  full((D_MODEL, D_MODEL)),
                  col(0), col(1), col(2), col(3), st_spec],
        out_specs=[x_spec, st_spec, pl.BlockSpec((G, HGRN_WIDTH), lambda i: (i, 0))],
        out_shape=[jax.ShapeDtypeStruct((nb, L, D_MODEL), F32),
                   jax.ShapeDtypeStruct((ns, HEADS, DK, DK), F32),
                   jax.ShapeDtypeStruct((ns, HGRN_WIDTH), F32)],
        scratch_shapes=[pltpu.VMEM((T, D_MODEL), BF16)],
        compiler_params=pltpu.CompilerParams(
            dimension_semantics=("arbitrary",), vmem_limit_bytes=VMEM_LIMIT_BYTES),
        name="attn_prompt_state_sample",
    )(x, g_cross, w_cq, kt, vb, w_co, gates, gates, gates, gates, state)


def _zero_after(x):
    u = lax.bitcast_convert_type(x, jnp.uint32)
    z = lax.shift_right_logical(lax.shift_right_logical(u, jnp.uint32(16)), jnp.uint32(16))
    return lax.bitcast_convert_type(z, F32)


def _ffn_body(x, g_ref, w1_ref, w2_ref, gf_ref, fillers=()):
    hb = _rms(x, g_ref[...]).astype(BF16)

    def up(c0, c1):
        return _dot(hb, w1_ref[:, c0:c1]), _dot(hb, w1_ref[:, D_FF + c0:D_FF + c1])

    def add_to_first_tile(m, z):
        top = jnp.concatenate([m[0:8, 0:128] + z, m[0:8, 128:]], axis=1)
        return jnp.concatenate([top, m[8:, :]], axis=0)

    y = x
    fillers = list(fillers)
    assert len(fillers) <= len(FFN_COL_CHUNKS)
    nxt = up(*FFN_COL_CHUNKS[0])
    for i, (c0, c1) in enumerate(FFN_COL_CHUNKS):
        a, bg = nxt
        if i + 1 < len(FFN_COL_CHUNKS):
            nxt = up(*FFN_COL_CHUNKS[i + 1])
        if fillers:
            bg = add_to_first_tile(bg, _zero_after(fillers.pop(0)()))
        act = (a * _sigmoid(a) * bg).astype(BF16)
        y = y + _dot(act, w2_ref[c0:c1, :])
    return _rms(y, gf_ref[...])


def _split_heads(x):
    lead = x.shape[:-2]
    x = x.reshape(lead + (X_HEADS, 2, 128))
    x = jnp.swapaxes(x, -3, -2)
    return x.reshape(lead + (2 * X_HEADS, 128))


def _merge_heads(x):
    lead = x.shape[:-2]
    x = x.reshape(lead + (2, X_HEADS, 128))
    x = jnp.swapaxes(x, -3, -2)
    return x.reshape(lead + (X_HEADS, X_HEAD_DIM))


def _memory_attention(j, q_ref, k_ref, v_ref, o_ref):
    prod = k_ref[j] * q_ref[j][None]
    s = jnp.sum(prod + pltpu.roll(prod, X_HEADS, axis=1), axis=-1, keepdims=True)
    e = jnp.exp2(s - jnp.max(s, axis=0, keepdims=True))
    den = jnp.sum(e, axis=0)
    o = jnp.sum(e * v_ref[j], axis=0) / den
    o_ref[j] = o
    return o


def _ffn_attn_kernel(x_ref, g_ref, w1_ref, w2_ref, gf_ref, q_ref, k_ref, v_ref, o_ref, att_ref):
    fillers = [functools.partial(_memory_attention, j, q_ref, k_ref, v_ref, att_ref)
               for j in range(ATTN_TOKENS)]
    o_ref[...] = _ffn_body(x_ref[...], g_ref, w1_ref, w2_ref, gf_ref, fillers)


def _ffn_prompt_attn_sample(x, g_ffn, w1, w2, g_final, q, cache_k, cache_v):
    n = x.shape[0]
    T = FFN_ROWS
    G = ATTN_TOKENS
    assert n // T == q.shape[0] // G
    full = lambda shape: pl.BlockSpec(shape, lambda i: (0,) * len(shape))
    kv_spec = pl.BlockSpec((G, MEM_LEN, 2 * X_HEADS, 128), lambda i: (i, 0, 0, 0))
    q_spec = pl.BlockSpec((G, 2 * X_HEADS, 128), lambda i: (i, 0, 0))
    return pl.pallas_call(
        _ffn_attn_kernel,
        grid=(n // T,),
        in_specs=[pl.BlockSpec((T, D_MODEL), lambda i: (i, 0)),
                  full((1, D_MODEL)), full((D_MODEL, 2 * D_FF)), full((D_FF, D_MODEL)),
                  full((1, D_MODEL)), q_spec, kv_spec, kv_spec],
        out_specs=[pl.BlockSpec((T, D_MODEL), lambda i: (i, 0)), q_spec],
        out_shape=[jax.ShapeDtypeStruct((n, D_MODEL), F32),
                   jax.ShapeDtypeStruct((q.shape[0], 2 * X_HEADS, 128), F32)],
        compiler_params=pltpu.CompilerParams(
            dimension_semantics=("arbitrary",), vmem_limit_bytes=VMEM_LIMIT_BYTES),
        name="ffn_prompt_attn_sample",
    )(x, g_ffn, w1, w2, g_final, q, cache_k, cache_v)


def _sample_gates_kernel(x_ref, g_ref, win_ref, lb_ref, o_ref, x2d_ref):
    x = x_ref[:, 0, :]
    x2d_ref[...] = x
    hb = _rms(x, g_ref[...]).astype(BF16)
    proj = _dot(hb, win_ref[...])
    lb = _lower_bound(lb_ref)
    qq, fgate, kk = _gates(proj[:, 0:512], proj[:, 512:1024], lb)
    o_ref[:, 0:512] = qq
    o_ref[:, 512:1024] = fgate
    o_ref[:, 1024:1536] = kk
    o_ref[:, 1536:2048] = proj[:, 1024:1536]
    o_ref[:, 2048:2560] = _sigmoid(proj[:, 1536:2048])
    o_ref[:, 2560:3072] = proj[:, 2048:2560]


def _sample_gates(x, g_mix, w_in, hgrn_lb):
    n = x.shape[0]
    full = lambda shape: pl.BlockSpec(shape, lambda i: (0,) * len(shape))
    return pl.pallas_call(
        _sample_gates_kernel,
        grid=(1,),
        in_specs=[full((n, 1, D_MODEL)), full((1, D_MODEL)), full((D_MODEL, IN_PROJ)),
                  full((2, HGRN_WIDTH))],
        out_specs=[full((n, 6 * 512)), full((n, D_MODEL))],
        out_shape=[jax.ShapeDtypeStruct((n, 6 * 512), F32),
                   jax.ShapeDtypeStruct((n, D_MODEL), F32)],
        compiler_params=pltpu.CompilerParams(
            dimension_semantics=("arbitrary",), vmem_limit_bytes=VMEM_LIMIT_BYTES),
        name="sample_gates",
    )(x, g_mix, w_in, hgrn_lb)


def _sample_mix_kernel(x_ref, gates_ref, o_ref, past_ref, hn_ref, pmix_ref, ps_ref, wout_ref,
                       gc_ref, wq_ref, x1_ref, pool_ref, q_ref, mrg_ref):
    for h in range(HEADS):
        cols = slice(h * DK, (h + 1) * DK)
        o = o_ref[:, cols]
        o = o * lax.rsqrt(jnp.mean(o * o, axis=-1, keepdims=True) + EPS) * hn_ref[:, cols]
        mrg_ref[:, cols] = (o * gates_ref[:, 2048 + h * DK:2048 + (h + 1) * DK]).astype(BF16)
    for gi, w in enumerate(POOL_WINDOWS):
        cols = slice(gi * POOL_GROUP, (gi + 1) * POOL_GROUP)
        u = gates_ref[:, 2560 + gi * POOL_GROUP:2560 + (gi + 1) * POOL_GROUP]
        acc = u
        for j in range(1, w):
            acc = acc + past_ref[POOL_STATE - j, :, cols]
        pooled = acc / float(w) - u
        yb = _dot(pooled.astype(BF16), pmix_ref[gi].astype(BF16)) * ps_ref[:, cols]
        mrg_ref[:, HGRN_WIDTH + gi * POOL_GROUP:HGRN_WIDTH + (gi + 1) * POOL_GROUP] = yb.astype(BF16)
    pool_ref[0:POOL_STATE - 1] = past_ref[1:POOL_STATE]
    pool_ref[POOL_STATE - 1] = gates_ref[:, 2560:3072]
    x1 = x_ref[...] + _dot(mrg_ref[...], wout_ref[...])
    x1_ref[...] = x1
    hb = _rms(x1, gc_ref[...]).astype(BF16)
    q = _dot(hb, wq_ref[...]) * QUERY_SCALE
    for r in range(2 * X_HEADS):
        half, head = divmod(r, X_HEADS)
        c0 = head * X_HEAD_DIM + half * 128
        q_ref[:, r, :] = q[:, c0:c0 + 128]


def _sample_mix(x, gates, o, past, hgrn_norm, pool_mix, pool_scale, w_out, g_cross, w_cq):
    n = x.shape[0]
    full = lambda shape: pl.BlockSpec(shape, lambda i: (0,) * len(shape))
    return pl.pallas_call(
        _sample_mix_kernel,
        grid=(1,),
        in_specs=[full((n, D_MODEL)), full((n, 6 * 512)), full((n, HGRN_WIDTH)),
                  full((POOL_STATE, n, POOL_WIDTH)), full((1, HGRN_WIDTH)),
                  full((4, POOL_GROUP, POOL_GROUP)), full((1, POOL_WIDTH)),
                  full((D_MODEL, D_MODEL)), full((1, D_MODEL)), full((D_MODEL, D_MODEL))],
        out_specs=[full((n, D_MODEL)), full((POOL_STATE, n, POOL_WIDTH)),
                   full((n, 2 * X_HEADS, 128))],
        out_shape=[jax.ShapeDtypeStruct((n, D_MODEL), F32),
                   jax.ShapeDtypeStruct((POOL_STATE, n, POOL_WIDTH), F32),
                   jax.ShapeDtypeStruct((n, 2 * X_HEADS, 128), F32)],
        scratch_shapes=[pltpu.VMEM((n, D_MODEL), BF16)],
        compiler_params=pltpu.CompilerParams(
            dimension_semantics=("arbitrary",), vmem_limit_bytes=VMEM_LIMIT_BYTES),
        name="sample_mix",
    )(x, gates, o, past, hgrn_norm, pool_mix, pool_scale, w_out, g_cross, w_cq)


def _sample_ffn_kernel(x_ref, att_ref, wo_ref, g_ref, w1_ref, w2_ref, gf_ref, o_ref):
    att = jnp.concatenate([att_ref[:, half * X_HEADS + head, :]
                           for head in range(X_HEADS) for half in range(2)], axis=1)
    x2 = x_ref[...] + _dot(att.astype(BF16), wo_ref[...])
    o_ref[:, 0, :] = _ffn_body(x2, g_ref, w1_ref, w2_ref, gf_ref)


def _sample_ffn(x1, att, w_co, g_ffn, w1, w2, g_final):
    n = x1.shape[0]
    full = lambda shape: pl.BlockSpec(shape, lambda i: (0,) * len(shape))
    return pl.pallas_call(
        _sample_ffn_kernel,
        grid=(1,),
        in_specs=[full((n, D_MODEL)), full((n, 2 * X_HEADS, 128)), full((D_MODEL, D_MODEL)),
                  full((1, D_MODEL)), full((D_MODEL, 2 * D_FF)), full((D_FF, D_MODEL)),
                  full((1, D_MODEL))],
        out_specs=full((n, 1, D_MODEL)),
        out_shape=jax.ShapeDtypeStruct((n, 1, D_MODEL), F32),
        compiler_params=pltpu.CompilerParams(
            dimension_semantics=("arbitrary",), vmem_limit_bytes=VMEM_LIMIT_BYTES),
        name="sample_ffn",
    )(x1, att, w_co, g_ffn, w1, w2, g_final)


def kernel(x_prompt, x_sample, mem_prompt, state_hgrn, state_pool, cache_mem_k, cache_mem_v,
           g_mix, w_in, hgrn_lb, hgrn_norm, pool_mix, pool_scale, w_out, g_mem, w_mem_kv,
           g_cross, w_cq, w_co, g_ffn, w_ffn_in, w_ffn_out, g_final):
    nb, L, _ = x_prompt.shape
    ns = x_sample.shape[0]

    pmix = pool_mix[0]
    g_final2 = g_final.reshape(1, D_MODEL)

    mem_k, mem_v, kt, vb, w_in_b, w_out_b = _mem_kv(mem_prompt, g_mem, w_mem_kv[0], w_in[0], w_out[0])
    gates, xs = _sample_gates(x_sample, g_mix, w_in_b, hgrn_lb)
    x1, hgrn_p, pool_p, w1_b, w2_b, w_cq_b, w_co_b = _mixer_prompt(
        x_prompt, g_mix, w_in_b, hgrn_lb, hgrn_norm, pmix, pool_scale, w_out_b,
        w_ffn_in[0], w_ffn_out[0], w_cq[0], w_co[0])
    x2, hgrn_s, o_s = _attn_prompt_state_sample(x1, g_cross, w_cq_b, kt, vb, w_co_b, gates,
                                                state_hgrn[0])
    x1s, pool_s, qs = _sample_mix(xs, gates, o_s, jnp.swapaxes(state_pool[0], 0, 1), hgrn_norm,
                                  pmix, pool_scale, w_out_b, g_cross, w_cq_b)
    y_prompt, att = _ffn_prompt_attn_sample(
        x2.reshape(nb * L, D_MODEL), g_ffn, w1_b, w2_b, g_final2, qs,
        _split_heads(cache_mem_k[0]), _split_heads(cache_mem_v[0]))
    y_sample = _sample_ffn(x1s, att, w_co_b, g_ffn, w1_b, w2_b, g_final2)

    return (y_prompt.reshape(nb, L, D_MODEL),
            y_sample,
            hgrn_p[None],
            pool_p[None],
            _merge_heads(mem_k)[None],
            _merge_heads(mem_v)[None],
            hgrn_s[None],
            jnp.swapaxes(pool_s, 0, 1)[None])
```

```python
import functools

import jax
import jax.numpy as jnp
from jax import lax
from jax.experimental import pallas as pl
from jax.experimental.pallas import tpu as pltpu

F32 = jnp.float32
BF16 = jnp.bfloat16

D_MODEL = 1024
HGRN_WIDTH = 512
HEADS = 4
DK = 128
CHUNK = 64
POOL_WIDTH = 512
POOL_WINDOWS = (2, 4, 8, 16)
POOL_GROUP = 128
POOL_STATE = 15
IN_PROJ = 4 * HGRN_WIDTH + POOL_WIDTH
MEM_LEN = 256
X_HEADS = 4
X_HEAD_DIM = 256
D_FF = 2816
EPS = 1e-6
ATTN_SCALE = X_HEAD_DIM ** -0.5
LOG2_E = 1.4426950408889634
QUERY_SCALE = ATTN_SCALE * LOG2_E

VMEM_LIMIT_BYTES = 56 * 1024 * 1024

MIX_ROWS = 512
MIX_BLOCKS_PER_SEQ = 4
IN_PROJ_PIECE = 256
MIX_FILLER_SCHEDULE = (3, 0, 0) + (1, 0, 1, 0, 1, 0, 1, 0) + (2, 2)
TRI_ROWS = 256
ATTN_ROWS = 1024
SAMPLE_FFN_COLS = 1408
SPLIT_PITCH = MEM_LEN + 4
FFN_ROWS = 512
STATE_TOKENS = 8
ATTN_TOKENS = 4
FFN_COL_CHUNKS = tuple((c, min(c + 512, D_FF)) for c in range(0, D_FF, 512))


def _dot(a, b):
    return jnp.dot(a, b, preferred_element_type=F32)


def _dot_nt(a, b):
    return lax.dot_general(a, b, (((1,), (1,)), ((), ())), preferred_element_type=F32)


def _dot_tn(a, b):
    return lax.dot_general(a, b, (((0,), (0,)), ((), ())), preferred_element_type=F32)


def _rms(x, g):
    ms = jnp.mean(x * x, axis=-1, keepdims=True)
    return x * lax.rsqrt(ms + EPS) * g


def _sigmoid(x):
    return 1.0 / (1.0 + jnp.exp(-x))


def _lower_bound(lb_ref):
    t = lb_ref[...]
    m = jnp.max(t, axis=0, keepdims=True)
    e = jnp.exp(t - m)
    return e[0:1, :] / jnp.sum(e, axis=0, keepdims=True)


def _gates(proj_q, proj_f, lb):
    qq = proj_q * _sigmoid(proj_q)
    sig = _sigmoid(proj_f)
    fgate = lb + (1.0 - lb) * sig
    kk = (1.0 - lb) * (1.0 - sig)
    return qq, fgate, kk


def _split2(x):
    hi = x.astype(BF16)
    return hi, (x - hi.astype(F32)).astype(BF16)


def _split3(x):
    hi = x.astype(BF16)
    r1 = x - hi.astype(F32)
    mid = r1.astype(BF16)
    lo = (r1 - mid.astype(F32)).astype(BF16)
    return hi, mid, lo


def _memkv_kernel(mem_ref, g_ref, w_ref, win_ref, wout_ref, pmix_ref, ps_ref,
                  k_ref, v_ref, kt_ref, vb_ref, winb_ref, woutb_ref, wb_ref, rows_ref):
    b = pl.program_id(0)
    first_pool_block = HGRN_WIDTH // POOL_GROUP

    @pl.when(b == 0)
    def _():
        wb_ref[...] = w_ref[...].astype(BF16)

    winb_ref[...] = win_ref[...].astype(BF16)

    @pl.when(b < first_pool_block)
    def _():
        woutb_ref[...] = wout_ref[...].astype(BF16)

    @pl.when(b >= first_pool_block)
    def _():
        gi = b - first_pool_block
        a_hi, a_lo = _split2(pmix_ref[gi] * ps_ref[gi])
        w_hi, w_lo = _split2(wout_ref[...])
        woutb_ref[...] = (_dot(a_hi, w_hi) + _dot(a_hi, w_lo) + _dot(a_lo, w_hi)).astype(BF16)

    h = _rms(mem_ref[0], g_ref[...]).astype(BF16)
    kv = _dot(h, wb_ref[...])
    k = kv[:, :D_MODEL]
    v = kv[:, D_MODEL:]
    kt_ref[0] = k.T.astype(BF16)
    vb_ref[0] = v.astype(BF16)
    for val, out_ref in ((k, k_ref), (v, v_ref)):
        for r in range(2 * X_HEADS):
            half, head = divmod(r, X_HEADS)
            c0 = head * X_HEAD_DIM + half * 128
            rows_ref[r * SPLIT_PITCH:r * SPLIT_PITCH + MEM_LEN, :] = val[:, c0:c0 + 128]
        for m in range(MEM_LEN):
            out_ref[0, m] = rows_ref[pl.ds(m, 2 * X_HEADS, stride=SPLIT_PITCH), :]


def _mem_kv(mem, g_mem, w_kv, w_in, w_out, pool_mix, pool_scale):
    nb = mem.shape[0]
    full = lambda shape: pl.BlockSpec(shape, lambda b: (0,) * len(shape))
    split_spec = pl.BlockSpec((1, MEM_LEN, 2 * X_HEADS, 128), lambda b: (b, 0, 0, 0))
    wrows = D_MODEL // nb
    assert wrows == POOL_GROUP
    win_spec = pl.BlockSpec((wrows, IN_PROJ), lambda b: (b, 0))
    wout_spec = pl.BlockSpec((wrows, D_MODEL), lambda b: (b, 0))
    return pl.pallas_call(
        _memkv_kernel,
        grid=(nb,),
        in_specs=[pl.BlockSpec((1, MEM_LEN, D_MODEL), lambda b: (b, 0, 0)),
                  full((1, D_MODEL)), full((D_MODEL, 2 * D_MODEL)), win_spec, wout_spec,
                  full((4, POOL_GROUP, POOL_GROUP)), full((4, 1, POOL_GROUP))],
        out_specs=[split_spec, split_spec,
                   pl.BlockSpec((1, D_MODEL, MEM_LEN), lambda b: (b, 0, 0)),
                   pl.BlockSpec((1, MEM_LEN, D_MODEL), lambda b: (b, 0, 0)),
                   win_spec, wout_spec],
        out_shape=[jax.ShapeDtypeStruct((nb, MEM_LEN, 2 * X_HEADS, 128), F32),
                   jax.ShapeDtypeStruct((nb, MEM_LEN, 2 * X_HEADS, 128), F32),
                   jax.ShapeDtypeStruct((nb, D_MODEL, MEM_LEN), BF16),
                   jax.ShapeDtypeStruct((nb, MEM_LEN, D_MODEL), BF16),
                   jax.ShapeDtypeStruct((D_MODEL, IN_PROJ), BF16),
                   jax.ShapeDtypeStruct((D_MODEL, D_MODEL), BF16)],
        scratch_shapes=[pltpu.VMEM((D_MODEL, 2 * D_MODEL), BF16),
                        pltpu.VMEM((2 * X_HEADS * SPLIT_PITCH, 128), F32)],
        compiler_params=pltpu.CompilerParams(
            dimension_semantics=("arbitrary",), vmem_limit_bytes=VMEM_LIMIT_BYTES),
        name="mem_kv",
    )(mem, g_mem, w_kv, w_in, w_out, pool_mix, pool_scale.reshape(4, 1, POOL_GROUP))


def _mix_block(n, x, proj_ref, out_ref, fillers, lb_ref, hn_ref, wout_ref,
               st_ref, ext_ref, qq_ref, kk_ref, b_ref, mrg_ref):
    T = MIX_ROWS
    fillers = list(fillers)
    schedule = list(MIX_FILLER_SCHEDULE)
    assert len(schedule) == 5 + T // CHUNK and sum(schedule) == len(fillers)

    def fill():
        for _ in range(schedule.pop(0)):
            fillers.pop(0)()

    l = n % MIX_BLOCKS_PER_SEQ
    first = l == 0
    for h in range(HEADS):
        st_ref[h] = jnp.where(first, 0.0, st_ref[h])
    ext_ref[0:16, :] = jnp.where(first, 0.0, ext_ref[0:16, :])

    fill()
    lb = _lower_bound(lb_ref)
    qq, fgate, kk = _gates(proj_ref[:, 0:512], proj_ref[:, 512:1024], lb)
    qq_ref[...] = qq
    kk_ref[...] = kk
    fill()

    r = lax.broadcasted_iota(jnp.int32, (TRI_ROWS, TRI_ROWS), 0)
    c = lax.broadcasted_iota(jnp.int32, (TRI_ROWS, TRI_ROWS), 1)
    tri = jnp.where((c <= r) & (c >= (r & -CHUNK)), 1.0, 0.0).astype(BF16)
    logf = jnp.log2(fgate)
    for blk in range(T // TRI_ROWS):
        rows = slice(blk * TRI_ROWS, (blk + 1) * TRI_ROWS)
        hi, mid, lo = _split3(logf[rows])
        b_ref[rows, :] = _dot(tri, hi) + _dot(tri, mid) + _dot(tri, lo)
    fill()

    cr = lax.broadcasted_iota(jnp.int32, (CHUNK, CHUNK), 0)
    cc = lax.broadcasted_iota(jnp.int32, (CHUNK, CHUNK), 1)
    causal = cc <= cr
    mid_row = (CHUNK - 1) // 2

    for ci in range(T // CHUNK):
        r0 = ci * CHUNK
        rows = slice(r0, r0 + CHUNK)
        scores, inters, vals = [], [], []
        for h in range(HEADS):
            cols = slice(h * DK, (h + 1) * DK)
            b = b_ref[rows, cols]
            m = b_ref[r0 + mid_row:r0 + mid_row + 1, cols]
            b_end = b_ref[r0 + CHUNK - 1:r0 + CHUNK, cols]
            e1 = jnp.exp2(b - m)
            e2 = jnp.exp2(m - b)
            q1 = qq_ref[rows, cols] * e1
            q0 = q1 * jnp.exp2(m)
            ks = kk_ref[rows, cols] * e2
            k2 = ks * jnp.exp2(b_end - m)
            v = proj_ref[rows, 1024 + h * DK:1024 + (h + 1) * DK].astype(BF16)
            st = st_ref[h]
            scores.append(_dot_nt(q1.astype(BF16), ks.astype(BF16)))
            inters.append(_dot_nt(q0.astype(BF16), st.astype(BF16)))
            st_ref[h] = st * jnp.exp2(b_end) + _dot_tn(v, k2.astype(BF16))
            vals.append(v)
        fill()
        for h in range(HEADS):
            cols = slice(h * DK, (h + 1) * DK)
            a = jnp.where(causal, scores[h], 0.0)
            o = inters[h] + _dot(a.astype(BF16), vals[h])
            o = o * lax.rsqrt(jnp.mean(o * o, axis=-1, keepdims=True) + EPS)
            o = o * hn_ref[:, cols]
            g = proj_ref[rows, 1536 + h * DK:1536 + (h + 1) * DK]
            mrg_ref[rows, cols] = (o * _sigmoid(g)).astype(BF16)

    ext_ref[16:16 + T, :] = proj_ref[:, 2048:2560]
    pos = l * T + lax.broadcasted_iota(jnp.int32, (16, POOL_GROUP), 0)
    for gi, w in enumerate(POOL_WINDOWS):
        if gi % 2 == 0:
            fill()
        cols = slice(gi * POOL_GROUP, (gi + 1) * POOL_GROUP)
        s = ext_ref[:, cols]
        u = s[16:]
        shift = 1
        while shift < w:
            s = s + pltpu.roll(s, shift, axis=0)
            shift *= 2
        acc = s[16:]
        cnt = jnp.minimum(pos + 1, w).astype(F32)
        pooled = jnp.concatenate([acc[0:16] / cnt, acc[16:] * (1.0 / w)], axis=0) - u
        mrg_ref[:, HGRN_WIDTH + gi * POOL_GROUP:HGRN_WIDTH + (gi + 1) * POOL_GROUP] = pooled.astype(BF16)

    ext_ref[0:16, :] = ext_ref[T:T + 16, :]
    out_ref[...] = x + _dot(mrg_ref[...], wout_ref[...])


def _mixer_kernel(xa_ref, xn_ref, g_ref, win_ref, lb_ref, hn_ref, wout_ref,
                  w1f_ref, w2f_ref, wcqf_ref, wcof_ref,
                  x1_ref, hst_ref, pst_ref, w1b_ref, w2b_ref, wcqb_ref, wcob_ref,
                  p0_ref, p1_ref, hb_ref, st_ref, ext_ref, qq_ref, kk_ref, b_ref, mrg_ref):
    T = MIX_ROWS
    g = pl.program_id(0)
    for src, dst in ((w1f_ref, w1b_ref), (w2f_ref, w2b_ref), (wcqf_ref, wcqb_ref), (wcof_ref, wcob_ref)):
        dst[...] = src[...].astype(BF16)
    rest = (lb_ref, hn_ref, wout_ref, st_ref, ext_ref, qq_ref, kk_ref, b_ref, mrg_ref)

    def in_proj_pieces(x_ref, rows, p_ref):
        def prep():
            hb_ref[...] = _rms(x_ref[rows, :], g_ref[...]).astype(BF16)

        def piece(k):
            cols = slice(k * IN_PROJ_PIECE, (k + 1) * IN_PROJ_PIECE)
            p_ref[:, cols] = _dot(hb_ref[...], win_ref[:, cols])

        return [prep] + [functools.partial(piece, k) for k in range(IN_PROJ // IN_PROJ_PIECE)]

    @pl.when(g == 0)
    def _():
        st_ref[...] = jnp.zeros_like(st_ref)
        ext_ref[0:16, :] = jnp.zeros((16, POOL_WIDTH), F32)
        for f in in_proj_pieces(xa_ref, slice(0, T), p0_ref):
            f()

    _mix_block(2 * g, xa_ref[0:T, :], p0_ref, x1_ref.at[0:T, :],
               in_proj_pieces(xa_ref, slice(T, 2 * T), p1_ref), *rest)
    _mix_block(2 * g + 1, xa_ref[T:2 * T, :], p1_ref, x1_ref.at[T:2 * T, :],
               in_proj_pieces(xn_ref, slice(0, T), p0_ref), *rest)

    @pl.when(g % (MIX_BLOCKS_PER_SEQ // 2) == MIX_BLOCKS_PER_SEQ // 2 - 1)
    def _():
        for h in range(HEADS):
            hst_ref[0, h] = st_ref[h].T
        pst_ref[0] = ext_ref[T + 1:T + 16, :]


def _mixer_prompt(x, g_mix, w_in, hgrn_lb, hgrn_norm, w_out, w_ffn_in, w_ffn_out, w_cq, w_co):
    nb, L, _ = x.shape
    T = MIX_ROWS
    assert L // T == MIX_BLOCKS_PER_SEQ and MIX_BLOCKS_PER_SEQ % 2 == 0
    n_blocks = nb * MIX_BLOCKS_PER_SEQ
    steps = n_blocks // 2
    steps_per_seq = MIX_BLOCKS_PER_SEQ // 2
    x2d = x.reshape(nb * L, D_MODEL)
    full = lambda shape: pl.BlockSpec(shape, lambda g: (0,) * len(shape))
    rows = lambda w: pl.BlockSpec((w.shape[0] // steps, w.shape[1]), lambda g: (g, 0))
    side = (w_ffn_in, w_ffn_out, w_cq, w_co)
    x1, hst, pst, w1b, w2b, wcqb, wcob = pl.pallas_call(
        _mixer_kernel,
        grid=(steps,),
        in_specs=[pl.BlockSpec((2 * T, D_MODEL), lambda g: (g, 0)),
                  pl.BlockSpec((T, D_MODEL), lambda g: (jnp.minimum(2 * g + 2, n_blocks - 1), 0)),
                  full((1, D_MODEL)), full((D_MODEL, IN_PROJ)), full((2, HGRN_WIDTH)),
                  full((1, HGRN_WIDTH)), full((D_MODEL, D_MODEL))] + [rows(w) for w in side],
        out_specs=[pl.BlockSpec((2 * T, D_MODEL), lambda g: (g, 0)),
                   pl.BlockSpec((1, HEADS, DK, DK), lambda g: (g // steps_per_seq, 0, 0, 0)),
                   pl.BlockSpec((1, POOL_STATE, POOL_WIDTH), lambda g: (g // steps_per_seq, 0, 0))]
                  + [rows(w) for w in side],
        out_shape=[jax.ShapeDtypeStruct((nb * L, D_MODEL), F32),
                   jax.ShapeDtypeStruct((nb, HEADS, DK, DK), F32),
                   jax.ShapeDtypeStruct((nb, POOL_STATE, POOL_WIDTH), F32)]
                  + [jax.ShapeDtypeStruct(w.shape, BF16) for w in side],
        scratch_shapes=[pltpu.VMEM((T, IN_PROJ), F32),
                        pltpu.VMEM((T, IN_PROJ), F32),
                        pltpu.VMEM((T, D_MODEL), BF16),
                        pltpu.VMEM((HEADS, DK, DK), F32),
                        pltpu.VMEM((16 + T, POOL_WIDTH), F32),
                        pltpu.VMEM((T, HGRN_WIDTH), F32),
                        pltpu.VMEM((T, HGRN_WIDTH), F32),
                        pltpu.VMEM((T, HGRN_WIDTH), F32),
                        pltpu.VMEM((T, D_MODEL), BF16)],
        compiler_params=pltpu.CompilerParams(
            dimension_semantics=("arbitrary",), vmem_limit_bytes=VMEM_LIMIT_BYTES),
        name="mixer_prompt",
    )(x2d, x2d, g_mix, w_in, hgrn_lb, hgrn_norm, w_out, *side)
    return x1.reshape(nb, L, D_MODEL), hst, pst, w1b, w2b, wcqb, wcob


def _state_step(q_ref, f_ref, k_ref, v_ref, s_ref, so_ref, o_ref):
    G = STATE_TOKENS
    pad = jnp.zeros((DK - G, DK), F32)
    for h in range(HEADS):
        cols = slice(h * DK, (h + 1) * DK)
        qt = jnp.concatenate([q_ref[:, cols], pad], axis=0).T
        ft = jnp.concatenate([f_ref[:, cols], pad], axis=0).T
        kt = jnp.concatenate([k_ref[:, cols], pad], axis=0).T
        for j in range(G):
            s_new = ft[:, j:j + 1] * s_ref[j, h] + kt[:, j:j + 1] * v_ref[j:j + 1, cols]
            so_ref[j, h] = s_new
            o_ref[j:j + 1, cols] = jnp.sum(qt[:, j:j + 1] * s_new, axis=0, keepdims=True)


def _attn_state_kernel(x_ref, g_ref, wq_ref, kt_ref, vb_ref, wo_ref,
                       sq_ref, sf_ref, sk_ref, sv_ref, s_ref,
                       o_ref, so_ref, oo_ref, att_ref):
    H = ATTN_ROWS // 2
    halves = (slice(0, H), slice(H, 2 * H))

    def query(rows):
        hb = _rms(x_ref[0, rows, :], g_ref[...]).astype(BF16)
        return (_dot(hb, wq_ref[...]) * QUERY_SCALE).astype(BF16)

    def scores(q):
        return [_dot(q[:, h * X_HEAD_DIM:(h + 1) * X_HEAD_DIM],
                     kt_ref[0, h * X_HEAD_DIM:(h + 1) * X_HEAD_DIM, :]) for h in range(X_HEADS)]

    def values(rows, ss):
        for h, s in enumerate(ss):
            cols = slice(h * X_HEAD_DIM, (h + 1) * X_HEAD_DIM)
            e = jnp.exp2(s - jnp.max(s, axis=-1, keepdims=True))
            den = jnp.sum(e, axis=-1, keepdims=True)
            o = _dot(e.astype(BF16), vb_ref[0, :, cols]) / den
            att_ref[rows, cols] = o.astype(BF16)

    def project(rows):
        o_ref[0, rows, :] = x_ref[0, rows, :] + _dot(att_ref[rows, :], wo_ref[...])

    s0 = scores(query(halves[0]))
    q1 = query(halves[1])
    values(halves[0], s0)
    s1 = scores(q1)
    project(halves[0])
    values(halves[1], s1)
    project(halves[1])
    _state_step(sq_ref, sf_ref, sk_ref, sv_ref, s_ref, so_ref, oo_ref)


def _attn_prompt_state_sample(x, g_cross, w_cq, kt, vb, w_co, gates, state):
    nb, L, _ = x.shape
    T = ATTN_ROWS
    G = STATE_TOKENS
    steps_per_seq = L // T
    ns = gates.shape[0]
    assert nb * steps_per_seq == ns // G
    full = lambda shape: pl.BlockSpec(shape, lambda i: (0,) * len(shape))
    col = lambda c: pl.BlockSpec((G, 512), lambda i: (i, c))
    st_spec = pl.BlockSpec((G, HEADS, DK, DK), lambda i: (i, 0, 0, 0))
    x_spec = pl.BlockSpec((1, T, D_MODEL), lambda i: (i // steps_per_seq, i % steps_per_seq, 0))
    return pl.pallas_call(
        _attn_state_kernel,
        grid=(nb * steps_per_seq,),
        in_specs=[x_spec, full((1, D_MODEL)), full((D_MODEL, D_MODEL)),
                  pl.BlockSpec((1, D_MODEL, MEM_LEN), lambda i: (i // steps_per_seq, 0, 0)),
                  pl.BlockSpec((1, MEM_LEN, D_MODEL), lambda i: (i // steps_per_seq, 0, 0)),
                  full((D_MODEL, D_MODEL)),
                  col(0), col(1), col(2), col(3), st_spec],
        out_specs=[x_spec, st_spec, pl.BlockSpec((G, HGRN_WIDTH), lambda i: (i, 0))],
        out_shape=[jax.ShapeDtypeStruct((nb, L, D_MODEL), F32),
                   jax.ShapeDtypeStruct((ns, HEADS, DK, DK), F32),
                   jax.ShapeDtypeStruct((ns, HGRN_WIDTH), F32)],
        scratch_shapes=[pltpu.VMEM((T, D_MODEL), BF16)],
        compiler_params=pltpu.CompilerParams(
            dimension_semantics=("arbitrary",), vmem_limit_bytes=VMEM_LIMIT_BYTES),
        name="attn_prompt_state_sample",
    )(x, g_cross, w_cq, kt, vb, w_co, gates, gates, gates, gates, state)


def _zero_after(x):
    u = lax.bitcast_convert_type(x, jnp.uint32)
    z = lax.shift_right_logical(lax.shift_right_logical(u, jnp.uint32(16)), jnp.uint32(16))
    return lax.bitcast_convert_type(z, F32)


def _ffn_body(x, g_ref, w1_ref, w2_ref, gf_ref, fillers=()):
    hb = _rms(x, g_ref[...]).astype(BF16)

    def up(c0, c1):
        return _dot(hb, w1_ref[:, c0:c1]), _dot(hb, w1_ref[:, D_FF + c0:D_FF + c1])

    def add_to_first_tile(m, z):
        top = jnp.concatenate([m[0:8, 0:128] + z, m[0:8, 128:]], axis=1)
        return jnp.concatenate([top, m[8:, :]], axis=0)

    y = x
    fillers = list(fillers)
    assert len(fillers) <= len(FFN_COL_CHUNKS)
    nxt = up(*FFN_COL_CHUNKS[0])
    for i, (c0, c1) in enumerate(FFN_COL_CHUNKS):
        a, bg = nxt
        if i + 1 < len(FFN_COL_CHUNKS):
            nxt = up(*FFN_COL_CHUNKS[i + 1])
        if fillers:
            bg = add_to_first_tile(bg, _zero_after(fillers.pop(0)()))
        act = (a * _sigmoid(a) * bg).astype(BF16)
        y = y + _dot(act, w2_ref[c0:c1, :])
    return _rms(y, gf_ref[...])


def _split_heads(x):
    lead = x.shape[:-2]
    x = x.reshape(lead + (X_HEADS, 2, 128))
    x = jnp.swapaxes(x, -3, -2)
    return x.reshape(lead + (2 * X_HEADS, 128))


def _merge_heads(x):
    lead = x.shape[:-2]
    x = x.reshape(lead + (2, X_HEADS, 128))
    x = jnp.swapaxes(x, -3, -2)
    return x.reshape(lead + (X_HEADS, X_HEAD_DIM))


def _memory_attention(j, q_ref, k_ref, v_ref, o_ref):
    prod = k_ref[j] * q_ref[j][None]
    s = jnp.sum(prod + pltpu.roll(prod, X_HEADS, axis=1), axis=-1, keepdims=True)
    e = jnp.exp2(s - jnp.max(s, axis=0, keepdims=True))
    den = jnp.sum(e, axis=0)
    o = jnp.sum(e * v_ref[j], axis=0) / den
    o_ref[j] = o
    return o


def _ffn_attn_kernel(x_ref, g_ref, w1_ref, w2_ref, gf_ref, q_ref, k_ref, v_ref, o_ref, att_ref):
    fillers = [functools.partial(_memory_attention, j, q_ref, k_ref, v_ref, att_ref)
               for j in range(ATTN_TOKENS)]
    o_ref[...] = _ffn_body(x_ref[...], g_ref, w1_ref, w2_ref, gf_ref, fillers)


def _ffn_prompt_attn_sample(x, g_ffn, w1, w2, g_final, q, cache_k, cache_v):
    n = x.shape[0]
    T = FFN_ROWS
    G = ATTN_TOKENS
    assert n // T == q.shape[0] // G
    full = lambda shape: pl.BlockSpec(shape, lambda i: (0,) * len(shape))
    kv_spec = pl.BlockSpec((G, MEM_LEN, 2 * X_HEADS, 128), lambda i: (i, 0, 0, 0))
    q_spec = pl.BlockSpec((G, 2 * X_HEADS, 128), lambda i: (i, 0, 0))
    return pl.pallas_call(
        _ffn_attn_kernel,
        grid=(n // T,),
        in_specs=[pl.BlockSpec((T, D_MODEL), lambda i: (i, 0)),
                  full((1, D_MODEL)), full((D_MODEL, 2 * D_FF)), full((D_FF, D_MODEL)),
                  full((1, D_MODEL)), q_spec, kv_spec, kv_spec],
        out_specs=[pl.BlockSpec((T, D_MODEL), lambda i: (i, 0)), q_spec],
        out_shape=[jax.ShapeDtypeStruct((n, D_MODEL), F32),
                   jax.ShapeDtypeStruct((q.shape[0], 2 * X_HEADS, 128), F32)],
        compiler_params=pltpu.CompilerParams(
            dimension_semantics=("arbitrary",), vmem_limit_bytes=VMEM_LIMIT_BYTES),
        name="ffn_prompt_attn_sample",
    )(x, g_ffn, w1, w2, g_final, q, cache_k, cache_v)


def _sample_gates_kernel(x_ref, g_ref, win_ref, lb_ref, o_ref, x2d_ref):
    x = x_ref[:, 0, :]
    x2d_ref[...] = x
    hb = _rms(x, g_ref[...]).astype(BF16)
    proj = _dot(hb, win_ref[...])
    lb = _lower_bound(lb_ref)
    qq, fgate, kk = _gates(proj[:, 0:512], proj[:, 512:1024], lb)
    o_ref[:, 0:512] = qq
    o_ref[:, 512:1024] = fgate
    o_ref[:, 1024:1536] = kk
    o_ref[:, 1536:2048] = proj[:, 1024:1536]
    o_ref[:, 2048:2560] = _sigmoid(proj[:, 1536:2048])
    o_ref[:, 2560:3072] = proj[:, 2048:2560]


def _sample_gates(x, g_mix, w_in, hgrn_lb):
    n = x.shape[0]
    full = lambda shape: pl.BlockSpec(shape, lambda i: (0,) * len(shape))
    return pl.pallas_call(
        _sample_gates_kernel,
        grid=(1,),
        in_specs=[full((n, 1, D_MODEL)), full((1, D_MODEL)), full((D_MODEL, IN_PROJ)),
                  full((2, HGRN_WIDTH))],
        out_specs=[full((n, 6 * 512)), full((n, D_MODEL))],
        out_shape=[jax.ShapeDtypeStruct((n, 6 * 512), F32),
                   jax.ShapeDtypeStruct((n, D_MODEL), F32)],
        compiler_params=pltpu.CompilerParams(
            dimension_semantics=("arbitrary",), vmem_limit_bytes=VMEM_LIMIT_BYTES),
        name="sample_gates",
    )(x, g_mix, w_in, hgrn_lb)


def _sample_mix_kernel(x_ref, gates_ref, o_ref, past_ref, hn_ref, wout_ref,
                       gc_ref, wq_ref, x1_ref, pool_ref, q_ref, mrg_ref):
    for h in range(HEADS):
        cols = slice(h * DK, (h + 1) * DK)
        o = o_ref[:, cols]
        o = o * lax.rsqrt(jnp.mean(o * o, axis=-1, keepdims=True) + EPS) * hn_ref[:, cols]
        mrg_ref[:, cols] = (o * gates_ref[:, 2048 + h * DK:2048 + (h + 1) * DK]).astype(BF16)
    for gi, w in enumerate(POOL_WINDOWS):
        cols = slice(gi * POOL_GROUP, (gi + 1) * POOL_GROUP)
        u = gates_ref[:, 2560 + gi * POOL_GROUP:2560 + (gi + 1) * POOL_GROUP]
        acc = u
        for j in range(1, w):
            acc = acc + past_ref[POOL_STATE - j, :, cols]
        pooled = acc / float(w) - u
        mrg_ref[:, HGRN_WIDTH + gi * POOL_GROUP:HGRN_WIDTH + (gi + 1) * POOL_GROUP] = pooled.astype(BF16)
    pool_ref[0:POOL_STATE - 1] = past_ref[1:POOL_STATE]
    pool_ref[POOL_STATE - 1] = gates_ref[:, 2560:3072]
    x1 = x_ref[...] + _dot(mrg_ref[...], wout_ref[...])
    x1_ref[...] = x1
    hb = _rms(x1, gc_ref[...]).astype(BF16)
    q = _dot(hb, wq_ref[...]) * QUERY_SCALE
    for r in range(2 * X_HEADS):
        half, head = divmod(r, X_HEADS)
        c0 = head * X_HEAD_DIM + half * 128
        q_ref[:, r, :] = q[:, c0:c0 + 128]


def _sample_mix(x, gates, o, past, hgrn_norm, w_out, g_cross, w_cq):
    n = x.shape[0]
    full = lambda shape: pl.BlockSpec(shape, lambda i: (0,) * len(shape))
    return pl.pallas_call(
        _sample_mix_kernel,
        grid=(1,),
        in_specs=[full((n, D_MODEL)), full((n, 6 * 512)), full((n, HGRN_WIDTH)),
                  full((POOL_STATE, n, POOL_WIDTH)), full((1, HGRN_WIDTH)),
                  full((D_MODEL, D_MODEL)), full((1, D_MODEL)), full((D_MODEL, D_MODEL))],
        out_specs=[full((n, D_MODEL)), full((POOL_STATE, n, POOL_WIDTH)),
                   full((n, 2 * X_HEADS, 128))],
        out_shape=[jax.ShapeDtypeStruct((n, D_MODEL), F32),
                   jax.ShapeDtypeStruct((POOL_STATE, n, POOL_WIDTH), F32),
                   jax.ShapeDtypeStruct((n, 2 * X_HEADS, 128), F32)],
        scratch_shapes=[pltpu.VMEM((n, D_MODEL), BF16)],
        compiler_params=pltpu.CompilerParams(
            dimension_semantics=("arbitrary",), vmem_limit_bytes=VMEM_LIMIT_BYTES),
        name="sample_mix",
    )(x, gates, o, past, hgrn_norm, w_out, g_cross, w_cq)


def _sample_ffn_kernel(x_ref, att_ref, wo_ref, g_ref, w1a_ref, w1b_ref, w2_ref, gf_ref, o_ref,
                       hb_ref, y_ref):
    j = pl.program_id(0)

    @pl.when(j == 0)
    def _():
        att = jnp.concatenate([att_ref[:, half * X_HEADS + head, :]
                               for head in range(X_HEADS) for half in range(2)], axis=1)
        x2 = x_ref[...] + _dot(att.astype(BF16), wo_ref[...])
        y_ref[...] = x2
        hb_ref[...] = _rms(x2, g_ref[...]).astype(BF16)

    hb = hb_ref[...]
    a = _dot(hb, w1a_ref[...])
    bg = _dot(hb, w1b_ref[...])
    y_ref[...] += _dot((a * _sigmoid(a) * bg).astype(BF16), w2_ref[...])

    @pl.when(j == pl.num_programs(0) - 1)
    def _():
        o_ref[:, 0, :] = _rms(y_ref[...], gf_ref[...])


def _sample_ffn(x1, att, w_co, g_ffn, w1, w2, g_final):
    n = x1.shape[0]
    full = lambda shape: pl.BlockSpec(shape, lambda j: (0,) * len(shape))
    nblk = D_FF // SAMPLE_FFN_COLS
    return pl.pallas_call(
        _sample_ffn_kernel,
        grid=(nblk,),
        in_specs=[full((n, D_MODEL)), full((n, 2 * X_HEADS, 128)), full((D_MODEL, D_MODEL)),
                  full((1, D_MODEL)),
                  pl.BlockSpec((D_MODEL, SAMPLE_FFN_COLS), lambda j: (0, j)),
                  pl.BlockSpec((D_MODEL, SAMPLE_FFN_COLS), lambda j: (0, nblk + j)),
                  pl.BlockSpec((SAMPLE_FFN_COLS, D_MODEL), lambda j: (j, 0)),
                  full((1, D_MODEL))],
        out_specs=full((n, 1, D_MODEL)),
        out_shape=jax.ShapeDtypeStruct((n, 1, D_MODEL), F32),
        scratch_shapes=[pltpu.VMEM((n, D_MODEL), BF16), pltpu.VMEM((n, D_MODEL), F32)],
        compiler_params=pltpu.CompilerParams(
            dimension_semantics=("arbitrary",), vmem_limit_bytes=VMEM_LIMIT_BYTES),
        name="sample_ffn",
    )(x1, att, w_co, g_ffn, w1, w1, w2, g_final)


def kernel(x_prompt, x_sample, mem_prompt, state_hgrn, state_pool, cache_mem_k, cache_mem_v,
           g_mix, w_in, hgrn_lb, hgrn_norm, pool_mix, pool_scale, w_out, g_mem, w_mem_kv,
           g_cross, w_cq, w_co, g_ffn, w_ffn_in, w_ffn_out, g_final):
    nb, L, _ = x_prompt.shape
    ns = x_sample.shape[0]

    g_final2 = g_final.reshape(1, D_MODEL)

    mem_k, mem_v, kt, vb, w_in_b, w_out_b = _mem_kv(mem_prompt, g_mem, w_mem_kv[0], w_in[0], w_out[0],
                                                    pool_mix[0], pool_scale)
    gates, xs = _sample_gates(x_sample, g_mix, w_in_b, hgrn_lb)
    x1, hgrn_p, pool_p, w1_b, w2_b, w_cq_b, w_co_b = _mixer_prompt(
        x_prompt, g_mix, w_in_b, hgrn_lb, hgrn_norm, w_out_b,
        w_ffn_in[0], w_ffn_out[0], w_cq[0], w_co[0])
    x2, hgrn_s, o_s = _attn_prompt_state_sample(x1, g_cross, w_cq_b, kt, vb, w_co_b, gates,
                                                state_hgrn[0])
    x1s, pool_s, qs = _sample_mix(xs, gates, o_s, jnp.swapaxes(state_pool[0], 0, 1), hgrn_norm,
                                  w_out_b, g_cross, w_cq_b)
    y_prompt, att = _ffn_prompt_attn_sample(
        x2.reshape(nb * L, D_MODEL), g_ffn, w1_b, w2_b, g_final2, qs,
        _split_heads(cache_mem_k[0]), _split_heads(cache_mem_v[0]))
    y_sample = _sample_ffn(x1s, att, w_co_b, g_ffn, w1_b, w2_b, g_final2)

    return (y_prompt.reshape(nb, L, D_MODEL),
            y_sample,
            hgrn_p[None],
            pool_p[None],
            _merge_heads(mem_k)[None],
            _merge_heads(mem_v)[None],
            hgrn_s[None],
            jnp.swapaxes(pool_s, 0, 1)[None])
```

```python
import functools

import jax
import jax.numpy as jnp
from jax import lax
from jax.experimental import pallas as pl
from jax.experimental.pallas import tpu as pltpu

F32 = jnp.float32
BF16 = jnp.bfloat16

D_MODEL = 1024
HGRN_WIDTH = 512
HEADS = 4
DK = 128
CHUNK = 64
POOL_WIDTH = 512
POOL_WINDOWS = (2, 4, 8, 16)
POOL_GROUP = 128
POOL_STATE = 15
IN_PROJ = 4 * HGRN_WIDTH + POOL_WIDTH
MEM_LEN = 256
X_HEADS = 4
X_HEAD_DIM = 256
D_FF = 2816
EPS = 1e-6
ATTN_SCALE = X_HEAD_DIM ** -0.5
LOG2_E = 1.4426950408889634
QUERY_SCALE = ATTN_SCALE * LOG2_E

VMEM_LIMIT_BYTES = 56 * 1024 * 1024

MIX_ROWS = 512
MIX_BLOCKS_PER_SEQ = 4
IN_PROJ_PIECE = 256
MIX_FILLER_SCHEDULE = (3, 0, 0) + (1, 0, 1, 0, 1, 0, 1, 0) + (2, 2)
TRI_ROWS = 256
ATTN_ROWS = 1024
SAMPLE_FFN_COLS = 1408
SPLIT_PITCH = MEM_LEN + 4
FFN_ROWS = 512
STATE_TOKENS = 8
ATTN_TOKENS = 4
FFN_COL_CHUNKS = tuple((c, min(c + 512, D_FF)) for c in range(0, D_FF, 512))


def _dot(a, b):
    return jnp.dot(a, b, preferred_element_type=F32)


def _dot_nt(a, b):
    return lax.dot_general(a, b, (((1,), (1,)), ((), ())), preferred_element_type=F32)


def _dot_tn(a, b):
    return lax.dot_general(a, b, (((0,), (0,)), ((), ())), preferred_element_type=F32)


def _rms(x, g):
    ms = jnp.mean(x * x, axis=-1, keepdims=True)
    return x * lax.rsqrt(ms + EPS) * g


def _sigmoid(x):
    return 1.0 / (1.0 + jnp.exp(-x))


def _lower_bound(lb_ref):
    t = lb_ref[...]
    m = jnp.max(t, axis=0, keepdims=True)
    e = jnp.exp(t - m)
    return e[0:1, :] / jnp.sum(e, axis=0, keepdims=True)


def _gates(proj_q, proj_f, lb):
    qq = proj_q * _sigmoid(proj_q)
    sig = _sigmoid(proj_f)
    fgate = lb + (1.0 - lb) * sig
    kk = (1.0 - lb) * (1.0 - sig)
    return qq, fgate, kk


def _split2(x):
    hi = x.astype(BF16)
    return hi, (x - hi.astype(F32)).astype(BF16)


def _split3(x):
    hi = x.astype(BF16)
    r1 = x - hi.astype(F32)
    mid = r1.astype(BF16)
    lo = (r1 - mid.astype(F32)).astype(BF16)
    return hi, mid, lo


def _memkv_kernel(mem_ref, g_ref, w_ref, win_ref, wout_ref, pmix_ref, ps_ref,
                  k_ref, v_ref, kt_ref, vb_ref, winb_ref, woutb_ref, wb_ref, rows_ref):
    b = pl.program_id(0)
    first_pool_block = HGRN_WIDTH // POOL_GROUP

    @pl.when(b == 0)
    def _():
        wb_ref[...] = w_ref[...].astype(BF16)

    winb_ref[...] = win_ref[...].astype(BF16)

    @pl.when(b < first_pool_block)
    def _():
        woutb_ref[...] = wout_ref[...].astype(BF16)

    @pl.when(b >= first_pool_block)
    def _():
        gi = b - first_pool_block
        a_hi, a_lo = _split2(pmix_ref[gi] * ps_ref[gi])
        w_hi, w_lo = _split2(wout_ref[...])
        woutb_ref[...] = (_dot(a_hi, w_hi) + _dot(a_hi, w_lo) + _dot(a_lo, w_hi)).astype(BF16)

    h = _rms(mem_ref[0], g_ref[...]).astype(BF16)
    kv = _dot(h, wb_ref[...])
    k = kv[:, :D_MODEL]
    v = kv[:, D_MODEL:]
    kt_ref[0] = k.T.astype(BF16)
    vb_ref[0] = v.astype(BF16)
    for val, out_ref in ((k, k_ref), (v, v_ref)):
        for r in range(2 * X_HEADS):
            half, head = divmod(r, X_HEADS)
            c0 = head * X_HEAD_DIM + half * 128
            rows_ref[r * SPLIT_PITCH:r * SPLIT_PITCH + MEM_LEN, :] = val[:, c0:c0 + 128]
        for m in range(MEM_LEN):
            out_ref[0, m] = rows_ref[pl.ds(m, 2 * X_HEADS, stride=SPLIT_PITCH), :]


def _mem_kv(mem, g_mem, w_kv, w_in, w_out, pool_mix, pool_scale):
    nb = mem.shape[0]
    full = lambda shape: pl.BlockSpec(shape, lambda b: (0,) * len(shape))
    split_spec = pl.BlockSpec((1, MEM_LEN, 2 * X_HEADS, 128), lambda b: (b, 0, 0, 0))
    wrows = D_MODEL // nb
    assert wrows == POOL_GROUP
    win_spec = pl.BlockSpec((wrows, IN_PROJ), lambda b: (b, 0))
    wout_spec = pl.BlockSpec((wrows, D_MODEL), lambda b: (b, 0))
    return pl.pallas_call(
        _memkv_kernel,
        grid=(nb,),
        in_specs=[pl.BlockSpec((1, MEM_LEN, D_MODEL), lambda b: (b, 0, 0)),
                  full((1, D_MODEL)), full((D_MODEL, 2 * D_MODEL)), win_spec, wout_spec,
                  full((4, POOL_GROUP, POOL_GROUP)), full((4, 1, POOL_GROUP))],
        out_specs=[split_spec, split_spec,
                   pl.BlockSpec((1, D_MODEL, MEM_LEN), lambda b: (b, 0, 0)),
                   pl.BlockSpec((1, MEM_LEN, D_MODEL), lambda b: (b, 0, 0)),
                   win_spec, wout_spec],
        out_shape=[jax.ShapeDtypeStruct((nb, MEM_LEN, 2 * X_HEADS, 128), F32),
                   jax.ShapeDtypeStruct((nb, MEM_LEN, 2 * X_HEADS, 128), F32),
                   jax.ShapeDtypeStruct((nb, D_MODEL, MEM_LEN), BF16),
                   jax.ShapeDtypeStruct((nb, MEM_LEN, D_MODEL), BF16),
                   jax.ShapeDtypeStruct((D_MODEL, IN_PROJ), BF16),
                   jax.ShapeDtypeStruct((D_MODEL, D_MODEL), BF16)],
        scratch_shapes=[pltpu.VMEM((D_MODEL, 2 * D_MODEL), BF16),
                        pltpu.VMEM((2 * X_HEADS * SPLIT_PITCH, 128), F32)],
        compiler_params=pltpu.CompilerParams(
            dimension_semantics=("arbitrary",), vmem_limit_bytes=VMEM_LIMIT_BYTES),
        name="mem_kv",
    )(mem, g_mem, w_kv, w_in, w_out, pool_mix, pool_scale.reshape(4, 1, POOL_GROUP))


def _mix_block(n, x, proj_ref, out_ref, fillers, lb_ref, hn_ref, wout_ref,
               st_ref, ext_ref, qq_ref, kk_ref, b_ref, mrg_ref):
    T = MIX_ROWS
    fillers = list(fillers)
    schedule = list(MIX_FILLER_SCHEDULE)
    assert len(schedule) == 5 + T // CHUNK and sum(schedule) == len(fillers)

    def fill():
        for _ in range(schedule.pop(0)):
            fillers.pop(0)()

    l = n % MIX_BLOCKS_PER_SEQ
    first = l == 0
    for h in range(HEADS):
        st_ref[h] = jnp.where(first, 0.0, st_ref[h])
    ext_ref[0:16, :] = jnp.where(first, 0.0, ext_ref[0:16, :])

    fill()
    lb = _lower_bound(lb_ref)
    qq, fgate, kk = _gates(proj_ref[:, 0:512], proj_ref[:, 512:1024], lb)
    qq_ref[...] = qq
    kk_ref[...] = kk
    fill()

    r = lax.broadcasted_iota(jnp.int32, (TRI_ROWS, TRI_ROWS), 0)
    c = lax.broadcasted_iota(jnp.int32, (TRI_ROWS, TRI_ROWS), 1)
    tri = jnp.where((c <= r) & (c >= (r & -CHUNK)), 1.0, 0.0).astype(BF16)
    logf = jnp.log2(fgate)
    for blk in range(T // TRI_ROWS):
        rows = slice(blk * TRI_ROWS, (blk + 1) * TRI_ROWS)
        hi, mid, lo = _split3(logf[rows])
        b_ref[rows, :] = _dot(tri, hi) + _dot(tri, mid) + _dot(tri, lo)
    fill()

    cr = lax.broadcasted_iota(jnp.int32, (CHUNK, CHUNK), 0)
    cc = lax.broadcasted_iota(jnp.int32, (CHUNK, CHUNK), 1)
    causal = cc <= cr
    mid_row = (CHUNK - 1) // 2

    for ci in range(T // CHUNK):
        r0 = ci * CHUNK
        rows = slice(r0, r0 + CHUNK)
        scores, inters, vals = [], [], []
        for h in range(HEADS):
            cols = slice(h * DK, (h + 1) * DK)
            b = b_ref[rows, cols]
            m = b_ref[r0 + mid_row:r0 + mid_row + 1, cols]
            b_end = b_ref[r0 + CHUNK - 1:r0 + CHUNK, cols]
            e1 = jnp.exp2(b - m)
            e2 = jnp.exp2(m - b)
            q1 = qq_ref[rows, cols] * e1
            q0 = q1 * jnp.exp2(m)
            ks = kk_ref[rows, cols] * e2
            k2 = ks * jnp.exp2(b_end - m)
            v = proj_ref[rows, 1024 + h * DK:1024 + (h + 1) * DK].astype(BF16)
            st = st_ref[h]
            scores.append(_dot_nt(q1.astype(BF16), ks.astype(BF16)))
            inters.append(_dot_nt(q0.astype(BF16), st.astype(BF16)))
            st_ref[h] = st * jnp.exp2(b_end) + _dot_tn(v, k2.astype(BF16))
            vals.append(v)
        fill()
        for h in range(HEADS):
            cols = slice(h * DK, (h + 1) * DK)
            a = jnp.where(causal, scores[h], 0.0)
            o = inters[h] + _dot(a.astype(BF16), vals[h])
            o = o * lax.rsqrt(jnp.mean(o * o, axis=-1, keepdims=True) + EPS)
            o = o * hn_ref[:, cols]
            g = proj_ref[rows, 1536 + h * DK:1536 + (h + 1) * DK]
            mrg_ref[rows, cols] = (o * _sigmoid(g)).astype(BF16)

    ext_ref[16:16 + T, :] = proj_ref[:, 2048:2560]
    pos = l * T + lax.broadcasted_iota(jnp.int32, (16, POOL_GROUP), 0)
    for gi, w in enumerate(POOL_WINDOWS):
        if gi % 2 == 0:
            fill()
        cols = slice(gi * POOL_GROUP, (gi + 1) * POOL_GROUP)
        s = ext_ref[:, cols]
        u = s[16:]
        shift = 1
        while shift < w:
            s = s + pltpu.roll(s, shift, axis=0)
            shift *= 2
        acc = s[16:]
        cnt = jnp.minimum(pos + 1, w).astype(F32)
        pooled = jnp.concatenate([acc[0:16] / cnt, acc[16:] * (1.0 / w)], axis=0) - u
        mrg_ref[:, HGRN_WIDTH + gi * POOL_GROUP:HGRN_WIDTH + (gi + 1) * POOL_GROUP] = pooled.astype(BF16)

    ext_ref[0:16, :] = ext_ref[T:T + 16, :]
    out_ref[...] = x + _dot(mrg_ref[...], wout_ref[...])


def _mixer_kernel(xa_ref, xn_ref, g_ref, win_ref, lb_ref, hn_ref, wout_ref,
                  w1f_ref,
                  x1_ref, hst_ref, pst_ref, w1b_ref,
                  p0_ref, p1_ref, hb_ref, st_ref, ext_ref, qq_ref, kk_ref, b_ref, mrg_ref):
    T = MIX_ROWS
    g = pl.program_id(0)
    w1b_ref[...] = w1f_ref[...].astype(BF16)
    rest = (lb_ref, hn_ref, wout_ref, st_ref, ext_ref, qq_ref, kk_ref, b_ref, mrg_ref)

    def in_proj_pieces(x_ref, rows, p_ref):
        def prep():
            hb_ref[...] = _rms(x_ref[rows, :], g_ref[...]).astype(BF16)

        def piece(k):
            cols = slice(k * IN_PROJ_PIECE, (k + 1) * IN_PROJ_PIECE)
            p_ref[:, cols] = _dot(hb_ref[...], win_ref[:, cols])

        return [prep] + [functools.partial(piece, k) for k in range(IN_PROJ // IN_PROJ_PIECE)]

    @pl.when(g == 0)
    def _():
        st_ref[...] = jnp.zeros_like(st_ref)
        ext_ref[0:16, :] = jnp.zeros((16, POOL_WIDTH), F32)
        for f in in_proj_pieces(xa_ref, slice(0, T), p0_ref):
            f()

    _mix_block(2 * g, xa_ref[0:T, :], p0_ref, x1_ref.at[0:T, :],
               in_proj_pieces(xa_ref, slice(T, 2 * T), p1_ref), *rest)
    _mix_block(2 * g + 1, xa_ref[T:2 * T, :], p1_ref, x1_ref.at[T:2 * T, :],
               in_proj_pieces(xn_ref, slice(0, T), p0_ref), *rest)

    @pl.when(g % (MIX_BLOCKS_PER_SEQ // 2) == MIX_BLOCKS_PER_SEQ // 2 - 1)
    def _():
        for h in range(HEADS):
            hst_ref[0, h] = st_ref[h].T
        pst_ref[0] = ext_ref[T + 1:T + 16, :]


def _mixer_prompt(x, g_mix, w_in, hgrn_lb, hgrn_norm, w_out, w_ffn_in):
    nb, L, _ = x.shape
    T = MIX_ROWS
    assert L // T == MIX_BLOCKS_PER_SEQ and MIX_BLOCKS_PER_SEQ % 2 == 0
    n_blocks = nb * MIX_BLOCKS_PER_SEQ
    steps = n_blocks // 2
    steps_per_seq = MIX_BLOCKS_PER_SEQ // 2
    x2d = x.reshape(nb * L, D_MODEL)
    full = lambda shape: pl.BlockSpec(shape, lambda g: (0,) * len(shape))
    rows = lambda w: pl.BlockSpec((w.shape[0] // steps, w.shape[1]), lambda g: (g, 0))
    side = (w_ffn_in,)
    x1, hst, pst, w1b = pl.pallas_call(
        _mixer_kernel,
        grid=(steps,),
        in_specs=[pl.BlockSpec((2 * T, D_MODEL), lambda g: (g, 0)),
                  pl.BlockSpec((T, D_MODEL), lambda g: (jnp.minimum(2 * g + 2, n_blocks - 1), 0)),
                  full((1, D_MODEL)), full((D_MODEL, IN_PROJ)), full((2, HGRN_WIDTH)),
                  full((1, HGRN_WIDTH)), full((D_MODEL, D_MODEL))] + [rows(w) for w in side],
        out_specs=[pl.BlockSpec((2 * T, D_MODEL), lambda g: (g, 0)),
                   pl.BlockSpec((1, HEADS, DK, DK), lambda g: (g // steps_per_seq, 0, 0, 0)),
                   pl.BlockSpec((1, POOL_STATE, POOL_WIDTH), lambda g: (g // steps_per_seq, 0, 0))]
                  + [rows(w) for w in side],
        out_shape=[jax.ShapeDtypeStruct((nb * L, D_MODEL), F32),
                   jax.ShapeDtypeStruct((nb, HEADS, DK, DK), F32),
                   jax.ShapeDtypeStruct((nb, POOL_STATE, POOL_WIDTH), F32)]
                  + [jax.ShapeDtypeStruct(w.shape, BF16) for w in side],
        scratch_shapes=[pltpu.VMEM((T, IN_PROJ), F32),
                        pltpu.VMEM((T, IN_PROJ), F32),
                        pltpu.VMEM((T, D_MODEL), BF16),
                        pltpu.VMEM((HEADS, DK, DK), F32),
                        pltpu.VMEM((16 + T, POOL_WIDTH), F32),
                        pltpu.VMEM((T, HGRN_WIDTH), F32),
                        pltpu.VMEM((T, HGRN_WIDTH), F32),
                        pltpu.VMEM((T, HGRN_WIDTH), F32),
                        pltpu.VMEM((T, D_MODEL), BF16)],
        compiler_params=pltpu.CompilerParams(
            dimension_semantics=("arbitrary",), vmem_limit_bytes=VMEM_LIMIT_BYTES),
        name="mixer_prompt",
    )(x2d, x2d, g_mix, w_in, hgrn_lb, hgrn_norm, w_out, *side)
    return x1.reshape(nb, L, D_MODEL), hst, pst, w1b


def _state_step(gates_ref, s_ref, so_ref, o_ref):
    G = STATE_TOKENS
    pad = jnp.zeros((DK - G, DK), F32)
    for h in range(HEADS):
        cols = slice(h * DK, (h + 1) * DK)
        qt = jnp.concatenate([gates_ref[:, h * DK:(h + 1) * DK], pad], axis=0).T
        ft = jnp.concatenate([gates_ref[:, 512 + h * DK:512 + (h + 1) * DK], pad], axis=0).T
        kt = jnp.concatenate([gates_ref[:, 1024 + h * DK:1024 + (h + 1) * DK], pad], axis=0).T
        for j in range(G):
            v = gates_ref[j:j + 1, 1536 + h * DK:1536 + (h + 1) * DK]
            s_new = ft[:, j:j + 1] * s_ref[j, h] + kt[:, j:j + 1] * v
            so_ref[j, h] = s_new
            o_ref[j:j + 1, cols] = jnp.sum(qt[:, j:j + 1] * s_new, axis=0, keepdims=True)


def _attn_state_kernel(x_ref, g_ref, wqf_ref, kt_ref, vb_ref, wof_ref, gates_ref, s_ref, w2f_ref,
                       o_ref, so_ref, oo_ref, wq_ref, wo_ref, w2b_ref, att_ref):
    H = ATTN_ROWS // 2
    halves = (slice(0, H), slice(H, 2 * H))

    @pl.when(pl.program_id(0) == 0)
    def _():
        wq_ref[...] = wqf_ref[...].astype(BF16)
        wo_ref[...] = wof_ref[...].astype(BF16)

    w2b_ref[...] = w2f_ref[...].astype(BF16)

    def query(rows):
        hb = _rms(x_ref[0, rows, :], g_ref[...]).astype(BF16)
        return (_dot(hb, wq_ref[...]) * QUERY_SCALE).astype(BF16)

    def scores(q):
        return [_dot(q[:, h * X_HEAD_DIM:(h + 1) * X_HEAD_DIM],
                     kt_ref[0, h * X_HEAD_DIM:(h + 1) * X_HEAD_DIM, :]) for h in range(X_HEADS)]

    def values(rows, ss):
        for h, s in enumerate(ss):
            cols = slice(h * X_HEAD_DIM, (h + 1) * X_HEAD_DIM)
            e = jnp.exp2(s - jnp.max(s, axis=-1, keepdims=True))
            den = jnp.sum(e, axis=-1, keepdims=True)
            o = _dot(e.astype(BF16), vb_ref[0, :, cols]) / den
            att_ref[rows, cols] = o.astype(BF16)

    def project(rows):
        o_ref[0, rows, :] = x_ref[0, rows, :] + _dot(att_ref[rows, :], wo_ref[...])

    s0 = scores(query(halves[0]))
    q1 = query(halves[1])
    values(halves[0], s0)
    s1 = scores(q1)
    project(halves[0])
    values(halves[1], s1)
    project(halves[1])
    _state_step(gates_ref, s_ref, so_ref, oo_ref)


def _attn_prompt_state_sample(x, g_cross, w_cq, kt, vb, w_co, gates, state, w_ffn_out):
    nb, L, _ = x.shape
    T = ATTN_ROWS
    G = STATE_TOKENS
    steps_per_seq = L // T
    steps = nb * steps_per_seq
    ns = gates.shape[0]
    assert steps == ns // G
    full = lambda shape: pl.BlockSpec(shape, lambda i: (0,) * len(shape))
    st_spec = pl.BlockSpec((G, HEADS, DK, DK), lambda i: (i, 0, 0, 0))
    x_spec = pl.BlockSpec((1, T, D_MODEL), lambda i: (i // steps_per_seq, i % steps_per_seq, 0))
    w2_spec = pl.BlockSpec((D_FF // steps, D_MODEL), lambda i: (i, 0))
    return pl.pallas_call(
        _attn_state_kernel,
        grid=(steps,),
        in_specs=[x_spec, full((1, D_MODEL)), full((D_MODEL, D_MODEL)),
                  pl.BlockSpec((1, D_MODEL, MEM_LEN), lambda i: (i // steps_per_seq, 0, 0)),
                  pl.BlockSpec((1, MEM_LEN, D_MODEL), lambda i: (i // steps_per_seq, 0, 0)),
                  full((D_MODEL, D_MODEL)),
                  pl.BlockSpec((G, 4 * 512), lambda i: (i, 0)), st_spec, w2_spec],
        out_specs=[x_spec, st_spec, pl.BlockSpec((G, HGRN_WIDTH), lambda i: (i, 0)),
                   full((D_MODEL, D_MODEL)), full((D_MODEL, D_MODEL)), w2_spec],
        out_shape=[jax.ShapeDtypeStruct((nb, L, D_MODEL), F32),
                   jax.ShapeDtypeStruct((ns, HEADS, DK, DK), F32),
                   jax.ShapeDtypeStruct((ns, HGRN_WIDTH), F32),
                   jax.ShapeDtypeStruct((D_MODEL, D_MODEL), BF16),
                   jax.ShapeDtypeStruct((D_MODEL, D_MODEL), BF16),
                   jax.ShapeDtypeStruct((D_FF, D_MODEL), BF16)],
        scratch_shapes=[pltpu.VMEM((T, D_MODEL), BF16)],
        compiler_params=pltpu.CompilerParams(
            dimension_semantics=("arbitrary",), vmem_limit_bytes=VMEM_LIMIT_BYTES),
        name="attn_prompt_state_sample",
    )(x, g_cross, w_cq, kt, vb, w_co, gates, state, w_ffn_out)


def _zero_after(x):
    u = lax.bitcast_convert_type(x, jnp.uint32)
    z = lax.shift_right_logical(lax.shift_right_logical(u, jnp.uint32(16)), jnp.uint32(16))
    return lax.bitcast_convert_type(z, F32)


def _ffn_body(x, g_ref, w1_ref, w2_ref, gf_ref, fillers=()):
    hb = _rms(x, g_ref[...]).astype(BF16)

    def up(c0, c1):
        return _dot(hb, w1_ref[:, c0:c1]), _dot(hb, w1_ref[:, D_FF + c0:D_FF + c1])

    def add_to_first_tile(m, z):
        top = jnp.concatenate([m[0:8, 0:128] + z, m[0:8, 128:]], axis=1)
        return jnp.concatenate([top, m[8:, :]], axis=0)

    y = x
    fillers = list(fillers)
    assert len(fillers) <= len(FFN_COL_CHUNKS)
    nxt = up(*FFN_COL_CHUNKS[0])
    for i, (c0, c1) in enumerate(FFN_COL_CHUNKS):
        a, bg = nxt
        if i + 1 < len(FFN_COL_CHUNKS):
            nxt = up(*FFN_COL_CHUNKS[i + 1])
        if fillers:
            bg = add_to_first_tile(bg, _zero_after(fillers.pop(0)()))
        act = (a * _sigmoid(a) * bg).astype(BF16)
        y = y + _dot(act, w2_ref[c0:c1, :])
    return _rms(y, gf_ref[...])


def _split_heads(x):
    lead = x.shape[:-2]
    x = x.reshape(lead + (X_HEADS, 2, 128))
    x = jnp.swapaxes(x, -3, -2)
    return x.reshape(lead + (2 * X_HEADS, 128))


def _merge_heads(x):
    lead = x.shape[:-2]
    x = x.reshape(lead + (2, X_HEADS, 128))
    x = jnp.swapaxes(x, -3, -2)
    return x.reshape(lead + (X_HEADS, X_HEAD_DIM))


def _memory_attention(j, base, q_ref, k_ref, v_ref, o_ref):
    prod = k_ref[j] * q_ref[base + j][None]
    s = jnp.sum(prod + pltpu.roll(prod, X_HEADS, axis=1), axis=-1, keepdims=True)
    e = jnp.exp2(s - jnp.max(s, axis=0, keepdims=True))
    den = jnp.sum(e, axis=0)
    o = jnp.sum(e * v_ref[j], axis=0) / den
    o_ref[base + j] = o
    return o


def _ffn_attn_kernel(x_ref, g_ref, w1_ref, w2_ref, gf_ref, q_ref, k_ref, v_ref, o_ref, att_ref):
    base = pl.program_id(0) * ATTN_TOKENS
    fillers = [functools.partial(_memory_attention, j, base, q_ref, k_ref, v_ref, att_ref)
               for j in range(ATTN_TOKENS)]
    o_ref[...] = _ffn_body(x_ref[...], g_ref, w1_ref, w2_ref, gf_ref, fillers)


def _ffn_prompt_attn_sample(x, g_ffn, w1, w2, g_final, q, cache_k, cache_v):
    n = x.shape[0]
    T = FFN_ROWS
    G = ATTN_TOKENS
    assert n // T == q.shape[0] // G
    full = lambda shape: pl.BlockSpec(shape, lambda i: (0,) * len(shape))
    kv_spec = pl.BlockSpec((G, MEM_LEN, 2 * X_HEADS, 128), lambda i: (i, 0, 0, 0))
    q_spec = full((q.shape[0], 2 * X_HEADS, 128))
    return pl.pallas_call(
        _ffn_attn_kernel,
        grid=(n // T,),
        in_specs=[pl.BlockSpec((T, D_MODEL), lambda i: (i, 0)),
                  full((1, D_MODEL)), full((D_MODEL, 2 * D_FF)), full((D_FF, D_MODEL)),
                  full((1, D_MODEL)), q_spec, kv_spec, kv_spec],
        out_specs=[pl.BlockSpec((T, D_MODEL), lambda i: (i, 0)), q_spec],
        out_shape=[jax.ShapeDtypeStruct((n, D_MODEL), F32),
                   jax.ShapeDtypeStruct((q.shape[0], 2 * X_HEADS, 128), F32)],
        compiler_params=pltpu.CompilerParams(
            dimension_semantics=("arbitrary",), vmem_limit_bytes=VMEM_LIMIT_BYTES),
        name="ffn_prompt_attn_sample",
    )(x, g_ffn, w1, w2, g_final, q, cache_k, cache_v)


def _sample_gates_kernel(x_ref, g_ref, win_ref, lb_ref, o_ref, x2d_ref):
    x = x_ref[:, 0, :]
    x2d_ref[...] = x
    hb = _rms(x, g_ref[...]).astype(BF16)
    proj = _dot(hb, win_ref[...])
    lb = _lower_bound(lb_ref)
    qq, fgate, kk = _gates(proj[:, 0:512], proj[:, 512:1024], lb)
    o_ref[:, 0:512] = qq
    o_ref[:, 512:1024] = fgate
    o_ref[:, 1024:1536] = kk
    o_ref[:, 1536:2048] = proj[:, 1024:1536]
    o_ref[:, 2048:2560] = _sigmoid(proj[:, 1536:2048])
    o_ref[:, 2560:3072] = proj[:, 2048:2560]


def _sample_gates(x, g_mix, w_in, hgrn_lb):
    n = x.shape[0]
    full = lambda shape: pl.BlockSpec(shape, lambda i: (0,) * len(shape))
    return pl.pallas_call(
        _sample_gates_kernel,
        grid=(1,),
        in_specs=[full((n, 1, D_MODEL)), full((1, D_MODEL)), full((D_MODEL, IN_PROJ)),
                  full((2, HGRN_WIDTH))],
        out_specs=[full((n, 6 * 512)), full((n, D_MODEL))],
        out_shape=[jax.ShapeDtypeStruct((n, 6 * 512), F32),
                   jax.ShapeDtypeStruct((n, D_MODEL), F32)],
        compiler_params=pltpu.CompilerParams(
            dimension_semantics=("arbitrary",), vmem_limit_bytes=VMEM_LIMIT_BYTES),
        name="sample_gates",
    )(x, g_mix, w_in, hgrn_lb)


def _sample_mix_kernel(x_ref, gates_ref, o_ref, past_ref, hn_ref, wout_ref,
                       gc_ref, wq_ref, x1_ref, pool_ref, q_ref, mrg_ref):
    for h in range(HEADS):
        cols = slice(h * DK, (h + 1) * DK)
        o = o_ref[:, cols]
        o = o * lax.rsqrt(jnp.mean(o * o, axis=-1, keepdims=True) + EPS) * hn_ref[:, cols]
        mrg_ref[:, cols] = (o * gates_ref[:, 2048 + h * DK:2048 + (h + 1) * DK]).astype(BF16)
    for gi, w in enumerate(POOL_WINDOWS):
        cols = slice(gi * POOL_GROUP, (gi + 1) * POOL_GROUP)
        u = gates_ref[:, 2560 + gi * POOL_GROUP:2560 + (gi + 1) * POOL_GROUP]
        acc = u
        for j in range(1, w):
            acc = acc + past_ref[POOL_STATE - j, :, cols]
        pooled = acc / float(w) - u
        mrg_ref[:, HGRN_WIDTH + gi * POOL_GROUP:HGRN_WIDTH + (gi + 1) * POOL_GROUP] = pooled.astype(BF16)
    pool_ref[0:POOL_STATE - 1] = past_ref[1:POOL_STATE]
    pool_ref[POOL_STATE - 1] = gates_ref[:, 2560:3072]
    x1 = x_ref[...] + _dot(mrg_ref[...], wout_ref[...])
    x1_ref[...] = x1
    hb = _rms(x1, gc_ref[...]).astype(BF16)
    q = _dot(hb, wq_ref[...]) * QUERY_SCALE
    for r in range(2 * X_HEADS):
        half, head = divmod(r, X_HEADS)
        c0 = head * X_HEAD_DIM + half * 128
        q_ref[:, r, :] = q[:, c0:c0 + 128]


def _sample_mix(x, gates, o, past, hgrn_norm, w_out, g_cross, w_cq):
    n = x.shape[0]
    full = lambda shape: pl.BlockSpec(shape, lambda i: (0,) * len(shape))
    return pl.pallas_call(
        _sample_mix_kernel,
        grid=(1,),
        in_specs=[full((n, D_MODEL)), full((n, 6 * 512)), full((n, HGRN_WIDTH)),
                  full((POOL_STATE, n, POOL_WIDTH)), full((1, HGRN_WIDTH)),
                  full((D_MODEL, D_MODEL)), full((1, D_MODEL)), full((D_MODEL, D_MODEL))],
        out_specs=[full((n, D_MODEL)), full((POOL_STATE, n, POOL_WIDTH)),
                   full((n, 2 * X_HEADS, 128))],
        out_shape=[jax.ShapeDtypeStruct((n, D_MODEL), F32),
                   jax.ShapeDtypeStruct((POOL_STATE, n, POOL_WIDTH), F32),
                   jax.ShapeDtypeStruct((n, 2 * X_HEADS, 128), F32)],
        scratch_shapes=[pltpu.VMEM((n, D_MODEL), BF16)],
        compiler_params=pltpu.CompilerParams(
            dimension_semantics=("arbitrary",), vmem_limit_bytes=VMEM_LIMIT_BYTES),
        name="sample_mix",
    )(x, gates, o, past, hgrn_norm, w_out, g_cross, w_cq)


def _sample_ffn_kernel(x_ref, att_ref, wo_ref, g_ref, w1a_ref, w1b_ref, w2_ref, gf_ref, o_ref,
                       hb_ref, y_ref):
    j = pl.program_id(0)

    @pl.when(j == 0)
    def _():
        att = jnp.concatenate([att_ref[:, half * X_HEADS + head, :]
                               for head in range(X_HEADS) for half in range(2)], axis=1)
        x2 = x_ref[...] + _dot(att.astype(BF16), wo_ref[...])
        y_ref[...] = x2
        hb_ref[...] = _rms(x2, g_ref[...]).astype(BF16)

    hb = hb_ref[...]
    a = _dot(hb, w1a_ref[...])
    bg = _dot(hb, w1b_ref[...])
    y_ref[...] += _dot((a * _sigmoid(a) * bg).astype(BF16), w2_ref[...])

    @pl.when(j == pl.num_programs(0) - 1)
    def _():
        o_ref[:, 0, :] = _rms(y_ref[...], gf_ref[...])


def _sample_ffn(x1, att, w_co, g_ffn, w1, w2, g_final):
    n = x1.shape[0]
    full = lambda shape: pl.BlockSpec(shape, lambda j: (0,) * len(shape))
    nblk = D_FF // SAMPLE_FFN_COLS
    return pl.pallas_call(
        _sample_ffn_kernel,
        grid=(nblk,),
        in_specs=[full((n, D_MODEL)), full((n, 2 * X_HEADS, 128)), full((D_MODEL, D_MODEL)),
                  full((1, D_MODEL)),
                  pl.BlockSpec((D_MODEL, SAMPLE_FFN_COLS), lambda j: (0, j)),
                  pl.BlockSpec((D_MODEL, SAMPLE_FFN_COLS), lambda j: (0, nblk + j)),
                  pl.BlockSpec((SAMPLE_FFN_COLS, D_MODEL), lambda j: (j, 0)),
                  full((1, D_MODEL))],
        out_specs=full((n, 1, D_MODEL)),
        out_shape=jax.ShapeDtypeStruct((n, 1, D_MODEL), F32),
        scratch_shapes=[pltpu.VMEM((n, D_MODEL), BF16), pltpu.VMEM((n, D_MODEL), F32)],
        compiler_params=pltpu.CompilerParams(
            dimension_semantics=("arbitrary",), vmem_limit_bytes=VMEM_LIMIT_BYTES),
        name="sample_ffn",
    )(x1, att, w_co, g_ffn, w1, w1, w2, g_final)


def kernel(x_prompt, x_sample, mem_prompt, state_hgrn, state_pool, cache_mem_k, cache_mem_v,
           g_mix, w_in, hgrn_lb, hgrn_norm, pool_mix, pool_scale, w_out, g_mem, w_mem_kv,
           g_cross, w_cq, w_co, g_ffn, w_ffn_in, w_ffn_out, g_final):
    nb, L, _ = x_prompt.shape
    ns = x_sample.shape[0]

    g_final2 = g_final.reshape(1, D_MODEL)

    mem_k, mem_v, kt, vb, w_in_b, w_out_b = _mem_kv(mem_prompt, g_mem, w_mem_kv[0], w_in[0], w_out[0],
                                                    pool_mix[0], pool_scale)
    gates, xs = _sample_gates(x_sample, g_mix, w_in_b, hgrn_lb)
    x1, hgrn_p, pool_p, w1_b = _mixer_prompt(x_prompt, g_mix, w_in_b, hgrn_lb, hgrn_norm, w_out_b,
                                             w_ffn_in[0])
    x2, hgrn_s, o_s, w_cq_b, w_co_b, w2_b = _attn_prompt_state_sample(
        x1, g_cross, w_cq[0], kt, vb, w_co[0], gates, state_hgrn[0], w_ffn_out[0])
    x1s, pool_s, qs = _sample_mix(xs, gates, o_s, jnp.swapaxes(state_pool[0], 0, 1), hgrn_norm,
                                  w_out_b, g_cross, w_cq_b)
    y_prompt, att = _ffn_prompt_attn_sample(
        x2.reshape(nb * L, D_MODEL), g_ffn, w1_b, w2_b, g_final2, qs,
        _split_heads(cache_mem_k[0]), _split_heads(cache_mem_v[0]))
    y_sample = _sample_ffn(x1s, att, w_co_b, g_ffn, w1_b, w2_b, g_final2)

    return (y_prompt.reshape(nb, L, D_MODEL),
            y_sample,
            hgrn_p[None],
            pool_p[None],
            _merge_heads(mem_k)[None],
            _merge_heads(mem_v)[None],
            hgrn_s[None],
            jnp.swapaxes(pool_s, 0, 1)[None])
```

```python
import functools

import jax
import jax.numpy as jnp
from jax import lax
from jax.experimental import pallas as pl
from jax.experimental.pallas import tpu as pltpu

F32 = jnp.float32
BF16 = jnp.bfloat16

D_MODEL = 1024
HGRN_WIDTH = 512
HEADS = 4
DK = 128
CHUNK = 64
POOL_WIDTH = 512
POOL_WINDOWS = (2, 4, 8, 16)
POOL_GROUP = 128
POOL_STATE = 15
IN_PROJ = 4 * HGRN_WIDTH + POOL_WIDTH
MEM_LEN = 256
X_HEADS = 4
X_HEAD_DIM = 256
D_FF = 2816
EPS = 1e-6
ATTN_SCALE = X_HEAD_DIM ** -0.5
LOG2_E = 1.4426950408889634
QUERY_SCALE = ATTN_SCALE * LOG2_E

VMEM_LIMIT_BYTES = 56 * 1024 * 1024

MIX_ROWS = 512
MIX_BLOCKS_PER_SEQ = 4
IN_PROJ_PIECE = 256
MIX_FILLER_SCHEDULE = (3, 0, 0) + (1, 0, 1, 0, 1, 0, 1, 0) + (2, 2)
TRI_ROWS = 256
ATTN_ROWS = 1024
SAMPLE_FFN_COLS = 1408
SPLIT_PITCH = MEM_LEN + 4
FFN_ROWS = 512
STATE_TOKENS = 8
ATTN_TOKENS = 4
FFN_COL_CHUNKS = tuple((c, min(c + 512, D_FF)) for c in range(0, D_FF, 512))


def _dot(a, b):
    return jnp.dot(a, b, preferred_element_type=F32)


def _dot_nt(a, b):
    return lax.dot_general(a, b, (((1,), (1,)), ((), ())), preferred_element_type=F32)


def _dot_tn(a, b):
    return lax.dot_general(a, b, (((0,), (0,)), ((), ())), preferred_element_type=F32)


def _rms(x, g):
    ms = jnp.mean(x * x, axis=-1, keepdims=True)
    return x * lax.rsqrt(ms + EPS) * g


def _sigmoid(x):
    return 1.0 / (1.0 + jnp.exp(-x))


def _lower_bound(lb_ref):
    t = lb_ref[...]
    m = jnp.max(t, axis=0, keepdims=True)
    e = jnp.exp(t - m)
    return e[0:1, :] / jnp.sum(e, axis=0, keepdims=True)


def _gates(proj_q, proj_f, lb):
    qq = proj_q * _sigmoid(proj_q)
    sig = _sigmoid(proj_f)
    fgate = lb + (1.0 - lb) * sig
    kk = (1.0 - lb) * (1.0 - sig)
    return qq, fgate, kk


def _split2(x):
    hi = x.astype(BF16)
    return hi, (x - hi.astype(F32)).astype(BF16)


def _split3(x):
    hi = x.astype(BF16)
    r1 = x - hi.astype(F32)
    mid = r1.astype(BF16)
    lo = (r1 - mid.astype(F32)).astype(BF16)
    return hi, mid, lo


def _memkv_kernel(mem_ref, g_ref, w_ref, win_ref, wout_ref, pmix_ref, ps_ref,
                  k_ref, v_ref, kt_ref, vb_ref, winb_ref, woutb_ref, wb_ref, rows_ref):
    b = pl.program_id(0)
    first_pool_block = HGRN_WIDTH // POOL_GROUP

    @pl.when(b == 0)
    def _():
        wb_ref[...] = w_ref[...].astype(BF16)

    winb_ref[...] = win_ref[...].astype(BF16)

    @pl.when(b < first_pool_block)
    def _():
        woutb_ref[...] = wout_ref[...].astype(BF16)

    @pl.when(b >= first_pool_block)
    def _():
        gi = b - first_pool_block
        a_hi, a_lo = _split2(pmix_ref[gi] * ps_ref[gi])
        w_hi, w_lo = _split2(wout_ref[...])
        woutb_ref[...] = (_dot(a_hi, w_hi) + _dot(a_hi, w_lo) + _dot(a_lo, w_hi)).astype(BF16)

    h = _rms(mem_ref[0], g_ref[...]).astype(BF16)
    kv = _dot(h, wb_ref[...])
    k = kv[:, :D_MODEL]
    v = kv[:, D_MODEL:]
    kt_ref[0] = k.T.astype(BF16)
    vb_ref[0] = v.astype(BF16)
    for val, out_ref in ((k, k_ref), (v, v_ref)):
        for r in range(2 * X_HEADS):
            half, head = divmod(r, X_HEADS)
            c0 = head * X_HEAD_DIM + half * 128
            rows_ref[r * SPLIT_PITCH:r * SPLIT_PITCH + MEM_LEN, :] = val[:, c0:c0 + 128]
        for m in range(MEM_LEN):
            out_ref[0, m] = rows_ref[pl.ds(m, 2 * X_HEADS, stride=SPLIT_PITCH), :]


def _mem_kv(mem, g_mem, w_kv, w_in, w_out, pool_mix, pool_scale):
    nb = mem.shape[0]
    full = lambda shape: pl.BlockSpec(shape, lambda b: (0,) * len(shape))
    split_spec = pl.BlockSpec((1, MEM_LEN, 2 * X_HEADS, 128), lambda b: (b, 0, 0, 0))
    wrows = D_MODEL // nb
    assert wrows == POOL_GROUP
    win_spec = pl.BlockSpec((wrows, IN_PROJ), lambda b: (b, 0))
    wout_spec = pl.BlockSpec((wrows, D_MODEL), lambda b: (b, 0))
    return pl.pallas_call(
        _memkv_kernel,
        grid=(nb,),
        in_specs=[pl.BlockSpec((1, MEM_LEN, D_MODEL), lambda b: (b, 0, 0)),
                  full((1, D_MODEL)), full((D_MODEL, 2 * D_MODEL)), win_spec, wout_spec,
                  full((4, POOL_GROUP, POOL_GROUP)), full((4, 1, POOL_GROUP))],
        out_specs=[split_spec, split_spec,
                   pl.BlockSpec((1, D_MODEL, MEM_LEN), lambda b: (b, 0, 0)),
                   pl.BlockSpec((1, MEM_LEN, D_MODEL), lambda b: (b, 0, 0)),
                   win_spec, wout_spec],
        out_shape=[jax.ShapeDtypeStruct((nb, MEM_LEN, 2 * X_HEADS, 128), F32),
                   jax.ShapeDtypeStruct((nb, MEM_LEN, 2 * X_HEADS, 128), F32),
                   jax.ShapeDtypeStruct((nb, D_MODEL, MEM_LEN), BF16),
                   jax.ShapeDtypeStruct((nb, MEM_LEN, D_MODEL), BF16),
                   jax.ShapeDtypeStruct((D_MODEL, IN_PROJ), BF16),
                   jax.ShapeDtypeStruct((D_MODEL, D_MODEL), BF16)],
        scratch_shapes=[pltpu.VMEM((D_MODEL, 2 * D_MODEL), BF16),
                        pltpu.VMEM((2 * X_HEADS * SPLIT_PITCH, 128), F32)],
        compiler_params=pltpu.CompilerParams(
            dimension_semantics=("arbitrary",), vmem_limit_bytes=VMEM_LIMIT_BYTES),
        name="mem_kv",
    )(mem, g_mem, w_kv, w_in, w_out, pool_mix, pool_scale.reshape(4, 1, POOL_GROUP))


def _mix_block(n, x, proj_ref, out_ref, fillers, lb_ref, hn_ref, wout_ref,
               st_ref, ext_ref, qq_ref, kk_ref, b_ref, mrg_ref):
    T = MIX_ROWS
    fillers = list(fillers)
    schedule = list(MIX_FILLER_SCHEDULE)
    assert len(schedule) == 5 + T // CHUNK and sum(schedule) == len(fillers)

    def fill():
        for _ in range(schedule.pop(0)):
            fillers.pop(0)()

    l = n % MIX_BLOCKS_PER_SEQ
    first = l == 0
    for h in range(HEADS):
        st_ref[h] = jnp.where(first, 0.0, st_ref[h])
    ext_ref[0:16, :] = jnp.where(first, 0.0, ext_ref[0:16, :])

    fill()
    lb = _lower_bound(lb_ref)
    qq, fgate, kk = _gates(proj_ref[:, 0:512], proj_ref[:, 512:1024], lb)
    qq_ref[...] = qq
    kk_ref[...] = kk
    fill()

    r = lax.broadcasted_iota(jnp.int32, (TRI_ROWS, TRI_ROWS), 0)
    c = lax.broadcasted_iota(jnp.int32, (TRI_ROWS, TRI_ROWS), 1)
    tri = jnp.where((c <= r) & (c >= (r & -CHUNK)), 1.0, 0.0).astype(BF16)
    logf = jnp.log2(fgate)
    for blk in range(T // TRI_ROWS):
        rows = slice(blk * TRI_ROWS, (blk + 1) * TRI_ROWS)
        hi, mid, lo = _split3(logf[rows])
        b_ref[rows, :] = _dot(tri, hi) + _dot(tri, mid) + _dot(tri, lo)
    fill()

    cr = lax.broadcasted_iota(jnp.int32, (CHUNK, CHUNK), 0)
    cc = lax.broadcasted_iota(jnp.int32, (CHUNK, CHUNK), 1)
    causal = cc <= cr
    mid_row = (CHUNK - 1) // 2

    for ci in range(T // CHUNK):
        r0 = ci * CHUNK
        rows = slice(r0, r0 + CHUNK)
        scores, inters, vals = [], [], []
        for h in range(HEADS):
            cols = slice(h * DK, (h + 1) * DK)
            b = b_ref[rows, cols]
            m = b_ref[r0 + mid_row:r0 + mid_row + 1, cols]
            b_end = b_ref[r0 + CHUNK - 1:r0 + CHUNK, cols]
            e1 = jnp.exp2(b - m)
            e2 = jnp.exp2(m - b)
            q1 = qq_ref[rows, cols] * e1
            q0 = q1 * jnp.exp2(m)
            ks = kk_ref[rows, cols] * e2
            k2 = ks * jnp.exp2(b_end - m)
            v = proj_ref[rows, 1024 + h * DK:1024 + (h + 1) * DK].astype(BF16)
            st = st_ref[h]
            scores.append(_dot_nt(q1.astype(BF16), ks.astype(BF16)))
            inters.append(_dot_nt(q0.astype(BF16), st.astype(BF16)))
            st_ref[h] = st * jnp.exp2(b_end) + _dot_tn(v, k2.astype(BF16))
            vals.append(v)
        fill()
        for h in range(HEADS):
            cols = slice(h * DK, (h + 1) * DK)
            a = jnp.where(causal, scores[h], 0.0)
            o = inters[h] + _dot(a.astype(BF16), vals[h])
            o = o * lax.rsqrt(jnp.mean(o * o, axis=-1, keepdims=True) + EPS)
            o = o * hn_ref[:, cols]
            g = proj_ref[rows, 1536 + h * DK:1536 + (h + 1) * DK]
            mrg_ref[rows, cols] = (o * _sigmoid(g)).astype(BF16)

    ext_ref[16:16 + T, :] = proj_ref[:, 2048:2560]
    pos = l * T + lax.broadcasted_iota(jnp.int32, (16, POOL_GROUP), 0)
    for gi, w in enumerate(POOL_WINDOWS):
        if gi % 2 == 0:
            fill()
        cols = slice(gi * POOL_GROUP, (gi + 1) * POOL_GROUP)
        s = ext_ref[:, cols]
        u = s[16:]
        shift = 1
        while shift < w:
            s = s + pltpu.roll(s, shift, axis=0)
            shift *= 2
        acc = s[16:]
        cnt = jnp.minimum(pos + 1, w).astype(F32)
        pooled = jnp.concatenate([acc[0:16] / cnt, acc[16:] * (1.0 / w)], axis=0) - u
        mrg_ref[:, HGRN_WIDTH + gi * POOL_GROUP:HGRN_WIDTH + (gi + 1) * POOL_GROUP] = pooled.astype(BF16)

    ext_ref[0:16, :] = ext_ref[T:T + 16, :]
    out_ref[...] = x + _dot(mrg_ref[...], wout_ref[...])


def _mixer_kernel(xa_ref, xn_ref, g_ref, win_ref, lb_ref, hn_ref, wout_ref,
                  w1f_ref,
                  x1_ref, hst_ref, pst_ref, w1b_ref,
                  p0_ref, p1_ref, hb_ref, st_ref, ext_ref, qq_ref, kk_ref, b_ref, mrg_ref):
    T = MIX_ROWS
    g = pl.program_id(0)
    w1b_ref[...] = w1f_ref[...].astype(BF16)
    rest = (lb_ref, hn_ref, wout_ref, st_ref, ext_ref, qq_ref, kk_ref, b_ref, mrg_ref)

    def in_proj_pieces(x_ref, rows, p_ref):
        def prep():
            hb_ref[...] = _rms(x_ref[rows, :], g_ref[...]).astype(BF16)

        def piece(k):
            cols = slice(k * IN_PROJ_PIECE, (k + 1) * IN_PROJ_PIECE)
            p_ref[:, cols] = _dot(hb_ref[...], win_ref[:, cols])

        return [prep] + [functools.partial(piece, k) for k in range(IN_PROJ // IN_PROJ_PIECE)]

    @pl.when(g == 0)
    def _():
        st_ref[...] = jnp.zeros_like(st_ref)
        ext_ref[0:16, :] = jnp.zeros((16, POOL_WIDTH), F32)
        for f in in_proj_pieces(xa_ref, slice(0, T), p0_ref):
            f()

    _mix_block(2 * g, xa_ref[0:T, :], p0_ref, x1_ref.at[0:T, :],
               in_proj_pieces(xa_ref, slice(T, 2 * T), p1_ref), *rest)
    _mix_block(2 * g + 1, xa_ref[T:2 * T, :], p1_ref, x1_ref.at[T:2 * T, :],
               in_proj_pieces(xn_ref, slice(0, T), p0_ref), *rest)

    @pl.when(g % (MIX_BLOCKS_PER_SEQ // 2) == MIX_BLOCKS_PER_SEQ // 2 - 1)
    def _():
        for h in range(HEADS):
            hst_ref[0, h] = st_ref[h].T
        pst_ref[0] = ext_ref[T + 1:T + 16, :]


def _mixer_prompt(x, g_mix, w_in, hgrn_lb, hgrn_norm, w_out, w_ffn_in):
    nb, L, _ = x.shape
    T = MIX_ROWS
    assert L // T == MIX_BLOCKS_PER_SEQ and MIX_BLOCKS_PER_SEQ % 2 == 0
    n_blocks = nb * MIX_BLOCKS_PER_SEQ
    steps = n_blocks // 2
    steps_per_seq = MIX_BLOCKS_PER_SEQ // 2
    x2d = x.reshape(nb * L, D_MODEL)
    full = lambda shape: pl.BlockSpec(shape, lambda g: (0,) * len(shape))
    rows = lambda w: pl.BlockSpec((w.shape[0] // steps, w.shape[1]), lambda g: (g, 0))
    side = (w_ffn_in,)
    x1, hst, pst, w1b = pl.pallas_call(
        _mixer_kernel,
        grid=(steps,),
        in_specs=[pl.BlockSpec((2 * T, D_MODEL), lambda g: (g, 0)),
                  pl.BlockSpec((T, D_MODEL), lambda g: (jnp.minimum(2 * g + 2, n_blocks - 1), 0)),
                  full((1, D_MODEL)), full((D_MODEL, IN_PROJ)), full((2, HGRN_WIDTH)),
                  full((1, HGRN_WIDTH)), full((D_MODEL, D_MODEL))] + [rows(w) for w in side],
        out_specs=[pl.BlockSpec((2 * T, D_MODEL), lambda g: (g, 0)),
                   pl.BlockSpec((1, HEADS, DK, DK), lambda g: (g // steps_per_seq, 0, 0, 0)),
                   pl.BlockSpec((1, POOL_STATE, POOL_WIDTH), lambda g: (g // steps_per_seq, 0, 0))]
                  + [rows(w) for w in side],
        out_shape=[jax.ShapeDtypeStruct((nb * L, D_MODEL), F32),
                   jax.ShapeDtypeStruct((nb, HEADS, DK, DK), F32),
                   jax.ShapeDtypeStruct((nb, POOL_STATE, POOL_WIDTH), F32)]
                  + [jax.ShapeDtypeStruct(w.shape, BF16) for w in side],
        scratch_shapes=[pltpu.VMEM((T, IN_PROJ), F32),
                        pltpu.VMEM((T, IN_PROJ), F32),
                        pltpu.VMEM((T, D_MODEL), BF16),
                        pltpu.VMEM((HEADS, DK, DK), F32),
                        pltpu.VMEM((16 + T, POOL_WIDTH), F32),
                        pltpu.VMEM((T, HGRN_WIDTH), F32),
                        pltpu.VMEM((T, HGRN_WIDTH), F32),
                        pltpu.VMEM((T, HGRN_WIDTH), F32),
                        pltpu.VMEM((T, D_MODEL), BF16)],
        compiler_params=pltpu.CompilerParams(
            dimension_semantics=("arbitrary",), vmem_limit_bytes=VMEM_LIMIT_BYTES),
        name="mixer_prompt",
    )(x2d, x2d, g_mix, w_in, hgrn_lb, hgrn_norm, w_out, *side)
    return x1.reshape(nb, L, D_MODEL), hst, pst, w1b


def _state_step(gates_ref, s_ref, so_ref, o_ref):
    G = STATE_TOKENS
    pad = jnp.zeros((DK - G, DK), F32)
    for h in range(HEADS):
        cols = slice(h * DK, (h + 1) * DK)
        qt = jnp.concatenate([gates_ref[:, h * DK:(h + 1) * DK], pad], axis=0).T
        ft = jnp.concatenate([gates_ref[:, 512 + h * DK:512 + (h + 1) * DK], pad], axis=0).T
        kt = jnp.concatenate([gates_ref[:, 1024 + h * DK:1024 + (h + 1) * DK], pad], axis=0).T
        for j in range(G):
            v = gates_ref[j:j + 1, 1536 + h * DK:1536 + (h + 1) * DK]
            s_new = ft[:, j:j + 1] * s_ref[j, h] + kt[:, j:j + 1] * v
            so_ref[j, h] = s_new
            o_ref[j:j + 1, cols] = jnp.sum(qt[:, j:j + 1] * s_new, axis=0, keepdims=True)


def _attn_state_kernel(x_ref, g_ref, wqf_ref, kt_ref, vb_ref, wof_ref, gates_ref, s_ref, w2f_ref,
                       o_ref, so_ref, oo_ref, wq_ref, wo_ref, w2b_ref, att_ref):
    H = ATTN_ROWS // 2
    halves = (slice(0, H), slice(H, 2 * H))

    @pl.when(pl.program_id(0) == 0)
    def _():
        wq_ref[...] = wqf_ref[...].astype(BF16)
        wo_ref[...] = wof_ref[...].astype(BF16)

    w2b_ref[...] = w2f_ref[...].astype(BF16)

    def query(rows):
        hb = _rms(x_ref[0, rows, :], g_ref[...]).astype(BF16)
        return (_dot(hb, wq_ref[...]) * QUERY_SCALE).astype(BF16)

    def scores(q):
        return [_dot(q[:, h * X_HEAD_DIM:(h + 1) * X_HEAD_DIM],
                     kt_ref[0, h * X_HEAD_DIM:(h + 1) * X_HEAD_DIM, :]) for h in range(X_HEADS)]

    def values(rows, ss):
        for h, s in enumerate(ss):
            cols = slice(h * X_HEAD_DIM, (h + 1) * X_HEAD_DIM)
            e = jnp.exp2(s - jnp.max(s, axis=-1, keepdims=True))
            den = jnp.sum(e, axis=-1, keepdims=True)
            o = _dot(e.astype(BF16), vb_ref[0, :, cols]) / den
            att_ref[rows, cols] = o.astype(BF16)

    def project(rows):
        o_ref[0, rows, :] = x_ref[0, rows, :] + _dot(att_ref[rows, :], wo_ref[...])

    s0 = scores(query(halves[0]))
    q1 = query(halves[1])
    values(halves[0], s0)
    s1 = scores(q1)
    project(halves[0])
    values(halves[1], s1)
    project(halves[1])
    _state_step(gates_ref, s_ref, so_ref, oo_ref)


def _attn_prompt_state_sample(x, g_cross, w_cq, kt, vb, w_co, gates, state, w_ffn_out):
    nb, L, _ = x.shape
    T = ATTN_ROWS
    G = STATE_TOKENS
    steps_per_seq = L // T
    steps = nb * steps_per_seq
    ns = gates.shape[0]
    assert steps == ns // G
    full = lambda shape: pl.BlockSpec(shape, lambda i: (0,) * len(shape))
    st_spec = pl.BlockSpec((G, HEADS, DK, DK), lambda i: (i, 0, 0, 0))
    x_spec = pl.BlockSpec((1, T, D_MODEL), lambda i: (i // steps_per_seq, i % steps_per_seq, 0))
    w2_spec = pl.BlockSpec((D_FF // steps, D_MODEL), lambda i: (i, 0))
    return pl.pallas_call(
        _attn_state_kernel,
        grid=(steps,),
        in_specs=[x_spec, full((1, D_MODEL)), full((D_MODEL, D_MODEL)),
                  pl.BlockSpec((1, D_MODEL, MEM_LEN), lambda i: (i // steps_per_seq, 0, 0)),
                  pl.BlockSpec((1, MEM_LEN, D_MODEL), lambda i: (i // steps_per_seq, 0, 0)),
                  full((D_MODEL, D_MODEL)),
                  pl.BlockSpec((G, 4 * 512), lambda i: (i, 0)), st_spec, w2_spec],
        out_specs=[x_spec, st_spec, pl.BlockSpec((G, HGRN_WIDTH), lambda i: (i, 0)),
                   full((D_MODEL, D_MODEL)), full((D_MODEL, D_MODEL)), w2_spec],
        out_shape=[jax.ShapeDtypeStruct((nb, L, D_MODEL), F32),
                   jax.ShapeDtypeStruct((ns, HEADS, DK, DK), F32),
                   jax.ShapeDtypeStruct((ns, HGRN_WIDTH), F32),
                   jax.ShapeDtypeStruct((D_MODEL, D_MODEL), BF16),
                   jax.ShapeDtypeStruct((D_MODEL, D_MODEL), BF16),
                   jax.ShapeDtypeStruct((D_FF, D_MODEL), BF16)],
        scratch_shapes=[pltpu.VMEM((T, D_MODEL), BF16)],
        compiler_params=pltpu.CompilerParams(
            dimension_semantics=("arbitrary",), vmem_limit_bytes=VMEM_LIMIT_BYTES),
        name="attn_prompt_state_sample",
    )(x, g_cross, w_cq, kt, vb, w_co, gates, state, w_ffn_out)


def _zero_after(x):
    u = lax.bitcast_convert_type(x, jnp.uint32)
    z = lax.shift_right_logical(lax.shift_right_logical(u, jnp.uint32(16)), jnp.uint32(16))
    return lax.bitcast_convert_type(z, F32)


def _ffn_body(x, g_ref, w1_ref, w2_ref, gf_ref, fillers=()):
    hb = _rms(x, g_ref[...]).astype(BF16)

    def up(c0, c1):
        return _dot(hb, w1_ref[:, c0:c1]), _dot(hb, w1_ref[:, D_FF + c0:D_FF + c1])

    def add_to_first_tile(m, z):
        top = jnp.concatenate([m[0:8, 0:128] + z, m[0:8, 128:]], axis=1)
        return jnp.concatenate([top, m[8:, :]], axis=0)

    y = x
    fillers = list(fillers)
    assert len(fillers) <= len(FFN_COL_CHUNKS)
    nxt = up(*FFN_COL_CHUNKS[0])
    for i, (c0, c1) in enumerate(FFN_COL_CHUNKS):
        a, bg = nxt
        if i + 1 < len(FFN_COL_CHUNKS):
            nxt = up(*FFN_COL_CHUNKS[i + 1])
        if fillers:
            bg = add_to_first_tile(bg, _zero_after(fillers.pop(0)()))
        act = (a * _sigmoid(a) * bg).astype(BF16)
        y = y + _dot(act, w2_ref[c0:c1, :])
    return _rms(y, gf_ref[...])


def _split_heads(x):
    lead = x.shape[:-2]
    x = x.reshape(lead + (X_HEADS, 2, 128))
    x = jnp.swapaxes(x, -3, -2)
    return x.reshape(lead + (2 * X_HEADS, 128))


def _merge_heads(x):
    lead = x.shape[:-2]
    x = x.reshape(lead + (2, X_HEADS, 128))
    x = jnp.swapaxes(x, -3, -2)
    return x.reshape(lead + (X_HEADS, X_HEAD_DIM))


def _memory_attention(j, base, q_ref, k_ref, v_ref, o_ref):
    prod = k_ref[j] * q_ref[base + j][None]
    s = jnp.sum(prod + pltpu.roll(prod, X_HEADS, axis=1), axis=-1, keepdims=True)
    e = jnp.exp2(s - jnp.max(s, axis=0, keepdims=True))
    den = jnp.sum(e, axis=0)
    o = jnp.sum(e * v_ref[j], axis=0) / den
    o_ref[base + j] = o
    return o


def _ffn_attn_kernel(x_ref, g_ref, w1_ref, w2_ref, gf_ref, q_ref, k_ref, v_ref, xs_ref, wo_ref,
                     o_ref, ys_ref, att_ref):
    i = pl.program_id(0)
    last = pl.num_programs(0) - 1

    @pl.when(i < last)
    def _():
        base = i * ATTN_TOKENS
        fillers = [functools.partial(_memory_attention, j, base, q_ref, k_ref, v_ref, att_ref)
                   for j in range(ATTN_TOKENS)]
        o_ref[...] = _ffn_body(x_ref[...], g_ref, w1_ref, w2_ref, gf_ref, fillers)

    @pl.when(i == last)
    def _():
        att = jnp.concatenate([att_ref[:, half * X_HEADS + head, :]
                               for head in range(X_HEADS) for half in range(2)], axis=1)
        x2 = xs_ref[...] + _dot(att.astype(BF16), wo_ref[...])
        ys_ref[:, 0, :] = _ffn_body(x2, g_ref, w1_ref, w2_ref, gf_ref)


def _ffn_prompt_attn_sample(x, g_ffn, w1, w2, g_final, q, cache_k, cache_v, x_sample, w_co):
    n = x.shape[0]
    ns = q.shape[0]
    T = FFN_ROWS
    G = ATTN_TOKENS
    steps = n // T
    assert steps == ns // G
    full = lambda shape: pl.BlockSpec(shape, lambda i: (0,) * len(shape))
    blk = lambda i: jnp.minimum(i, steps - 1)
    kv_spec = pl.BlockSpec((G, MEM_LEN, 2 * X_HEADS, 128), lambda i: (blk(i), 0, 0, 0))
    q_spec = full((ns, 2 * X_HEADS, 128))
    y, ys = pl.pallas_call(
        _ffn_attn_kernel,
        grid=(steps + 1,),
        in_specs=[pl.BlockSpec((T, D_MODEL), lambda i: (blk(i), 0)),
                  full((1, D_MODEL)), full((D_MODEL, 2 * D_FF)), full((D_FF, D_MODEL)),
                  full((1, D_MODEL)), q_spec, kv_spec, kv_spec,
                  full((ns, D_MODEL)), full((D_MODEL, D_MODEL))],
        out_specs=[pl.BlockSpec((T, D_MODEL), lambda i: (blk(i), 0)),
                   full((ns, 1, D_MODEL))],
        out_shape=[jax.ShapeDtypeStruct((n, D_MODEL), F32),
                   jax.ShapeDtypeStruct((ns, 1, D_MODEL), F32)],
        scratch_shapes=[pltpu.VMEM((ns, 2 * X_HEADS, 128), F32)],
        compiler_params=pltpu.CompilerParams(
            dimension_semantics=("arbitrary",), vmem_limit_bytes=VMEM_LIMIT_BYTES),
        name="ffn_prompt_attn_sample",
    )(x, g_ffn, w1, w2, g_final, q, cache_k, cache_v, x_sample, w_co)
    return y, ys


def _sample_gates_kernel(x_ref, g_ref, win_ref, lb_ref, o_ref, x2d_ref):
    x = x_ref[:, 0, :]
    x2d_ref[...] = x
    hb = _rms(x, g_ref[...]).astype(BF16)
    proj = _dot(hb, win_ref[...])
    lb = _lower_bound(lb_ref)
    qq, fgate, kk = _gates(proj[:, 0:512], proj[:, 512:1024], lb)
    o_ref[:, 0:512] = qq
    o_ref[:, 512:1024] = fgate
    o_ref[:, 1024:1536] = kk
    o_ref[:, 1536:2048] = proj[:, 1024:1536]
    o_ref[:, 2048:2560] = _sigmoid(proj[:, 1536:2048])
    o_ref[:, 2560:3072] = proj[:, 2048:2560]


def _sample_gates(x, g_mix, w_in, hgrn_lb):
    n = x.shape[0]
    full = lambda shape: pl.BlockSpec(shape, lambda i: (0,) * len(shape))
    return pl.pallas_call(
        _sample_gates_kernel,
        grid=(1,),
        in_specs=[full((n, 1, D_MODEL)), full((1, D_MODEL)), full((D_MODEL, IN_PROJ)),
                  full((2, HGRN_WIDTH))],
        out_specs=[full((n, 6 * 512)), full((n, D_MODEL))],
        out_shape=[jax.ShapeDtypeStruct((n, 6 * 512), F32),
                   jax.ShapeDtypeStruct((n, D_MODEL), F32)],
        compiler_params=pltpu.CompilerParams(
            dimension_semantics=("arbitrary",), vmem_limit_bytes=VMEM_LIMIT_BYTES),
        name="sample_gates",
    )(x, g_mix, w_in, hgrn_lb)


def _sample_mix_kernel(x_ref, gates_ref, o_ref, past_ref, hn_ref, wout_ref,
                       gc_ref, wq_ref, x1_ref, pool_ref, q_ref, mrg_ref):
    for h in range(HEADS):
        cols = slice(h * DK, (h + 1) * DK)
        o = o_ref[:, cols]
        o = o * lax.rsqrt(jnp.mean(o * o, axis=-1, keepdims=True) + EPS) * hn_ref[:, cols]
        mrg_ref[:, cols] = (o * gates_ref[:, 2048 + h * DK:2048 + (h + 1) * DK]).astype(BF16)
    for gi, w in enumerate(POOL_WINDOWS):
        cols = slice(gi * POOL_GROUP, (gi + 1) * POOL_GROUP)
        u = gates_ref[:, 2560 + gi * POOL_GROUP:2560 + (gi + 1) * POOL_GROUP]
        acc = u
        for j in range(1, w):
            acc = acc + past_ref[POOL_STATE - j, :, cols]
        pooled = acc / float(w) - u
        mrg_ref[:, HGRN_WIDTH + gi * POOL_GROUP:HGRN_WIDTH + (gi + 1) * POOL_GROUP] = pooled.astype(BF16)
    pool_ref[0:POOL_STATE - 1] = past_ref[1:POOL_STATE]
    pool_ref[POOL_STATE - 1] = gates_ref[:, 2560:3072]
    x1 = x_ref[...] + _dot(mrg_ref[...], wout_ref[...])
    x1_ref[...] = x1
    hb = _rms(x1, gc_ref[...]).astype(BF16)
    q = _dot(hb, wq_ref[...]) * QUERY_SCALE
    for r in range(2 * X_HEADS):
        half, head = divmod(r, X_HEADS)
        c0 = head * X_HEAD_DIM + half * 128
        q_ref[:, r, :] = q[:, c0:c0 + 128]


def _sample_mix(x, gates, o, past, hgrn_norm, w_out, g_cross, w_cq):
    n = x.shape[0]
    full = lambda shape: pl.BlockSpec(shape, lambda i: (0,) * len(shape))
    return pl.pallas_call(
        _sample_mix_kernel,
        grid=(1,),
        in_specs=[full((n, D_MODEL)), full((n, 6 * 512)), full((n, HGRN_WIDTH)),
                  full((POOL_STATE, n, POOL_WIDTH)), full((1, HGRN_WIDTH)),
                  full((D_MODEL, D_MODEL)), full((1, D_MODEL)), full((D_MODEL, D_MODEL))],
        out_specs=[full((n, D_MODEL)), full((POOL_STATE, n, POOL_WIDTH)),
                   full((n, 2 * X_HEADS, 128))],
        out_shape=[jax.ShapeDtypeStruct((n, D_MODEL), F32),
                   jax.ShapeDtypeStruct((POOL_STATE, n, POOL_WIDTH), F32),
                   jax.ShapeDtypeStruct((n, 2 * X_HEADS, 128), F32)],
        scratch_shapes=[pltpu.VMEM((n, D_MODEL), BF16)],
        compiler_params=pltpu.CompilerParams(
            dimension_semantics=("arbitrary",), vmem_limit_bytes=VMEM_LIMIT_BYTES),
        name="sample_mix",
    )(x, gates, o, past, hgrn_norm, w_out, g_cross, w_cq)


def kernel(x_prompt, x_sample, mem_prompt, state_hgrn, state_pool, cache_mem_k, cache_mem_v,
           g_mix, w_in, hgrn_lb, hgrn_norm, pool_mix, pool_scale, w_out, g_mem, w_mem_kv,
           g_cross, w_cq, w_co, g_ffn, w_ffn_in, w_ffn_out, g_final):
    nb, L, _ = x_prompt.shape
    ns = x_sample.shape[0]

    g_final2 = g_final.reshape(1, D_MODEL)

    mem_k, mem_v, kt, vb, w_in_b, w_out_b = _mem_kv(mem_prompt, g_mem, w_mem_kv[0], w_in[0], w_out[0],
                                                    pool_mix[0], pool_scale)
    gates, xs = _sample_gates(x_sample, g_mix, w_in_b, hgrn_lb)
    x1, hgrn_p, pool_p, w1_b = _mixer_prompt(x_prompt, g_mix, w_in_b, hgrn_lb, hgrn_norm, w_out_b,
                                             w_ffn_in[0])
    x2, hgrn_s, o_s, w_cq_b, w_co_b, w2_b = _attn_prompt_state_sample(
        x1, g_cross, w_cq[0], kt, vb, w_co[0], gates, state_hgrn[0], w_ffn_out[0])
    x1s, pool_s, qs = _sample_mix(xs, gates, o_s, jnp.swapaxes(state_pool[0], 0, 1), hgrn_norm,
                                  w_out_b, g_cross, w_cq_b)
    y_prompt, y_sample = _ffn_prompt_attn_sample(
        x2.reshape(nb * L, D_MODEL), g_ffn, w1_b, w2_b, g_final2, qs,
        _split_heads(cache_mem_k[0]), _split_heads(cache_mem_v[0]), x1s, w_co_b)

    return (y_prompt.reshape(nb, L, D_MODEL),
            y_sample,
            hgrn_p[None],
            pool_p[None],
            _merge_heads(mem_k)[None],
            _merge_heads(mem_v)[None],
            hgrn_s[None],
            jnp.swapaxes(pool_s, 0, 1)[None])
```

```python
import functools

import jax
import jax.numpy as jnp
from jax import lax
from jax.experimental import pallas as pl
from jax.experimental.pallas import tpu as pltpu

F32 = jnp.float32
BF16 = jnp.bfloat16

D_MODEL = 1024
HGRN_WIDTH = 512
HEADS = 4
DK = 128
CHUNK = 64
POOL_WIDTH = 512
POOL_WINDOWS = (2, 4, 8, 16)
POOL_GROUP = 128
POOL_STATE = 15
IN_PROJ = 4 * HGRN_WIDTH + POOL_WIDTH
MEM_LEN = 256
X_HEADS = 4
X_HEAD_DIM = 256
D_FF = 2816
EPS = 1e-6
ATTN_SCALE = X_HEAD_DIM ** -0.5
LOG2_E = 1.4426950408889634
QUERY_SCALE = ATTN_SCALE * LOG2_E

VMEM_LIMIT_BYTES = 56 * 1024 * 1024

MIX_ROWS = 512
MIX_BLOCKS_PER_SEQ = 4
IN_PROJ_PIECE = 256
MIX_FILLER_SCHEDULE = (3, 0, 0) + (1, 0, 1, 0, 1, 0, 1, 0) + (2, 2)
TRI_ROWS = 256
ATTN_ROWS = 1024
SAMPLE_FFN_COLS = 1408
SPLIT_PITCH = MEM_LEN + 4
FFN_ROWS = 512
STATE_TOKENS = 8
ATTN_TOKENS = 4
FFN_COL_CHUNKS = tuple((c, min(c + 512, D_FF)) for c in range(0, D_FF, 512))


def _dot(a, b):
    return jnp.dot(a, b, preferred_element_type=F32)


def _dot_nt(a, b):
    return lax.dot_general(a, b, (((1,), (1,)), ((), ())), preferred_element_type=F32)


def _dot_tn(a, b):
    return lax.dot_general(a, b, (((0,), (0,)), ((), ())), preferred_element_type=F32)


def _rms(x, g):
    ms = jnp.mean(x * x, axis=-1, keepdims=True)
    return x * lax.rsqrt(ms + EPS) * g


def _sigmoid(x):
    return 1.0 / (1.0 + jnp.exp(-x))


def _lower_bound(lb_ref):
    t = lb_ref[...]
    m = jnp.max(t, axis=0, keepdims=True)
    e = jnp.exp(t - m)
    return e[0:1, :] / jnp.sum(e, axis=0, keepdims=True)


def _gates(proj_q, proj_f, lb):
    qq = proj_q * _sigmoid(proj_q)
    sig = _sigmoid(proj_f)
    fgate = lb + (1.0 - lb) * sig
    kk = (1.0 - lb) * (1.0 - sig)
    return qq, fgate, kk


def _split2(x):
    hi = x.astype(BF16)
    return hi, (x - hi.astype(F32)).astype(BF16)


def _split3(x):
    hi = x.astype(BF16)
    r1 = x - hi.astype(F32)
    mid = r1.astype(BF16)
    lo = (r1 - mid.astype(F32)).astype(BF16)
    return hi, mid, lo


def _sample_gates(x, g_ref, winb_ref, lb_ref, o_ref, x2d_ref):
    x2d_ref[...] = x
    hb = _rms(x, g_ref[...]).astype(BF16)
    proj = _dot(hb, winb_ref[...])
    lb = _lower_bound(lb_ref)
    qq, fgate, kk = _gates(proj[:, 0:512], proj[:, 512:1024], lb)
    o_ref[:, 0:512] = qq
    o_ref[:, 512:1024] = fgate
    o_ref[:, 1024:1536] = kk
    o_ref[:, 1536:2048] = proj[:, 1024:1536]
    o_ref[:, 2048:2560] = _sigmoid(proj[:, 1536:2048])
    o_ref[:, 2560:3072] = proj[:, 2048:2560]


def _memkv_kernel(mem_ref, g_ref, w_ref, win_ref, wout_ref, wcq_ref, wco_ref, pmix_ref, ps_ref,
                  xs_ref, gmix_ref, lb_ref,
                  k_ref, v_ref, kt_ref, vb_ref, winb_ref, woutb_ref, wcqb_ref, wcob_ref,
                  gates_ref, xs2d_ref,
                  wb_ref, rows_ref):
    b = pl.program_id(0)
    nb = pl.num_programs(0) - 1
    first_pool_block = HGRN_WIDTH // POOL_GROUP

    @pl.when(b == 0)
    def _():
        wb_ref[...] = w_ref[...].astype(BF16)

    @pl.when(b < nb)
    def _():
        wrows = win_ref.shape[0]
        winb_ref[pl.ds(pl.multiple_of(b * wrows, wrows), wrows), :] = win_ref[...].astype(BF16)
        wcqb_ref[...] = wcq_ref[...].astype(BF16)
        wcob_ref[...] = wco_ref[...].astype(BF16)

        @pl.when(b < first_pool_block)
        def _():
            woutb_ref[...] = wout_ref[...].astype(BF16)

        @pl.when(b >= first_pool_block)
        def _():
            gi = b - first_pool_block
            a_hi, a_lo = _split2(pmix_ref[gi] * ps_ref[gi])
            w_hi, w_lo = _split2(wout_ref[...])
            woutb_ref[...] = (_dot(a_hi, w_hi) + _dot(a_hi, w_lo) + _dot(a_lo, w_hi)).astype(BF16)

        h = _rms(mem_ref[0], g_ref[...]).astype(BF16)
        kv = _dot(h, wb_ref[...])
        k = kv[:, :D_MODEL]
        v = kv[:, D_MODEL:]
        kt_ref[0] = k.T.astype(BF16)
        vb_ref[0] = v.astype(BF16)
        for val, out_ref in ((k, k_ref), (v, v_ref)):
            for r in range(2 * X_HEADS):
                half, head = divmod(r, X_HEADS)
                c0 = head * X_HEAD_DIM + half * 128
                rows_ref[r * SPLIT_PITCH:r * SPLIT_PITCH + MEM_LEN, :] = val[:, c0:c0 + 128]
            for m in range(MEM_LEN):
                out_ref[0, m] = rows_ref[pl.ds(m, 2 * X_HEADS, stride=SPLIT_PITCH), :]

    @pl.when(b == nb)
    def _():
        _sample_gates(xs_ref[:, 0, :], gmix_ref, winb_ref, lb_ref, gates_ref, xs2d_ref)


def _mem_kv_sample_gates(mem, g_mem, w_kv, w_in, w_out, w_cq, w_co, pool_mix, pool_scale,
                         x_sample, g_mix, hgrn_lb):
    nb = mem.shape[0]
    ns = x_sample.shape[0]
    full = lambda shape: pl.BlockSpec(shape, lambda b: (0,) * len(shape))
    blk = lambda b: jnp.minimum(b, nb - 1)
    split_spec = pl.BlockSpec((1, MEM_LEN, 2 * X_HEADS, 128), lambda b: (blk(b), 0, 0, 0))
    wrows = D_MODEL // nb
    assert wrows == POOL_GROUP
    win_spec = pl.BlockSpec((wrows, IN_PROJ), lambda b: (blk(b), 0))
    wout_spec = pl.BlockSpec((wrows, D_MODEL), lambda b: (blk(b), 0))
    return pl.pallas_call(
        _memkv_kernel,
        grid=(nb + 1,),
        in_specs=[pl.BlockSpec((1, MEM_LEN, D_MODEL), lambda b: (blk(b), 0, 0)),
                  full((1, D_MODEL)), full((D_MODEL, 2 * D_MODEL)), win_spec, wout_spec,
                  wout_spec, wout_spec,
                  full((4, POOL_GROUP, POOL_GROUP)), full((4, 1, POOL_GROUP)),
                  full((ns, 1, D_MODEL)), full((1, D_MODEL)), full((2, HGRN_WIDTH))],
        out_specs=[split_spec, split_spec,
                   pl.BlockSpec((1, D_MODEL, MEM_LEN), lambda b: (blk(b), 0, 0)),
                   pl.BlockSpec((1, MEM_LEN, D_MODEL), lambda b: (blk(b), 0, 0)),
                   full((D_MODEL, IN_PROJ)), wout_spec, wout_spec, wout_spec,
                   full((ns, 6 * 512)), full((ns, D_MODEL))],
        out_shape=[jax.ShapeDtypeStruct((nb, MEM_LEN, 2 * X_HEADS, 128), F32),
                   jax.ShapeDtypeStruct((nb, MEM_LEN, 2 * X_HEADS, 128), F32),
                   jax.ShapeDtypeStruct((nb, D_MODEL, MEM_LEN), BF16),
                   jax.ShapeDtypeStruct((nb, MEM_LEN, D_MODEL), BF16),
                   jax.ShapeDtypeStruct((D_MODEL, IN_PROJ), BF16),
                   jax.ShapeDtypeStruct((D_MODEL, D_MODEL), BF16),
                   jax.ShapeDtypeStruct((D_MODEL, D_MODEL), BF16),
                   jax.ShapeDtypeStruct((D_MODEL, D_MODEL), BF16),
                   jax.ShapeDtypeStruct((ns, 6 * 512), F32),
                   jax.ShapeDtypeStruct((ns, D_MODEL), F32)],
        scratch_shapes=[pltpu.VMEM((D_MODEL, 2 * D_MODEL), BF16),
                        pltpu.VMEM((2 * X_HEADS * SPLIT_PITCH, 128), F32)],
        compiler_params=pltpu.CompilerParams(
            dimension_semantics=("arbitrary",), vmem_limit_bytes=VMEM_LIMIT_BYTES),
        name="mem_kv_sample_gates",
    )(mem, g_mem, w_kv, w_in, w_out, w_cq, w_co, pool_mix, pool_scale.reshape(4, 1, POOL_GROUP),
      x_sample, g_mix, hgrn_lb)


def _mix_block(n, x, proj_ref, out_ref, fillers, lb_ref, hn_ref, wout_ref,
               st_ref, ext_ref, qq_ref, kk_ref, b_ref, mrg_ref):
    T = MIX_ROWS
    fillers = list(fillers)
    schedule = list(MIX_FILLER_SCHEDULE)
    assert len(schedule) == 5 + T // CHUNK and sum(schedule) == len(fillers)

    def fill():
        for _ in range(schedule.pop(0)):
            fillers.pop(0)()

    l = n % MIX_BLOCKS_PER_SEQ
    first = l == 0
    for h in range(HEADS):
        st_ref[h] = jnp.where(first, 0.0, st_ref[h])
    ext_ref[0:16, :] = jnp.where(first, 0.0, ext_ref[0:16, :])

    fill()
    lb = _lower_bound(lb_ref)
    qq, fgate, kk = _gates(proj_ref[:, 0:512], proj_ref[:, 512:1024], lb)
    qq_ref[...] = qq
    kk_ref[...] = kk
    fill()

    r = lax.broadcasted_iota(jnp.int32, (TRI_ROWS, TRI_ROWS), 0)
    c = lax.broadcasted_iota(jnp.int32, (TRI_ROWS, TRI_ROWS), 1)
    tri = jnp.where((c <= r) & (c >= (r & -CHUNK)), 1.0, 0.0).astype(BF16)
    logf = jnp.log2(fgate)
    for blk in range(T // TRI_ROWS):
        rows = slice(blk * TRI_ROWS, (blk + 1) * TRI_ROWS)
        hi, mid, lo = _split3(logf[rows])
        b_ref[rows, :] = _dot(tri, hi) + _dot(tri, mid) + _dot(tri, lo)
    fill()

    cr = lax.broadcasted_iota(jnp.int32, (CHUNK, CHUNK), 0)
    cc = lax.broadcasted_iota(jnp.int32, (CHUNK, CHUNK), 1)
    causal = cc <= cr
    mid_row = (CHUNK - 1) // 2

    for ci in range(T // CHUNK):
        r0 = ci * CHUNK
        rows = slice(r0, r0 + CHUNK)
        scores, inters, vals = [], [], []
        for h in range(HEADS):
            cols = slice(h * DK, (h + 1) * DK)
            b = b_ref[rows, cols]
            m = b_ref[r0 + mid_row:r0 + mid_row + 1, cols]
            b_end = b_ref[r0 + CHUNK - 1:r0 + CHUNK, cols]
            e1 = jnp.exp2(b - m)
            e2 = jnp.exp2(m - b)
            q1 = qq_ref[rows, cols] * e1
            q0 = q1 * jnp.exp2(m)
            ks = kk_ref[rows, cols] * e2
            k2 = ks * jnp.exp2(b_end - m)
            v = proj_ref[rows, 1024 + h * DK:1024 + (h + 1) * DK].astype(BF16)
            st = st_ref[h]
            scores.append(_dot_nt(q1.astype(BF16), ks.astype(BF16)))
            inters.append(_dot_nt(q0.astype(BF16), st.astype(BF16)))
            st_ref[h] = st * jnp.exp2(b_end) + _dot_tn(v, k2.astype(BF16))
            vals.append(v)
        fill()
        for h in range(HEADS):
            cols = slice(h * DK, (h + 1) * DK)
            a = jnp.where(causal, scores[h], 0.0)
            o = inters[h] + _dot(a.astype(BF16), vals[h])
            o = o * lax.rsqrt(jnp.mean(o * o, axis=-1, keepdims=True) + EPS)
            o = o * hn_ref[:, cols]
            g = proj_ref[rows, 1536 + h * DK:1536 + (h + 1) * DK]
            mrg_ref[rows, cols] = (o * _sigmoid(g)).astype(BF16)

    ext_ref[16:16 + T, :] = proj_ref[:, 2048:2560]
    pos = l * T + lax.broadcasted_iota(jnp.int32, (16, POOL_GROUP), 0)
    for gi, w in enumerate(POOL_WINDOWS):
        if gi % 2 == 0:
            fill()
        cols = slice(gi * POOL_GROUP, (gi + 1) * POOL_GROUP)
        s = ext_ref[:, cols]
        u = s[16:]
        shift = 1
        while shift < w:
            s = s + pltpu.roll(s, shift, axis=0)
            shift *= 2
        acc = s[16:]
        cnt = jnp.minimum(pos + 1, w).astype(F32)
        pooled = jnp.concatenate([acc[0:16] / cnt, acc[16:] * (1.0 / w)], axis=0) - u
        mrg_ref[:, HGRN_WIDTH + gi * POOL_GROUP:HGRN_WIDTH + (gi + 1) * POOL_GROUP] = pooled.astype(BF16)

    ext_ref[0:16, :] = ext_ref[T:T + 16, :]
    out_ref[...] = x + _dot(mrg_ref[...], wout_ref[...])


def _mixer_kernel(xa_ref, xn_ref, g_ref, win_ref, lb_ref, hn_ref, wout_ref,
                  w1f_ref,
                  x1_ref, hst_ref, pst_ref, w1b_ref,
                  p0_ref, p1_ref, hb_ref, st_ref, ext_ref, qq_ref, kk_ref, b_ref, mrg_ref):
    T = MIX_ROWS
    g = pl.program_id(0)
    w1b_ref[...] = w1f_ref[...].astype(BF16)
    rest = (lb_ref, hn_ref, wout_ref, st_ref, ext_ref, qq_ref, kk_ref, b_ref, mrg_ref)

    def in_proj_pieces(x_ref, rows, p_ref):
        def prep():
            hb_ref[...] = _rms(x_ref[rows, :], g_ref[...]).astype(BF16)

        def piece(k):
            cols = slice(k * IN_PROJ_PIECE, (k + 1) * IN_PROJ_PIECE)
            p_ref[:, cols] = _dot(hb_ref[...], win_ref[:, cols])

        return [prep] + [functools.partial(piece, k) for k in range(IN_PROJ // IN_PROJ_PIECE)]

    @pl.when(g == 0)
    def _():
        st_ref[...] = jnp.zeros_like(st_ref)
        ext_ref[0:16, :] = jnp.zeros((16, POOL_WIDTH), F32)
        for f in in_proj_pieces(xa_ref, slice(0, T), p0_ref):
            f()

    _mix_block(2 * g, xa_ref[0:T, :], p0_ref, x1_ref.at[0:T, :],
               in_proj_pieces(xa_ref, slice(T, 2 * T), p1_ref), *rest)
    _mix_block(2 * g + 1, xa_ref[T:2 * T, :], p1_ref, x1_ref.at[T:2 * T, :],
               in_proj_pieces(xn_ref, slice(0, T), p0_ref), *rest)

    @pl.when(g % (MIX_BLOCKS_PER_SEQ // 2) == MIX_BLOCKS_PER_SEQ // 2 - 1)
    def _():
        for h in range(HEADS):
            hst_ref[0, h] = st_ref[h].T
        pst_ref[0] = ext_ref[T + 1:T + 16, :]


def _mixer_prompt(x, g_mix, w_in, hgrn_lb, hgrn_norm, w_out, w_ffn_in):
    nb, L, _ = x.shape
    T = MIX_ROWS
    assert L // T == MIX_BLOCKS_PER_SEQ and MIX_BLOCKS_PER_SEQ % 2 == 0
    n_blocks = nb * MIX_BLOCKS_PER_SEQ
    steps = n_blocks // 2
    steps_per_seq = MIX_BLOCKS_PER_SEQ // 2
    x2d = x.reshape(nb * L, D_MODEL)
    full = lambda shape: pl.BlockSpec(shape, lambda g: (0,) * len(shape))
    rows = lambda w: pl.BlockSpec((w.shape[0] // steps, w.shape[1]), lambda g: (g, 0))
    side = (w_ffn_in,)
    x1, hst, pst, w1b = pl.pallas_call(
        _mixer_kernel,
        grid=(steps,),
        in_specs=[pl.BlockSpec((2 * T, D_MODEL), lambda g: (g, 0)),
                  pl.BlockSpec((T, D_MODEL), lambda g: (jnp.minimum(2 * g + 2, n_blocks - 1), 0)),
                  full((1, D_MODEL)), full((D_MODEL, IN_PROJ)), full((2, HGRN_WIDTH)),
                  full((1, HGRN_WIDTH)), full((D_MODEL, D_MODEL))] + [rows(w) for w in side],
        out_specs=[pl.BlockSpec((2 * T, D_MODEL), lambda g: (g, 0)),
                   pl.BlockSpec((1, HEADS, DK, DK), lambda g: (g // steps_per_seq, 0, 0, 0)),
                   pl.BlockSpec((1, POOL_STATE, POOL_WIDTH), lambda g: (g // steps_per_seq, 0, 0))]
                  + [rows(w) for w in side],
        out_shape=[jax.ShapeDtypeStruct((nb * L, D_MODEL), F32),
                   jax.ShapeDtypeStruct((nb, HEADS, DK, DK), F32),
                   jax.ShapeDtypeStruct((nb, POOL_STATE, POOL_WIDTH), F32)]
                  + [jax.ShapeDtypeStruct(w.shape, BF16) for w in side],
        scratch_shapes=[pltpu.VMEM((T, IN_PROJ), F32),
                        pltpu.VMEM((T, IN_PROJ), F32),
                        pltpu.VMEM((T, D_MODEL), BF16),
                        pltpu.VMEM((HEADS, DK, DK), F32),
                        pltpu.VMEM((16 + T, POOL_WIDTH), F32),
                        pltpu.VMEM((T, HGRN_WIDTH), F32),
                        pltpu.VMEM((T, HGRN_WIDTH), F32),
                        pltpu.VMEM((T, HGRN_WIDTH), F32),
                        pltpu.VMEM((T, D_MODEL), BF16)],
        compiler_params=pltpu.CompilerParams(
            dimension_semantics=("arbitrary",), vmem_limit_bytes=VMEM_LIMIT_BYTES),
        name="mixer_prompt",
    )(x2d, x2d, g_mix, w_in, hgrn_lb, hgrn_norm, w_out, *side)
    return x1.reshape(nb, L, D_MODEL), hst, pst, w1b


def _state_step(base, gates_ref, s_ref, so_ref, o_ref):
    G = STATE_TOKENS
    pad = jnp.zeros((DK - G, DK), F32)
    for h in range(HEADS):
        cols = slice(h * DK, (h + 1) * DK)
        qt = jnp.concatenate([gates_ref[:, h * DK:(h + 1) * DK], pad], axis=0).T
        ft = jnp.concatenate([gates_ref[:, 512 + h * DK:512 + (h + 1) * DK], pad], axis=0).T
        kt = jnp.concatenate([gates_ref[:, 1024 + h * DK:1024 + (h + 1) * DK], pad], axis=0).T
        readouts = []
        for j in range(G):
            v = gates_ref[j:j + 1, 1536 + h * DK:1536 + (h + 1) * DK]
            s_new = ft[:, j:j + 1] * s_ref[j, h] + kt[:, j:j + 1] * v
            so_ref[j, h] = s_new
            readouts.append(jnp.sum(qt[:, j:j + 1] * s_new, axis=0, keepdims=True))
        o_ref[pl.ds(pl.multiple_of(base, G), G), cols] = jnp.concatenate(readouts, axis=0)


def _sample_mix(x_ref, gates_ref, o_ref, past_ref, hn_ref, wout_ref, gc_ref, wq_ref,
                x1_ref, pool_ref, q_ref, mrg_ref):
    n = x_ref.shape[0]
    for h in range(HEADS):
        cols = slice(h * DK, (h + 1) * DK)
        o = o_ref[:, cols]
        o = o * lax.rsqrt(jnp.mean(o * o, axis=-1, keepdims=True) + EPS) * hn_ref[:, cols]
        mrg_ref[0:n, cols] = (o * gates_ref[:, 2048 + h * DK:2048 + (h + 1) * DK]).astype(BF16)
    for gi, w in enumerate(POOL_WINDOWS):
        cols = slice(gi * POOL_GROUP, (gi + 1) * POOL_GROUP)
        u = gates_ref[:, 2560 + gi * POOL_GROUP:2560 + (gi + 1) * POOL_GROUP]
        acc = u
        for j in range(1, w):
            acc = acc + past_ref[POOL_STATE - j, :, cols]
        pooled = acc / float(w) - u
        mrg_ref[0:n, HGRN_WIDTH + gi * POOL_GROUP:HGRN_WIDTH + (gi + 1) * POOL_GROUP] = pooled.astype(BF16)
    pool_ref[0:POOL_STATE - 1] = past_ref[1:POOL_STATE]
    pool_ref[POOL_STATE - 1] = gates_ref[:, 2560:3072]
    x1 = x_ref[...] + _dot(mrg_ref[0:n, :], wout_ref[...])
    x1_ref[...] = x1
    hb = _rms(x1, gc_ref[...]).astype(BF16)
    q = _dot(hb, wq_ref[...]) * QUERY_SCALE
    for r in range(2 * X_HEADS):
        half, head = divmod(r, X_HEADS)
        c0 = head * X_HEAD_DIM + half * 128
        q_ref[:, r, :] = q[:, c0:c0 + 128]


def _attn_state_kernel(x_ref, g_ref, wq_ref, kt_ref, vb_ref, wo_ref, gates_ref, s_ref, w2f_ref,
                       xs_ref, gall_ref, past_ref, hn_ref, wout_ref,
                       o_ref, so_ref, w2b_ref, x1s_ref, pool_ref, qs_ref,
                       att_ref, oo_ref):
    i = pl.program_id(0)
    last = pl.num_programs(0) - 1

    @pl.when(i < last)
    def _():
        w2b_ref[...] = w2f_ref[...].astype(BF16)
        _attention_block(x_ref, g_ref, kt_ref, vb_ref, wq_ref, wo_ref, o_ref, att_ref)
        _state_step(i * STATE_TOKENS, gates_ref, s_ref, so_ref, oo_ref)

    @pl.when(i == last)
    def _():
        _sample_mix(xs_ref, gall_ref, oo_ref, past_ref, hn_ref, wout_ref, g_ref, wq_ref,
                    x1s_ref, pool_ref, qs_ref, att_ref)


def _attention_block(x_ref, g_ref, kt_ref, vb_ref, wq_ref, wo_ref, o_ref, att_ref):
    H = ATTN_ROWS // 2
    halves = (slice(0, H), slice(H, 2 * H))

    def query(rows):
        hb = _rms(x_ref[0, rows, :], g_ref[...]).astype(BF16)
        return (_dot(hb, wq_ref[...]) * QUERY_SCALE).astype(BF16)

    def scores(q):
        return [_dot(q[:, h * X_HEAD_DIM:(h + 1) * X_HEAD_DIM],
                     kt_ref[0, h * X_HEAD_DIM:(h + 1) * X_HEAD_DIM, :]) for h in range(X_HEADS)]

    def values(rows, ss):
        for h, s in enumerate(ss):
            cols = slice(h * X_HEAD_DIM, (h + 1) * X_HEAD_DIM)
            e = jnp.exp2(s - jnp.max(s, axis=-1, keepdims=True))
            den = jnp.sum(e, axis=-1, keepdims=True)
            o = _dot(e.astype(BF16), vb_ref[0, :, cols]) / den
            att_ref[rows, cols] = o.astype(BF16)

    def project(rows):
        o_ref[0, rows, :] = x_ref[0, rows, :] + _dot(att_ref[rows, :], wo_ref[...])

    s0 = scores(query(halves[0]))
    q1 = query(halves[1])
    values(halves[0], s0)
    s1 = scores(q1)
    project(halves[0])
    values(halves[1], s1)
    project(halves[1])


def _attn_prompt_state_sample(x, g_cross, w_cq, kt, vb, w_co, gates, state, w_ffn_out,
                              x_sample, past, hgrn_norm, w_out):
    nb, L, _ = x.shape
    T = ATTN_ROWS
    G = STATE_TOKENS
    steps_per_seq = L // T
    steps = nb * steps_per_seq
    ns = gates.shape[0]
    assert steps == ns // G
    full = lambda shape: pl.BlockSpec(shape, lambda i: (0,) * len(shape))
    blk = lambda i: jnp.minimum(i, steps - 1)
    st_spec = pl.BlockSpec((G, HEADS, DK, DK), lambda i: (blk(i), 0, 0, 0))
    x_spec = pl.BlockSpec((1, T, D_MODEL),
                          lambda i: (blk(i) // steps_per_seq, blk(i) % steps_per_seq, 0))
    kv_map = lambda i: (blk(i) // steps_per_seq, 0, 0)
    w2_spec = pl.BlockSpec((D_FF // steps, D_MODEL), lambda i: (blk(i), 0))
    return pl.pallas_call(
        _attn_state_kernel,
        grid=(steps + 1,),
        in_specs=[x_spec, full((1, D_MODEL)), full((D_MODEL, D_MODEL)),
                  pl.BlockSpec((1, D_MODEL, MEM_LEN), kv_map),
                  pl.BlockSpec((1, MEM_LEN, D_MODEL), kv_map),
                  full((D_MODEL, D_MODEL)),
                  pl.BlockSpec((G, 4 * 512), lambda i: (blk(i), 0)), st_spec, w2_spec,
                  full((ns, D_MODEL)), full((ns, 6 * 512)), full((POOL_STATE, ns, POOL_WIDTH)),
                  full((1, HGRN_WIDTH)), full((D_MODEL, D_MODEL))],
        out_specs=[x_spec, st_spec, w2_spec,
                   full((ns, D_MODEL)), full((POOL_STATE, ns, POOL_WIDTH)),
                   full((ns, 2 * X_HEADS, 128))],
        out_shape=[jax.ShapeDtypeStruct((nb, L, D_MODEL), F32),
                   jax.ShapeDtypeStruct((ns, HEADS, DK, DK), F32),
                   jax.ShapeDtypeStruct((D_FF, D_MODEL), BF16),
                   jax.ShapeDtypeStruct((ns, D_MODEL), F32),
                   jax.ShapeDtypeStruct((POOL_STATE, ns, POOL_WIDTH), F32),
                   jax.ShapeDtypeStruct((ns, 2 * X_HEADS, 128), F32)],
        scratch_shapes=[pltpu.VMEM((T, D_MODEL), BF16), pltpu.VMEM((ns, HGRN_WIDTH), F32)],
        compiler_params=pltpu.CompilerParams(
            dimension_semantics=("arbitrary",), vmem_limit_bytes=VMEM_LIMIT_BYTES),
        name="attn_prompt_state_sample",
    )(x, g_cross, w_cq, kt, vb, w_co, gates, state, w_ffn_out,
      x_sample, gates, past, hgrn_norm, w_out)


def _zero_after(x):
    u = lax.bitcast_convert_type(x, jnp.uint32)
    z = lax.shift_right_logical(lax.shift_right_logical(u, jnp.uint32(16)), jnp.uint32(16))
    return lax.bitcast_convert_type(z, F32)


def _ffn_body(x, g_ref, w1_ref, w2_ref, gf_ref, fillers=()):
    hb = _rms(x, g_ref[...]).astype(BF16)

    def up(c0, c1):
        return _dot(hb, w1_ref[:, c0:c1]), _dot(hb, w1_ref[:, D_FF + c0:D_FF + c1])

    def add_to_first_tile(m, z):
        top = jnp.concatenate([m[0:8, 0:128] + z, m[0:8, 128:]], axis=1)
        return jnp.concatenate([top, m[8:, :]], axis=0)

    y = x
    fillers = list(fillers)
    assert len(fillers) <= len(FFN_COL_CHUNKS)
    nxt = up(*FFN_COL_CHUNKS[0])
    for i, (c0, c1) in enumerate(FFN_COL_CHUNKS):
        a, bg = nxt
        if i + 1 < len(FFN_COL_CHUNKS):
            nxt = up(*FFN_COL_CHUNKS[i + 1])
        if fillers:
            bg = add_to_first_tile(bg, _zero_after(fillers.pop(0)()))
        act = (a * _sigmoid(a) * bg).astype(BF16)
        y = y + _dot(act, w2_ref[c0:c1, :])
    return _rms(y, gf_ref[...])


def _split_heads(x):
    lead = x.shape[:-2]
    x = x.reshape(lead + (X_HEADS, 2, 128))
    x = jnp.swapaxes(x, -3, -2)
    return x.reshape(lead + (2 * X_HEADS, 128))


def _merge_heads(x):
    lead = x.shape[:-2]
    x = x.reshape(lead + (2, X_HEADS, 128))
    x = jnp.swapaxes(x, -3, -2)
    return x.reshape(lead + (X_HEADS, X_HEAD_DIM))


def _memory_attention(j, base, q_ref, k_ref, v_ref, o_ref):
    prod = k_ref[j] * q_ref[base + j][None]
    s = jnp.sum(prod + pltpu.roll(prod, X_HEADS, axis=1), axis=-1, keepdims=True)
    e = jnp.exp2(s - jnp.max(s, axis=0, keepdims=True))
    den = jnp.sum(e, axis=0)
    o = jnp.sum(e * v_ref[j], axis=0) / den
    o_ref[base + j] = o
    return o


def _ffn_attn_kernel(x_ref, g_ref, w1_ref, w2_ref, gf_ref, q_ref, k_ref, v_ref, xs_ref, wo_ref,
                     o_ref, ys_ref, att_ref):
    i = pl.program_id(0)
    last = pl.num_programs(0) - 1

    @pl.when(i < last)
    def _():
        base = i * ATTN_TOKENS
        fillers = [functools.partial(_memory_attention, j, base, q_ref, k_ref, v_ref, att_ref)
                   for j in range(ATTN_TOKENS)]
        o_ref[...] = _ffn_body(x_ref[...], g_ref, w1_ref, w2_ref, gf_ref, fillers)

    @pl.when(i == last)
    def _():
        att = jnp.concatenate([att_ref[:, half * X_HEADS + head, :]
                               for head in range(X_HEADS) for half in range(2)], axis=1)
        x2 = xs_ref[...] + _dot(att.astype(BF16), wo_ref[...])
        ys_ref[:, 0, :] = _ffn_body(x2, g_ref, w1_ref, w2_ref, gf_ref)


def _ffn_prompt_attn_sample(x, g_ffn, w1, w2, g_final, q, cache_k, cache_v, x_sample, w_co):
    n = x.shape[0]
    ns = q.shape[0]
    T = FFN_ROWS
    G = ATTN_TOKENS
    steps = n // T
    assert steps == ns // G
    full = lambda shape: pl.BlockSpec(shape, lambda i: (0,) * len(shape))
    blk = lambda i: jnp.minimum(i, steps - 1)
    kv_spec = pl.BlockSpec((G, MEM_LEN, 2 * X_HEADS, 128), lambda i: (blk(i), 0, 0, 0))
    q_spec = full((ns, 2 * X_HEADS, 128))
    y, ys = pl.pallas_call(
        _ffn_attn_kernel,
        grid=(steps + 1,),
        in_specs=[pl.BlockSpec((T, D_MODEL), lambda i: (blk(i), 0)),
                  full((1, D_MODEL)), full((D_MODEL, 2 * D_FF)), full((D_FF, D_MODEL)),
                  full((1, D_MODEL)), q_spec, kv_spec, kv_spec,
                  full((ns, D_MODEL)), full((D_MODEL, D_MODEL))],
        out_specs=[pl.BlockSpec((T, D_MODEL), lambda i: (blk(i), 0)),
                   full((ns, 1, D_MODEL))],
        out_shape=[jax.ShapeDtypeStruct((n, D_MODEL), F32),
                   jax.ShapeDtypeStruct((ns, 1, D_MODEL), F32)],
        scratch_shapes=[pltpu.VMEM((ns, 2 * X_HEADS, 128), F32)],
        compiler_params=pltpu.CompilerParams(
            dimension_semantics=("arbitrary",), vmem_limit_bytes=VMEM_LIMIT_BYTES),
        name="ffn_prompt_attn_sample",
    )(x, g_ffn, w1, w2, g_final, q, cache_k, cache_v, x_sample, w_co)
    return y, ys


def kernel(x_prompt, x_sample, mem_prompt, state_hgrn, state_pool, cache_mem_k, cache_mem_v,
           g_mix, w_in, hgrn_lb, hgrn_norm, pool_mix, pool_scale, w_out, g_mem, w_mem_kv,
           g_cross, w_cq, w_co, g_ffn, w_ffn_in, w_ffn_out, g_final):
    nb, L, _ = x_prompt.shape
    ns = x_sample.shape[0]

    g_final2 = g_final.reshape(1, D_MODEL)

    mem_k, mem_v, kt, vb, w_in_b, w_out_b, w_cq_b, w_co_b, gates, xs = _mem_kv_sample_gates(
        mem_prompt, g_mem, w_mem_kv[0], w_in[0], w_out[0], w_cq[0], w_co[0], pool_mix[0], pool_scale,
        x_sample, g_mix, hgrn_lb)
    x1, hgrn_p, pool_p, w1_b = _mixer_prompt(x_prompt, g_mix, w_in_b, hgrn_lb, hgrn_norm, w_out_b,
                                             w_ffn_in[0])
    x2, hgrn_s, w2_b, x1s, pool_s, qs = _attn_prompt_state_sample(
        x1, g_cross, w_cq_b, kt, vb, w_co_b, gates, state_hgrn[0], w_ffn_out[0],
        xs, jnp.swapaxes(state_pool[0], 0, 1), hgrn_norm, w_out_b)
    y_prompt, y_sample = _ffn_prompt_attn_sample(
        x2.reshape(nb * L, D_MODEL), g_ffn, w1_b, w2_b, g_final2, qs,
        _split_heads(cache_mem_k[0]), _split_heads(cache_mem_v[0]), x1s, w_co_b)

    return (y_prompt.reshape(nb, L, D_MODEL),
            y_sample,
            hgrn_p[None],
            pool_p[None],
            _merge_heads(mem_k)[None],
            _merge_heads(mem_v)[None],
            hgrn_s[None],
            jnp.swapaxes(pool_s, 0, 1)[None])
```

```python
import functools

import jax
import jax.numpy as jnp
from jax import lax
from jax.experimental import pallas as pl
from jax.experimental.pallas import tpu as pltpu

F32 = jnp.float32
BF16 = jnp.bfloat16

D_MODEL = 1024
HGRN_WIDTH = 512
HEADS = 4
DK = 128
CHUNK = 64
POOL_WIDTH = 512
POOL_WINDOWS = (2, 4, 8, 16)
POOL_GROUP = 128
POOL_STATE = 15
IN_PROJ = 4 * HGRN_WIDTH + POOL_WIDTH
MEM_LEN = 256
X_HEADS = 4
X_HEAD_DIM = 256
D_FF = 2816
EPS = 1e-6
ATTN_SCALE = X_HEAD_DIM ** -0.5
LOG2_E = 1.4426950408889634
QUERY_SCALE = ATTN_SCALE * LOG2_E

VMEM_LIMIT_BYTES = 56 * 1024 * 1024

MIX_ROWS = 512
MIX_BLOCKS_PER_SEQ = 4
IN_PROJ_PIECE = 256
MIX_FILLER_SCHEDULE = (3, 0, 0) + (1, 0, 1, 0, 1, 0, 1, 0) + (2, 2)
TRI_ROWS = 256
ATTN_ROWS = 1024
SAMPLE_FFN_COLS = 1408
SPLIT_PITCH = MEM_LEN + 4
FFN_ROWS = 512
STATE_TOKENS = 8
ATTN_TOKENS = 4
FFN_COL_CHUNKS = tuple((c, min(c + 512, D_FF)) for c in range(0, D_FF, 512))


def _dot(a, b):
    return jnp.dot(a, b, preferred_element_type=F32)


def _dot_nt(a, b):
    return lax.dot_general(a, b, (((1,), (1,)), ((), ())), preferred_element_type=F32)


def _dot_tn(a, b):
    return lax.dot_general(a, b, (((0,), (0,)), ((), ())), preferred_element_type=F32)


def _rms(x, g):
    ms = jnp.mean(x * x, axis=-1, keepdims=True)
    return x * lax.rsqrt(ms + EPS) * g


def _sigmoid(x):
    return 1.0 / (1.0 + jnp.exp(-x))


def _lower_bound(lb_ref):
    t = lb_ref[...]
    m = jnp.max(t, axis=0, keepdims=True)
    e = jnp.exp(t - m)
    return e[0:1, :] / jnp.sum(e, axis=0, keepdims=True)


def _gates(proj_q, proj_f, lb):
    qq = proj_q * _sigmoid(proj_q)
    sig = _sigmoid(proj_f)
    fgate = lb + (1.0 - lb) * sig
    kk = (1.0 - lb) * (1.0 - sig)
    return qq, fgate, kk


def _split2(x):
    hi = x.astype(BF16)
    return hi, (x - hi.astype(F32)).astype(BF16)


def _split3(x):
    hi = x.astype(BF16)
    r1 = x - hi.astype(F32)
    mid = r1.astype(BF16)
    lo = (r1 - mid.astype(F32)).astype(BF16)
    return hi, mid, lo


def _sample_gates(x, g_ref, winb_ref, lb_ref, o_ref, x2d_ref):
    x2d_ref[...] = x
    hb = _rms(x, g_ref[...]).astype(BF16)
    proj = _dot(hb, winb_ref[...])
    lb = _lower_bound(lb_ref)
    qq, fgate, kk = _gates(proj[:, 0:512], proj[:, 512:1024], lb)
    o_ref[:, 0:512] = qq
    o_ref[:, 512:1024] = fgate
    o_ref[:, 1024:1536] = kk
    o_ref[:, 1536:2048] = proj[:, 1024:1536]
    o_ref[:, 2048:2560] = _sigmoid(proj[:, 1536:2048])
    o_ref[:, 2560:3072] = proj[:, 2048:2560]


def _memkv_kernel(mem_ref, g_ref, w_ref, win_ref, wout_ref, wcq_ref, wco_ref, pmix_ref, ps_ref,
                  xs_ref, gmix_ref, lb_ref,
                  k_ref, v_ref, kt_ref, vb_ref, winb_ref, woutb_ref, wcqb_ref, wcob_ref,
                  gates_ref, xs2d_ref,
                  wb_ref, rows_ref):
    b = pl.program_id(0)
    nb = pl.num_programs(0) - 1
    first_pool_block = HGRN_WIDTH // POOL_GROUP

    @pl.when(b == 0)
    def _():
        wb_ref[...] = w_ref[...].astype(BF16)

    @pl.when(b < nb)
    def _():
        wrows = win_ref.shape[0]
        winb_ref[pl.ds(pl.multiple_of(b * wrows, wrows), wrows), :] = win_ref[...].astype(BF16)
        wcqb_ref[...] = wcq_ref[...].astype(BF16)
        wcob_ref[...] = wco_ref[...].astype(BF16)

        @pl.when(b < first_pool_block)
        def _():
            woutb_ref[...] = wout_ref[...].astype(BF16)

        @pl.when(b >= first_pool_block)
        def _():
            gi = b - first_pool_block
            a_hi, a_lo = _split2(pmix_ref[gi] * ps_ref[gi])
            w_hi, w_lo = _split2(wout_ref[...])
            woutb_ref[...] = (_dot(a_hi, w_hi) + _dot(a_hi, w_lo) + _dot(a_lo, w_hi)).astype(BF16)

        h = _rms(mem_ref[0], g_ref[...]).astype(BF16)
        kv = _dot(h, wb_ref[...])
        k = kv[:, :D_MODEL]
        v = kv[:, D_MODEL:]
        kt_ref[0] = k.T.astype(BF16)
        vb_ref[0] = v.astype(BF16)
        for val, out_ref in ((k, k_ref), (v, v_ref)):
            for r in range(2 * X_HEADS):
                half, head = divmod(r, X_HEADS)
                c0 = head * X_HEAD_DIM + half * 128
                rows_ref[r * SPLIT_PITCH:r * SPLIT_PITCH + MEM_LEN, :] = val[:, c0:c0 + 128]
            for m in range(MEM_LEN):
                out_ref[0, m] = rows_ref[pl.ds(m, 2 * X_HEADS, stride=SPLIT_PITCH), :]

    @pl.when(b == nb)
    def _():
        _sample_gates(xs_ref[:, 0, :], gmix_ref, winb_ref, lb_ref, gates_ref, xs2d_ref)


def _mem_kv_sample_gates(mem, g_mem, w_kv, w_in, w_out, w_cq, w_co, pool_mix, pool_scale,
                         x_sample, g_mix, hgrn_lb):
    nb = mem.shape[0]
    ns = x_sample.shape[0]
    full = lambda shape: pl.BlockSpec(shape, lambda b: (0,) * len(shape))
    blk = lambda b: jnp.minimum(b, nb - 1)
    split_spec = pl.BlockSpec((1, MEM_LEN, 2 * X_HEADS, 128), lambda b: (blk(b), 0, 0, 0))
    wrows = D_MODEL // nb
    assert wrows == POOL_GROUP
    win_spec = pl.BlockSpec((wrows, IN_PROJ), lambda b: (blk(b), 0))
    wout_spec = pl.BlockSpec((wrows, D_MODEL), lambda b: (blk(b), 0))
    return pl.pallas_call(
        _memkv_kernel,
        grid=(nb + 1,),
        in_specs=[pl.BlockSpec((1, MEM_LEN, D_MODEL), lambda b: (blk(b), 0, 0)),
                  full((1, D_MODEL)), full((D_MODEL, 2 * D_MODEL)), win_spec, wout_spec,
                  wout_spec, wout_spec,
                  full((4, POOL_GROUP, POOL_GROUP)), full((4, 1, POOL_GROUP)),
                  full((ns, 1, D_MODEL)), full((1, D_MODEL)), full((2, HGRN_WIDTH))],
        out_specs=[split_spec, split_spec,
                   pl.BlockSpec((1, D_MODEL, MEM_LEN), lambda b: (blk(b), 0, 0)),
                   pl.BlockSpec((1, MEM_LEN, D_MODEL), lambda b: (blk(b), 0, 0)),
                   full((D_MODEL, IN_PROJ)), wout_spec, wout_spec, wout_spec,
                   full((ns, 6 * 512)), full((ns, D_MODEL))],
        out_shape=[jax.ShapeDtypeStruct((nb, MEM_LEN, 2 * X_HEADS, 128), F32),
                   jax.ShapeDtypeStruct((nb, MEM_LEN, 2 * X_HEADS, 128), F32),
                   jax.ShapeDtypeStruct((nb, D_MODEL, MEM_LEN), BF16),
                   jax.ShapeDtypeStruct((nb, MEM_LEN, D_MODEL), BF16),
                   jax.ShapeDtypeStruct((D_MODEL, IN_PROJ), BF16),
                   jax.ShapeDtypeStruct((D_MODEL, D_MODEL), BF16),
                   jax.ShapeDtypeStruct((D_MODEL, D_MODEL), BF16),
                   jax.ShapeDtypeStruct((D_MODEL, D_MODEL), BF16),
                   jax.ShapeDtypeStruct((ns, 6 * 512), F32),
                   jax.ShapeDtypeStruct((ns, D_MODEL), F32)],
        scratch_shapes=[pltpu.VMEM((D_MODEL, 2 * D_MODEL), BF16),
                        pltpu.VMEM((2 * X_HEADS * SPLIT_PITCH, 128), F32)],
        compiler_params=pltpu.CompilerParams(
            dimension_semantics=("arbitrary",), vmem_limit_bytes=VMEM_LIMIT_BYTES),
        name="mem_kv_sample_gates",
    )(mem, g_mem, w_kv, w_in, w_out, w_cq, w_co, pool_mix, pool_scale.reshape(4, 1, POOL_GROUP),
      x_sample, g_mix, hgrn_lb)


def _mix_block(n, x, proj_ref, out_ref, fillers, lb_ref, hn_ref, wout_ref,
               st_ref, ext_ref, qq_ref, kk_ref, b_ref, mrg_ref):
    T = MIX_ROWS
    fillers = list(fillers)
    schedule = list(MIX_FILLER_SCHEDULE)
    assert len(schedule) == 5 + T // CHUNK and sum(schedule) == len(fillers)

    def fill():
        for _ in range(schedule.pop(0)):
            fillers.pop(0)()

    l = n % MIX_BLOCKS_PER_SEQ
    first = l == 0
    for h in range(HEADS):
        st_ref[h] = jnp.where(first, 0.0, st_ref[h])
    ext_ref[0:16, :] = jnp.where(first, 0.0, ext_ref[0:16, :])

    fill()
    lb = _lower_bound(lb_ref)
    qq, fgate, kk = _gates(proj_ref[:, 0:512], proj_ref[:, 512:1024], lb)
    qq_ref[...] = qq
    kk_ref[...] = kk
    fill()

    r = lax.broadcasted_iota(jnp.int32, (TRI_ROWS, TRI_ROWS), 0)
    c = lax.broadcasted_iota(jnp.int32, (TRI_ROWS, TRI_ROWS), 1)
    tri = jnp.where((c <= r) & (c >= (r & -CHUNK)), 1.0, 0.0).astype(BF16)
    logf = jnp.log2(fgate)
    for blk in range(T // TRI_ROWS):
        rows = slice(blk * TRI_ROWS, (blk + 1) * TRI_ROWS)
        hi, mid, lo = _split3(logf[rows])
        b_ref[rows, :] = _dot(tri, hi) + _dot(tri, mid) + _dot(tri, lo)
    fill()

    cr = lax.broadcasted_iota(jnp.int32, (CHUNK, CHUNK), 0)
    cc = lax.broadcasted_iota(jnp.int32, (CHUNK, CHUNK), 1)
    causal = cc <= cr
    mid_row = (CHUNK - 1) // 2

    for ci in range(T // CHUNK):
        r0 = ci * CHUNK
        rows = slice(r0, r0 + CHUNK)
        scores, inters, vals = [], [], []
        for h in range(HEADS):
            cols = slice(h * DK, (h + 1) * DK)
            b = b_ref[rows, cols]
            m = b_ref[r0 + mid_row:r0 + mid_row + 1, cols]
            b_end = b_ref[r0 + CHUNK - 1:r0 + CHUNK, cols]
            e1 = jnp.exp2(b - m)
            e2 = jnp.exp2(m - b)
            q1 = qq_ref[rows, cols] * e1
            q0 = q1 * jnp.exp2(m)
            ks = kk_ref[rows, cols] * e2
            k2 = ks * jnp.exp2(b_end - m)
            v = proj_ref[rows, 1024 + h * DK:1024 + (h + 1) * DK].astype(BF16)
            st = st_ref[h]
            scores.append(_dot_nt(q1.astype(BF16), ks.astype(BF16)))
            inters.append(_dot_nt(q0.astype(BF16), st.astype(BF16)))
            st_ref[h] = st * jnp.exp2(b_end) + _dot_tn(v, k2.astype(BF16))
            vals.append(v)
        fill()
        for h in range(HEADS):
            cols = slice(h * DK, (h + 1) * DK)
            a = jnp.where(causal, scores[h], 0.0)
            o = inters[h] + _dot(a.astype(BF16), vals[h])
            o = o * lax.rsqrt(jnp.mean(o * o, axis=-1, keepdims=True) + EPS)
            o = o * hn_ref[:, cols]
            g = proj_ref[rows, 1536 + h * DK:1536 + (h + 1) * DK]
            mrg_ref[rows, cols] = (o * _sigmoid(g)).astype(BF16)

    ext_ref[16:16 + T, :] = proj_ref[:, 2048:2560]
    pos = l * T + lax.broadcasted_iota(jnp.int32, (16, POOL_GROUP), 0)
    for gi, w in enumerate(POOL_WINDOWS):
        if gi % 2 == 0:
            fill()
        cols = slice(gi * POOL_GROUP, (gi + 1) * POOL_GROUP)
        s = ext_ref[:, cols]
        u = s[16:]
        shift = 1
        while shift < w:
            s = s + pltpu.roll(s, shift, axis=0)
            shift *= 2
        acc = s[16:]
        cnt = jnp.minimum(pos + 1, w).astype(F32)
        pooled = jnp.concatenate([acc[0:16] / cnt, acc[16:] * (1.0 / w)], axis=0) - u
        mrg_ref[:, HGRN_WIDTH + gi * POOL_GROUP:HGRN_WIDTH + (gi + 1) * POOL_GROUP] = pooled.astype(BF16)

    ext_ref[0:16, :] = ext_ref[T:T + 16, :]
    out_ref[...] = x + _dot(mrg_ref[...], wout_ref[...])


def _mixer_kernel(xa_ref, xn_ref, g_ref, win_ref, lb_ref, hn_ref, wout_ref,
                  w1f_ref,
                  x1_ref, hst_ref, pst_ref, w1b_ref,
                  p0_ref, p1_ref, hb_ref, st_ref, ext_ref, qq_ref, kk_ref, b_ref, mrg_ref):
    T = MIX_ROWS
    g = pl.program_id(0)
    w1b_ref[...] = w1f_ref[...].astype(BF16)
    rest = (lb_ref, hn_ref, wout_ref, st_ref, ext_ref, qq_ref, kk_ref, b_ref, mrg_ref)

    def in_proj_pieces(x_ref, rows, p_ref):
        def prep():
            hb_ref[...] = _rms(x_ref[rows, :], g_ref[...]).astype(BF16)

        def piece(k):
            cols = slice(k * IN_PROJ_PIECE, (k + 1) * IN_PROJ_PIECE)
            p_ref[:, cols] = _dot(hb_ref[...], win_ref[:, cols])

        return [prep] + [functools.partial(piece, k) for k in range(IN_PROJ // IN_PROJ_PIECE)]

    @pl.when(g == 0)
    def _():
        st_ref[...] = jnp.zeros_like(st_ref)
        ext_ref[0:16, :] = jnp.zeros((16, POOL_WIDTH), F32)
        for f in in_proj_pieces(xa_ref, slice(0, T), p0_ref):
            f()

    _mix_block(2 * g, xa_ref[0:T, :], p0_ref, x1_ref.at[0:T, :],
               in_proj_pieces(xa_ref, slice(T, 2 * T), p1_ref), *rest)
    _mix_block(2 * g + 1, xa_ref[T:2 * T, :], p1_ref, x1_ref.at[T:2 * T, :],
               in_proj_pieces(xn_ref, slice(0, T), p0_ref), *rest)

    steps_per_seq = MIX_BLOCKS_PER_SEQ // 2

    @pl.when(g % steps_per_seq == steps_per_seq - 1)
    def _():
        for h in range(HEADS):
            hst_ref[0, h] = st_ref[h].T

    for seq in range(pst_ref.shape[1]):
        @pl.when(g == seq * steps_per_seq + steps_per_seq - 1)
        def _():
            pst_ref[:, seq, :] = ext_ref[T + 1:T + 16, :]


def _mixer_prompt(x, g_mix, w_in, hgrn_lb, hgrn_norm, w_out, w_ffn_in):
    nb, L, _ = x.shape
    T = MIX_ROWS
    assert L // T == MIX_BLOCKS_PER_SEQ and MIX_BLOCKS_PER_SEQ % 2 == 0
    n_blocks = nb * MIX_BLOCKS_PER_SEQ
    steps = n_blocks // 2
    steps_per_seq = MIX_BLOCKS_PER_SEQ // 2
    x2d = x.reshape(nb * L, D_MODEL)
    full = lambda shape: pl.BlockSpec(shape, lambda g: (0,) * len(shape))
    rows = lambda w: pl.BlockSpec((w.shape[0] // steps, w.shape[1]), lambda g: (g, 0))
    side = (w_ffn_in,)
    x1, hst, pst, w1b = pl.pallas_call(
        _mixer_kernel,
        grid=(steps,),
        in_specs=[pl.BlockSpec((2 * T, D_MODEL), lambda g: (g, 0)),
                  pl.BlockSpec((T, D_MODEL), lambda g: (jnp.minimum(2 * g + 2, n_blocks - 1), 0)),
                  full((1, D_MODEL)), full((D_MODEL, IN_PROJ)), full((2, HGRN_WIDTH)),
                  full((1, HGRN_WIDTH)), full((D_MODEL, D_MODEL))] + [rows(w) for w in side],
        out_specs=[pl.BlockSpec((2 * T, D_MODEL), lambda g: (g, 0)),
                   pl.BlockSpec((1, HEADS, DK, DK), lambda g: (g // steps_per_seq, 0, 0, 0)),
                   full((POOL_STATE, nb, POOL_WIDTH))]
                  + [rows(w) for w in side],
        out_shape=[jax.ShapeDtypeStruct((nb * L, D_MODEL), F32),
                   jax.ShapeDtypeStruct((nb, HEADS, DK, DK), F32),
                   jax.ShapeDtypeStruct((POOL_STATE, nb, POOL_WIDTH), F32)]
                  + [jax.ShapeDtypeStruct(w.shape, BF16) for w in side],
        scratch_shapes=[pltpu.VMEM((T, IN_PROJ), F32),
                        pltpu.VMEM((T, IN_PROJ), F32),
                        pltpu.VMEM((T, D_MODEL), BF16),
                        pltpu.VMEM((HEADS, DK, DK), F32),
                        pltpu.VMEM((16 + T, POOL_WIDTH), F32),
                        pltpu.VMEM((T, HGRN_WIDTH), F32),
                        pltpu.VMEM((T, HGRN_WIDTH), F32),
                        pltpu.VMEM((T, HGRN_WIDTH), F32),
                        pltpu.VMEM((T, D_MODEL), BF16)],
        compiler_params=pltpu.CompilerParams(
            dimension_semantics=("arbitrary",), vmem_limit_bytes=VMEM_LIMIT_BYTES),
        name="mixer_prompt",
    )(x2d, x2d, g_mix, w_in, hgrn_lb, hgrn_norm, w_out, *side)
    return x1.reshape(nb, L, D_MODEL), hst, pst, w1b


def _state_step(base, gates_ref, s_ref, so_ref, o_ref):
    G = STATE_TOKENS
    pad = jnp.zeros((DK - G, DK), F32)
    for h in range(HEADS):
        cols = slice(h * DK, (h + 1) * DK)
        qt = jnp.concatenate([gates_ref[:, h * DK:(h + 1) * DK], pad], axis=0).T
        ft = jnp.concatenate([gates_ref[:, 512 + h * DK:512 + (h + 1) * DK], pad], axis=0).T
        kt = jnp.concatenate([gates_ref[:, 1024 + h * DK:1024 + (h + 1) * DK], pad], axis=0).T
        readouts = []
        for j in range(G):
            v = gates_ref[j:j + 1, 1536 + h * DK:1536 + (h + 1) * DK]
            s_new = ft[:, j:j + 1] * s_ref[j, h] + kt[:, j:j + 1] * v
            so_ref[j, h] = s_new
            readouts.append(jnp.sum(qt[:, j:j + 1] * s_new, axis=0, keepdims=True))
        o_ref[pl.ds(pl.multiple_of(base, G), G), cols] = jnp.concatenate(readouts, axis=0)


def _sample_mix(x_ref, gates_ref, o_ref, past_ref, hn_ref, wout_ref, gc_ref, wq_ref,
                x1_ref, pool_ref, q_ref, mrg_ref):
    n = x_ref.shape[0]
    for h in range(HEADS):
        cols = slice(h * DK, (h + 1) * DK)
        o = o_ref[:, cols]
        o = o * lax.rsqrt(jnp.mean(o * o, axis=-1, keepdims=True) + EPS) * hn_ref[:, cols]
        mrg_ref[0:n, cols] = (o * gates_ref[:, 2048 + h * DK:2048 + (h + 1) * DK]).astype(BF16)
    for gi, w in enumerate(POOL_WINDOWS):
        cols = slice(gi * POOL_GROUP, (gi + 1) * POOL_GROUP)
        u = gates_ref[:, 2560 + gi * POOL_GROUP:2560 + (gi + 1) * POOL_GROUP]
        acc = u
        for j in range(1, w):
            acc = acc + past_ref[POOL_STATE - j, :, cols]
        pooled = acc / float(w) - u
        mrg_ref[0:n, HGRN_WIDTH + gi * POOL_GROUP:HGRN_WIDTH + (gi + 1) * POOL_GROUP] = pooled.astype(BF16)
    pool_ref[0:POOL_STATE - 1] = past_ref[1:POOL_STATE]
    pool_ref[POOL_STATE - 1] = gates_ref[:, 2560:3072]
    x1 = x_ref[...] + _dot(mrg_ref[0:n, :], wout_ref[...])
    x1_ref[...] = x1
    hb = _rms(x1, gc_ref[...]).astype(BF16)
    q = _dot(hb, wq_ref[...]) * QUERY_SCALE
    for r in range(2 * X_HEADS):
        half, head = divmod(r, X_HEADS)
        c0 = head * X_HEAD_DIM + half * 128
        q_ref[:, r, :] = q[:, c0:c0 + 128]


def _attn_state_kernel(x_ref, g_ref, wq_ref, kt_ref, vb_ref, wo_ref, gates_ref, s_ref, w2f_ref,
                       xs_ref, gall_ref, past_ref, hn_ref, wout_ref,
                       o_ref, so_ref, w2b_ref, x1s_ref, pool_ref, qs_ref,
                       att_ref, oo_ref):
    i = pl.program_id(0)
    last = pl.num_programs(0) - 1

    @pl.when(i < last)
    def _():
        w2b_ref[...] = w2f_ref[...].astype(BF16)
        _attention_block(x_ref, g_ref, kt_ref, vb_ref, wq_ref, wo_ref, o_ref, att_ref)
        _state_step(i * STATE_TOKENS, gates_ref, s_ref, so_ref, oo_ref)

    @pl.when(i == last)
    def _():
        _sample_mix(xs_ref, gall_ref, oo_ref, past_ref, hn_ref, wout_ref, g_ref, wq_ref,
                    x1s_ref, pool_ref, qs_ref, att_ref)


def _attention_block(x_ref, g_ref, kt_ref, vb_ref, wq_ref, wo_ref, o_ref, att_ref):
    H = ATTN_ROWS // 2
    halves = (slice(0, H), slice(H, 2 * H))

    def query(rows):
        hb = _rms(x_ref[0, rows, :], g_ref[...]).astype(BF16)
        return (_dot(hb, wq_ref[...]) * QUERY_SCALE).astype(BF16)

    def scores(q):
        return [_dot(q[:, h * X_HEAD_DIM:(h + 1) * X_HEAD_DIM],
                     kt_ref[0, h * X_HEAD_DIM:(h + 1) * X_HEAD_DIM, :]) for h in range(X_HEADS)]

    def values(rows, ss):
        for h, s in enumerate(ss):
            cols = slice(h * X_HEAD_DIM, (h + 1) * X_HEAD_DIM)
            e = jnp.exp2(s - jnp.max(s, axis=-1, keepdims=True))
            den = jnp.sum(e, axis=-1, keepdims=True)
            o = _dot(e.astype(BF16), vb_ref[0, :, cols]) / den
            att_ref[rows, cols] = o.astype(BF16)

    def project(rows):
        o_ref[0, rows, :] = x_ref[0, rows, :] + _dot(att_ref[rows, :], wo_ref[...])

    s0 = scores(query(halves[0]))
    q1 = query(halves[1])
    values(halves[0], s0)
    s1 = scores(q1)
    project(halves[0])
    values(halves[1], s1)
    project(halves[1])


def _attn_prompt_state_sample(x, g_cross, w_cq, kt, vb, w_co, gates, state, w_ffn_out,
                              x_sample, past, hgrn_norm, w_out):
    nb, L, _ = x.shape
    T = ATTN_ROWS
    G = STATE_TOKENS
    steps_per_seq = L // T
    steps = nb * steps_per_seq
    ns = gates.shape[0]
    assert steps == ns // G
    full = lambda shape: pl.BlockSpec(shape, lambda i: (0,) * len(shape))
    blk = lambda i: jnp.minimum(i, steps - 1)
    st_spec = pl.BlockSpec((G, HEADS, DK, DK), lambda i: (blk(i), 0, 0, 0))
    x_spec = pl.BlockSpec((1, T, D_MODEL),
                          lambda i: (blk(i) // steps_per_seq, blk(i) % steps_per_seq, 0))
    kv_map = lambda i: (blk(i) // steps_per_seq, 0, 0)
    w2_spec = pl.BlockSpec((D_FF // steps, D_MODEL), lambda i: (blk(i), 0))
    return pl.pallas_call(
        _attn_state_kernel,
        grid=(steps + 1,),
        in_specs=[x_spec, full((1, D_MODEL)), full((D_MODEL, D_MODEL)),
                  pl.BlockSpec((1, D_MODEL, MEM_LEN), kv_map),
                  pl.BlockSpec((1, MEM_LEN, D_MODEL), kv_map),
                  full((D_MODEL, D_MODEL)),
                  pl.BlockSpec((G, 4 * 512), lambda i: (blk(i), 0)), st_spec, w2_spec,
                  full((ns, D_MODEL)), full((ns, 6 * 512)), full((POOL_STATE, ns, POOL_WIDTH)),
                  full((1, HGRN_WIDTH)), full((D_MODEL, D_MODEL))],
        out_specs=[x_spec, st_spec, w2_spec,
                   full((ns, D_MODEL)), full((POOL_STATE, ns, POOL_WIDTH)),
                   full((ns, 2 * X_HEADS, 128))],
        out_shape=[jax.ShapeDtypeStruct((nb, L, D_MODEL), F32),
                   jax.ShapeDtypeStruct((ns, HEADS, DK, DK), F32),
                   jax.ShapeDtypeStruct((D_FF, D_MODEL), BF16),
                   jax.ShapeDtypeStruct((ns, D_MODEL), F32),
                   jax.ShapeDtypeStruct((POOL_STATE, ns, POOL_WIDTH), F32),
                   jax.ShapeDtypeStruct((ns, 2 * X_HEADS, 128), F32)],
        scratch_shapes=[pltpu.VMEM((T, D_MODEL), BF16), pltpu.VMEM((ns, HGRN_WIDTH), F32)],
        compiler_params=pltpu.CompilerParams(
            dimension_semantics=("arbitrary",), vmem_limit_bytes=VMEM_LIMIT_BYTES),
        name="attn_prompt_state_sample",
    )(x, g_cross, w_cq, kt, vb, w_co, gates, state, w_ffn_out,
      x_sample, gates, past, hgrn_norm, w_out)


def _zero_after(x):
    u = lax.bitcast_convert_type(x, jnp.uint32)
    z = lax.shift_right_logical(lax.shift_right_logical(u, jnp.uint32(16)), jnp.uint32(16))
    return lax.bitcast_convert_type(z, F32)


def _ffn_body(x, g_ref, w1_ref, w2_ref, gf_ref, fillers=()):
    hb = _rms(x, g_ref[...]).astype(BF16)

    def up(c0, c1):
        return _dot(hb, w1_ref[:, c0:c1]), _dot(hb, w1_ref[:, D_FF + c0:D_FF + c1])

    def add_to_first_tile(m, z):
        top = jnp.concatenate([m[0:8, 0:128] + z, m[0:8, 128:]], axis=1)
        return jnp.concatenate([top, m[8:, :]], axis=0)

    y = x
    fillers = list(fillers)
    assert len(fillers) <= len(FFN_COL_CHUNKS)
    nxt = up(*FFN_COL_CHUNKS[0])
    for i, (c0, c1) in enumerate(FFN_COL_CHUNKS):
        a, bg = nxt
        if i + 1 < len(FFN_COL_CHUNKS):
            nxt = up(*FFN_COL_CHUNKS[i + 1])
        if fillers:
            bg = add_to_first_tile(bg, _zero_after(fillers.pop(0)()))
        act = (a * _sigmoid(a) * bg).astype(BF16)
        y = y + _dot(act, w2_ref[c0:c1, :])
    return _rms(y, gf_ref[...])


def _split_heads(x):
    lead = x.shape[:-2]
    x = x.reshape(lead + (X_HEADS, 2, 128))
    x = jnp.swapaxes(x, -3, -2)
    return x.reshape(lead + (2 * X_HEADS, 128))


def _merge_heads(x):
    lead = x.shape[:-2]
    x = x.reshape(lead + (2, X_HEADS, 128))
    x = jnp.swapaxes(x, -3, -2)
    return x.reshape(lead + (X_HEADS, X_HEAD_DIM))


def _memory_attention(j, base, q_ref, k_ref, v_ref, o_ref):
    prod = k_ref[j] * q_ref[base + j][None]
    s = jnp.sum(prod + pltpu.roll(prod, X_HEADS, axis=1), axis=-1, keepdims=True)
    e = jnp.exp2(s - jnp.max(s, axis=0, keepdims=True))
    den = jnp.sum(e, axis=0)
    o = jnp.sum(e * v_ref[j], axis=0) / den
    o_ref[base + j] = o
    return o


def _ffn_attn_kernel(x_ref, g_ref, w1_ref, w2_ref, gf_ref, q_ref, k_ref, v_ref, xs_ref, wo_ref,
                     o_ref, ys_ref, att_ref):
    i = pl.program_id(0)
    last = pl.num_programs(0) - 1

    @pl.when(i < last)
    def _():
        base = i * ATTN_TOKENS
        fillers = [functools.partial(_memory_attention, j, base, q_ref, k_ref, v_ref, att_ref)
                   for j in range(ATTN_TOKENS)]
        o_ref[...] = _ffn_body(x_ref[...], g_ref, w1_ref, w2_ref, gf_ref, fillers)

    @pl.when(i == last)
    def _():
        att = jnp.concatenate([att_ref[:, half * X_HEADS + head, :]
                               for head in range(X_HEADS) for half in range(2)], axis=1)
        x2 = xs_ref[...] + _dot(att.astype(BF16), wo_ref[...])
        ys_ref[:, 0, :] = _ffn_body(x2, g_ref, w1_ref, w2_ref, gf_ref)


def _ffn_prompt_attn_sample(x, g_ffn, w1, w2, g_final, q, cache_k, cache_v, x_sample, w_co):
    n = x.shape[0]
    ns = q.shape[0]
    T = FFN_ROWS
    G = ATTN_TOKENS
    steps = n // T
    assert steps == ns // G
    full = lambda shape: pl.BlockSpec(shape, lambda i: (0,) * len(shape))
    blk = lambda i: jnp.minimum(i, steps - 1)
    kv_spec = pl.BlockSpec((G, MEM_LEN, 2 * X_HEADS, 128), lambda i: (blk(i), 0, 0, 0))
    q_spec = full((ns, 2 * X_HEADS, 128))
    y, ys = pl.pallas_call(
        _ffn_attn_kernel,
        grid=(steps + 1,),
        in_specs=[pl.BlockSpec((T, D_MODEL), lambda i: (blk(i), 0)),
                  full((1, D_MODEL)), full((D_MODEL, 2 * D_FF)), full((D_FF, D_MODEL)),
                  full((1, D_MODEL)), q_spec, kv_spec, kv_spec,
                  full((ns, D_MODEL)), full((D_MODEL, D_MODEL))],
        out_specs=[pl.BlockSpec((T, D_MODEL), lambda i: (blk(i), 0)),
                   full((ns, 1, D_MODEL))],
        out_shape=[jax.ShapeDtypeStruct((n, D_MODEL), F32),
                   jax.ShapeDtypeStruct((ns, 1, D_MODEL), F32)],
        scratch_shapes=[pltpu.VMEM((ns, 2 * X_HEADS, 128), F32)],
        compiler_params=pltpu.CompilerParams(
            dimension_semantics=("arbitrary",), vmem_limit_bytes=VMEM_LIMIT_BYTES),
        name="ffn_prompt_attn_sample",
    )(x, g_ffn, w1, w2, g_final, q, cache_k, cache_v, x_sample, w_co)
    return y, ys


def kernel(x_prompt, x_sample, mem_prompt, state_hgrn, state_pool, cache_mem_k, cache_mem_v,
           g_mix, w_in, hgrn_lb, hgrn_norm, pool_mix, pool_scale, w_out, g_mem, w_mem_kv,
           g_cross, w_cq, w_co, g_ffn, w_ffn_in, w_ffn_out, g_final):
    nb, L, _ = x_prompt.shape
    ns = x_sample.shape[0]

    g_final2 = g_final.reshape(1, D_MODEL)

    mem_k, mem_v, kt, vb, w_in_b, w_out_b, w_cq_b, w_co_b, gates, xs = _mem_kv_sample_gates(
        mem_prompt, g_mem, w_mem_kv[0], w_in[0], w_out[0], w_cq[0], w_co[0], pool_mix[0], pool_scale,
        x_sample, g_mix, hgrn_lb)
    x1, hgrn_p, pool_p, w1_b = _mixer_prompt(x_prompt, g_mix, w_in_b, hgrn_lb, hgrn_norm, w_out_b,
                                             w_ffn_in[0])
    x2, hgrn_s, w2_b, x1s, pool_s, qs = _attn_prompt_state_sample(
        x1, g_cross, w_cq_b, kt, vb, w_co_b, gates, state_hgrn[0], w_ffn_out[0],
        xs, jnp.swapaxes(state_pool[0], 0, 1), hgrn_norm, w_out_b)
    y_prompt, y_sample = _ffn_prompt_attn_sample(
        x2.reshape(nb * L, D_MODEL), g_ffn, w1_b, w2_b, g_final2, qs,
        _split_heads(cache_mem_k[0]), _split_heads(cache_mem_v[0]), x1s, w_co_b)

    return (y_prompt.reshape(nb, L, D_MODEL),
            y_sample,
            hgrn_p[None],
            jnp.swapaxes(pool_p, 0, 1)[None],
            _merge_heads(mem_k)[None],
            _merge_heads(mem_v)[None],
            hgrn_s[None],
            jnp.swapaxes(pool_s, 0, 1)[None])
```

```python
import functools

import jax
import jax.numpy as jnp
from jax import lax
from jax.experimental import pallas as pl
from jax.experimental.pallas import tpu as pltpu

F32 = jnp.float32
BF16 = jnp.bfloat16

D_MODEL = 1024
HGRN_WIDTH = 512
HEADS = 4
DK = 128
CHUNK = 64
POOL_WIDTH = 512
POOL_WINDOWS = (2, 4, 8, 16)
POOL_GROUP = 128
POOL_STATE = 15
IN_PROJ = 4 * HGRN_WIDTH + POOL_WIDTH
MEM_LEN = 256
X_HEADS = 4
X_HEAD_DIM = 256
D_FF = 2816
EPS = 1e-6
ATTN_SCALE = X_HEAD_DIM ** -0.5
LOG2_E = 1.4426950408889634
QUERY_SCALE = ATTN_SCALE * LOG2_E

VMEM_LIMIT_BYTES = 56 * 1024 * 1024

MIX_ROWS = 512
MIX_BLOCKS_PER_SEQ = 4
IN_PROJ_PIECE = 256
MIX_FILLER_SCHEDULE = (3, 0, 0) + (1, 0, 1, 0, 1, 0, 1, 0) + (2, 2)
TRI_ROWS = 256
ATTN_ROWS = 1024
SAMPLE_FFN_COLS = 1408
SPLIT_PITCH = MEM_LEN + 4
FFN_ROWS = 512
STATE_TOKENS = 8
ATTN_TOKENS = 4
FFN_PASS_WIDTHS = (512, 768, 768, 768)
FFN_COL_CHUNKS = tuple((sum(FFN_PASS_WIDTHS[:i]), sum(FFN_PASS_WIDTHS[:i + 1]))
                       for i in range(len(FFN_PASS_WIDTHS)))
assert FFN_COL_CHUNKS[-1][1] == D_FF


def _dot(a, b):
    return jnp.dot(a, b, preferred_element_type=F32)


def _dot_nt(a, b):
    return lax.dot_general(a, b, (((1,), (1,)), ((), ())), preferred_element_type=F32)


def _dot_tn(a, b):
    return lax.dot_general(a, b, (((0,), (0,)), ((), ())), preferred_element_type=F32)


def _rms(x, g):
    ms = jnp.mean(x * x, axis=-1, keepdims=True)
    return x * lax.rsqrt(ms + EPS) * g


def _sigmoid(x):
    return 1.0 / (1.0 + jnp.exp(-x))


def _lower_bound(lb_ref):
    t = lb_ref[...]
    m = jnp.max(t, axis=0, keepdims=True)
    e = jnp.exp(t - m)
    return e[0:1, :] / jnp.sum(e, axis=0, keepdims=True)


def _gates(proj_q, proj_f, lb):
    qq = proj_q * _sigmoid(proj_q)
    sig = _sigmoid(proj_f)
    fgate = lb + (1.0 - lb) * sig
    kk = (1.0 - lb) * (1.0 - sig)
    return qq, fgate, kk


def _split2(x):
    hi = x.astype(BF16)
    return hi, (x - hi.astype(F32)).astype(BF16)


def _split3(x):
    hi = x.astype(BF16)
    r1 = x - hi.astype(F32)
    mid = r1.astype(BF16)
    lo = (r1 - mid.astype(F32)).astype(BF16)
    return hi, mid, lo


def _sample_gates(x, g_ref, winb_ref, lb_ref, o_ref, x2d_ref):
    x2d_ref[...] = x
    hb = _rms(x, g_ref[...]).astype(BF16)
    proj = _dot(hb, winb_ref[...])
    lb = _lower_bound(lb_ref)
    qq, fgate, kk = _gates(proj[:, 0:512], proj[:, 512:1024], lb)
    o_ref[:, 0:512] = qq
    o_ref[:, 512:1024] = fgate
    o_ref[:, 1024:1536] = kk
    o_ref[:, 1536:2048] = proj[:, 1024:1536]
    o_ref[:, 2048:2560] = _sigmoid(proj[:, 1536:2048])
    o_ref[:, 2560:3072] = proj[:, 2048:2560]


def _memkv_kernel(mem_ref, g_ref, w_ref, win_ref, wout_ref, wcq_ref, wco_ref, pmix_ref, ps_ref,
                  xs_ref, gmix_ref, lb_ref,
                  k_ref, v_ref, kt_ref, vb_ref, winb_ref, woutb_ref, wcqb_ref, wcob_ref,
                  gates_ref, xs2d_ref,
                  wb_ref, rows_ref):
    b = pl.program_id(0)
    nb = pl.num_programs(0) - 1
    first_pool_block = HGRN_WIDTH // POOL_GROUP

    @pl.when(b == 0)
    def _():
        wb_ref[...] = w_ref[...].astype(BF16)

    @pl.when(b < nb)
    def _():
        wrows = win_ref.shape[0]
        winb_ref[pl.ds(pl.multiple_of(b * wrows, wrows), wrows), :] = win_ref[...].astype(BF16)
        wcqb_ref[...] = wcq_ref[...].astype(BF16)
        wcob_ref[...] = wco_ref[...].astype(BF16)

        @pl.when(b < first_pool_block)
        def _():
            woutb_ref[...] = wout_ref[...].astype(BF16)

        @pl.when(b >= first_pool_block)
        def _():
            gi = b - first_pool_block
            a_hi, a_lo = _split2(pmix_ref[gi] * ps_ref[gi])
            w_hi, w_lo = _split2(wout_ref[...])
            woutb_ref[...] = (_dot(a_hi, w_hi) + _dot(a_hi, w_lo) + _dot(a_lo, w_hi)).astype(BF16)

        h = _rms(mem_ref[0], g_ref[...]).astype(BF16)
        kv = _dot(h, wb_ref[...])
        k = kv[:, :D_MODEL]
        v = kv[:, D_MODEL:]
        kt_ref[0] = k.T.astype(BF16)
        vb_ref[0] = v.astype(BF16)
        for val, out_ref in ((k, k_ref), (v, v_ref)):
            for r in range(2 * X_HEADS):
                half, head = divmod(r, X_HEADS)
                c0 = head * X_HEAD_DIM + half * 128
                rows_ref[r * SPLIT_PITCH:r * SPLIT_PITCH + MEM_LEN, :] = val[:, c0:c0 + 128]
            for m in range(MEM_LEN):
                out_ref[0, m] = rows_ref[pl.ds(m, 2 * X_HEADS, stride=SPLIT_PITCH), :]

    @pl.when(b == nb)
    def _():
        _sample_gates(xs_ref[:, 0, :], gmix_ref, winb_ref, lb_ref, gates_ref, xs2d_ref)


def _mem_kv_sample_gates(mem, g_mem, w_kv, w_in, w_out, w_cq, w_co, pool_mix, pool_scale,
                         x_sample, g_mix, hgrn_lb):
    nb = mem.shape[0]
    ns = x_sample.shape[0]
    full = lambda shape: pl.BlockSpec(shape, lambda b: (0,) * len(shape))
    blk = lambda b: jnp.minimum(b, nb - 1)
    split_spec = pl.BlockSpec((1, MEM_LEN, 2 * X_HEADS, 128), lambda b: (blk(b), 0, 0, 0))
    wrows = D_MODEL // nb
    assert wrows == POOL_GROUP
    win_spec = pl.BlockSpec((wrows, IN_PROJ), lambda b: (blk(b), 0))
    wout_spec = pl.BlockSpec((wrows, D_MODEL), lambda b: (blk(b), 0))
    return pl.pallas_call(
        _memkv_kernel,
        grid=(nb + 1,),
        in_specs=[pl.BlockSpec((1, MEM_LEN, D_MODEL), lambda b: (blk(b), 0, 0)),
                  full((1, D_MODEL)), full((D_MODEL, 2 * D_MODEL)), win_spec, wout_spec,
                  wout_spec, wout_spec,
                  full((4, POOL_GROUP, POOL_GROUP)), full((4, 1, POOL_GROUP)),
                  full((ns, 1, D_MODEL)), full((1, D_MODEL)), full((2, HGRN_WIDTH))],
        out_specs=[split_spec, split_spec,
                   pl.BlockSpec((1, D_MODEL, MEM_LEN), lambda b: (blk(b), 0, 0)),
                   pl.BlockSpec((1, MEM_LEN, D_MODEL), lambda b: (blk(b), 0, 0)),
                   full((D_MODEL, IN_PROJ)), wout_spec, wout_spec, wout_spec,
                   full((ns, 6 * 512)), full((ns, D_MODEL))],
        out_shape=[jax.ShapeDtypeStruct((nb, MEM_LEN, 2 * X_HEADS, 128), F32),
                   jax.ShapeDtypeStruct((nb, MEM_LEN, 2 * X_HEADS, 128), F32),
                   jax.ShapeDtypeStruct((nb, D_MODEL, MEM_LEN), BF16),
                   jax.ShapeDtypeStruct((nb, MEM_LEN, D_MODEL), BF16),
                   jax.ShapeDtypeStruct((D_MODEL, IN_PROJ), BF16),
                   jax.ShapeDtypeStruct((D_MODEL, D_MODEL), BF16),
                   jax.ShapeDtypeStruct((D_MODEL, D_MODEL), BF16),
                   jax.ShapeDtypeStruct((D_MODEL, D_MODEL), BF16),
                   jax.ShapeDtypeStruct((ns, 6 * 512), F32),
                   jax.ShapeDtypeStruct((ns, D_MODEL), F32)],
        scratch_shapes=[pltpu.VMEM((D_MODEL, 2 * D_MODEL), BF16),
                        pltpu.VMEM((2 * X_HEADS * SPLIT_PITCH, 128), F32)],
        compiler_params=pltpu.CompilerParams(
            dimension_semantics=("arbitrary",), vmem_limit_bytes=VMEM_LIMIT_BYTES),
        name="mem_kv_sample_gates",
    )(mem, g_mem, w_kv, w_in, w_out, w_cq, w_co, pool_mix, pool_scale.reshape(4, 1, POOL_GROUP),
      x_sample, g_mix, hgrn_lb)


def _mix_block(n, x, proj_ref, out_ref, fillers, lb_ref, hn_ref, wout_ref,
               st_ref, ext_ref, qq_ref, kk_ref, b_ref, mrg_ref):
    T = MIX_ROWS
    fillers = list(fillers)
    schedule = list(MIX_FILLER_SCHEDULE)
    assert len(schedule) == 5 + T // CHUNK and sum(schedule) == len(fillers)

    def fill():
        for _ in range(schedule.pop(0)):
            fillers.pop(0)()

    l = n % MIX_BLOCKS_PER_SEQ
    first = l == 0
    for h in range(HEADS):
        st_ref[h] = jnp.where(first, 0.0, st_ref[h])
    ext_ref[0:16, :] = jnp.where(first, 0.0, ext_ref[0:16, :])

    fill()
    lb = _lower_bound(lb_ref)
    qq, fgate, kk = _gates(proj_ref[:, 0:512], proj_ref[:, 512:1024], lb)
    qq_ref[...] = qq
    kk_ref[...] = kk
    fill()

    r = lax.broadcasted_iota(jnp.int32, (TRI_ROWS, TRI_ROWS), 0)
    c = lax.broadcasted_iota(jnp.int32, (TRI_ROWS, TRI_ROWS), 1)
    tri = jnp.where((c <= r) & (c >= (r & -CHUNK)), 1.0, 0.0).astype(BF16)
    logf = jnp.log2(fgate)
    for blk in range(T // TRI_ROWS):
        rows = slice(blk * TRI_ROWS, (blk + 1) * TRI_ROWS)
        hi, mid, lo = _split3(logf[rows])
        b_ref[rows, :] = _dot(tri, hi) + _dot(tri, mid) + _dot(tri, lo)
    fill()

    cr = lax.broadcasted_iota(jnp.int32, (CHUNK, CHUNK), 0)
    cc = lax.broadcasted_iota(jnp.int32, (CHUNK, CHUNK), 1)
    causal = cc <= cr
    mid_row = (CHUNK - 1) // 2

    for ci in range(T // CHUNK):
        r0 = ci * CHUNK
        rows = slice(r0, r0 + CHUNK)
        scores, inters, vals = [], [], []
        for h in range(HEADS):
            cols = slice(h * DK, (h + 1) * DK)
            b = b_ref[rows, cols]
            m = b_ref[r0 + mid_row:r0 + mid_row + 1, cols]
            b_end = b_ref[r0 + CHUNK - 1:r0 + CHUNK, cols]
            e1 = jnp.exp2(b - m)
            e2 = jnp.exp2(m - b)
            q1 = qq_ref[rows, cols] * e1
            q0 = q1 * jnp.exp2(m)
            ks = kk_ref[rows, cols] * e2
            k2 = ks * jnp.exp2(b_end - m)
            v = proj_ref[rows, 1024 + h * DK:1024 + (h + 1) * DK].astype(BF16)
            st = st_ref[h]
            scores.append(_dot_nt(q1.astype(BF16), ks.astype(BF16)))
            inters.append(_dot_nt(q0.astype(BF16), st.astype(BF16)))
            st_ref[h] = st * jnp.exp2(b_end) + _dot_tn(v, k2.astype(BF16))
            vals.append(v)
        fill()
        for h in range(HEADS):
            cols = slice(h * DK, (h + 1) * DK)
            a = jnp.where(causal, scores[h], 0.0)
            o = inters[h] + _dot(a.astype(BF16), vals[h])
            o = o * lax.rsqrt(jnp.mean(o * o, axis=-1, keepdims=True) + EPS)
            o = o * hn_ref[:, cols]
            g = proj_ref[rows, 1536 + h * DK:1536 + (h + 1) * DK]
            mrg_ref[rows, cols] = (o * _sigmoid(g)).astype(BF16)

    ext_ref[16:16 + T, :] = proj_ref[:, 2048:2560]
    pos = l * T + lax.broadcasted_iota(jnp.int32, (16, POOL_GROUP), 0)
    for gi, w in enumerate(POOL_WINDOWS):
        if gi % 2 == 0:
            fill()
        cols = slice(gi * POOL_GROUP, (gi + 1) * POOL_GROUP)
        s = ext_ref[:, cols]
        u = s[16:]
        shift = 1
        while shift < w:
            s = s + pltpu.roll(s, shift, axis=0)
            shift *= 2
        acc = s[16:]
        cnt = jnp.minimum(pos + 1, w).astype(F32)
        pooled = jnp.concatenate([acc[0:16] / cnt, acc[16:] * (1.0 / w)], axis=0) - u
        mrg_ref[:, HGRN_WIDTH + gi * POOL_GROUP:HGRN_WIDTH + (gi + 1) * POOL_GROUP] = pooled.astype(BF16)

    ext_ref[0:16, :] = ext_ref[T:T + 16, :]
    out_ref[...] = x + _dot(mrg_ref[...], wout_ref[...])


def _mixer_kernel(xa_ref, xn_ref, g_ref, win_ref, lb_ref, hn_ref, wout_ref,
                  w1f_ref,
                  x1_ref, hst_ref, pst_ref, w1b_ref,
                  p0_ref, p1_ref, hb_ref, st_ref, ext_ref, qq_ref, kk_ref, b_ref, mrg_ref):
    T = MIX_ROWS
    g = pl.program_id(0)
    w1b_ref[...] = w1f_ref[...].astype(BF16)
    rest = (lb_ref, hn_ref, wout_ref, st_ref, ext_ref, qq_ref, kk_ref, b_ref, mrg_ref)

    def in_proj_pieces(x_ref, rows, p_ref):
        def prep():
            hb_ref[...] = _rms(x_ref[rows, :], g_ref[...]).astype(BF16)

        def piece(k):
            cols = slice(k * IN_PROJ_PIECE, (k + 1) * IN_PROJ_PIECE)
            p_ref[:, cols] = _dot(hb_ref[...], win_ref[:, cols])

        return [prep] + [functools.partial(piece, k) for k in range(IN_PROJ // IN_PROJ_PIECE)]

    @pl.when(g == 0)
    def _():
        st_ref[...] = jnp.zeros_like(st_ref)
        ext_ref[0:16, :] = jnp.zeros((16, POOL_WIDTH), F32)
        for f in in_proj_pieces(xa_ref, slice(0, T), p0_ref):
            f()

    _mix_block(2 * g, xa_ref[0:T, :], p0_ref, x1_ref.at[0:T, :],
               in_proj_pieces(xa_ref, slice(T, 2 * T), p1_ref), *rest)
    _mix_block(2 * g + 1, xa_ref[T:2 * T, :], p1_ref, x1_ref.at[T:2 * T, :],
               in_proj_pieces(xn_ref, slice(0, T), p0_ref), *rest)

    steps_per_seq = MIX_BLOCKS_PER_SEQ // 2

    @pl.when(g % steps_per_seq == steps_per_seq - 1)
    def _():
        for h in range(HEADS):
            hst_ref[0, h] = st_ref[h].T

    for seq in range(pst_ref.shape[1]):
        @pl.when(g == seq * steps_per_seq + steps_per_seq - 1)
        def _():
            pst_ref[:, seq, :] = ext_ref[T + 1:T + 16, :]


def _mixer_prompt(x, g_mix, w_in, hgrn_lb, hgrn_norm, w_out, w_ffn_in):
    nb, L, _ = x.shape
    T = MIX_ROWS
    assert L // T == MIX_BLOCKS_PER_SEQ and MIX_BLOCKS_PER_SEQ % 2 == 0
    n_blocks = nb * MIX_BLOCKS_PER_SEQ
    steps = n_blocks // 2
    steps_per_seq = MIX_BLOCKS_PER_SEQ // 2
    x2d = x.reshape(nb * L, D_MODEL)
    full = lambda shape: pl.BlockSpec(shape, lambda g: (0,) * len(shape))
    rows = lambda w: pl.BlockSpec((w.shape[0] // steps, w.shape[1]), lambda g: (g, 0))
    side = (w_ffn_in,)
    x1, hst, pst, w1b = pl.pallas_call(
        _mixer_kernel,
        grid=(steps,),
        in_specs=[pl.BlockSpec((2 * T, D_MODEL), lambda g: (g, 0)),
                  pl.BlockSpec((T, D_MODEL), lambda g: (jnp.minimum(2 * g + 2, n_blocks - 1), 0)),
                  full((1, D_MODEL)), full((D_MODEL, IN_PROJ)), full((2, HGRN_WIDTH)),
                  full((1, HGRN_WIDTH)), full((D_MODEL, D_MODEL))] + [rows(w) for w in side],
        out_specs=[pl.BlockSpec((2 * T, D_MODEL), lambda g: (g, 0)),
                   pl.BlockSpec((1, HEADS, DK, DK), lambda g: (g // steps_per_seq, 0, 0, 0)),
                   full((POOL_STATE, nb, POOL_WIDTH))]
                  + [rows(w) for w in side],
        out_shape=[jax.ShapeDtypeStruct((nb * L, D_MODEL), F32),
                   jax.ShapeDtypeStruct((nb, HEADS, DK, DK), F32),
                   jax.ShapeDtypeStruct((POOL_STATE, nb, POOL_WIDTH), F32)]
                  + [jax.ShapeDtypeStruct(w.shape, BF16) for w in side],
        scratch_shapes=[pltpu.VMEM((T, IN_PROJ), F32),
                        pltpu.VMEM((T, IN_PROJ), F32),
                        pltpu.VMEM((T, D_MODEL), BF16),
                        pltpu.VMEM((HEADS, DK, DK), F32),
                        pltpu.VMEM((16 + T, POOL_WIDTH), F32),
                        pltpu.VMEM((T, HGRN_WIDTH), F32),
                        pltpu.VMEM((T, HGRN_WIDTH), F32),
                        pltpu.VMEM((T, HGRN_WIDTH), F32),
                        pltpu.VMEM((T, D_MODEL), BF16)],
        compiler_params=pltpu.CompilerParams(
            dimension_semantics=("arbitrary",), vmem_limit_bytes=VMEM_LIMIT_BYTES),
        name="mixer_prompt",
    )(x2d, x2d, g_mix, w_in, hgrn_lb, hgrn_norm, w_out, *side)
    return x1.reshape(nb, L, D_MODEL), hst, pst, w1b


def _state_step(base, gates_ref, s_ref, so_ref, o_ref):
    G = STATE_TOKENS
    pad = jnp.zeros((DK - G, DK), F32)
    for h in range(HEADS):
        cols = slice(h * DK, (h + 1) * DK)
        qt = jnp.concatenate([gates_ref[:, h * DK:(h + 1) * DK], pad], axis=0).T
        ft = jnp.concatenate([gates_ref[:, 512 + h * DK:512 + (h + 1) * DK], pad], axis=0).T
        kt = jnp.concatenate([gates_ref[:, 1024 + h * DK:1024 + (h + 1) * DK], pad], axis=0).T
        readouts = []
        for j in range(G):
            v = gates_ref[j:j + 1, 1536 + h * DK:1536 + (h + 1) * DK]
            s_new = ft[:, j:j + 1] * s_ref[j, h] + kt[:, j:j + 1] * v
            so_ref[j, h] = s_new
            readouts.append(jnp.sum(qt[:, j:j + 1] * s_new, axis=0, keepdims=True))
        o_ref[pl.ds(pl.multiple_of(base, G), G), cols] = jnp.concatenate(readouts, axis=0)


def _sample_mix(x_ref, gates_ref, o_ref, past_ref, hn_ref, wout_ref, gc_ref, wq_ref,
                x1_ref, pool_ref, q_ref, mrg_ref):
    n = x_ref.shape[0]
    for h in range(HEADS):
        cols = slice(h * DK, (h + 1) * DK)
        o = o_ref[:, cols]
        o = o * lax.rsqrt(jnp.mean(o * o, axis=-1, keepdims=True) + EPS) * hn_ref[:, cols]
        mrg_ref[0:n, cols] = (o * gates_ref[:, 2048 + h * DK:2048 + (h + 1) * DK]).astype(BF16)
    for gi, w in enumerate(POOL_WINDOWS):
        cols = slice(gi * POOL_GROUP, (gi + 1) * POOL_GROUP)
        u = gates_ref[:, 2560 + gi * POOL_GROUP:2560 + (gi + 1) * POOL_GROUP]
        acc = u
        for j in range(1, w):
            acc = acc + past_ref[POOL_STATE - j, :, cols]
        pooled = acc / float(w) - u
        mrg_ref[0:n, HGRN_WIDTH + gi * POOL_GROUP:HGRN_WIDTH + (gi + 1) * POOL_GROUP] = pooled.astype(BF16)
    pool_ref[0:POOL_STATE - 1] = past_ref[1:POOL_STATE]
    pool_ref[POOL_STATE - 1] = gates_ref[:, 2560:3072]
    x1 = x_ref[...] + _dot(mrg_ref[0:n, :], wout_ref[...])
    x1_ref[...] = x1
    hb = _rms(x1, gc_ref[...]).astype(BF16)
    q = _dot(hb, wq_ref[...]) * QUERY_SCALE
    for r in range(2 * X_HEADS):
        half, head = divmod(r, X_HEADS)
        c0 = head * X_HEAD_DIM + half * 128
        q_ref[:, r, :] = q[:, c0:c0 + 128]


def _attn_state_kernel(x_ref, g_ref, wq_ref, kt_ref, vb_ref, wo_ref, gates_ref, s_ref, w2f_ref,
                       xs_ref, gall_ref, past_ref, hn_ref, wout_ref,
                       o_ref, so_ref, w2b_ref, x1s_ref, pool_ref, qs_ref,
                       att_ref, oo_ref):
    i = pl.program_id(0)
    last = pl.num_programs(0) - 1

    @pl.when(i < last)
    def _():
        w2b_ref[...] = w2f_ref[...].astype(BF16)
        _attention_block(x_ref, g_ref, kt_ref, vb_ref, wq_ref, wo_ref, o_ref, att_ref)
        _state_step(i * STATE_TOKENS, gates_ref, s_ref, so_ref, oo_ref)

    @pl.when(i == last)
    def _():
        _sample_mix(xs_ref, gall_ref, oo_ref, past_ref, hn_ref, wout_ref, g_ref, wq_ref,
                    x1s_ref, pool_ref, qs_ref, att_ref)


def _attention_block(x_ref, g_ref, kt_ref, vb_ref, wq_ref, wo_ref, o_ref, att_ref):
    H = ATTN_ROWS // 2
    halves = (slice(0, H), slice(H, 2 * H))

    def query(rows):
        hb = _rms(x_ref[0, rows, :], g_ref[...]).astype(BF16)
        return (_dot(hb, wq_ref[...]) * QUERY_SCALE).astype(BF16)

    def scores(q):
        return [_dot(q[:, h * X_HEAD_DIM:(h + 1) * X_HEAD_DIM],
                     kt_ref[0, h * X_HEAD_DIM:(h + 1) * X_HEAD_DIM, :]) for h in range(X_HEADS)]

    def values(rows, ss):
        for h, s in enumerate(ss):
            cols = slice(h * X_HEAD_DIM, (h + 1) * X_HEAD_DIM)
            e = jnp.exp2(s - jnp.max(s, axis=-1, keepdims=True))
            den = jnp.sum(e, axis=-1, keepdims=True)
            o = _dot(e.astype(BF16), vb_ref[0, :, cols]) / den
            att_ref[rows, cols] = o.astype(BF16)

    def project(rows):
        o_ref[0, rows, :] = x_ref[0, rows, :] + _dot(att_ref[rows, :], wo_ref[...])

    s0 = scores(query(halves[0]))
    q1 = query(halves[1])
    values(halves[0], s0)
    s1 = scores(q1)
    project(halves[0])
    values(halves[1], s1)
    project(halves[1])


def _attn_prompt_state_sample(x, g_cross, w_cq, kt, vb, w_co, gates, state, w_ffn_out,
                              x_sample, past, hgrn_norm, w_out):
    nb, L, _ = x.shape
    T = ATTN_ROWS
    G = STATE_TOKENS
    steps_per_seq = L // T
    steps = nb * steps_per_seq
    ns = gates.shape[0]
    assert steps == ns // G
    full = lambda shape: pl.BlockSpec(shape, lambda i: (0,) * len(shape))
    blk = lambda i: jnp.minimum(i, steps - 1)
    st_spec = pl.BlockSpec((G, HEADS, DK, DK), lambda i: (blk(i), 0, 0, 0))
    x_spec = pl.BlockSpec((1, T, D_MODEL),
                          lambda i: (blk(i) // steps_per_seq, blk(i) % steps_per_seq, 0))
    kv_map = lambda i: (blk(i) // steps_per_seq, 0, 0)
    w2_spec = pl.BlockSpec((D_FF // steps, D_MODEL), lambda i: (blk(i), 0))
    return pl.pallas_call(
        _attn_state_kernel,
        grid=(steps + 1,),
        in_specs=[x_spec, full((1, D_MODEL)), full((D_MODEL, D_MODEL)),
                  pl.BlockSpec((1, D_MODEL, MEM_LEN), kv_map),
                  pl.BlockSpec((1, MEM_LEN, D_MODEL), kv_map),
                  full((D_MODEL, D_MODEL)),
                  pl.BlockSpec((G, 4 * 512), lambda i: (blk(i), 0)), st_spec, w2_spec,
                  full((ns, D_MODEL)), full((ns, 6 * 512)), full((POOL_STATE, ns, POOL_WIDTH)),
                  full((1, HGRN_WIDTH)), full((D_MODEL, D_MODEL))],
        out_specs=[x_spec, st_spec, w2_spec,
                   full((ns, D_MODEL)), full((POOL_STATE, ns, POOL_WIDTH)),
                   full((ns, 2 * X_HEADS, 128))],
        out_shape=[jax.ShapeDtypeStruct((nb, L, D_MODEL), F32),
                   jax.ShapeDtypeStruct((ns, HEADS, DK, DK), F32),
                   jax.ShapeDtypeStruct((D_FF, D_MODEL), BF16),
                   jax.ShapeDtypeStruct((ns, D_MODEL), F32),
                   jax.ShapeDtypeStruct((POOL_STATE, ns, POOL_WIDTH), F32),
                   jax.ShapeDtypeStruct((ns, 2 * X_HEADS, 128), F32)],
        scratch_shapes=[pltpu.VMEM((T, D_MODEL), BF16), pltpu.VMEM((ns, HGRN_WIDTH), F32)],
        compiler_params=pltpu.CompilerParams(
            dimension_semantics=("arbitrary",), vmem_limit_bytes=VMEM_LIMIT_BYTES),
        name="attn_prompt_state_sample",
    )(x, g_cross, w_cq, kt, vb, w_co, gates, state, w_ffn_out,
      x_sample, gates, past, hgrn_norm, w_out)


def _zero_after(x):
    u = lax.bitcast_convert_type(x, jnp.uint32)
    z = lax.shift_right_logical(lax.shift_right_logical(u, jnp.uint32(16)), jnp.uint32(16))
    return lax.bitcast_convert_type(z, F32)


def _ffn_body(x, g_ref, w1_ref, w2_ref, gf_ref, fillers=()):
    hb = _rms(x, g_ref[...]).astype(BF16)

    def up(c0, c1):
        return _dot(hb, w1_ref[:, c0:c1]), _dot(hb, w1_ref[:, D_FF + c0:D_FF + c1])

    def add_to_first_tile(m, z):
        top = jnp.concatenate([m[0:8, 0:128] + z, m[0:8, 128:]], axis=1)
        return jnp.concatenate([top, m[8:, :]], axis=0)

    y = x
    fillers = list(fillers)
    nxt = up(*FFN_COL_CHUNKS[0])
    for i, (c0, c1) in enumerate(FFN_COL_CHUNKS):
        a, bg = nxt
        if i + 1 < len(FFN_COL_CHUNKS):
            nxt = up(*FFN_COL_CHUNKS[i + 1])
        for _ in range(-(-len(fillers) // (len(FFN_COL_CHUNKS) - i))):
            bg = add_to_first_tile(bg, _zero_after(fillers.pop(0)()))
        act = (a * _sigmoid(a) * bg).astype(BF16)
        y = y + _dot(act, w2_ref[c0:c1, :])
    return _rms(y, gf_ref[...])


def _split_heads(x):
    lead = x.shape[:-2]
    x = x.reshape(lead + (X_HEADS, 2, 128))
    x = jnp.swapaxes(x, -3, -2)
    return x.reshape(lead + (2 * X_HEADS, 128))


def _merge_heads(x):
    lead = x.shape[:-2]
    x = x.reshape(lead + (2, X_HEADS, 128))
    x = jnp.swapaxes(x, -3, -2)
    return x.reshape(lead + (X_HEADS, X_HEAD_DIM))


def _memory_attention(j, base, q_ref, k_ref, v_ref, o_ref):
    prod = k_ref[j] * q_ref[base + j][None]
    s = jnp.sum(prod + pltpu.roll(prod, X_HEADS, axis=1), axis=-1, keepdims=True)
    e = jnp.exp2(s - jnp.max(s, axis=0, keepdims=True))
    den = jnp.sum(e, axis=0)
    o = jnp.sum(e * v_ref[j], axis=0) / den
    o_ref[base + j] = o
    return o


def _ffn_attn_kernel(x_ref, g_ref, w1_ref, w2_ref, gf_ref, q_ref, k_ref, v_ref, xs_ref, wo_ref,
                     o_ref, ys_ref, att_ref):
    i = pl.program_id(0)
    last = pl.num_programs(0) - 1

    @pl.when(i < last)
    def _():
        base = i * ATTN_TOKENS
        fillers = [functools.partial(_memory_attention, j, base, q_ref, k_ref, v_ref, att_ref)
                   for j in range(ATTN_TOKENS)]
        o_ref[...] = _ffn_body(x_ref[...], g_ref, w1_ref, w2_ref, gf_ref, fillers)

    @pl.when(i == last)
    def _():
        att = jnp.concatenate([att_ref[:, half * X_HEADS + head, :]
                               for head in range(X_HEADS) for half in range(2)], axis=1)
        x2 = xs_ref[...] + _dot(att.astype(BF16), wo_ref[...])
        ys_ref[:, 0, :] = _ffn_body(x2, g_ref, w1_ref, w2_ref, gf_ref)


def _ffn_prompt_attn_sample(x, g_ffn, w1, w2, g_final, q, cache_k, cache_v, x_sample, w_co):
    n = x.shape[0]
    ns = q.shape[0]
    T = FFN_ROWS
    G = ATTN_TOKENS
    steps = n // T
    assert steps == ns // G
    full = lambda shape: pl.BlockSpec(shape, lambda i: (0,) * len(shape))
    blk = lambda i: jnp.minimum(i, steps - 1)
    kv_spec = pl.BlockSpec((G, MEM_LEN, 2 * X_HEADS, 128), lambda i: (blk(i), 0, 0, 0))
    q_spec = full((ns, 2 * X_HEADS, 128))
    y, ys = pl.pallas_call(
        _ffn_attn_kernel,
        grid=(steps + 1,),
        in_specs=[pl.BlockSpec((T, D_MODEL), lambda i: (blk(i), 0)),
                  full((1, D_MODEL)), full((D_MODEL, 2 * D_FF)), full((D_FF, D_MODEL)),
                  full((1, D_MODEL)), q_spec, kv_spec, kv_spec,
                  full((ns, D_MODEL)), full((D_MODEL, D_MODEL))],
        out_specs=[pl.BlockSpec((T, D_MODEL), lambda i: (blk(i), 0)),
                   full((ns, 1, D_MODEL))],
        out_shape=[jax.ShapeDtypeStruct((n, D_MODEL), F32),
                   jax.ShapeDtypeStruct((ns, 1, D_MODEL), F32)],
        scratch_shapes=[pltpu.VMEM((ns, 2 * X_HEADS, 128), F32)],
        compiler_params=pltpu.CompilerParams(
            dimension_semantics=("arbitrary",), vmem_limit_bytes=VMEM_LIMIT_BYTES),
        name="ffn_prompt_attn_sample",
    )(x, g_ffn, w1, w2, g_final, q, cache_k, cache_v, x_sample, w_co)
    return y, ys


def kernel(x_prompt, x_sample, mem_prompt, state_hgrn, state_pool, cache_mem_k, cache_mem_v,
           g_mix, w_in, hgrn_lb, hgrn_norm, pool_mix, pool_scale, w_out, g_mem, w_mem_kv,
           g_cross, w_cq, w_co, g_ffn, w_ffn_in, w_ffn_out, g_final):
    nb, L, _ = x_prompt.shape
    ns = x_sample.shape[0]

    g_final2 = g_final.reshape(1, D_MODEL)

    mem_k, mem_v, kt, vb, w_in_b, w_out_b, w_cq_b, w_co_b, gates, xs = _mem_kv_sample_gates(
        mem_prompt, g_mem, w_mem_kv[0], w_in[0], w_out[0], w_cq[0], w_co[0], pool_mix[0], pool_scale,
        x_sample, g_mix, hgrn_lb)
    x1, hgrn_p, pool_p, w1_b = _mixer_prompt(x_prompt, g_mix, w_in_b, hgrn_lb, hgrn_norm, w_out_b,
                                             w_ffn_in[0])
    x2, hgrn_s, w2_b, x1s, pool_s, qs = _attn_prompt_state_sample(
        x1, g_cross, w_cq_b, kt, vb, w_co_b, gates, state_hgrn[0], w_ffn_out[0],
        xs, jnp.swapaxes(state_pool[0], 0, 1), hgrn_norm, w_out_b)
    y_prompt, y_sample = _ffn_prompt_attn_sample(
        x2.reshape(nb * L, D_MODEL), g_ffn, w1_b, w2_b, g_final2, qs,
        _split_heads(cache_mem_k[0]), _split_heads(cache_mem_v[0]), x1s, w_co_b)

    return (y_prompt.reshape(nb, L, D_MODEL),
            y_sample,
            hgrn_p[None],
            jnp.swapaxes(pool_p, 0, 1)[None],
            _merge_heads(mem_k)[None],
            _merge_heads(mem_v)[None],
            hgrn_s[None],
            jnp.swapaxes(pool_s, 0, 1)[None])
```

```python
import functools

import jax
import jax.numpy as jnp
from jax import lax
from jax.experimental import pallas as pl
from jax.experimental.pallas import tpu as pltpu

F32 = jnp.float32
BF16 = jnp.bfloat16

D_MODEL = 1024
HGRN_WIDTH = 512
HEADS = 4
DK = 128
CHUNK = 64
POOL_WIDTH = 512
POOL_WINDOWS = (2, 4, 8, 16)
POOL_GROUP = 128
POOL_STATE = 15
IN_PROJ = 4 * HGRN_WIDTH + POOL_WIDTH
MEM_LEN = 256
X_HEADS = 4
X_HEAD_DIM = 256
D_FF = 2816
EPS = 1e-6
ATTN_SCALE = X_HEAD_DIM ** -0.5
LOG2_E = 1.4426950408889634
QUERY_SCALE = ATTN_SCALE * LOG2_E

VMEM_LIMIT_BYTES = 56 * 1024 * 1024

MIX_ROWS = 512
MIX_BLOCKS_PER_SEQ = 4
IN_PROJ_PIECE = 256
MIX_FILLER_SCHEDULE = (3, 0, 0) + (1, 0, 1, 0, 1, 0, 1, 0) + (2, 2)
TRI_ROWS = 256
ATTN_ROWS = 1024
SAMPLE_FFN_COLS = 1408
SPLIT_PITCH = MEM_LEN + 4
FFN_ROWS = 512
STATE_TOKENS = 8
ATTN_TOKENS = 4
FFN_PASS_WIDTHS = (512, 768, 768, 768)
FFN_COL_CHUNKS = tuple((sum(FFN_PASS_WIDTHS[:i]), sum(FFN_PASS_WIDTHS[:i + 1]))
                       for i in range(len(FFN_PASS_WIDTHS)))
assert FFN_COL_CHUNKS[-1][1] == D_FF


def _dot(a, b):
    return jnp.dot(a, b, preferred_element_type=F32)


def _dot_nt(a, b):
    return lax.dot_general(a, b, (((1,), (1,)), ((), ())), preferred_element_type=F32)


def _dot_tn(a, b):
    return lax.dot_general(a, b, (((0,), (0,)), ((), ())), preferred_element_type=F32)


def _rms(x, g):
    ms = jnp.mean(x * x, axis=-1, keepdims=True)
    return x * lax.rsqrt(ms + EPS) * g


def _sigmoid(x):
    return 1.0 / (1.0 + jnp.exp(-x))


def _lower_bound(lb_ref):
    t = lb_ref[...]
    m = jnp.max(t, axis=0, keepdims=True)
    e = jnp.exp(t - m)
    return e[0:1, :] / jnp.sum(e, axis=0, keepdims=True)


def _gates(proj_q, proj_f, lb):
    qq = proj_q * _sigmoid(proj_q)
    sig = _sigmoid(proj_f)
    fgate = lb + (1.0 - lb) * sig
    kk = (1.0 - lb) * (1.0 - sig)
    return qq, fgate, kk


def _split2(x):
    hi = x.astype(BF16)
    return hi, (x - hi.astype(F32)).astype(BF16)


def _sample_gates(x, g_ref, winb_ref, lb_ref, o_ref, x2d_ref):
    x2d_ref[...] = x
    hb = _rms(x, g_ref[...]).astype(BF16)
    proj = _dot(hb, winb_ref[...])
    lb = _lower_bound(lb_ref)
    qq, fgate, kk = _gates(proj[:, 0:512], proj[:, 512:1024], lb)
    o_ref[:, 0:512] = qq
    o_ref[:, 512:1024] = fgate
    o_ref[:, 1024:1536] = kk
    o_ref[:, 1536:2048] = proj[:, 1024:1536]
    o_ref[:, 2048:2560] = _sigmoid(proj[:, 1536:2048])
    o_ref[:, 2560:3072] = proj[:, 2048:2560]


def _memkv_kernel(mem_ref, g_ref, w_ref, win_ref, wout_ref, wcq_ref, wco_ref, pmix_ref, ps_ref,
                  xs_ref, gmix_ref, lb_ref,
                  k_ref, v_ref, kt_ref, vb_ref, winb_ref, woutb_ref, wcqb_ref, wcob_ref,
                  gates_ref, xs2d_ref,
                  wb_ref, rows_ref):
    b = pl.program_id(0)
    nb = pl.num_programs(0) - 1
    first_pool_block = HGRN_WIDTH // POOL_GROUP

    @pl.when(b == 0)
    def _():
        wb_ref[...] = w_ref[...].astype(BF16)

    @pl.when(b < nb)
    def _():
        wrows = win_ref.shape[0]
        winb_ref[pl.ds(pl.multiple_of(b * wrows, wrows), wrows), :] = win_ref[...].astype(BF16)
        wcqb_ref[...] = wcq_ref[...].astype(BF16)
        wcob_ref[...] = wco_ref[...].astype(BF16)

        @pl.when(b < first_pool_block)
        def _():
            woutb_ref[...] = wout_ref[...].astype(BF16)

        @pl.when(b >= first_pool_block)
        def _():
            gi = b - first_pool_block
            a_hi, a_lo = _split2(pmix_ref[gi] * ps_ref[gi])
            w_hi, w_lo = _split2(wout_ref[...])
            woutb_ref[...] = (_dot(a_hi, w_hi) + _dot(a_hi, w_lo) + _dot(a_lo, w_hi)).astype(BF16)

        h = _rms(mem_ref[0], g_ref[...]).astype(BF16)
        kv = _dot(h, wb_ref[...])
        k = kv[:, :D_MODEL]
        v = kv[:, D_MODEL:]
        kt_ref[0] = k.T.astype(BF16)
        vb_ref[0] = v.astype(BF16)
        for val, out_ref in ((k, k_ref), (v, v_ref)):
            for r in range(2 * X_HEADS):
                half, head = divmod(r, X_HEADS)
                c0 = head * X_HEAD_DIM + half * 128
                rows_ref[r * SPLIT_PITCH:r * SPLIT_PITCH + MEM_LEN, :] = val[:, c0:c0 + 128]
            for m in range(MEM_LEN):
                out_ref[0, m] = rows_ref[pl.ds(m, 2 * X_HEADS, stride=SPLIT_PITCH), :]

    @pl.when(b == nb)
    def _():
        _sample_gates(xs_ref[:, 0, :], gmix_ref, winb_ref, lb_ref, gates_ref, xs2d_ref)


def _mem_kv_sample_gates(mem, g_mem, w_kv, w_in, w_out, w_cq, w_co, pool_mix, pool_scale,
                         x_sample, g_mix, hgrn_lb):
    nb = mem.shape[0]
    ns = x_sample.shape[0]
    full = lambda shape: pl.BlockSpec(shape, lambda b: (0,) * len(shape))
    blk = lambda b: jnp.minimum(b, nb - 1)
    split_spec = pl.BlockSpec((1, MEM_LEN, 2 * X_HEADS, 128), lambda b: (blk(b), 0, 0, 0))
    wrows = D_MODEL // nb
    assert wrows == POOL_GROUP
    win_spec = pl.BlockSpec((wrows, IN_PROJ), lambda b: (blk(b), 0))
    wout_spec = pl.BlockSpec((wrows, D_MODEL), lambda b: (blk(b), 0))
    return pl.pallas_call(
        _memkv_kernel,
        grid=(nb + 1,),
        in_specs=[pl.BlockSpec((1, MEM_LEN, D_MODEL), lambda b: (blk(b), 0, 0)),
                  full((1, D_MODEL)), full((D_MODEL, 2 * D_MODEL)), win_spec, wout_spec,
                  wout_spec, wout_spec,
                  full((4, POOL_GROUP, POOL_GROUP)), full((4, 1, POOL_GROUP)),
                  full((ns, 1, D_MODEL)), full((1, D_MODEL)), full((2, HGRN_WIDTH))],
        out_specs=[split_spec, split_spec,
                   pl.BlockSpec((1, D_MODEL, MEM_LEN), lambda b: (blk(b), 0, 0)),
                   pl.BlockSpec((1, MEM_LEN, D_MODEL), lambda b: (blk(b), 0, 0)),
                   full((D_MODEL, IN_PROJ)), wout_spec, wout_spec, wout_spec,
                   full((ns, 6 * 512)), full((ns, D_MODEL))],
        out_shape=[jax.ShapeDtypeStruct((nb, MEM_LEN, 2 * X_HEADS, 128), F32),
                   jax.ShapeDtypeStruct((nb, MEM_LEN, 2 * X_HEADS, 128), F32),
                   jax.ShapeDtypeStruct((nb, D_MODEL, MEM_LEN), BF16),
                   jax.ShapeDtypeStruct((nb, MEM_LEN, D_MODEL), BF16),
                   jax.ShapeDtypeStruct((D_MODEL, IN_PROJ), BF16),
                   jax.ShapeDtypeStruct((D_MODEL, D_MODEL), BF16),
                   jax.ShapeDtypeStruct((D_MODEL, D_MODEL), BF16),
                   jax.ShapeDtypeStruct((D_MODEL, D_MODEL), BF16),
                   jax.ShapeDtypeStruct((ns, 6 * 512), F32),
                   jax.ShapeDtypeStruct((ns, D_MODEL), F32)],
        scratch_shapes=[pltpu.VMEM((D_MODEL, 2 * D_MODEL), BF16),
                        pltpu.VMEM((2 * X_HEADS * SPLIT_PITCH, 128), F32)],
        compiler_params=pltpu.CompilerParams(
            dimension_semantics=("arbitrary",), vmem_limit_bytes=VMEM_LIMIT_BYTES),
        name="mem_kv_sample_gates",
    )(mem, g_mem, w_kv, w_in, w_out, w_cq, w_co, pool_mix, pool_scale.reshape(4, 1, POOL_GROUP),
      x_sample, g_mix, hgrn_lb)


def _mix_block(n, x, proj_ref, out_ref, fillers, lb_ref, hn_ref, wout_ref,
               st_ref, ext_ref, qq_ref, kk_ref, b_ref, mrg_ref):
    T = MIX_ROWS
    fillers = list(fillers)
    schedule = list(MIX_FILLER_SCHEDULE)
    assert len(schedule) == 5 + T // CHUNK and sum(schedule) == len(fillers)

    def fill():
        for _ in range(schedule.pop(0)):
            fillers.pop(0)()

    l = n % MIX_BLOCKS_PER_SEQ
    first = l == 0
    for h in range(HEADS):
        st_ref[h] = jnp.where(first, 0.0, st_ref[h])
    ext_ref[0:16, :] = jnp.where(first, 0.0, ext_ref[0:16, :])

    fill()
    lb = _lower_bound(lb_ref)
    qq, fgate, kk = _gates(proj_ref[:, 0:512], proj_ref[:, 512:1024], lb)
    qq_ref[...] = qq
    kk_ref[...] = kk
    fill()

    r = lax.broadcasted_iota(jnp.int32, (TRI_ROWS, TRI_ROWS), 0)
    c = lax.broadcasted_iota(jnp.int32, (TRI_ROWS, TRI_ROWS), 1)
    tri = jnp.where((c <= r) & (c >= (r & -CHUNK)), 1.0, 0.0).astype(BF16)
    logf = jnp.log2(fgate)
    for blk in range(T // TRI_ROWS):
        rows = slice(blk * TRI_ROWS, (blk + 1) * TRI_ROWS)
        hi, lo = _split2(logf[rows])
        b_ref[rows, :] = _dot(tri, hi) + _dot(tri, lo)
    fill()

    cr = lax.broadcasted_iota(jnp.int32, (CHUNK, CHUNK), 0)
    cc = lax.broadcasted_iota(jnp.int32, (CHUNK, CHUNK), 1)
    causal = cc <= cr
    mid_row = (CHUNK - 1) // 2

    for ci in range(T // CHUNK):
        r0 = ci * CHUNK
        rows = slice(r0, r0 + CHUNK)
        scores, inters, vals = [], [], []
        for h in range(HEADS):
            cols = slice(h * DK, (h + 1) * DK)
            b = b_ref[rows, cols]
            m = b_ref[r0 + mid_row:r0 + mid_row + 1, cols]
            b_end = b_ref[r0 + CHUNK - 1:r0 + CHUNK, cols]
            e1 = jnp.exp2(b - m)
            e2 = jnp.exp2(m - b)
            q1 = qq_ref[rows, cols] * e1
            q0 = q1 * jnp.exp2(m)
            ks = kk_ref[rows, cols] * e2
            k2 = ks * jnp.exp2(b_end - m)
            v = proj_ref[rows, 1024 + h * DK:1024 + (h + 1) * DK].astype(BF16)
            st = st_ref[h]
            scores.append(_dot_nt(q1.astype(BF16), ks.astype(BF16)))
            inters.append(_dot_nt(q0.astype(BF16), st.astype(BF16)))
            st_ref[h] = st * jnp.exp2(b_end) + _dot_tn(v, k2.astype(BF16))
            vals.append(v)
        fill()
        for h in range(HEADS):
            cols = slice(h * DK, (h + 1) * DK)
            a = jnp.where(causal, scores[h], 0.0)
            o = inters[h] + _dot(a.astype(BF16), vals[h])
            o = o * lax.rsqrt(jnp.mean(o * o, axis=-1, keepdims=True) + EPS)
            o = o * hn_ref[:, cols]
            g = proj_ref[rows, 1536 + h * DK:1536 + (h + 1) * DK]
            mrg_ref[rows, cols] = (o * _sigmoid(g)).astype(BF16)

    ext_ref[16:16 + T, :] = proj_ref[:, 2048:2560]
    pos = l * T + lax.broadcasted_iota(jnp.int32, (16, POOL_GROUP), 0)
    for gi, w in enumerate(POOL_WINDOWS):
        if gi % 2 == 0:
            fill()
        cols = slice(gi * POOL_GROUP, (gi + 1) * POOL_GROUP)
        s = ext_ref[:, cols]
        u = s[16:]
        shift = 1
        while shift < w:
            s = s + pltpu.roll(s, shift, axis=0)
            shift *= 2
        acc = s[16:]
        cnt = jnp.minimum(pos + 1, w).astype(F32)
        pooled = jnp.concatenate([acc[0:16] / cnt, acc[16:] * (1.0 / w)], axis=0) - u
        mrg_ref[:, HGRN_WIDTH + gi * POOL_GROUP:HGRN_WIDTH + (gi + 1) * POOL_GROUP] = pooled.astype(BF16)

    ext_ref[0:16, :] = ext_ref[T:T + 16, :]
    out_ref[...] = x + _dot(mrg_ref[...], wout_ref[...])


def _mixer_kernel(xa_ref, xn_ref, g_ref, win_ref, lb_ref, hn_ref, wout_ref,
                  w1f_ref,
                  x1_ref, hst_ref, pst_ref, w1b_ref,
                  p0_ref, p1_ref, hb_ref, st_ref, ext_ref, qq_ref, kk_ref, b_ref, mrg_ref):
    T = MIX_ROWS
    g = pl.program_id(0)
    w1b_ref[...] = w1f_ref[...].astype(BF16)
    rest = (lb_ref, hn_ref, wout_ref, st_ref, ext_ref, qq_ref, kk_ref, b_ref, mrg_ref)

    def in_proj_pieces(x_ref, rows, p_ref):
        def prep():
            hb_ref[...] = _rms(x_ref[rows, :], g_ref[...]).astype(BF16)

        def piece(k):
            cols = slice(k * IN_PROJ_PIECE, (k + 1) * IN_PROJ_PIECE)
            p_ref[:, cols] = _dot(hb_ref[...], win_ref[:, cols])

        return [prep] + [functools.partial(piece, k) for k in range(IN_PROJ // IN_PROJ_PIECE)]

    @pl.when(g == 0)
    def _():
        st_ref[...] = jnp.zeros_like(st_ref)
        ext_ref[0:16, :] = jnp.zeros((16, POOL_WIDTH), F32)
        for f in in_proj_pieces(xa_ref, slice(0, T), p0_ref):
            f()

    _mix_block(2 * g, xa_ref[0:T, :], p0_ref, x1_ref.at[0:T, :],
               in_proj_pieces(xa_ref, slice(T, 2 * T), p1_ref), *rest)
    _mix_block(2 * g + 1, xa_ref[T:2 * T, :], p1_ref, x1_ref.at[T:2 * T, :],
               in_proj_pieces(xn_ref, slice(0, T), p0_ref), *rest)

    steps_per_seq = MIX_BLOCKS_PER_SEQ // 2

    @pl.when(g % steps_per_seq == steps_per_seq - 1)
    def _():
        for h in range(HEADS):
            hst_ref[0, h] = st_ref[h].T

    for seq in range(pst_ref.shape[1]):
        @pl.when(g == seq * steps_per_seq + steps_per_seq - 1)
        def _():
            pst_ref[:, seq, :] = ext_ref[T + 1:T + 16, :]


def _mixer_prompt(x, g_mix, w_in, hgrn_lb, hgrn_norm, w_out, w_ffn_in):
    nb, L, _ = x.shape
    T = MIX_ROWS
    assert L // T == MIX_BLOCKS_PER_SEQ and MIX_BLOCKS_PER_SEQ % 2 == 0
    n_blocks = nb * MIX_BLOCKS_PER_SEQ
    steps = n_blocks // 2
    steps_per_seq = MIX_BLOCKS_PER_SEQ // 2
    x2d = x.reshape(nb * L, D_MODEL)
    full = lambda shape: pl.BlockSpec(shape, lambda g: (0,) * len(shape))
    rows = lambda w: pl.BlockSpec((w.shape[0] // steps, w.shape[1]), lambda g: (g, 0))
    side = (w_ffn_in,)
    x1, hst, pst, w1b = pl.pallas_call(
        _mixer_kernel,
        grid=(steps,),
        in_specs=[pl.BlockSpec((2 * T, D_MODEL), lambda g: (g, 0)),
                  pl.BlockSpec((T, D_MODEL), lambda g: (jnp.minimum(2 * g + 2, n_blocks - 1), 0)),
                  full((1, D_MODEL)), full((D_MODEL, IN_PROJ)), full((2, HGRN_WIDTH)),
                  full((1, HGRN_WIDTH)), full((D_MODEL, D_MODEL))] + [rows(w) for w in side],
        out_specs=[pl.BlockSpec((2 * T, D_MODEL), lambda g: (g, 0)),
                   pl.BlockSpec((1, HEADS, DK, DK), lambda g: (g // steps_per_seq, 0, 0, 0)),
                   full((POOL_STATE, nb, POOL_WIDTH))]
                  + [rows(w) for w in side],
        out_shape=[jax.ShapeDtypeStruct((nb * L, D_MODEL), F32),
                   jax.ShapeDtypeStruct((nb, HEADS, DK, DK), F32),
                   jax.ShapeDtypeStruct((POOL_STATE, nb, POOL_WIDTH), F32)]
                  + [jax.ShapeDtypeStruct(w.shape, BF16) for w in side],
        scratch_shapes=[pltpu.VMEM((T, IN_PROJ), F32),
                        pltpu.VMEM((T, IN_PROJ), F32),
                        pltpu.VMEM((T, D_MODEL), BF16),
                        pltpu.VMEM((HEADS, DK, DK), F32),
                        pltpu.VMEM((16 + T, POOL_WIDTH), F32),
                        pltpu.VMEM((T, HGRN_WIDTH), F32),
                        pltpu.VMEM((T, HGRN_WIDTH), F32),
                        pltpu.VMEM((T, HGRN_WIDTH), F32),
                        pltpu.VMEM((T, D_MODEL), BF16)],
        compiler_params=pltpu.CompilerParams(
            dimension_semantics=("arbitrary",), vmem_limit_bytes=VMEM_LIMIT_BYTES),
        name="mixer_prompt",
    )(x2d, x2d, g_mix, w_in, hgrn_lb, hgrn_norm, w_out, *side)
    return x1.reshape(nb, L, D_MODEL), hst, pst, w1b


def _state_step(base, gates_ref, s_ref, so_ref, o_ref):
    G = STATE_TOKENS
    pad = jnp.zeros((DK - G, DK), F32)
    for h in range(HEADS):
        cols = slice(h * DK, (h + 1) * DK)
        qt = jnp.concatenate([gates_ref[:, h * DK:(h + 1) * DK], pad], axis=0).T
        ft = jnp.concatenate([gates_ref[:, 512 + h * DK:512 + (h + 1) * DK], pad], axis=0).T
        kt = jnp.concatenate([gates_ref[:, 1024 + h * DK:1024 + (h + 1) * DK], pad], axis=0).T
        readouts = []
        for j in range(G):
            v = gates_ref[j:j + 1, 1536 + h * DK:1536 + (h + 1) * DK]
            s_new = ft[:, j:j + 1] * s_ref[j, h] + kt[:, j:j + 1] * v
            so_ref[j, h] = s_new
            readouts.append(jnp.sum(qt[:, j:j + 1] * s_new, axis=0, keepdims=True))
        o_ref[pl.ds(pl.multiple_of(base, G), G), cols] = jnp.concatenate(readouts, axis=0)


def _sample_mix(x_ref, gates_ref, o_ref, past_ref, hn_ref, wout_ref, gc_ref, wq_ref,
                x1_ref, pool_ref, q_ref, mrg_ref):
    n = x_ref.shape[0]
    for h in range(HEADS):
        cols = slice(h * DK, (h + 1) * DK)
        o = o_ref[:, cols]
        o = o * lax.rsqrt(jnp.mean(o * o, axis=-1, keepdims=True) + EPS) * hn_ref[:, cols]
        mrg_ref[0:n, cols] = (o * gates_ref[:, 2048 + h * DK:2048 + (h + 1) * DK]).astype(BF16)
    for gi, w in enumerate(POOL_WINDOWS):
        cols = slice(gi * POOL_GROUP, (gi + 1) * POOL_GROUP)
        u = gates_ref[:, 2560 + gi * POOL_GROUP:2560 + (gi + 1) * POOL_GROUP]
        acc = u
        for j in range(1, w):
            acc = acc + past_ref[POOL_STATE - j, :, cols]
        pooled = acc / float(w) - u
        mrg_ref[0:n, HGRN_WIDTH + gi * POOL_GROUP:HGRN_WIDTH + (gi + 1) * POOL_GROUP] = pooled.astype(BF16)
    pool_ref[0:POOL_STATE - 1] = past_ref[1:POOL_STATE]
    pool_ref[POOL_STATE - 1] = gates_ref[:, 2560:3072]
    x1 = x_ref[...] + _dot(mrg_ref[0:n, :], wout_ref[...])
    x1_ref[...] = x1
    hb = _rms(x1, gc_ref[...]).astype(BF16)
    q = _dot(hb, wq_ref[...]) * QUERY_SCALE
    for r in range(2 * X_HEADS):
        half, head = divmod(r, X_HEADS)
        c0 = head * X_HEAD_DIM + half * 128
        q_ref[:, r, :] = q[:, c0:c0 + 128]


def _attn_state_kernel(x_ref, g_ref, wq_ref, kt_ref, vb_ref, wo_ref, gates_ref, s_ref, w2f_ref,
                       xs_ref, gall_ref, past_ref, hn_ref, wout_ref,
                       o_ref, so_ref, w2b_ref, x1s_ref, pool_ref, qs_ref,
                       att_ref, oo_ref):
    i = pl.program_id(0)
    last = pl.num_programs(0) - 1

    @pl.when(i < last)
    def _():
        w2b_ref[...] = w2f_ref[...].astype(BF16)
        _attention_block(x_ref, g_ref, kt_ref, vb_ref, wq_ref, wo_ref, o_ref, att_ref)
        _state_step(i * STATE_TOKENS, gates_ref, s_ref, so_ref, oo_ref)

    @pl.when(i == last)
    def _():
        _sample_mix(xs_ref, gall_ref, oo_ref, past_ref, hn_ref, wout_ref, g_ref, wq_ref,
                    x1s_ref, pool_ref, qs_ref, att_ref)


def _attention_block(x_ref, g_ref, kt_ref, vb_ref, wq_ref, wo_ref, o_ref, att_ref):
    H = ATTN_ROWS // 2
    halves = (slice(0, H), slice(H, 2 * H))

    def query(rows):
        hb = _rms(x_ref[0, rows, :], g_ref[...]).astype(BF16)
        return (_dot(hb, wq_ref[...]) * QUERY_SCALE).astype(BF16)

    def scores(q):
        return [_dot(q[:, h * X_HEAD_DIM:(h + 1) * X_HEAD_DIM],
                     kt_ref[0, h * X_HEAD_DIM:(h + 1) * X_HEAD_DIM, :]) for h in range(X_HEADS)]

    def values(rows, ss):
        for h, s in enumerate(ss):
            cols = slice(h * X_HEAD_DIM, (h + 1) * X_HEAD_DIM)
            e = jnp.exp2(s - jnp.max(s, axis=-1, keepdims=True))
            den = jnp.sum(e, axis=-1, keepdims=True)
            o = _dot(e.astype(BF16), vb_ref[0, :, cols]) / den
            att_ref[rows, cols] = o.astype(BF16)

    def project(rows):
        o_ref[0, rows, :] = x_ref[0, rows, :] + _dot(att_ref[rows, :], wo_ref[...])

    s0 = scores(query(halves[0]))
    q1 = query(halves[1])
    values(halves[0], s0)
    s1 = scores(q1)
    project(halves[0])
    values(halves[1], s1)
    project(halves[1])


def _attn_prompt_state_sample(x, g_cross, w_cq, kt, vb, w_co, gates, state, w_ffn_out,
                              x_sample, past, hgrn_norm, w_out):
    nb, L, _ = x.shape
    T = ATTN_ROWS
    G = STATE_TOKENS
    steps_per_seq = L // T
    steps = nb * steps_per_seq
    ns = gates.shape[0]
    assert steps == ns // G
    full = lambda shape: pl.BlockSpec(shape, lambda i: (0,) * len(shape))
    blk = lambda i: jnp.minimum(i, steps - 1)
    st_spec = pl.BlockSpec((G, HEADS, DK, DK), lambda i: (blk(i), 0, 0, 0))
    x_spec = pl.BlockSpec((1, T, D_MODEL),
                          lambda i: (blk(i) // steps_per_seq, blk(i) % steps_per_seq, 0))
    kv_map = lambda i: (blk(i) // steps_per_seq, 0, 0)
    w2_spec = pl.BlockSpec((D_FF // steps, D_MODEL), lambda i: (blk(i), 0))
    return pl.pallas_call(
        _attn_state_kernel,
        grid=(steps + 1,),
        in_specs=[x_spec, full((1, D_MODEL)), full((D_MODEL, D_MODEL)),
                  pl.BlockSpec((1, D_MODEL, MEM_LEN), kv_map),
                  pl.BlockSpec((1, MEM_LEN, D_MODEL), kv_map),
                  full((D_MODEL, D_MODEL)),
                  pl.BlockSpec((G, 4 * 512), lambda i: (blk(i), 0)), st_spec, w2_spec,
                  full((ns, D_MODEL)), full((ns, 6 * 512)), full((POOL_STATE, ns, POOL_WIDTH)),
                  full((1, HGRN_WIDTH)), full((D_MODEL, D_MODEL))],
        out_specs=[x_spec, st_spec, w2_spec,
                   full((ns, D_MODEL)), full((POOL_STATE, ns, POOL_WIDTH)),
                   full((ns, 2 * X_HEADS, 128))],
        out_shape=[jax.ShapeDtypeStruct((nb, L, D_MODEL), F32),
                   jax.ShapeDtypeStruct((ns, HEADS, DK, DK), F32),
                   jax.ShapeDtypeStruct((D_FF, D_MODEL), BF16),
                   jax.ShapeDtypeStruct((ns, D_MODEL), F32),
                   jax.ShapeDtypeStruct((POOL_STATE, ns, POOL_WIDTH), F32),
                   jax.ShapeDtypeStruct((ns, 2 * X_HEADS, 128), F32)],
        scratch_shapes=[pltpu.VMEM((T, D_MODEL), BF16), pltpu.VMEM((ns, HGRN_WIDTH), F32)],
        compiler_params=pltpu.CompilerParams(
            dimension_semantics=("arbitrary",), vmem_limit_bytes=VMEM_LIMIT_BYTES),
        name="attn_prompt_state_sample",
    )(x, g_cross, w_cq, kt, vb, w_co, gates, state, w_ffn_out,
      x_sample, gates, past, hgrn_norm, w_out)


def _zero_after(x):
    u = lax.bitcast_convert_type(x, jnp.uint32)
    z = lax.shift_right_logical(lax.shift_right_logical(u, jnp.uint32(16)), jnp.uint32(16))
    return lax.bitcast_convert_type(z, F32)


def _ffn_body(x, g_ref, w1_ref, w2_ref, gf_ref, fillers=()):
    hb = _rms(x, g_ref[...]).astype(BF16)

    def up(c0, c1):
        return _dot(hb, w1_ref[:, c0:c1]), _dot(hb, w1_ref[:, D_FF + c0:D_FF + c1])

    def add_to_first_tile(m, z):
        top = jnp.concatenate([m[0:8, 0:128] + z, m[0:8, 128:]], axis=1)
        return jnp.concatenate([top, m[8:, :]], axis=0)

    y = x
    fillers = list(fillers)
    nxt = up(*FFN_COL_CHUNKS[0])
    for i, (c0, c1) in enumerate(FFN_COL_CHUNKS):
        a, bg = nxt
        if i + 1 < len(FFN_COL_CHUNKS):
            nxt = up(*FFN_COL_CHUNKS[i + 1])
        for _ in range(-(-len(fillers) // (len(FFN_COL_CHUNKS) - i))):
            bg = add_to_first_tile(bg, _zero_after(fillers.pop(0)()))
        act = (a * _sigmoid(a) * bg).astype(BF16)
        y = y + _dot(act, w2_ref[c0:c1, :])
    return _rms(y, gf_ref[...])


def _split_heads(x):
    lead = x.shape[:-2]
    x = x.reshape(lead + (X_HEADS, 2, 128))
    x = jnp.swapaxes(x, -3, -2)
    return x.reshape(lead + (2 * X_HEADS, 128))


def _merge_heads(x):
    lead = x.shape[:-2]
    x = x.reshape(lead + (2, X_HEADS, 128))
    x = jnp.swapaxes(x, -3, -2)
    return x.reshape(lead + (X_HEADS, X_HEAD_DIM))


def _memory_attention(j, base, q_ref, k_ref, v_ref, o_ref):
    prod = k_ref[j] * q_ref[base + j][None]
    s = jnp.sum(prod + pltpu.roll(prod, X_HEADS, axis=1), axis=-1, keepdims=True)
    e = jnp.exp2(s - jnp.max(s, axis=0, keepdims=True))
    den = jnp.sum(e, axis=0)
    o = jnp.sum(e * v_ref[j], axis=0) / den
    o_ref[base + j] = o
    return o


def _ffn_attn_kernel(x_ref, g_ref, w1_ref, w2_ref, gf_ref, q_ref, k_ref, v_ref, xs_ref, wo_ref,
                     o_ref, ys_ref, att_ref):
    i = pl.program_id(0)
    last = pl.num_programs(0) - 1

    @pl.when(i < last)
    def _():
        base = i * ATTN_TOKENS
        fillers = [functools.partial(_memory_attention, j, base, q_ref, k_ref, v_ref, att_ref)
                   for j in range(ATTN_TOKENS)]
        o_ref[...] = _ffn_body(x_ref[...], g_ref, w1_ref, w2_ref, gf_ref, fillers)

    @pl.when(i == last)
    def _():
        att = jnp.concatenate([att_ref[:, half * X_HEADS + head, :]
                               for head in range(X_HEADS) for half in range(2)], axis=1)
        x2 = xs_ref[...] + _dot(att.astype(BF16), wo_ref[...])
        ys_ref[:, 0, :] = _ffn_body(x2, g_ref, w1_ref, w2_ref, gf_ref)


def _ffn_prompt_attn_sample(x, g_ffn, w1, w2, g_final, q, cache_k, cache_v, x_sample, w_co):
    n = x.shape[0]
    ns = q.shape[0]
    T = FFN_ROWS
    G = ATTN_TOKENS
    steps = n // T
    assert steps == ns // G
    full = lambda shape: pl.BlockSpec(shape, lambda i: (0,) * len(shape))
    blk = lambda i: jnp.minimum(i, steps - 1)
    kv_spec = pl.BlockSpec((G, MEM_LEN, 2 * X_HEADS, 128), lambda i: (blk(i), 0, 0, 0))
    q_spec = full((ns, 2 * X_HEADS, 128))
    y, ys = pl.pallas_call(
        _ffn_attn_kernel,
        grid=(steps + 1,),
        in_specs=[pl.BlockSpec((T, D_MODEL), lambda i: (blk(i), 0)),
                  full((1, D_MODEL)), full((D_MODEL, 2 * D_FF)), full((D_FF, D_MODEL)),
                  full((1, D_MODEL)), q_spec, kv_spec, kv_spec,
                  full((ns, D_MODEL)), full((D_MODEL, D_MODEL))],
        out_specs=[pl.BlockSpec((T, D_MODEL), lambda i: (blk(i), 0)),
                   full((ns, 1, D_MODEL))],
        out_shape=[jax.ShapeDtypeStruct((n, D_MODEL), F32),
                   jax.ShapeDtypeStruct((ns, 1, D_MODEL), F32)],
        scratch_shapes=[pltpu.VMEM((ns, 2 * X_HEADS, 128), F32)],
        compiler_params=pltpu.CompilerParams(
            dimension_semantics=("arbitrary",), vmem_limit_bytes=VMEM_LIMIT_BYTES),
        name="ffn_prompt_attn_sample",
    )(x, g_ffn, w1, w2, g_final, q, cache_k, cache_v, x_sample, w_co)
    return y, ys


def kernel(x_prompt, x_sample, mem_prompt, state_hgrn, state_pool, cache_mem_k, cache_mem_v,
           g_mix, w_in, hgrn_lb, hgrn_norm, pool_mix, pool_scale, w_out, g_mem, w_mem_kv,
           g_cross, w_cq, w_co, g_ffn, w_ffn_in, w_ffn_out, g_final):
    nb, L, _ = x_prompt.shape
    ns = x_sample.shape[0]

    g_final2 = g_final.reshape(1, D_MODEL)

    mem_k, mem_v, kt, vb, w_in_b, w_out_b, w_cq_b, w_co_b, gates, xs = _mem_kv_sample_gates(
        mem_prompt, g_mem, w_mem_kv[0], w_in[0], w_out[0], w_cq[0], w_co[0], pool_mix[0], pool_scale,
        x_sample, g_mix, hgrn_lb)
    x1, hgrn_p, pool_p, w1_b = _mixer_prompt(x_prompt, g_mix, w_in_b, hgrn_lb, hgrn_norm, w_out_b,
                                             w_ffn_in[0])
    x2, hgrn_s, w2_b, x1s, pool_s, qs = _attn_prompt_state_sample(
        x1, g_cross, w_cq_b, kt, vb, w_co_b, gates, state_hgrn[0], w_ffn_out[0],
        xs, jnp.swapaxes(state_pool[0], 0, 1), hgrn_norm, w_out_b)
    y_prompt, y_sample = _ffn_prompt_attn_sample(
        x2.reshape(nb * L, D_MODEL), g_ffn, w1_b, w2_b, g_final2, qs,
        _split_heads(cache_mem_k[0]), _split_heads(cache_mem_v[0]), x1s, w_co_b)

    return (y_prompt.reshape(nb, L, D_MODEL),
            y_sample,
            hgrn_p[None],
            jnp.swapaxes(pool_p, 0, 1)[None],
            _merge_heads(mem_k)[None],
            _merge_heads(mem_v)[None],
            hgrn_s[None],
            jnp.swapaxes(pool_s, 0, 1)[None])
```

```python
import functools

import jax
import jax.numpy as jnp
from jax import lax
from jax.experimental import pallas as pl
from jax.experimental.pallas import tpu as pltpu

F32 = jnp.float32
BF16 = jnp.bfloat16

D_MODEL = 1024
HGRN_WIDTH = 512
HEADS = 4
DK = 128
CHUNK = 64
POOL_WIDTH = 512
POOL_WINDOWS = (2, 4, 8, 16)
POOL_GROUP = 128
POOL_STATE = 15
IN_PROJ = 4 * HGRN_WIDTH + POOL_WIDTH
MEM_LEN = 256
X_HEADS = 4
X_HEAD_DIM = 256
D_FF = 2816
EPS = 1e-6
ATTN_SCALE = X_HEAD_DIM ** -0.5
LOG2_E = 1.4426950408889634
QUERY_SCALE = ATTN_SCALE * LOG2_E

VMEM_LIMIT_BYTES = 56 * 1024 * 1024

MIX_ROWS = 512
MIX_BLOCKS_PER_SEQ = 4
IN_PROJ_PIECE = 256
MIX_FILLER_SCHEDULE = (3, 0, 0) + (1, 0, 1, 0, 1, 0, 1, 0) + (2, 2)
TRI_ROWS = 256
ATTN_ROWS = 1024
SAMPLE_FFN_COLS = 1408
SPLIT_PITCH = MEM_LEN + 4
FFN_ROWS = 512
STATE_TOKENS = 8
ATTN_TOKENS = 4
FFN_PASS_WIDTHS = (512, 768, 768, 768)
FFN_COL_CHUNKS = tuple((sum(FFN_PASS_WIDTHS[:i]), sum(FFN_PASS_WIDTHS[:i + 1]))
                       for i in range(len(FFN_PASS_WIDTHS)))
assert FFN_COL_CHUNKS[-1][1] == D_FF


def _dot(a, b):
    return jnp.dot(a, b, preferred_element_type=F32)


def _dot_nt(a, b):
    return lax.dot_general(a, b, (((1,), (1,)), ((), ())), preferred_element_type=F32)


def _dot_tn(a, b):
    return lax.dot_general(a, b, (((0,), (0,)), ((), ())), preferred_element_type=F32)


def _rms(x, g):
    ms = jnp.mean(x * x, axis=-1, keepdims=True)
    return x * lax.rsqrt(ms + EPS) * g


def _sigmoid(x):
    return 1.0 / (1.0 + jnp.exp(-x))


def _lower_bound(lb_ref):
    t = lb_ref[...]
    m = jnp.max(t, axis=0, keepdims=True)
    e = jnp.exp(t - m)
    return e[0:1, :] / jnp.sum(e, axis=0, keepdims=True)


def _gates(proj_q, proj_f, lb):
    qq = proj_q * _sigmoid(proj_q)
    sig = _sigmoid(proj_f)
    fgate = lb + (1.0 - lb) * sig
    kk = (1.0 - lb) * (1.0 - sig)
    return qq, fgate, kk


def _split2(x):
    hi = x.astype(BF16)
    return hi, (x - hi.astype(F32)).astype(BF16)


def _sample_gates(x, g_ref, winb_ref, lb_ref, o_ref, x2d_ref):
    x2d_ref[...] = x
    hb = _rms(x, g_ref[...]).astype(BF16)
    proj = _dot(hb, winb_ref[...])
    lb = _lower_bound(lb_ref)
    qq, fgate, kk = _gates(proj[:, 0:512], proj[:, 512:1024], lb)
    o_ref[:, 0:512] = qq
    o_ref[:, 512:1024] = fgate
    o_ref[:, 1024:1536] = kk
    o_ref[:, 1536:2048] = proj[:, 1024:1536]
    o_ref[:, 2048:2560] = _sigmoid(proj[:, 1536:2048])
    o_ref[:, 2560:3072] = proj[:, 2048:2560]


def _memkv_kernel(mem_ref, g_ref, w_ref, win_ref, wout_ref, wcq_ref, wco_ref, pmix_ref, ps_ref,
                  xs_ref, gmix_ref, lb_ref,
                  k_ref, v_ref, kt_ref, vb_ref, winb_ref, woutb_ref, wcqb_ref, wcob_ref,
                  gates_ref, xs2d_ref,
                  wb_ref, rows_ref):
    b = pl.program_id(0)
    nb = pl.num_programs(0) - 1
    first_pool_block = HGRN_WIDTH // POOL_GROUP

    @pl.when(b == 0)
    def _():
        wb_ref[...] = w_ref[...].astype(BF16)

    @pl.when(b < nb)
    def _():
        wrows = win_ref.shape[0]
        winb_ref[pl.ds(pl.multiple_of(b * wrows, wrows), wrows), :] = win_ref[...].astype(BF16)
        wcqb_ref[...] = wcq_ref[...].astype(BF16)
        wcob_ref[...] = wco_ref[...].astype(BF16)

        @pl.when(b < first_pool_block)
        def _():
            woutb_ref[...] = wout_ref[...].astype(BF16)

        @pl.when(b >= first_pool_block)
        def _():
            gi = b - first_pool_block
            a_hi, a_lo = _split2(pmix_ref[gi] * ps_ref[gi])
            w_hi, w_lo = _split2(wout_ref[...])
            woutb_ref[...] = (_dot(a_hi, w_hi) + _dot(a_hi, w_lo) + _dot(a_lo, w_hi)).astype(BF16)

        h = _rms(mem_ref[0], g_ref[...]).astype(BF16)
        kv = _dot(h, wb_ref[...])
        k = kv[:, :D_MODEL]
        v = kv[:, D_MODEL:]
        kt_ref[0] = k.T.astype(BF16)
        vb_ref[0] = v.astype(BF16)
        for val, out_ref in ((k, k_ref), (v, v_ref)):
            for r in range(2 * X_HEADS):
                half, head = divmod(r, X_HEADS)
                c0 = head * X_HEAD_DIM + half * 128
                rows_ref[r * SPLIT_PITCH:r * SPLIT_PITCH + MEM_LEN, :] = val[:, c0:c0 + 128]
            for m in range(MEM_LEN):
                out_ref[0, m] = rows_ref[pl.ds(m, 2 * X_HEADS, stride=SPLIT_PITCH), :]

    @pl.when(b == nb)
    def _():
        _sample_gates(xs_ref[:, 0, :], gmix_ref, winb_ref, lb_ref, gates_ref, xs2d_ref)


def _mem_kv_sample_gates(mem, g_mem, w_kv, w_in, w_out, w_cq, w_co, pool_mix, pool_scale,
                         x_sample, g_mix, hgrn_lb):
    nb = mem.shape[0]
    ns = x_sample.shape[0]
    full = lambda shape: pl.BlockSpec(shape, lambda b: (0,) * len(shape))
    blk = lambda b: jnp.minimum(b, nb - 1)
    split_spec = pl.BlockSpec((1, MEM_LEN, 2 * X_HEADS, 128), lambda b: (blk(b), 0, 0, 0))
    wrows = D_MODEL // nb
    assert wrows == POOL_GROUP
    win_spec = pl.BlockSpec((wrows, IN_PROJ), lambda b: (blk(b), 0))
    wout_spec = pl.BlockSpec((wrows, D_MODEL), lambda b: (blk(b), 0))
    return pl.pallas_call(
        _memkv_kernel,
        grid=(nb + 1,),
        in_specs=[pl.BlockSpec((1, MEM_LEN, D_MODEL), lambda b: (blk(b), 0, 0)),
                  full((1, D_MODEL)), full((D_MODEL, 2 * D_MODEL)), win_spec, wout_spec,
                  wout_spec, wout_spec,
                  full((4, POOL_GROUP, POOL_GROUP)), full((4, 1, POOL_GROUP)),
                  full((ns, 1, D_MODEL)), full((1, D_MODEL)), full((2, HGRN_WIDTH))],
        out_specs=[split_spec, split_spec,
                   pl.BlockSpec((1, D_MODEL, MEM_LEN), lambda b: (blk(b), 0, 0)),
                   pl.BlockSpec((1, MEM_LEN, D_MODEL), lambda b: (blk(b), 0, 0)),
                   full((D_MODEL, IN_PROJ)), wout_spec, wout_spec, wout_spec,
                   full((ns, 6 * 512)), full((ns, D_MODEL))],
        out_shape=[jax.ShapeDtypeStruct((nb, MEM_LEN, 2 * X_HEADS, 128), F32),
                   jax.ShapeDtypeStruct((nb, MEM_LEN, 2 * X_HEADS, 128), F32),
                   jax.ShapeDtypeStruct((nb, D_MODEL, MEM_LEN), BF16),
                   jax.ShapeDtypeStruct((nb, MEM_LEN, D_MODEL), BF16),
                   jax.ShapeDtypeStruct((D_MODEL, IN_PROJ), BF16),
                   jax.ShapeDtypeStruct((D_MODEL, D_MODEL), BF16),
                   jax.ShapeDtypeStruct((D_MODEL, D_MODEL), BF16),
                   jax.ShapeDtypeStruct((D_MODEL, D_MODEL), BF16),
                   jax.ShapeDtypeStruct((ns, 6 * 512), F32),
                   jax.ShapeDtypeStruct((ns, D_MODEL), F32)],
        scratch_shapes=[pltpu.VMEM((D_MODEL, 2 * D_MODEL), BF16),
                        pltpu.VMEM((2 * X_HEADS * SPLIT_PITCH, 128), F32)],
        compiler_params=pltpu.CompilerParams(
            dimension_semantics=("arbitrary",), vmem_limit_bytes=VMEM_LIMIT_BYTES),
        name="mem_kv_sample_gates",
    )(mem, g_mem, w_kv, w_in, w_out, w_cq, w_co, pool_mix, pool_scale.reshape(4, 1, POOL_GROUP),
      x_sample, g_mix, hgrn_lb)


def _mix_block(n, x, proj_ref, out_ref, fillers, lb_ref, hn_ref, wout_ref,
               st_ref, ext_ref, qq_ref, kk_ref, b_ref, mrg_ref):
    T = MIX_ROWS
    fillers = list(fillers)
    schedule = list(MIX_FILLER_SCHEDULE)
    assert len(schedule) == 5 + T // CHUNK and sum(schedule) == len(fillers)

    def fill():
        for _ in range(schedule.pop(0)):
            fillers.pop(0)()

    l = n % MIX_BLOCKS_PER_SEQ
    first = l == 0
    for h in range(HEADS):
        st_ref[h] = jnp.where(first, 0.0, st_ref[h])
    ext_ref[0:16, :] = jnp.where(first, 0.0, ext_ref[0:16, :])

    fill()
    lb = _lower_bound(lb_ref)
    qq, fgate, kk = _gates(proj_ref[:, 0:512], proj_ref[:, 512:1024], lb)
    qq_ref[...] = qq
    kk_ref[...] = kk
    fill()

    r = lax.broadcasted_iota(jnp.int32, (TRI_ROWS, TRI_ROWS), 0)
    c = lax.broadcasted_iota(jnp.int32, (TRI_ROWS, TRI_ROWS), 1)
    tri = jnp.where((c <= r) & (c >= (r & -CHUNK)), 1.0, 0.0).astype(BF16)
    logf = jnp.log2(fgate)
    for blk in range(T // TRI_ROWS):
        rows = slice(blk * TRI_ROWS, (blk + 1) * TRI_ROWS)
        hi, lo = _split2(logf[rows])
        b_ref[rows, :] = _dot(tri, hi) + _dot(tri, lo)
    fill()

    cr = lax.broadcasted_iota(jnp.int32, (CHUNK, CHUNK), 0)
    cc = lax.broadcasted_iota(jnp.int32, (CHUNK, CHUNK), 1)
    causal = cc <= cr
    mid_row = (CHUNK - 1) // 2

    for ci in range(T // CHUNK):
        r0 = ci * CHUNK
        rows = slice(r0, r0 + CHUNK)
        scores, inters, vals = [], [], []
        for h in range(HEADS):
            cols = slice(h * DK, (h + 1) * DK)
            b = b_ref[rows, cols]
            m = b_ref[r0 + mid_row:r0 + mid_row + 1, cols]
            b_end = b_ref[r0 + CHUNK - 1:r0 + CHUNK, cols]
            e1 = jnp.exp2(b - m)
            e2 = jnp.exp2(m - b)
            q1 = qq_ref[rows, cols] * e1
            q0 = q1 * jnp.exp2(m)
            ks = kk_ref[rows, cols] * e2
            k2 = ks * jnp.exp2(b_end - m)
            v = proj_ref[rows, 1024 + h * DK:1024 + (h + 1) * DK].astype(BF16)
            st = st_ref[h]
            scores.append(_dot_nt(q1.astype(BF16), ks.astype(BF16)))
            inters.append(_dot_nt(q0.astype(BF16), st.astype(BF16)))
            st_ref[h] = st * jnp.exp2(b_end) + _dot_tn(v, k2.astype(BF16))
            vals.append(v)
        fill()
        for h in range(HEADS):
            cols = slice(h * DK, (h + 1) * DK)
            a = jnp.where(causal, scores[h], 0.0)
            o = inters[h] + _dot(a.astype(BF16), vals[h])
            o = o * lax.rsqrt(jnp.mean(o * o, axis=-1, keepdims=True) + EPS)
            o = o * hn_ref[:, cols]
            g = proj_ref[rows, 1536 + h * DK:1536 + (h + 1) * DK]
            mrg_ref[rows, cols] = (o * _sigmoid(g)).astype(BF16)

    ext_ref[16:16 + T, :] = proj_ref[:, 2048:2560]
    pos = l * T + lax.broadcasted_iota(jnp.int32, (16, POOL_GROUP), 0)
    for gi, w in enumerate(POOL_WINDOWS):
        if gi % 2 == 0:
            fill()
        cols = slice(gi * POOL_GROUP, (gi + 1) * POOL_GROUP)
        s = ext_ref[:, cols]
        u = s[16:]
        shift = 1
        while shift < w:
            s = s + pltpu.roll(s, shift, axis=0)
            shift *= 2
        acc = s[16:]
        cnt = jnp.minimum(pos + 1, w).astype(F32)
        pooled = jnp.concatenate([acc[0:16] / cnt, acc[16:] * (1.0 / w)], axis=0) - u
        mrg_ref[:, HGRN_WIDTH + gi * POOL_GROUP:HGRN_WIDTH + (gi + 1) * POOL_GROUP] = pooled.astype(BF16)

    ext_ref[0:16, :] = ext_ref[T:T + 16, :]
    out_ref[...] = x + _dot(mrg_ref[...], wout_ref[...])


def _mixer_kernel(xa_ref, xn_ref, g_ref, win_ref, lb_ref, hn_ref, wout_ref,
                  w1f_ref,
                  x1_ref, hst_ref, pst_ref, w1b_ref,
                  p0_ref, p1_ref, hb_ref, st_ref, ext_ref, qq_ref, kk_ref, b_ref, mrg_ref):
    T = MIX_ROWS
    g = pl.program_id(0)
    w1b_ref[...] = w1f_ref[...].astype(BF16)
    rest = (lb_ref, hn_ref, wout_ref, st_ref, ext_ref, qq_ref, kk_ref, b_ref, mrg_ref)

    def in_proj_pieces(x_ref, rows, p_ref):
        def prep():
            hb_ref[...] = _rms(x_ref[rows, :], g_ref[...]).astype(BF16)

        def piece(k):
            cols = slice(k * IN_PROJ_PIECE, (k + 1) * IN_PROJ_PIECE)
            p_ref[:, cols] = _dot(hb_ref[...], win_ref[:, cols])

        return [prep] + [functools.partial(piece, k) for k in range(IN_PROJ // IN_PROJ_PIECE)]

    @pl.when(g == 0)
    def _():
        st_ref[...] = jnp.zeros_like(st_ref)
        ext_ref[0:16, :] = jnp.zeros((16, POOL_WIDTH), F32)
        for f in in_proj_pieces(xa_ref, slice(0, T), p0_ref):
            f()

    _mix_block(2 * g, xa_ref[0:T, :], p0_ref, x1_ref.at[0:T, :],
               in_proj_pieces(xa_ref, slice(T, 2 * T), p1_ref), *rest)
    _mix_block(2 * g + 1, xa_ref[T:2 * T, :], p1_ref, x1_ref.at[T:2 * T, :],
               in_proj_pieces(xn_ref, slice(0, T), p0_ref), *rest)

    steps_per_seq = MIX_BLOCKS_PER_SEQ // 2

    @pl.when(g % steps_per_seq == steps_per_seq - 1)
    def _():
        for h in range(HEADS):
            hst_ref[0, h] = st_ref[h].T

    for seq in range(pst_ref.shape[1]):
        @pl.when(g == seq * steps_per_seq + steps_per_seq - 1)
        def _():
            pst_ref[:, seq, :] = ext_ref[T + 1:T + 16, :]


def _mixer_prompt(x, g_mix, w_in, hgrn_lb, hgrn_norm, w_out, w_ffn_in):
    nb, L, _ = x.shape
    T = MIX_ROWS
    assert L // T == MIX_BLOCKS_PER_SEQ and MIX_BLOCKS_PER_SEQ % 2 == 0
    n_blocks = nb * MIX_BLOCKS_PER_SEQ
    steps = n_blocks // 2
    steps_per_seq = MIX_BLOCKS_PER_SEQ // 2
    x2d = x.reshape(nb * L, D_MODEL)
    full = lambda shape: pl.BlockSpec(shape, lambda g: (0,) * len(shape))
    rows = lambda w: pl.BlockSpec((w.shape[0] // steps, w.shape[1]), lambda g: (g, 0))
    side = (w_ffn_in,)
    x1, hst, pst, w1b = pl.pallas_call(
        _mixer_kernel,
        grid=(steps,),
        in_specs=[pl.BlockSpec((2 * T, D_MODEL), lambda g: (g, 0)),
                  pl.BlockSpec((T, D_MODEL), lambda g: (jnp.minimum(2 * g + 2, n_blocks - 1), 0)),
                  full((1, D_MODEL)), full((D_MODEL, IN_PROJ)), full((2, HGRN_WIDTH)),
                  full((1, HGRN_WIDTH)), full((D_MODEL, D_MODEL))] + [rows(w) for w in side],
        out_specs=[pl.BlockSpec((2 * T, D_MODEL), lambda g: (g, 0)),
                   pl.BlockSpec((1, HEADS, DK, DK), lambda g: (g // steps_per_seq, 0, 0, 0)),
                   full((POOL_STATE, nb, POOL_WIDTH))]
                  + [rows(w) for w in side],
        out_shape=[jax.ShapeDtypeStruct((nb * L, D_MODEL), F32),
                   jax.ShapeDtypeStruct((nb, HEADS, DK, DK), F32),
                   jax.ShapeDtypeStruct((POOL_STATE, nb, POOL_WIDTH), F32)]
                  + [jax.ShapeDtypeStruct(w.shape, BF16) for w in side],
        scratch_shapes=[pltpu.VMEM((T, IN_PROJ), F32),
                        pltpu.VMEM((T, IN_PROJ), F32),
                        pltpu.VMEM((T, D_MODEL), BF16),
                        pltpu.VMEM((HEADS, DK, DK), F32),
                        pltpu.VMEM((16 + T, POOL_WIDTH), F32),
                        pltpu.VMEM((T, HGRN_WIDTH), F32),
                        pltpu.VMEM((T, HGRN_WIDTH), F32),
                        pltpu.VMEM((T, HGRN_WIDTH), F32),
                        pltpu.VMEM((T, D_MODEL), BF16)],
        compiler_params=pltpu.CompilerParams(
            dimension_semantics=("arbitrary",), vmem_limit_bytes=VMEM_LIMIT_BYTES),
        name="mixer_prompt",
    )(x2d, x2d, g_mix, w_in, hgrn_lb, hgrn_norm, w_out, *side)
    return x1.reshape(nb, L, D_MODEL), hst, pst, w1b


def _state_step(base, gates_ref, s_ref, so_ref, o_ref):
    G = STATE_TOKENS
    pad = jnp.zeros((DK - G, DK), F32)
    for h in range(HEADS):
        cols = slice(h * DK, (h + 1) * DK)
        qt = jnp.concatenate([gates_ref[:, h * DK:(h + 1) * DK], pad], axis=0).T
        ft = jnp.concatenate([gates_ref[:, 512 + h * DK:512 + (h + 1) * DK], pad], axis=0).T
        kt = jnp.concatenate([gates_ref[:, 1024 + h * DK:1024 + (h + 1) * DK], pad], axis=0).T
        readouts = []
        for j in range(G):
            v = gates_ref[j:j + 1, 1536 + h * DK:1536 + (h + 1) * DK]
            s_new = ft[:, j:j + 1] * s_ref[j, h] + kt[:, j:j + 1] * v
            so_ref[j, h] = s_new
            readouts.append(jnp.sum(qt[:, j:j + 1] * s_new, axis=0, keepdims=True))
        o_ref[pl.ds(pl.multiple_of(base, G), G), cols] = jnp.concatenate(readouts, axis=0)


def _sample_mix(x_ref, gates_ref, o_ref, past_ref, hn_ref, wout_ref, gc_ref, wq_ref,
                x1_ref, pool_ref, q_ref, mrg_ref):
    n = x_ref.shape[0]
    for h in range(HEADS):
        cols = slice(h * DK, (h + 1) * DK)
        o = o_ref[:, cols]
        o = o * lax.rsqrt(jnp.mean(o * o, axis=-1, keepdims=True) + EPS) * hn_ref[:, cols]
        mrg_ref[0:n, cols] = (o * gates_ref[:, 2048 + h * DK:2048 + (h + 1) * DK]).astype(BF16)
    for gi, w in enumerate(POOL_WINDOWS):
        cols = slice(gi * POOL_GROUP, (gi + 1) * POOL_GROUP)
        u = gates_ref[:, 2560 + gi * POOL_GROUP:2560 + (gi + 1) * POOL_GROUP]
        acc = u
        for j in range(1, w):
            acc = acc + past_ref[POOL_STATE - j, :, cols]
        pooled = acc / float(w) - u
        mrg_ref[0:n, HGRN_WIDTH + gi * POOL_GROUP:HGRN_WIDTH + (gi + 1) * POOL_GROUP] = pooled.astype(BF16)
    pool_ref[0:POOL_STATE - 1] = past_ref[1:POOL_STATE]
    pool_ref[POOL_STATE - 1] = gates_ref[:, 2560:3072]
    x1 = x_ref[...] + _dot(mrg_ref[0:n, :], wout_ref[...])
    x1_ref[...] = x1
    hb = _rms(x1, gc_ref[...]).astype(BF16)
    q = _dot(hb, wq_ref[...]) * QUERY_SCALE
    for r in range(2 * X_HEADS):
        half, head = divmod(r, X_HEADS)
        c0 = head * X_HEAD_DIM + half * 128
        q_ref[:, r, :] = q[:, c0:c0 + 128]


def _attn_state_kernel(x_ref, g_ref, wq_ref, kt_ref, vb_ref, wo_ref, gates_ref, s_ref, w2f_ref,
                       xs_ref, gall_ref, past_ref, hn_ref, wout_ref,
                       o_ref, so_ref, w2b_ref, x1s_ref, pool_ref, qs_ref,
                       att_ref, oo_ref):
    i = pl.program_id(0)
    last = pl.num_programs(0) - 1

    @pl.when(i < last)
    def _():
        w2b_ref[...] = w2f_ref[...].astype(BF16)
        _attention_block(x_ref, g_ref, kt_ref, vb_ref, wq_ref, wo_ref, o_ref, att_ref)
        _state_step(i * STATE_TOKENS, gates_ref, s_ref, so_ref, oo_ref)

    @pl.when(i == last)
    def _():
        _sample_mix(xs_ref, gall_ref, oo_ref, past_ref, hn_ref, wout_ref, g_ref, wq_ref,
                    x1s_ref, pool_ref, qs_ref, att_ref)


def _attention_block(x_ref, g_ref, kt_ref, vb_ref, wq_ref, wo_ref, o_ref, att_ref):
    H = ATTN_ROWS // 2
    halves = (slice(0, H), slice(H, 2 * H))

    def query(rows):
        hb = _rms(x_ref[0, rows, :], g_ref[...]).astype(BF16)
        return (_dot(hb, wq_ref[...]) * QUERY_SCALE).astype(BF16)

    def scores(q):
        return [_dot(q[:, h * X_HEAD_DIM:(h + 1) * X_HEAD_DIM],
                     kt_ref[0, h * X_HEAD_DIM:(h + 1) * X_HEAD_DIM, :]) for h in range(X_HEADS)]

    def values(rows, ss):
        for h, s in enumerate(ss):
            cols = slice(h * X_HEAD_DIM, (h + 1) * X_HEAD_DIM)
            e = jnp.exp2(s - jnp.max(s, axis=-1, keepdims=True))
            den = jnp.sum(e, axis=-1, keepdims=True)
            o = _dot(e.astype(BF16), vb_ref[0, :, cols]) / den
            att_ref[rows, cols] = o.astype(BF16)

    def project(rows):
        o_ref[0, rows, :] = x_ref[0, rows, :] + _dot(att_ref[rows, :], wo_ref[...])

    s0 = scores(query(halves[0]))
    q1 = query(halves[1])
    values(halves[0], s0)
    s1 = scores(q1)
    project(halves[0])
    values(halves[1], s1)
    project(halves[1])


def _attn_prompt_state_sample(x, g_cross, w_cq, kt, vb, w_co, gates, state, w_ffn_out,
                              x_sample, past, hgrn_norm, w_out):
    nb, L, _ = x.shape
    T = ATTN_ROWS
    G = STATE_TOKENS
    steps_per_seq = L // T
    steps = nb * steps_per_seq
    ns = gates.shape[0]
    assert steps == ns // G
    full = lambda shape: pl.BlockSpec(shape, lambda i: (0,) * len(shape))
    blk = lambda i: jnp.minimum(i, steps - 1)
    st_spec = pl.BlockSpec((G, HEADS, DK, DK), lambda i: (blk(i), 0, 0, 0))
    x_spec = pl.BlockSpec((1, T, D_MODEL),
                          lambda i: (blk(i) // steps_per_seq, blk(i) % steps_per_seq, 0))
    kv_map = lambda i: (blk(i) // steps_per_seq, 0, 0)
    w2_spec = pl.BlockSpec((D_FF // steps, D_MODEL), lambda i: (blk(i), 0))
    return pl.pallas_call(
        _attn_state_kernel,
        grid=(steps + 1,),
        in_specs=[x_spec, full((1, D_MODEL)), full((D_MODEL, D_MODEL)),
                  pl.BlockSpec((1, D_MODEL, MEM_LEN), kv_map),
                  pl.BlockSpec((1, MEM_LEN, D_MODEL), kv_map),
                  full((D_MODEL, D_MODEL)),
                  pl.BlockSpec((G, 4 * 512), lambda i: (blk(i), 0)), st_spec, w2_spec,
                  full((ns, D_MODEL)), full((ns, 6 * 512)), full((POOL_STATE, ns, POOL_WIDTH)),
                  full((1, HGRN_WIDTH)), full((D_MODEL, D_MODEL))],
        out_specs=[x_spec, st_spec, w2_spec,
                   full((ns, D_MODEL)), full((POOL_STATE, ns, POOL_WIDTH)),
                   full((ns, 2 * X_HEADS, 128))],
        out_shape=[jax.ShapeDtypeStruct((nb, L, D_MODEL), F32),
                   jax.ShapeDtypeStruct((ns, HEADS, DK, DK), F32),
                   jax.ShapeDtypeStruct((D_FF, D_MODEL), BF16),
                   jax.ShapeDtypeStruct((ns, D_MODEL), F32),
                   jax.ShapeDtypeStruct((POOL_STATE, ns, POOL_WIDTH), F32),
                   jax.ShapeDtypeStruct((ns, 2 * X_HEADS, 128), F32)],
        scratch_shapes=[pltpu.VMEM((T, D_MODEL), BF16), pltpu.VMEM((ns, HGRN_WIDTH), F32)],
        compiler_params=pltpu.CompilerParams(
            dimension_semantics=("arbitrary",), vmem_limit_bytes=VMEM_LIMIT_BYTES),
        name="attn_prompt_state_sample",
    )(x, g_cross, w_cq, kt, vb, w_co, gates, state, w_ffn_out,
      x_sample, gates, past, hgrn_norm, w_out)


def _zero_after(x):
    u = lax.bitcast_convert_type(x, jnp.uint32)
    z = lax.shift_right_logical(lax.shift_right_logical(u, jnp.uint32(16)), jnp.uint32(16))
    return lax.bitcast_convert_type(z, F32)


def _ffn_body(x, g_ref, w1_ref, w2_ref, gf_ref, fillers=()):
    hb = _rms(x, g_ref[...]).astype(BF16)

    def up(c0, c1):
        return _dot(hb, w1_ref[:, c0:c1]), _dot(hb, w1_ref[:, D_FF + c0:D_FF + c1])

    def add_to_first_tile(m, z):
        top = jnp.concatenate([m[0:8, 0:128] + z, m[0:8, 128:]], axis=1)
        return jnp.concatenate([top, m[8:, :]], axis=0)

    fillers = list(fillers)
    acts = []
    nxt = up(*FFN_COL_CHUNKS[0])
    for i, (c0, c1) in enumerate(FFN_COL_CHUNKS):
        a, bg = nxt
        if i + 1 < len(FFN_COL_CHUNKS):
            nxt = up(*FFN_COL_CHUNKS[i + 1])
        for _ in range(-(-len(fillers) // (len(FFN_COL_CHUNKS) - i))):
            bg = add_to_first_tile(bg, _zero_after(fillers.pop(0)()))
        acts.append((a * _sigmoid(a) * bg).astype(BF16))
    split = FFN_COL_CHUNKS[-1][0]
    y = x + _dot(jnp.concatenate(acts[:-1], axis=1), w2_ref[0:split, :])
    y = y + _dot(acts[-1], w2_ref[split:, :])
    return _rms(y, gf_ref[...])


def _split_heads(x):
    lead = x.shape[:-2]
    x = x.reshape(lead + (X_HEADS, 2, 128))
    x = jnp.swapaxes(x, -3, -2)
    return x.reshape(lead + (2 * X_HEADS, 128))


def _merge_heads(x):
    lead = x.shape[:-2]
    x = x.reshape(lead + (2, X_HEADS, 128))
    x = jnp.swapaxes(x, -3, -2)
    return x.reshape(lead + (X_HEADS, X_HEAD_DIM))


def _memory_attention(j, base, q_ref, k_ref, v_ref, o_ref):
    prod = k_ref[j] * q_ref[base + j][None]
    s = jnp.sum(prod + pltpu.roll(prod, X_HEADS, axis=1), axis=-1, keepdims=True)
    e = jnp.exp2(s - jnp.max(s, axis=0, keepdims=True))
    den = jnp.sum(e, axis=0)
    o = jnp.sum(e * v_ref[j], axis=0) / den
    o_ref[base + j] = o
    return o


def _ffn_attn_kernel(x_ref, g_ref, w1_ref, w2_ref, gf_ref, q_ref, k_ref, v_ref, xs_ref, wo_ref,
                     o_ref, ys_ref, att_ref):
    i = pl.program_id(0)
    last = pl.num_programs(0) - 1

    @pl.when(i < last)
    def _():
        base = i * ATTN_TOKENS
        fillers = [functools.partial(_memory_attention, j, base, q_ref, k_ref, v_ref, att_ref)
                   for j in range(ATTN_TOKENS)]
        o_ref[...] = _ffn_body(x_ref[...], g_ref, w1_ref, w2_ref, gf_ref, fillers)

    @pl.when(i == last)
    def _():
        att = jnp.concatenate([att_ref[:, half * X_HEADS + head, :]
                               for head in range(X_HEADS) for half in range(2)], axis=1)
        x2 = xs_ref[...] + _dot(att.astype(BF16), wo_ref[...])
        ys_ref[:, 0, :] = _ffn_body(x2, g_ref, w1_ref, w2_ref, gf_ref)


def _ffn_prompt_attn_sample(x, g_ffn, w1, w2, g_final, q, cache_k, cache_v, x_sample, w_co):
    n = x.shape[0]
    ns = q.shape[0]
    T = FFN_ROWS
    G = ATTN_TOKENS
    steps = n // T
    assert steps == ns // G
    full = lambda shape: pl.BlockSpec(shape, lambda i: (0,) * len(shape))
    blk = lambda i: jnp.minimum(i, steps - 1)
    kv_spec = pl.BlockSpec((G, MEM_LEN, 2 * X_HEADS, 128), lambda i: (blk(i), 0, 0, 0))
    q_spec = full((ns, 2 * X_HEADS, 128))
    y, ys = pl.pallas_call(
        _ffn_attn_kernel,
        grid=(steps + 1,),
        in_specs=[pl.BlockSpec((T, D_MODEL), lambda i: (blk(i), 0)),
                  full((1, D_MODEL)), full((D_MODEL, 2 * D_FF)), full((D_FF, D_MODEL)),
                  full((1, D_MODEL)), q_spec, kv_spec, kv_spec,
                  full((ns, D_MODEL)), full((D_MODEL, D_MODEL))],
        out_specs=[pl.BlockSpec((T, D_MODEL), lambda i: (blk(i), 0)),
                   full((ns, 1, D_MODEL))],
        out_shape=[jax.ShapeDtypeStruct((n, D_MODEL), F32),
                   jax.ShapeDtypeStruct((ns, 1, D_MODEL), F32)],
        scratch_shapes=[pltpu.VMEM((ns, 2 * X_HEADS, 128), F32)],
        compiler_params=pltpu.CompilerParams(
            dimension_semantics=("arbitrary",), vmem_limit_bytes=VMEM_LIMIT_BYTES),
        name="ffn_prompt_attn_sample",
    )(x, g_ffn, w1, w2, g_final, q, cache_k, cache_v, x_sample, w_co)
    return y, ys


def kernel(x_prompt, x_sample, mem_prompt, state_hgrn, state_pool, cache_mem_k, cache_mem_v,
           g_mix, w_in, hgrn_lb, hgrn_norm, pool_mix, pool_scale, w_out, g_mem, w_mem_kv,
           g_cross, w_cq, w_co, g_ffn, w_ffn_in, w_ffn_out, g_final):
    nb, L, _ = x_prompt.shape
    ns = x_sample.shape[0]

    g_final2 = g_final.reshape(1, D_MODEL)

    mem_k, mem_v, kt, vb, w_in_b, w_out_b, w_cq_b, w_co_b, gates, xs = _mem_kv_sample_gates(
        mem_prompt, g_mem, w_mem_kv[0], w_in[0], w_out[0], w_cq[0], w_co[0], pool_mix[0], pool_scale,
        x_sample, g_mix, hgrn_lb)
    x1, hgrn_p, pool_p, w1_b = _mixer_prompt(x_prompt, g_mix, w_in_b, hgrn_lb, hgrn_norm, w_out_b,
                                             w_ffn_in[0])
    x2, hgrn_s, w2_b, x1s, pool_s, qs = _attn_prompt_state_sample(
        x1, g_cross, w_cq_b, kt, vb, w_co_b, gates, state_hgrn[0], w_ffn_out[0],
        xs, jnp.swapaxes(state_pool[0], 0, 1), hgrn_norm, w_out_b)
    y_prompt, y_sample = _ffn_prompt_attn_sample(
        x2.reshape(nb * L, D_MODEL), g_ffn, w1_b, w2_b, g_final2, qs,
        _split_heads(cache_mem_k[0]), _split_heads(cache_mem_v[0]), x1s, w_co_b)

    return (y_prompt.reshape(nb, L, D_MODEL),
            y_sample,
            hgrn_p[None],
            jnp.swapaxes(pool_p, 0, 1)[None],
            _merge_heads(mem_k)[None],
            _merge_heads(mem_v)[None],
            hgrn_s[None],
            jnp.swapaxes(pool_s, 0, 1)[None])
```

```python
import functools

import jax
import jax.numpy as jnp
from jax import lax
from jax.experimental import pallas as pl
from jax.experimental.pallas import tpu as pltpu

F32 = jnp.float32
BF16 = jnp.bfloat16

D_MODEL = 1024
HGRN_WIDTH = 512
HEADS = 4
DK = 128
CHUNK = 64
POOL_WIDTH = 512
POOL_WINDOWS = (2, 4, 8, 16)
POOL_GROUP = 128
POOL_STATE = 15
IN_PROJ = 4 * HGRN_WIDTH + POOL_WIDTH
MEM_LEN = 256
X_HEADS = 4
X_HEAD_DIM = 256
D_FF = 2816
EPS = 1e-6
ATTN_SCALE = X_HEAD_DIM ** -0.5
LOG2_E = 1.4426950408889634
QUERY_SCALE = ATTN_SCALE * LOG2_E

VMEM_LIMIT_BYTES = 56 * 1024 * 1024

MIX_ROWS = 512
MIX_BLOCKS_PER_SEQ = 4
IN_PROJ_PIECE = 256
MIX_FILLER_SCHEDULE = (3, 0, 0) + (1, 0, 1, 0, 1, 0, 1, 0) + (2, 2)
TRI_ROWS = 256
ATTN_ROWS = 1024
SAMPLE_FFN_COLS = 1408
SPLIT_PITCH = MEM_LEN + 4
FFN_ROWS = 512
STATE_TOKENS = 8
ATTN_TOKENS = 4
FFN_PASS_WIDTHS = (512, 768, 768, 768)
FFN_COL_CHUNKS = tuple((sum(FFN_PASS_WIDTHS[:i]), sum(FFN_PASS_WIDTHS[:i + 1]))
                       for i in range(len(FFN_PASS_WIDTHS)))
assert FFN_COL_CHUNKS[-1][1] == D_FF


def _dot(a, b):
    return jnp.dot(a, b, preferred_element_type=F32)


def _dot_nt(a, b):
    return lax.dot_general(a, b, (((1,), (1,)), ((), ())), preferred_element_type=F32)


def _dot_tn(a, b):
    return lax.dot_general(a, b, (((0,), (0,)), ((), ())), preferred_element_type=F32)


def _rms(x, g):
    ms = jnp.mean(x * x, axis=-1, keepdims=True)
    return x * lax.rsqrt(ms + EPS) * g


def _sigmoid(x):
    return 1.0 / (1.0 + jnp.exp(-x))


def _lower_bound(lb_ref):
    t = lb_ref[...]
    m = jnp.max(t, axis=0, keepdims=True)
    e = jnp.exp(t - m)
    return e[0:1, :] / jnp.sum(e, axis=0, keepdims=True)


def _gates(proj_q, proj_f, lb):
    qq = proj_q * _sigmoid(proj_q)
    sig = _sigmoid(proj_f)
    fgate = lb + (1.0 - lb) * sig
    kk = (1.0 - lb) * (1.0 - sig)
    return qq, fgate, kk


def _split2(x):
    hi = x.astype(BF16)
    return hi, (x - hi.astype(F32)).astype(BF16)


def _sample_gates(x, g_ref, winb_ref, lb_ref, o_ref, x2d_ref):
    x2d_ref[...] = x
    hb = _rms(x, g_ref[...]).astype(BF16)
    proj = _dot(hb, winb_ref[...])
    lb = _lower_bound(lb_ref)
    qq, fgate, kk = _gates(proj[:, 0:512], proj[:, 512:1024], lb)
    o_ref[:, 0:512] = qq
    o_ref[:, 512:1024] = fgate
    o_ref[:, 1024:1536] = kk
    o_ref[:, 1536:2048] = proj[:, 1024:1536]
    o_ref[:, 2048:2560] = _sigmoid(proj[:, 1536:2048])
    o_ref[:, 2560:3072] = proj[:, 2048:2560]


def _memkv_kernel(mem_ref, g_ref, w_ref, win_ref, wout_ref, wcq_ref, wco_ref, pmix_ref, ps_ref,
                  xs_ref, gmix_ref, lb_ref,
                  k_ref, v_ref, kt_ref, vb_ref, winb_ref, woutb_ref, wcqb_ref, wcob_ref,
                  gates_ref, xs2d_ref,
                  wb_ref, rows_ref):
    b = pl.program_id(0)
    nb = pl.num_programs(0) - 1
    first_pool_block = HGRN_WIDTH // POOL_GROUP

    @pl.when(b == 0)
    def _():
        wb_ref[...] = w_ref[...].astype(BF16)

    @pl.when(b < nb)
    def _():
        wrows = win_ref.shape[0]
        winb_ref[pl.ds(pl.multiple_of(b * wrows, wrows), wrows), :] = win_ref[...].astype(BF16)
        wcqb_ref[...] = wcq_ref[...].astype(BF16)
        wcob_ref[...] = wco_ref[...].astype(BF16)

        @pl.when(b < first_pool_block)
        def _():
            woutb_ref[...] = wout_ref[...].astype(BF16)

        @pl.when(b >= first_pool_block)
        def _():
            gi = b - first_pool_block
            a_hi, a_lo = _split2(pmix_ref[gi] * ps_ref[gi])
            w_hi, w_lo = _split2(wout_ref[...])
            woutb_ref[...] = (_dot(a_hi, w_hi) + _dot(a_hi, w_lo) + _dot(a_lo, w_hi)).astype(BF16)

        h = _rms(mem_ref[0], g_ref[...]).astype(BF16)
        kv = _dot(h, wb_ref[...])
        k = kv[:, :D_MODEL]
        v = kv[:, D_MODEL:]
        kt_ref[0] = k.T.astype(BF16)
        vb_ref[0] = v.astype(BF16)
        for val, out_ref in ((k, k_ref), (v, v_ref)):
            for r in range(2 * X_HEADS):
                half, head = divmod(r, X_HEADS)
                c0 = head * X_HEAD_DIM + half * 128
                rows_ref[r * SPLIT_PITCH:r * SPLIT_PITCH + MEM_LEN, :] = val[:, c0:c0 + 128]
            for m in range(MEM_LEN):
                out_ref[0, m] = rows_ref[pl.ds(m, 2 * X_HEADS, stride=SPLIT_PITCH), :]

    @pl.when(b == nb)
    def _():
        _sample_gates(xs_ref[:, 0, :], gmix_ref, winb_ref, lb_ref, gates_ref, xs2d_ref)


def _mem_kv_sample_gates(mem, g_mem, w_kv, w_in, w_out, w_cq, w_co, pool_mix, pool_scale,
                         x_sample, g_mix, hgrn_lb):
    nb = mem.shape[0]
    ns = x_sample.shape[0]
    full = lambda shape: pl.BlockSpec(shape, lambda b: (0,) * len(shape))
    blk = lambda b: jnp.minimum(b, nb - 1)
    split_spec = pl.BlockSpec((1, MEM_LEN, 2 * X_HEADS, 128), lambda b: (blk(b), 0, 0, 0))
    wrows = D_MODEL // nb
    assert wrows == POOL_GROUP
    win_spec = pl.BlockSpec((wrows, IN_PROJ), lambda b: (blk(b), 0))
    wout_spec = pl.BlockSpec((wrows, D_MODEL), lambda b: (blk(b), 0))
    return pl.pallas_call(
        _memkv_kernel,
        grid=(nb + 1,),
        in_specs=[pl.BlockSpec((1, MEM_LEN, D_MODEL), lambda b: (blk(b), 0, 0)),
                  full((1, D_MODEL)), full((D_MODEL, 2 * D_MODEL)), win_spec, wout_spec,
                  wout_spec, wout_spec,
                  full((4, POOL_GROUP, POOL_GROUP)), full((4, 1, POOL_GROUP)),
                  full((ns, 1, D_MODEL)), full((1, D_MODEL)), full((2, HGRN_WIDTH))],
        out_specs=[split_spec, split_spec,
                   pl.BlockSpec((1, D_MODEL, MEM_LEN), lambda b: (blk(b), 0, 0)),
                   pl.BlockSpec((1, MEM_LEN, D_MODEL), lambda b: (blk(b), 0, 0)),
                   full((D_MODEL, IN_PROJ)), wout_spec, wout_spec, wout_spec,
                   full((ns, 6 * 512)), full((ns, D_MODEL))],
        out_shape=[jax.ShapeDtypeStruct((nb, MEM_LEN, 2 * X_HEADS, 128), F32),
                   jax.ShapeDtypeStruct((nb, MEM_LEN, 2 * X_HEADS, 128), F32),
                   jax.ShapeDtypeStruct((nb, D_MODEL, MEM_LEN), BF16),
                   jax.ShapeDtypeStruct((nb, MEM_LEN, D_MODEL), BF16),
                   jax.ShapeDtypeStruct((D_MODEL, IN_PROJ), BF16),
                   jax.ShapeDtypeStruct((D_MODEL, D_MODEL), BF16),
                   jax.ShapeDtypeStruct((D_MODEL, D_MODEL), BF16),
                   jax.ShapeDtypeStruct((D_MODEL, D_MODEL), BF16),
                   jax.ShapeDtypeStruct((ns, 6 * 512), F32),
                   jax.ShapeDtypeStruct((ns, D_MODEL), F32)],
        scratch_shapes=[pltpu.VMEM((D_MODEL, 2 * D_MODEL), BF16),
                        pltpu.VMEM((2 * X_HEADS * SPLIT_PITCH, 128), F32)],
        compiler_params=pltpu.CompilerParams(
            dimension_semantics=("arbitrary",), vmem_limit_bytes=VMEM_LIMIT_BYTES),
        name="mem_kv_sample_gates",
    )(mem, g_mem, w_kv, w_in, w_out, w_cq, w_co, pool_mix, pool_scale.reshape(4, 1, POOL_GROUP),
      x_sample, g_mix, hgrn_lb)


def _mix_block(n, x, proj_ref, out_ref, fillers, lb_ref, hn_ref, wout_ref,
               st_ref, ext_ref, qq_ref, kk_ref, b_ref, mrg_ref):
    T = MIX_ROWS
    fillers = list(fillers)
    schedule = list(MIX_FILLER_SCHEDULE)
    assert len(schedule) == 5 + T // CHUNK and sum(schedule) == len(fillers)

    def fill():
        for _ in range(schedule.pop(0)):
            fillers.pop(0)()

    l = n % MIX_BLOCKS_PER_SEQ
    first = l == 0
    for h in range(HEADS):
        st_ref[h] = jnp.where(first, 0.0, st_ref[h])
    ext_ref[0:16, :] = jnp.where(first, 0.0, ext_ref[0:16, :])

    fill()
    lb = _lower_bound(lb_ref)
    qq, fgate, kk = _gates(proj_ref[:, 0:512], proj_ref[:, 512:1024], lb)
    qq_ref[...] = qq
    kk_ref[...] = kk
    fill()

    r = lax.broadcasted_iota(jnp.int32, (TRI_ROWS, TRI_ROWS), 0)
    c = lax.broadcasted_iota(jnp.int32, (TRI_ROWS, TRI_ROWS), 1)
    tri = jnp.where((c <= r) & (c >= (r & -CHUNK)), 1.0, 0.0).astype(BF16)
    logf = jnp.log2(fgate)
    for blk in range(T // TRI_ROWS):
        rows = slice(blk * TRI_ROWS, (blk + 1) * TRI_ROWS)
        hi, lo = _split2(logf[rows])
        b_ref[rows, :] = _dot(tri, hi) + _dot(tri, lo)
    fill()

    cr = lax.broadcasted_iota(jnp.int32, (CHUNK, CHUNK), 0)
    cc = lax.broadcasted_iota(jnp.int32, (CHUNK, CHUNK), 1)
    causal = cc <= cr
    mid_row = (CHUNK - 1) // 2

    for ci in range(T // CHUNK):
        r0 = ci * CHUNK
        rows = slice(r0, r0 + CHUNK)
        scores, inters, vals = [], [], []
        for h in range(HEADS):
            cols = slice(h * DK, (h + 1) * DK)
            b = b_ref[rows, cols]
            m = b_ref[r0 + mid_row:r0 + mid_row + 1, cols]
            b_end = b_ref[r0 + CHUNK - 1:r0 + CHUNK, cols]
            e1 = jnp.exp2(b - m)
            e2 = jnp.exp2(m - b)
            q1 = qq_ref[rows, cols] * e1
            q0 = q1 * jnp.exp2(m)
            ks = kk_ref[rows, cols] * e2
            k2 = ks * jnp.exp2(b_end - m)
            v = proj_ref[rows, 1024 + h * DK:1024 + (h + 1) * DK].astype(BF16)
            st = st_ref[h]
            scores.append(_dot_nt(q1.astype(BF16), ks.astype(BF16)))
            inters.append(_dot_nt(q0.astype(BF16), st.astype(BF16)))
            st_ref[h] = st * jnp.exp2(b_end) + _dot_tn(v, k2.astype(BF16))
            vals.append(v)
        fill()
        for h in range(HEADS):
            cols = slice(h * DK, (h + 1) * DK)
            a = jnp.where(causal, scores[h], 0.0)
            o = inters[h] + _dot(a.astype(BF16), vals[h])
            o = o * lax.rsqrt(jnp.mean(o * o, axis=-1, keepdims=True) + EPS)
            o = o * hn_ref[:, cols]
            g = proj_ref[rows, 1536 + h * DK:1536 + (h + 1) * DK]
            mrg_ref[rows, cols] = (o * _sigmoid(g)).astype(BF16)

    ext_ref[16:16 + T, :] = proj_ref[:, 2048:2560]
    pos = l * T + lax.broadcasted_iota(jnp.int32, (16, POOL_GROUP), 0)
    for gi, w in enumerate(POOL_WINDOWS):
        if gi % 2 == 0:
            fill()
        cols = slice(gi * POOL_GROUP, (gi + 1) * POOL_GROUP)
        s = ext_ref[:, cols]
        u = s[16:]
        shift = 1
        while shift < w:
            s = s + pltpu.roll(s, shift, axis=0)
            shift *= 2
        acc = s[16:]
        cnt = jnp.minimum(pos + 1, w).astype(F32)
        pooled = jnp.concatenate([acc[0:16] / cnt, acc[16:] * (1.0 / w)], axis=0) - u
        mrg_ref[:, HGRN_WIDTH + gi * POOL_GROUP:HGRN_WIDTH + (gi + 1) * POOL_GROUP] = pooled.astype(BF16)

    ext_ref[0:16, :] = ext_ref[T:T + 16, :]
    out_ref[...] = x + _dot(mrg_ref[...], wout_ref[...])


def _mixer_kernel(xa_ref, xn_ref, g_ref, win_ref, lb_ref, hn_ref, wout_ref,
                  w1f_ref,
                  x1_ref, hst_ref, pst_ref, w1b_ref,
                  p0_ref, p1_ref, hb_ref, st_ref, ext_ref, qq_ref, kk_ref, b_ref, mrg_ref):
    T = MIX_ROWS
    g = pl.program_id(0)
    w1b_ref[...] = w1f_ref[...].astype(BF16)
    rest = (lb_ref, hn_ref, wout_ref, st_ref, ext_ref, qq_ref, kk_ref, b_ref, mrg_ref)

    def in_proj_pieces(x_ref, rows, p_ref):
        def prep():
            hb_ref[...] = _rms(x_ref[rows, :], g_ref[...]).astype(BF16)

        def piece(k):
            cols = slice(k * IN_PROJ_PIECE, (k + 1) * IN_PROJ_PIECE)
            p_ref[:, cols] = _dot(hb_ref[...], win_ref[:, cols])

        return [prep] + [functools.partial(piece, k) for k in range(IN_PROJ // IN_PROJ_PIECE)]

    @pl.when(g == 0)
    def _():
        st_ref[...] = jnp.zeros_like(st_ref)
        ext_ref[0:16, :] = jnp.zeros((16, POOL_WIDTH), F32)
        for f in in_proj_pieces(xa_ref, slice(0, T), p0_ref):
            f()

    _mix_block(2 * g, xa_ref[0:T, :], p0_ref, x1_ref.at[0:T, :],
               in_proj_pieces(xa_ref, slice(T, 2 * T), p1_ref), *rest)
    _mix_block(2 * g + 1, xa_ref[T:2 * T, :], p1_ref, x1_ref.at[T:2 * T, :],
               in_proj_pieces(xn_ref, slice(0, T), p0_ref), *rest)

    steps_per_seq = MIX_BLOCKS_PER_SEQ // 2

    @pl.when(g % steps_per_seq == steps_per_seq - 1)
    def _():
        for h in range(HEADS):
            hst_ref[0, h] = st_ref[h].T

    for seq in range(pst_ref.shape[1]):
        @pl.when(g == seq * steps_per_seq + steps_per_seq - 1)
        def _():
            pst_ref[:, seq, :] = ext_ref[T + 1:T + 16, :]


def _mixer_prompt(x, g_mix, w_in, hgrn_lb, hgrn_norm, w_out, w_ffn_in):
    nb, L, _ = x.shape
    T = MIX_ROWS
    assert L // T == MIX_BLOCKS_PER_SEQ and MIX_BLOCKS_PER_SEQ % 2 == 0
    n_blocks = nb * MIX_BLOCKS_PER_SEQ
    steps = n_blocks // 2
    steps_per_seq = MIX_BLOCKS_PER_SEQ // 2
    x2d = x.reshape(nb * L, D_MODEL)
    full = lambda shape: pl.BlockSpec(shape, lambda g: (0,) * len(shape))
    rows = lambda w: pl.BlockSpec((w.shape[0] // steps, w.shape[1]), lambda g: (g, 0))
    side = (w_ffn_in,)
    x1, hst, pst, w1b = pl.pallas_call(
        _mixer_kernel,
        grid=(steps,),
        in_specs=[pl.BlockSpec((2 * T, D_MODEL), lambda g: (g, 0)),
                  pl.BlockSpec((T, D_MODEL), lambda g: (jnp.minimum(2 * g + 2, n_blocks - 1), 0)),
                  full((1, D_MODEL)), full((D_MODEL, IN_PROJ)), full((2, HGRN_WIDTH)),
                  full((1, HGRN_WIDTH)), full((D_MODEL, D_MODEL))] + [rows(w) for w in side],
        out_specs=[pl.BlockSpec((2 * T, D_MODEL), lambda g: (g, 0)),
                   pl.BlockSpec((1, HEADS, DK, DK), lambda g: (g // steps_per_seq, 0, 0, 0)),
                   full((POOL_STATE, nb, POOL_WIDTH))]
                  + [rows(w) for w in side],
        out_shape=[jax.ShapeDtypeStruct((nb * L, D_MODEL), F32),
                   jax.ShapeDtypeStruct((nb, HEADS, DK, DK), F32),
                   jax.ShapeDtypeStruct((POOL_STATE, nb, POOL_WIDTH), F32)]
                  + [jax.ShapeDtypeStruct(w.shape, BF16) for w in side],
        scratch_shapes=[pltpu.VMEM((T, IN_PROJ), F32),
                        pltpu.VMEM((T, IN_PROJ), F32),
                        pltpu.VMEM((T, D_MODEL), BF16),
                        pltpu.VMEM((HEADS, DK, DK), F32),
                        pltpu.VMEM((16 + T, POOL_WIDTH), F32),
                        pltpu.VMEM((T, HGRN_WIDTH), F32),
                        pltpu.VMEM((T, HGRN_WIDTH), F32),
                        pltpu.VMEM((T, HGRN_WIDTH), F32),
                        pltpu.VMEM((T, D_MODEL), BF16)],
        compiler_params=pltpu.CompilerParams(
            dimension_semantics=("arbitrary",), vmem_limit_bytes=VMEM_LIMIT_BYTES),
        name="mixer_prompt",
    )(x2d, x2d, g_mix, w_in, hgrn_lb, hgrn_norm, w_out, *side)
    return x1.reshape(nb, L, D_MODEL), hst, pst, w1b


def _state_step(base, gates_ref, s_ref, so_ref, o_ref):
    G = STATE_TOKENS
    pad = jnp.zeros((DK - G, DK), F32)
    for h in range(HEADS):
        cols = slice(h * DK, (h + 1) * DK)
        qt = jnp.concatenate([gates_ref[:, h * DK:(h + 1) * DK], pad], axis=0).T
        ft = jnp.concatenate([gates_ref[:, 512 + h * DK:512 + (h + 1) * DK], pad], axis=0).T
        kt = jnp.concatenate([gates_ref[:, 1024 + h * DK:1024 + (h + 1) * DK], pad], axis=0).T
        readouts = []
        for j in range(G):
            v = gates_ref[j:j + 1, 1536 + h * DK:1536 + (h + 1) * DK]
            s_new = ft[:, j:j + 1] * s_ref[j, h] + kt[:, j:j + 1] * v
            so_ref[j, h] = s_new
            readouts.append(jnp.sum(qt[:, j:j + 1] * s_new, axis=0, keepdims=True))
        o_ref[pl.ds(pl.multiple_of(base, G), G), cols] = jnp.concatenate(readouts, axis=0)


def _sample_mix(x_ref, gates_ref, o_ref, past_ref, hn_ref, wout_ref, gc_ref, wq_ref,
                x1_ref, pool_ref, q_ref, mrg_ref):
    n = x_ref.shape[0]
    for h in range(HEADS):
        cols = slice(h * DK, (h + 1) * DK)
        o = o_ref[:, cols]
        o = o * lax.rsqrt(jnp.mean(o * o, axis=-1, keepdims=True) + EPS) * hn_ref[:, cols]
        mrg_ref[0:n, cols] = (o * gates_ref[:, 2048 + h * DK:2048 + (h + 1) * DK]).astype(BF16)
    for gi, w in enumerate(POOL_WINDOWS):
        cols = slice(gi * POOL_GROUP, (gi + 1) * POOL_GROUP)
        u = gates_ref[:, 2560 + gi * POOL_GROUP:2560 + (gi + 1) * POOL_GROUP]
        acc = u
        for j in range(1, w):
            acc = acc + past_ref[POOL_STATE - j, :, cols]
        pooled = acc / float(w) - u
        mrg_ref[0:n, HGRN_WIDTH + gi * POOL_GROUP:HGRN_WIDTH + (gi + 1) * POOL_GROUP] = pooled.astype(BF16)
    pool_ref[0:POOL_STATE - 1] = past_ref[1:POOL_STATE]
    pool_ref[POOL_STATE - 1] = gates_ref[:, 2560:3072]
    x1 = x_ref[...] + _dot(mrg_ref[0:n, :], wout_ref[...])
    x1_ref[...] = x1
    hb = _rms(x1, gc_ref[...]).astype(BF16)
    q = _dot(hb, wq_ref[...]) * QUERY_SCALE
    for r in range(2 * X_HEADS):
        half, head = divmod(r, X_HEADS)
        c0 = head * X_HEAD_DIM + half * 128
        q_ref[:, r, :] = q[:, c0:c0 + 128]


def _attn_state_kernel(x_ref, g_ref, wq_ref, kt_ref, vb_ref, wo_ref, gates_ref, s_ref, w2f_ref,
                       xs_ref, gall_ref, past_ref, hn_ref, wout_ref,
                       o_ref, so_ref, w2b_ref, x1s_ref, pool_ref, qs_ref,
                       att_ref, oo_ref):
    i = pl.program_id(0)
    last = pl.num_programs(0) - 1

    @pl.when(i < last)
    def _():
        w2b_ref[...] = w2f_ref[...].astype(BF16)
        _attention_block(x_ref, g_ref, kt_ref, vb_ref, wq_ref, wo_ref, o_ref, att_ref)
        _state_step(i * STATE_TOKENS, gates_ref, s_ref, so_ref, oo_ref)

    @pl.when(i == last)
    def _():
        _sample_mix(xs_ref, gall_ref, oo_ref, past_ref, hn_ref, wout_ref, g_ref, wq_ref,
                    x1s_ref, pool_ref, qs_ref, att_ref)


def _attention_block(x_ref, g_ref, kt_ref, vb_ref, wq_ref, wo_ref, o_ref, att_ref):
    H = ATTN_ROWS // 2
    halves = (slice(0, H), slice(H, 2 * H))

    def query(rows):
        hb = _rms(x_ref[0, rows, :], g_ref[...]).astype(BF16)
        return (_dot(hb, wq_ref[...]) * QUERY_SCALE).astype(BF16)

    def scores(q):
        return [_dot(q[:, h * X_HEAD_DIM:(h + 1) * X_HEAD_DIM],
                     kt_ref[0, h * X_HEAD_DIM:(h + 1) * X_HEAD_DIM, :]) for h in range(X_HEADS)]

    def values(rows, ss):
        for h, s in enumerate(ss):
            cols = slice(h * X_HEAD_DIM, (h + 1) * X_HEAD_DIM)
            e = jnp.exp2(s - jnp.max(s, axis=-1, keepdims=True))
            den = jnp.sum(e, axis=-1, keepdims=True)
            o = _dot(e.astype(BF16), vb_ref[0, :, cols]) / den
            att_ref[rows, cols] = o.astype(BF16)

    def project(rows):
        o_ref[0, rows, :] = x_ref[0, rows, :] + _dot(att_ref[rows, :], wo_ref[...])

    s0 = scores(query(halves[0]))
    q1 = query(halves[1])
    values(halves[0], s0)
    s1 = scores(q1)
    project(halves[0])
    values(halves[1], s1)
    project(halves[1])


def _attn_prompt_state_sample(x, g_cross, w_cq, kt, vb, w_co, gates, state, w_ffn_out,
                              x_sample, past, hgrn_norm, w_out):
    nb, L, _ = x.shape
    T = ATTN_ROWS
    G = STATE_TOKENS
    steps_per_seq = L // T
    steps = nb * steps_per_seq
    ns = gates.shape[0]
    assert steps == ns // G
    full = lambda shape: pl.BlockSpec(shape, lambda i: (0,) * len(shape))
    blk = lambda i: jnp.minimum(i, steps - 1)
    st_spec = pl.BlockSpec((G, HEADS, DK, DK), lambda i: (blk(i), 0, 0, 0))
    x_spec = pl.BlockSpec((1, T, D_MODEL),
                          lambda i: (blk(i) // steps_per_seq, blk(i) % steps_per_seq, 0))
    kv_map = lambda i: (blk(i) // steps_per_seq, 0, 0)
    w2_spec = pl.BlockSpec((D_FF // steps, D_MODEL), lambda i: (blk(i), 0))
    return pl.pallas_call(
        _attn_state_kernel,
        grid=(steps + 1,),
        in_specs=[x_spec, full((1, D_MODEL)), full((D_MODEL, D_MODEL)),
                  pl.BlockSpec((1, D_MODEL, MEM_LEN), kv_map),
                  pl.BlockSpec((1, MEM_LEN, D_MODEL), kv_map),
                  full((D_MODEL, D_MODEL)),
                  pl.BlockSpec((G, 4 * 512), lambda i: (blk(i), 0)), st_spec, w2_spec,
                  full((ns, D_MODEL)), full((ns, 6 * 512)), full((POOL_STATE, ns, POOL_WIDTH)),
                  full((1, HGRN_WIDTH)), full((D_MODEL, D_MODEL))],
        out_specs=[x_spec, st_spec, w2_spec,
                   full((ns, D_MODEL)), full((POOL_STATE, ns, POOL_WIDTH)),
                   full((ns, 2 * X_HEADS, 128))],
        out_shape=[jax.ShapeDtypeStruct((nb, L, D_MODEL), F32),
                   jax.ShapeDtypeStruct((ns, HEADS, DK, DK), F32),
                   jax.ShapeDtypeStruct((D_FF, D_MODEL), BF16),
                   jax.ShapeDtypeStruct((ns, D_MODEL), F32),
                   jax.ShapeDtypeStruct((POOL_STATE, ns, POOL_WIDTH), F32),
                   jax.ShapeDtypeStruct((ns, 2 * X_HEADS, 128), F32)],
        scratch_shapes=[pltpu.VMEM((T, D_MODEL), BF16), pltpu.VMEM((ns, HGRN_WIDTH), F32)],
        compiler_params=pltpu.CompilerParams(
            dimension_semantics=("arbitrary",), vmem_limit_bytes=VMEM_LIMIT_BYTES),
        name="attn_prompt_state_sample",
    )(x, g_cross, w_cq, kt, vb, w_co, gates, state, w_ffn_out,
      x_sample, gates, past, hgrn_norm, w_out)


def _zero_after(x):
    u = lax.bitcast_convert_type(x, jnp.uint32)
    z = lax.shift_right_logical(lax.shift_right_logical(u, jnp.uint32(16)), jnp.uint32(16))
    return lax.bitcast_convert_type(z, F32)


def _ffn_body(x, g_ref, w1_ref, w2_ref, gf_ref, fillers=()):
    hb = _rms(x, g_ref[...]).astype(BF16)

    def up(c0, c1):
        return _dot(hb, w1_ref[:, c0:c1]), _dot(hb, w1_ref[:, D_FF + c0:D_FF + c1])

    def add_to_first_tile(m, z):
        top = jnp.concatenate([m[0:8, 0:128] + z, m[0:8, 128:]], axis=1)
        return jnp.concatenate([top, m[8:, :]], axis=0)

    fillers = list(fillers)
    acts = []
    done = jnp.zeros((8, 128), F32)
    nxt = up(*FFN_COL_CHUNKS[0])
    for i, (c0, c1) in enumerate(FFN_COL_CHUNKS):
        a, bg = nxt
        if i + 1 < len(FFN_COL_CHUNKS):
            nxt = up(*FFN_COL_CHUNKS[i + 1])
        for _ in range(-(-len(fillers) // (len(FFN_COL_CHUNKS) - i))):
            done = _zero_after(fillers.pop(0)(done))
            bg = add_to_first_tile(bg, done)
        acts.append((a * _sigmoid(a) * bg).astype(BF16))
    split = FFN_COL_CHUNKS[-1][0]
    y = x + _dot(jnp.concatenate(acts[:-1], axis=1), w2_ref[0:split, :])
    y = y + _dot(acts[-1], w2_ref[split:, :])
    return _rms(y, gf_ref[...])


def _split_heads(x):
    lead = x.shape[:-2]
    x = x.reshape(lead + (X_HEADS, 2, 128))
    x = jnp.swapaxes(x, -3, -2)
    return x.reshape(lead + (2 * X_HEADS, 128))


def _merge_heads(x):
    lead = x.shape[:-2]
    x = x.reshape(lead + (2, X_HEADS, 128))
    x = jnp.swapaxes(x, -3, -2)
    return x.reshape(lead + (X_HEADS, X_HEAD_DIM))


def _memory_attention(j, base, q_ref, k_ref, v_ref, o_ref, start):
    prod = k_ref[j] * (q_ref[base + j] + start)[None]
    s = jnp.sum(prod + pltpu.roll(prod, X_HEADS, axis=1), axis=-1, keepdims=True)
    e = jnp.exp2(s - jnp.max(s, axis=0, keepdims=True))
    den = jnp.sum(e, axis=0)
    o = jnp.sum(e * v_ref[j], axis=0) / den
    o_ref[base + j] = o
    return o


def _ffn_attn_kernel(x_ref, g_ref, w1_ref, w2_ref, gf_ref, q_ref, k_ref, v_ref, xs_ref, wo_ref,
                     o_ref, ys_ref, att_ref):
    i = pl.program_id(0)
    last = pl.num_programs(0) - 1

    @pl.when(i < last)
    def _():
        base = i * ATTN_TOKENS
        fillers = [functools.partial(_memory_attention, j, base, q_ref, k_ref, v_ref, att_ref)
                   for j in range(ATTN_TOKENS)]
        o_ref[...] = _ffn_body(x_ref[...], g_ref, w1_ref, w2_ref, gf_ref, fillers)

    @pl.when(i == last)
    def _():
        att = jnp.concatenate([att_ref[:, half * X_HEADS + head, :]
                               for head in range(X_HEADS) for half in range(2)], axis=1)
        x2 = xs_ref[...] + _dot(att.astype(BF16), wo_ref[...])
        ys_ref[:, 0, :] = _ffn_body(x2, g_ref, w1_ref, w2_ref, gf_ref)


def _ffn_prompt_attn_sample(x, g_ffn, w1, w2, g_final, q, cache_k, cache_v, x_sample, w_co):
    n = x.shape[0]
    ns = q.shape[0]
    T = FFN_ROWS
    G = ATTN_TOKENS
    steps = n // T
    assert steps == ns // G
    full = lambda shape: pl.BlockSpec(shape, lambda i: (0,) * len(shape))
    blk = lambda i: jnp.minimum(i, steps - 1)
    kv_spec = pl.BlockSpec((G, MEM_LEN, 2 * X_HEADS, 128), lambda i: (blk(i), 0, 0, 0))
    q_spec = full((ns, 2 * X_HEADS, 128))
    y, ys = pl.pallas_call(
        _ffn_attn_kernel,
        grid=(steps + 1,),
        in_specs=[pl.BlockSpec((T, D_MODEL), lambda i: (blk(i), 0)),
                  full((1, D_MODEL)), full((D_MODEL, 2 * D_FF)), full((D_FF, D_MODEL)),
                  full((1, D_MODEL)), q_spec, kv_spec, kv_spec,
                  full((ns, D_MODEL)), full((D_MODEL, D_MODEL))],
        out_specs=[pl.BlockSpec((T, D_MODEL), lambda i: (blk(i), 0)),
                   full((ns, 1, D_MODEL))],
        out_shape=[jax.ShapeDtypeStruct((n, D_MODEL), F32),
                   jax.ShapeDtypeStruct((ns, 1, D_MODEL), F32)],
        scratch_shapes=[pltpu.VMEM((ns, 2 * X_HEADS, 128), F32)],
        compiler_params=pltpu.CompilerParams(
            dimension_semantics=("arbitrary",), vmem_limit_bytes=VMEM_LIMIT_BYTES),
        name="ffn_prompt_attn_sample",
    )(x, g_ffn, w1, w2, g_final, q, cache_k, cache_v, x_sample, w_co)
    return y, ys


def kernel(x_prompt, x_sample, mem_prompt, state_hgrn, state_pool, cache_mem_k, cache_mem_v,
           g_mix, w_in, hgrn_lb, hgrn_norm, pool_mix, pool_scale, w_out, g_mem, w_mem_kv,
           g_cross, w_cq, w_co, g_ffn, w_ffn_in, w_ffn_out, g_final):
    nb, L, _ = x_prompt.shape
    ns = x_sample.shape[0]

    g_final2 = g_final.reshape(1, D_MODEL)

    mem_k, mem_v, kt, vb, w_in_b, w_out_b, w_cq_b, w_co_b, gates, xs = _mem_kv_sample_gates(
        mem_prompt, g_mem, w_mem_kv[0], w_in[0], w_out[0], w_cq[0], w_co[0], pool_mix[0], pool_scale,
        x_sample, g_mix, hgrn_lb)
    x1, hgrn_p, pool_p, w1_b = _mixer_prompt(x_prompt, g_mix, w_in_b, hgrn_lb, hgrn_norm, w_out_b,
                                             w_ffn_in[0])
    x2, hgrn_s, w2_b, x1s, pool_s, qs = _attn_prompt_state_sample(
        x1, g_cross, w_cq_b, kt, vb, w_co_b, gates, state_hgrn[0], w_ffn_out[0],
        xs, jnp.swapaxes(state_pool[0], 0, 1), hgrn_norm, w_out_b)
    y_prompt, y_sample = _ffn_prompt_attn_sample(
        x2.reshape(nb * L, D_MODEL), g_ffn, w1_b, w2_b, g_final2, qs,
        _split_heads(cache_mem_k[0]), _split_heads(cache_mem_v[0]), x1s, w_co_b)

    return (y_prompt.reshape(nb, L, D_MODEL),
            y_sample,
            hgrn_p[None],
            jnp.swapaxes(pool_p, 0, 1)[None],
            _merge_heads(mem_k)[None],
            _merge_heads(mem_v)[None],
            hgrn_s[None],
            jnp.swapaxes(pool_s, 0, 1)[None])
```

```python
import functools

import jax
import jax.numpy as jnp
from jax import lax
from jax.experimental import pallas as pl
from jax.experimental.pallas import tpu as pltpu

F32 = jnp.float32
BF16 = jnp.bfloat16

D_MODEL = 1024
HGRN_WIDTH = 512
HEADS = 4
DK = 128
CHUNK = 64
POOL_WIDTH = 512
POOL_WINDOWS = (2, 4, 8, 16)
POOL_GROUP = 128
POOL_STATE = 15
IN_PROJ = 4 * HGRN_WIDTH + POOL_WIDTH
MEM_LEN = 256
X_HEADS = 4
X_HEAD_DIM = 256
D_FF = 2816
EPS = 1e-6
ATTN_SCALE = X_HEAD_DIM ** -0.5
LOG2_E = 1.4426950408889634
QUERY_SCALE = ATTN_SCALE * LOG2_E

VMEM_LIMIT_BYTES = 56 * 1024 * 1024

MIX_ROWS = 512
MIX_BLOCKS_PER_SEQ = 4
IN_PROJ_PIECE = 256
MIX_FILLER_SCHEDULE = (3, 0, 0) + (1, 0, 1, 0, 1, 0, 1, 0) + (2, 2)
TRI_ROWS = 256
ATTN_ROWS = 1024
SPLIT_PITCH = MEM_LEN + 4
FFN_ROWS = 512
STATE_TOKENS = 8
ATTN_TOKENS = 4
FFN_PASS_WIDTHS = (768, 768, 768, 512)
FFN_COL_CHUNKS = tuple((sum(FFN_PASS_WIDTHS[:i]), sum(FFN_PASS_WIDTHS[:i + 1]))
                       for i in range(len(FFN_PASS_WIDTHS)))
assert FFN_COL_CHUNKS[-1][1] == D_FF


def _dot(a, b):
    return jnp.dot(a, b, preferred_element_type=F32)


def _dot_nt(a, b):
    return lax.dot_general(a, b, (((1,), (1,)), ((), ())), preferred_element_type=F32)


def _dot_tn(a, b):
    return lax.dot_general(a, b, (((0,), (0,)), ((), ())), preferred_element_type=F32)


def _rms(x, g):
    ms = jnp.mean(x * x, axis=-1, keepdims=True)
    return x * lax.rsqrt(ms + EPS) * g


def _sigmoid(x):
    return 1.0 / (1.0 + jnp.exp(-x))


def _lower_bound(lb_ref):
    t = lb_ref[...]
    m = jnp.max(t, axis=0, keepdims=True)
    e = jnp.exp(t - m)
    return e[0:1, :] / jnp.sum(e, axis=0, keepdims=True)


def _gates(proj_q, proj_f, lb):
    qq = proj_q * _sigmoid(proj_q)
    sig = _sigmoid(proj_f)
    fgate = lb + (1.0 - lb) * sig
    kk = (1.0 - lb) * (1.0 - sig)
    return qq, fgate, kk


def _split2(x):
    hi = x.astype(BF16)
    return hi, (x - hi.astype(F32)).astype(BF16)


def _sample_gates(x, g_ref, winb_ref, lb_ref, o_ref, x2d_ref):
    x2d_ref[...] = x
    hb = _rms(x, g_ref[...]).astype(BF16)
    proj = _dot(hb, winb_ref[...])
    lb = _lower_bound(lb_ref)
    qq, fgate, kk = _gates(proj[:, 0:512], proj[:, 512:1024], lb)
    o_ref[:, 0:512] = qq
    o_ref[:, 512:1024] = fgate
    o_ref[:, 1024:1536] = kk
    o_ref[:, 1536:2048] = proj[:, 1024:1536]
    o_ref[:, 2048:2560] = _sigmoid(proj[:, 1536:2048])
    o_ref[:, 2560:3072] = proj[:, 2048:2560]


def _memkv_kernel(mem_ref, g_ref, w_ref, win_ref, wout_ref, wcq_ref, wco_ref, pmix_ref, ps_ref,
                  xs_ref, gmix_ref, lb_ref,
                  k_ref, v_ref, kt_ref, vb_ref, winb_ref, woutb_ref, wcqb_ref, wcob_ref,
                  gates_ref, xs2d_ref,
                  wb_ref, rows_ref):
    b = pl.program_id(0)
    nb = pl.num_programs(0) - 1
    first_pool_block = HGRN_WIDTH // POOL_GROUP

    @pl.when(b == 0)
    def _():
        wb_ref[...] = w_ref[...].astype(BF16)

    @pl.when(b < nb)
    def _():
        wrows = win_ref.shape[0]
        winb_ref[pl.ds(pl.multiple_of(b * wrows, wrows), wrows), :] = win_ref[...].astype(BF16)
        wcqb_ref[...] = wcq_ref[...].astype(BF16)
        wcob_ref[...] = wco_ref[...].astype(BF16)

        @pl.when(b < first_pool_block)
        def _():
            woutb_ref[...] = wout_ref[...].astype(BF16)

        @pl.when(b >= first_pool_block)
        def _():
            gi = b - first_pool_block
            a_hi, a_lo = _split2(pmix_ref[gi] * ps_ref[gi])
            w_hi, w_lo = _split2(wout_ref[...])
            woutb_ref[...] = (_dot(a_hi, w_hi) + _dot(a_hi, w_lo) + _dot(a_lo, w_hi)).astype(BF16)

        h = _rms(mem_ref[0], g_ref[...]).astype(BF16)
        kv = _dot(h, wb_ref[...])
        k = kv[:, :D_MODEL]
        v = kv[:, D_MODEL:]
        kt_ref[0] = k.T.astype(BF16)
        vb_ref[0] = v.astype(BF16)
        for val, out_ref in ((k, k_ref), (v, v_ref)):
            for r in range(2 * X_HEADS):
                half, head = divmod(r, X_HEADS)
                c0 = head * X_HEAD_DIM + half * 128
                rows_ref[r * SPLIT_PITCH:r * SPLIT_PITCH + MEM_LEN, :] = val[:, c0:c0 + 128]
            for m in range(MEM_LEN):
                out_ref[0, m] = rows_ref[pl.ds(m, 2 * X_HEADS, stride=SPLIT_PITCH), :]

    @pl.when(b == nb)
    def _():
        _sample_gates(xs_ref[:, 0, :], gmix_ref, winb_ref, lb_ref, gates_ref, xs2d_ref)


def _mem_kv_sample_gates(mem, g_mem, w_kv, w_in, w_out, w_cq, w_co, pool_mix, pool_scale,
                         x_sample, g_mix, hgrn_lb):
    nb = mem.shape[0]
    ns = x_sample.shape[0]
    full = lambda shape: pl.BlockSpec(shape, lambda b: (0,) * len(shape))
    blk = lambda b: jnp.minimum(b, nb - 1)
    split_spec = pl.BlockSpec((1, MEM_LEN, 2 * X_HEADS, 128), lambda b: (blk(b), 0, 0, 0))
    wrows = D_MODEL // nb
    assert wrows == POOL_GROUP
    win_spec = pl.BlockSpec((wrows, IN_PROJ), lambda b: (blk(b), 0))
    wout_spec = pl.BlockSpec((wrows, D_MODEL), lambda b: (blk(b), 0))
    return pl.pallas_call(
        _memkv_kernel,
        grid=(nb + 1,),
        in_specs=[pl.BlockSpec((1, MEM_LEN, D_MODEL), lambda b: (blk(b), 0, 0)),
                  full((1, D_MODEL)), full((D_MODEL, 2 * D_MODEL)), win_spec, wout_spec,
                  wout_spec, wout_spec,
                  full((4, POOL_GROUP, POOL_GROUP)), full((4, 1, POOL_GROUP)),
                  full((ns, 1, D_MODEL)), full((1, D_MODEL)), full((2, HGRN_WIDTH))],
        out_specs=[split_spec, split_spec,
                   pl.BlockSpec((1, D_MODEL, MEM_LEN), lambda b: (blk(b), 0, 0)),
                   pl.BlockSpec((1, MEM_LEN, D_MODEL), lambda b: (blk(b), 0, 0)),
                   full((D_MODEL, IN_PROJ)), wout_spec, wout_spec, wout_spec,
                   full((ns, 6 * 512)), full((ns, D_MODEL))],
        out_shape=[jax.ShapeDtypeStruct((nb, MEM_LEN, 2 * X_HEADS, 128), F32),
                   jax.ShapeDtypeStruct((nb, MEM_LEN, 2 * X_HEADS, 128), F32),
                   jax.ShapeDtypeStruct((nb, D_MODEL, MEM_LEN), BF16),
                   jax.ShapeDtypeStruct((nb, MEM_LEN, D_MODEL), BF16),
                   jax.ShapeDtypeStruct((D_MODEL, IN_PROJ), BF16),
                   jax.ShapeDtypeStruct((D_MODEL, D_MODEL), BF16),
                   jax.ShapeDtypeStruct((D_MODEL, D_MODEL), BF16),
                   jax.ShapeDtypeStruct((D_MODEL, D_MODEL), BF16),
                   jax.ShapeDtypeStruct((ns, 6 * 512), F32),
                   jax.ShapeDtypeStruct((ns, D_MODEL), F32)],
        scratch_shapes=[pltpu.VMEM((D_MODEL, 2 * D_MODEL), BF16),
                        pltpu.VMEM((2 * X_HEADS * SPLIT_PITCH, 128), F32)],
        compiler_params=pltpu.CompilerParams(
            dimension_semantics=("arbitrary",), vmem_limit_bytes=VMEM_LIMIT_BYTES),
        name="mem_kv_sample_gates",
    )(mem, g_mem, w_kv, w_in, w_out, w_cq, w_co, pool_mix, pool_scale.reshape(4, 1, POOL_GROUP),
      x_sample, g_mix, hgrn_lb)


def _mix_block(n, x, proj_ref, out_ref, fillers, lb_ref, hn_ref, wout_ref,
               st_ref, ext_ref, qq_ref, kk_ref, b_ref, mrg_ref):
    T = MIX_ROWS
    fillers = list(fillers)
    schedule = list(MIX_FILLER_SCHEDULE)
    assert len(schedule) == 5 + T // CHUNK and sum(schedule) == len(fillers)

    def fill():
        for _ in range(schedule.pop(0)):
            fillers.pop(0)()

    l = n % MIX_BLOCKS_PER_SEQ
    first = l == 0
    for h in range(HEADS):
        st_ref[h] = jnp.where(first, 0.0, st_ref[h])
    ext_ref[0:16, :] = jnp.where(first, 0.0, ext_ref[0:16, :])

    fill()
    lb = _lower_bound(lb_ref)
    qq, fgate, kk = _gates(proj_ref[:, 0:512], proj_ref[:, 512:1024], lb)
    qq_ref[...] = qq
    kk_ref[...] = kk
    fill()

    r = lax.broadcasted_iota(jnp.int32, (TRI_ROWS, TRI_ROWS), 0)
    c = lax.broadcasted_iota(jnp.int32, (TRI_ROWS, TRI_ROWS), 1)
    tri = jnp.where((c <= r) & (c >= (r & -CHUNK)), 1.0, 0.0).astype(BF16)
    logf = jnp.log2(fgate)
    for blk in range(T // TRI_ROWS):
        rows = slice(blk * TRI_ROWS, (blk + 1) * TRI_ROWS)
        hi, lo = _split2(logf[rows])
        b_ref[rows, :] = _dot(tri, hi) + _dot(tri, lo)
    fill()

    cr = lax.broadcasted_iota(jnp.int32, (CHUNK, CHUNK), 0)
    cc = lax.broadcasted_iota(jnp.int32, (CHUNK, CHUNK), 1)
    causal = cc <= cr
    mid_row = (CHUNK - 1) // 2

    for ci in range(T // CHUNK):
        r0 = ci * CHUNK
        rows = slice(r0, r0 + CHUNK)
        scores, inters, vals = [], [], []
        for h in range(HEADS):
            cols = slice(h * DK, (h + 1) * DK)
            b = b_ref[rows, cols]
            m = b_ref[r0 + mid_row:r0 + mid_row + 1, cols]
            b_end = b_ref[r0 + CHUNK - 1:r0 + CHUNK, cols]
            e1 = jnp.exp2(b - m)
            e2 = jnp.exp2(m - b)
            q1 = qq_ref[rows, cols] * e1
            q0 = q1 * jnp.exp2(m)
            ks = kk_ref[rows, cols] * e2
            k2 = ks * jnp.exp2(b_end - m)
            v = proj_ref[rows, 1024 + h * DK:1024 + (h + 1) * DK].astype(BF16)
            st = st_ref[h]
            scores.append(_dot_nt(q1.astype(BF16), ks.astype(BF16)))
            inters.append(_dot_nt(q0.astype(BF16), st.astype(BF16)))
            st_ref[h] = st * jnp.exp2(b_end) + _dot_tn(v, k2.astype(BF16))
            vals.append(v)
        fill()
        for h in range(HEADS):
            cols = slice(h * DK, (h + 1) * DK)
            a = jnp.where(causal, scores[h], 0.0)
            o = inters[h] + _dot(a.astype(BF16), vals[h])
            o = o * lax.rsqrt(jnp.mean(o * o, axis=-1, keepdims=True) + EPS)
            o = o * hn_ref[:, cols]
            g = proj_ref[rows, 1536 + h * DK:1536 + (h + 1) * DK]
            mrg_ref[rows, cols] = (o * _sigmoid(g)).astype(BF16)

    ext_ref[16:16 + T, :] = proj_ref[:, 2048:2560]
    pos = l * T + lax.broadcasted_iota(jnp.int32, (16, POOL_GROUP), 0)
    for gi, w in enumerate(POOL_WINDOWS):
        if gi % 2 == 0:
            fill()
        cols = slice(gi * POOL_GROUP, (gi + 1) * POOL_GROUP)
        s = ext_ref[:, cols]
        u = s[16:]
        shift = 1
        while shift < w:
            s = s + pltpu.roll(s, shift, axis=0)
            shift *= 2
        acc = s[16:]
        cnt = jnp.minimum(pos + 1, w).astype(F32)
        pooled = jnp.concatenate([acc[0:16] / cnt, acc[16:] * (1.0 / w)], axis=0) - u
        mrg_ref[:, HGRN_WIDTH + gi * POOL_GROUP:HGRN_WIDTH + (gi + 1) * POOL_GROUP] = pooled.astype(BF16)

    ext_ref[0:16, :] = ext_ref[T:T + 16, :]
    out_ref[...] = x + _dot(mrg_ref[...], wout_ref[...])


def _mixer_kernel(xa_ref, xn_ref, g_ref, win_ref, lb_ref, hn_ref, wout_ref,
                  w1f_ref,
                  x1_ref, hst_ref, pst_ref, w1b_ref,
                  p0_ref, p1_ref, hb_ref, st_ref, ext_ref, qq_ref, kk_ref, b_ref, mrg_ref):
    T = MIX_ROWS
    g = pl.program_id(0)
    w1b_ref[...] = w1f_ref[...].astype(BF16)
    rest = (lb_ref, hn_ref, wout_ref, st_ref, ext_ref, qq_ref, kk_ref, b_ref, mrg_ref)

    def in_proj_pieces(x_ref, rows, p_ref):
        def prep():
            hb_ref[...] = _rms(x_ref[rows, :], g_ref[...]).astype(BF16)

        def piece(k):
            cols = slice(k * IN_PROJ_PIECE, (k + 1) * IN_PROJ_PIECE)
            p_ref[:, cols] = _dot(hb_ref[...], win_ref[:, cols])

        return [prep] + [functools.partial(piece, k) for k in range(IN_PROJ // IN_PROJ_PIECE)]

    @pl.when(g == 0)
    def _():
        st_ref[...] = jnp.zeros_like(st_ref)
        ext_ref[0:16, :] = jnp.zeros((16, POOL_WIDTH), F32)
        for f in in_proj_pieces(xa_ref, slice(0, T), p0_ref):
            f()

    _mix_block(2 * g, xa_ref[0:T, :], p0_ref, x1_ref.at[0:T, :],
               in_proj_pieces(xa_ref, slice(T, 2 * T), p1_ref), *rest)
    _mix_block(2 * g + 1, xa_ref[T:2 * T, :], p1_ref, x1_ref.at[T:2 * T, :],
               in_proj_pieces(xn_ref, slice(0, T), p0_ref), *rest)

    steps_per_seq = MIX_BLOCKS_PER_SEQ // 2

    @pl.when(g % steps_per_seq == steps_per_seq - 1)
    def _():
        for h in range(HEADS):
            hst_ref[0, h] = st_ref[h].T

    for seq in range(pst_ref.shape[1]):
        @pl.when(g == seq * steps_per_seq + steps_per_seq - 1)
        def _():
            pst_ref[:, seq, :] = ext_ref[T + 1:T + 16, :]


def _mixer_prompt(x, g_mix, w_in, hgrn_lb, hgrn_norm, w_out, w_ffn_in):
    nb, L, _ = x.shape
    T = MIX_ROWS
    assert L // T == MIX_BLOCKS_PER_SEQ and MIX_BLOCKS_PER_SEQ % 2 == 0
    n_blocks = nb * MIX_BLOCKS_PER_SEQ
    steps = n_blocks // 2
    steps_per_seq = MIX_BLOCKS_PER_SEQ // 2
    x2d = x.reshape(nb * L, D_MODEL)
    full = lambda shape: pl.BlockSpec(shape, lambda g: (0,) * len(shape))
    rows = lambda w: pl.BlockSpec((w.shape[0] // steps, w.shape[1]), lambda g: (g, 0))
    side = (w_ffn_in,)
    x1, hst, pst, w1b = pl.pallas_call(
        _mixer_kernel,
        grid=(steps,),
        in_specs=[pl.BlockSpec((2 * T, D_MODEL), lambda g: (g, 0)),
                  pl.BlockSpec((T, D_MODEL), lambda g: (jnp.minimum(2 * g + 2, n_blocks - 1), 0)),
                  full((1, D_MODEL)), full((D_MODEL, IN_PROJ)), full((2, HGRN_WIDTH)),
                  full((1, HGRN_WIDTH)), full((D_MODEL, D_MODEL))] + [rows(w) for w in side],
        out_specs=[pl.BlockSpec((2 * T, D_MODEL), lambda g: (g, 0)),
                   pl.BlockSpec((1, HEADS, DK, DK), lambda g: (g // steps_per_seq, 0, 0, 0)),
                   full((POOL_STATE, nb, POOL_WIDTH))]
                  + [rows(w) for w in side],
        out_shape=[jax.ShapeDtypeStruct((nb * L, D_MODEL), F32),
                   jax.ShapeDtypeStruct((nb, HEADS, DK, DK), F32),
                   jax.ShapeDtypeStruct((POOL_STATE, nb, POOL_WIDTH), F32)]
                  + [jax.ShapeDtypeStruct(w.shape, BF16) for w in side],
        scratch_shapes=[pltpu.VMEM((T, IN_PROJ), F32),
                        pltpu.VMEM((T, IN_PROJ), F32),
                        pltpu.VMEM((T, D_MODEL), BF16),
                        pltpu.VMEM((HEADS, DK, DK), F32),
                        pltpu.VMEM((16 + T, POOL_WIDTH), F32),
                        pltpu.VMEM((T, HGRN_WIDTH), F32),
                        pltpu.VMEM((T, HGRN_WIDTH), F32),
                        pltpu.VMEM((T, HGRN_WIDTH), F32),
                        pltpu.VMEM((T, D_MODEL), BF16)],
        compiler_params=pltpu.CompilerParams(
            dimension_semantics=("arbitrary",), vmem_limit_bytes=VMEM_LIMIT_BYTES),
        name="mixer_prompt",
    )(x2d, x2d, g_mix, w_in, hgrn_lb, hgrn_norm, w_out, *side)
    return x1.reshape(nb, L, D_MODEL), hst, pst, w1b


def _state_step(base, gates_ref, s_ref, so_ref, o_ref):
    G = STATE_TOKENS
    pad = jnp.zeros((DK - G, DK), F32)
    for h in range(HEADS):
        cols = slice(h * DK, (h + 1) * DK)
        qt = jnp.concatenate([gates_ref[:, h * DK:(h + 1) * DK], pad], axis=0).T
        ft = jnp.concatenate([gates_ref[:, 512 + h * DK:512 + (h + 1) * DK], pad], axis=0).T
        kt = jnp.concatenate([gates_ref[:, 1024 + h * DK:1024 + (h + 1) * DK], pad], axis=0).T
        readouts = []
        for j in range(G):
            v = gates_ref[j:j + 1, 1536 + h * DK:1536 + (h + 1) * DK]
            s_new = ft[:, j:j + 1] * s_ref[j, h] + kt[:, j:j + 1] * v
            so_ref[j, h] = s_new
            readouts.append(jnp.sum(qt[:, j:j + 1] * s_new, axis=0, keepdims=True))
        o_ref[pl.ds(pl.multiple_of(base, G), G), cols] = jnp.concatenate(readouts, axis=0)


def _sample_mix(x_ref, gates_ref, o_ref, past_ref, hn_ref, wout_ref, gc_ref, wq_ref,
                x1_ref, pool_ref, q_ref, mrg_ref):
    n = x_ref.shape[0]
    for h in range(HEADS):
        cols = slice(h * DK, (h + 1) * DK)
        o = o_ref[:, cols]
        o = o * lax.rsqrt(jnp.mean(o * o, axis=-1, keepdims=True) + EPS) * hn_ref[:, cols]
        mrg_ref[0:n, cols] = (o * gates_ref[:, 2048 + h * DK:2048 + (h + 1) * DK]).astype(BF16)
    for gi, w in enumerate(POOL_WINDOWS):
        cols = slice(gi * POOL_GROUP, (gi + 1) * POOL_GROUP)
        u = gates_ref[:, 2560 + gi * POOL_GROUP:2560 + (gi + 1) * POOL_GROUP]
        acc = u
        for j in range(1, w):
            acc = acc + past_ref[POOL_STATE - j, :, cols]
        pooled = acc / float(w) - u
        mrg_ref[0:n, HGRN_WIDTH + gi * POOL_GROUP:HGRN_WIDTH + (gi + 1) * POOL_GROUP] = pooled.astype(BF16)
    pool_ref[0:POOL_STATE - 1] = past_ref[1:POOL_STATE]
    pool_ref[POOL_STATE - 1] = gates_ref[:, 2560:3072]
    x1 = x_ref[...] + _dot(mrg_ref[0:n, :], wout_ref[...])
    x1_ref[...] = x1
    hb = _rms(x1, gc_ref[...]).astype(BF16)
    q = _dot(hb, wq_ref[...]) * QUERY_SCALE
    for r in range(2 * X_HEADS):
        half, head = divmod(r, X_HEADS)
        c0 = head * X_HEAD_DIM + half * 128
        q_ref[:, r, :] = q[:, c0:c0 + 128]


def _attn_state_kernel(x_ref, g_ref, wq_ref, kt_ref, vb_ref, wo_ref, gates_ref, s_ref, w2f_ref,
                       xs_ref, gall_ref, past_ref, hn_ref, wout_ref,
                       o_ref, so_ref, w2b_ref, x1s_ref, pool_ref, qs_ref,
                       att_ref, oo_ref):
    i = pl.program_id(0)
    last = pl.num_programs(0) - 1

    @pl.when(i < last)
    def _():
        w2b_ref[...] = w2f_ref[...].astype(BF16)
        _attention_block(x_ref, g_ref, kt_ref, vb_ref, wq_ref, wo_ref, o_ref, att_ref)
        _state_step(i * STATE_TOKENS, gates_ref, s_ref, so_ref, oo_ref)

    @pl.when(i == last)
    def _():
        _sample_mix(xs_ref, gall_ref, oo_ref, past_ref, hn_ref, wout_ref, g_ref, wq_ref,
                    x1s_ref, pool_ref, qs_ref, att_ref)


def _attention_block(x_ref, g_ref, kt_ref, vb_ref, wq_ref, wo_ref, o_ref, att_ref):
    H = ATTN_ROWS // 2
    halves = (slice(0, H), slice(H, 2 * H))

    def query(rows):
        hb = _rms(x_ref[0, rows, :], g_ref[...]).astype(BF16)
        return (_dot(hb, wq_ref[...]) * QUERY_SCALE).astype(BF16)

    def scores(q):
        return [_dot(q[:, h * X_HEAD_DIM:(h + 1) * X_HEAD_DIM],
                     kt_ref[0, h * X_HEAD_DIM:(h + 1) * X_HEAD_DIM, :]) for h in range(X_HEADS)]

    def values(rows, ss):
        for h, s in enumerate(ss):
            cols = slice(h * X_HEAD_DIM, (h + 1) * X_HEAD_DIM)
            e = jnp.exp2(s - jnp.max(s, axis=-1, keepdims=True))
            den = jnp.sum(e, axis=-1, keepdims=True)
            o = _dot(e.astype(BF16), vb_ref[0, :, cols]) / den
            att_ref[rows, cols] = o.astype(BF16)

    def project(rows):
        o_ref[0, rows, :] = x_ref[0, rows, :] + _dot(att_ref[rows, :], wo_ref[...])

    s0 = scores(query(halves[0]))
    q1 = query(halves[1])
    values(halves[0], s0)
    s1 = scores(q1)
    project(halves[0])
    values(halves[1], s1)
    project(halves[1])


def _attn_prompt_state_sample(x, g_cross, w_cq, kt, vb, w_co, gates, state, w_ffn_out,
                              x_sample, past, hgrn_norm, w_out):
    nb, L, _ = x.shape
    T = ATTN_ROWS
    G = STATE_TOKENS
    steps_per_seq = L // T
    steps = nb * steps_per_seq
    ns = gates.shape[0]
    assert steps == ns // G
    full = lambda shape: pl.BlockSpec(shape, lambda i: (0,) * len(shape))
    blk = lambda i: jnp.minimum(i, steps - 1)
    st_spec = pl.BlockSpec((G, HEADS, DK, DK), lambda i: (blk(i), 0, 0, 0))
    x_spec = pl.BlockSpec((1, T, D_MODEL),
                          lambda i: (blk(i) // steps_per_seq, blk(i) % steps_per_seq, 0))
    kv_map = lambda i: (blk(i) // steps_per_seq, 0, 0)
    w2_spec = pl.BlockSpec((D_FF // steps, D_MODEL), lambda i: (blk(i), 0))
    return pl.pallas_call(
        _attn_state_kernel,
        grid=(steps + 1,),
        in_specs=[x_spec, full((1, D_MODEL)), full((D_MODEL, D_MODEL)),
                  pl.BlockSpec((1, D_MODEL, MEM_LEN), kv_map),
                  pl.BlockSpec((1, MEM_LEN, D_MODEL), kv_map),
                  full((D_MODEL, D_MODEL)),
                  pl.BlockSpec((G, 4 * 512), lambda i: (blk(i), 0)), st_spec, w2_spec,
                  full((ns, D_MODEL)), full((ns, 6 * 512)), full((POOL_STATE, ns, POOL_WIDTH)),
                  full((1, HGRN_WIDTH)), full((D_MODEL, D_MODEL))],
        out_specs=[x_spec, st_spec, w2_spec,
                   full((ns, D_MODEL)), full((POOL_STATE, ns, POOL_WIDTH)),
                   full((ns, 2 * X_HEADS, 128))],
        out_shape=[jax.ShapeDtypeStruct((nb, L, D_MODEL), F32),
                   jax.ShapeDtypeStruct((ns, HEADS, DK, DK), F32),
                   jax.ShapeDtypeStruct((D_FF, D_MODEL), BF16),
                   jax.ShapeDtypeStruct((ns, D_MODEL), F32),
                   jax.ShapeDtypeStruct((POOL_STATE, ns, POOL_WIDTH), F32),
                   jax.ShapeDtypeStruct((ns, 2 * X_HEADS, 128), F32)],
        scratch_shapes=[pltpu.VMEM((T, D_MODEL), BF16), pltpu.VMEM((ns, HGRN_WIDTH), F32)],
        compiler_params=pltpu.CompilerParams(
            dimension_semantics=("arbitrary",), vmem_limit_bytes=VMEM_LIMIT_BYTES),
        name="attn_prompt_state_sample",
    )(x, g_cross, w_cq, kt, vb, w_co, gates, state, w_ffn_out,
      x_sample, gates, past, hgrn_norm, w_out)


def _zero_after(x):
    u = lax.bitcast_convert_type(x, jnp.uint32)
    z = lax.shift_right_logical(lax.shift_right_logical(u, jnp.uint32(16)), jnp.uint32(16))
    return lax.bitcast_convert_type(z, F32)


def _ffn_body(x, g_ref, w1_ref, w2_ref, gf_ref, fillers=()):
    hb = _rms(x, g_ref[...]).astype(BF16)

    def up(c0, c1):
        return _dot(hb, w1_ref[:, c0:c1]), _dot(hb, w1_ref[:, D_FF + c0:D_FF + c1])

    def add_to_first_tile(m, z):
        top = jnp.concatenate([m[0:8, 0:128] + z, m[0:8, 128:]], axis=1)
        return jnp.concatenate([top, m[8:, :]], axis=0)

    fillers = list(fillers)
    acts = []
    nxt = up(*FFN_COL_CHUNKS[0])
    for i, (c0, c1) in enumerate(FFN_COL_CHUNKS):
        a, bg = nxt
        if i + 1 < len(FFN_COL_CHUNKS):
            nxt = up(*FFN_COL_CHUNKS[i + 1])
        for _ in range(-(-len(fillers) // (len(FFN_COL_CHUNKS) - i))):
            bg = add_to_first_tile(bg, _zero_after(fillers.pop(0)()))
        acts.append((a * _sigmoid(a) * bg).astype(BF16))
    split = FFN_COL_CHUNKS[-1][0]
    y = x + _dot(jnp.concatenate(acts[:-1], axis=1), w2_ref[0:split, :])
    y = y + _dot(acts[-1], w2_ref[split:, :])
    return _rms(y, gf_ref[...])


def _split_heads(x):
    lead = x.shape[:-2]
    x = x.reshape(lead + (X_HEADS, 2, 128))
    x = jnp.swapaxes(x, -3, -2)
    return x.reshape(lead + (2 * X_HEADS, 128))


def _merge_heads(x):
    lead = x.shape[:-2]
    x = x.reshape(lead + (2, X_HEADS, 128))
    x = jnp.swapaxes(x, -3, -2)
    return x.reshape(lead + (X_HEADS, X_HEAD_DIM))


def _memory_attention(j, base, q_ref, k_ref, v_ref, o_ref):
    prod = k_ref[j] * q_ref[base + j][None]
    s = jnp.sum(prod + pltpu.roll(prod, X_HEADS, axis=1), axis=-1, keepdims=True)
    e = jnp.exp2(s - jnp.max(s, axis=0, keepdims=True))
    den = jnp.sum(e, axis=0)
    o = jnp.sum(e * v_ref[j], axis=0) / den
    o_ref[base + j] = o
    return o


def _ffn_attn_kernel(x_ref, g_ref, w1_ref, w2_ref, gf_ref, q_ref, k_ref, v_ref, xs_ref, wo_ref,
                     o_ref, ys_ref, att_ref):
    i = pl.program_id(0)
    last = pl.num_programs(0) - 1

    @pl.when(i < last)
    def _():
        base = i * ATTN_TOKENS
        fillers = [functools.partial(_memory_attention, j, base, q_ref, k_ref, v_ref, att_ref)
                   for j in range(ATTN_TOKENS)]
        o_ref[...] = _ffn_body(x_ref[...], g_ref, w1_ref, w2_ref, gf_ref, fillers)

    @pl.when(i == last)
    def _():
        att = jnp.concatenate([att_ref[:, half * X_HEADS + head, :]
                               for head in range(X_HEADS) for half in range(2)], axis=1)
        x2 = xs_ref[...] + _dot(att.astype(BF16), wo_ref[...])
        ys_ref[:, 0, :] = _ffn_body(x2, g_ref, w1_ref, w2_ref, gf_ref)


def _ffn_prompt_attn_sample(x, g_ffn, w1, w2, g_final, q, cache_k, cache_v, x_sample, w_co):
    n = x.shape[0]
    ns = q.shape[0]
    T = FFN_ROWS
    G = ATTN_TOKENS
    steps = n // T
    assert steps == ns // G
    full = lambda shape: pl.BlockSpec(shape, lambda i: (0,) * len(shape))
    blk = lambda i: jnp.minimum(i, steps - 1)
    kv_spec = pl.BlockSpec((G, MEM_LEN, 2 * X_HEADS, 128), lambda i: (blk(i), 0, 0, 0))
    q_spec = full((ns, 2 * X_HEADS, 128))
    y, ys = pl.pallas_call(
        _ffn_attn_kernel,
        grid=(steps + 1,),
        in_specs=[pl.BlockSpec((T, D_MODEL), lambda i: (blk(i), 0)),
                  full((1, D_MODEL)), full((D_MODEL, 2 * D_FF)), full((D_FF, D_MODEL)),
                  full((1, D_MODEL)), q_spec, kv_spec, kv_spec,
                  full((ns, D_MODEL)), full((D_MODEL, D_MODEL))],
        out_specs=[pl.BlockSpec((T, D_MODEL), lambda i: (blk(i), 0)),
                   full((ns, 1, D_MODEL))],
        out_shape=[jax.ShapeDtypeStruct((n, D_MODEL), F32),
                   jax.ShapeDtypeStruct((ns, 1, D_MODEL), F32)],
        scratch_shapes=[pltpu.VMEM((ns, 2 * X_HEADS, 128), F32)],
        compiler_params=pltpu.CompilerParams(
            dimension_semantics=("arbitrary",), vmem_limit_bytes=VMEM_LIMIT_BYTES),
        name="ffn_prompt_attn_sample",
    )(x, g_ffn, w1, w2, g_final, q, cache_k, cache_v, x_sample, w_co)
    return y, ys


def kernel(x_prompt, x_sample, mem_prompt, state_hgrn, state_pool, cache_mem_k, cache_mem_v,
           g_mix, w_in, hgrn_lb, hgrn_norm, pool_mix, pool_scale, w_out, g_mem, w_mem_kv,
           g_cross, w_cq, w_co, g_ffn, w_ffn_in, w_ffn_out, g_final):
    nb, L, _ = x_prompt.shape
    ns = x_sample.shape[0]

    g_final2 = g_final.reshape(1, D_MODEL)

    mem_k, mem_v, kt, vb, w_in_b, w_out_b, w_cq_b, w_co_b, gates, xs = _mem_kv_sample_gates(
        mem_prompt, g_mem, w_mem_kv[0], w_in[0], w_out[0], w_cq[0], w_co[0], pool_mix[0], pool_scale,
        x_sample, g_mix, hgrn_lb)
    x1, hgrn_p, pool_p, w1_b = _mixer_prompt(x_prompt, g_mix, w_in_b, hgrn_lb, hgrn_norm, w_out_b,
                                             w_ffn_in[0])
    x2, hgrn_s, w2_b, x1s, pool_s, qs = _attn_prompt_state_sample(
        x1, g_cross, w_cq_b, kt, vb, w_co_b, gates, state_hgrn[0], w_ffn_out[0],
        xs, jnp.swapaxes(state_pool[0], 0, 1), hgrn_norm, w_out_b)
    y_prompt, y_sample = _ffn_prompt_attn_sample(
        x2.reshape(nb * L, D_MODEL), g_ffn, w1_b, w2_b, g_final2, qs,
        _split_heads(cache_mem_k[0]), _split_heads(cache_mem_v[0]), x1s, w_co_b)

    return (y_prompt.reshape(nb, L, D_MODEL),
            y_sample,
            hgrn_p[None],
            jnp.swapaxes(pool_p, 0, 1)[None],
            _merge_heads(mem_k)[None],
            _merge_heads(mem_v)[None],
            hgrn_s[None],
            jnp.swapaxes(pool_s, 0, 1)[None])
```

```python
import functools

import jax
import jax.numpy as jnp
from jax import lax
from jax.experimental import pallas as pl
from jax.experimental.pallas import tpu as pltpu

F32 = jnp.float32
BF16 = jnp.bfloat16

D_MODEL = 1024
HGRN_WIDTH = 512
HEADS = 4
DK = 128
CHUNK = 64
POOL_WIDTH = 512
POOL_WINDOWS = (2, 4, 8, 16)
POOL_GROUP = 128
POOL_STATE = 15
IN_PROJ = 4 * HGRN_WIDTH + POOL_WIDTH
MEM_LEN = 256
X_HEADS = 4
X_HEAD_DIM = 256
D_FF = 2816
EPS = 1e-6
ATTN_SCALE = X_HEAD_DIM ** -0.5
LOG2_E = 1.4426950408889634
QUERY_SCALE = ATTN_SCALE * LOG2_E

VMEM_LIMIT_BYTES = 56 * 1024 * 1024

MIX_ROWS = 512
MIX_BLOCKS_PER_SEQ = 4
IN_PROJ_PIECE = 256
MIX_FILLER_SCHEDULE = (3, 0, 0) + (1, 0, 1, 0, 1, 0, 1, 0) + (2, 2)
TRI_ROWS = 256
ATTN_ROWS = 1024
SPLIT_PITCH = MEM_LEN + 4
FFN_ROWS = 512
STATE_TOKENS = 8
ATTN_TOKENS = 4
FFN_PASS_WIDTHS = (768, 768, 768, 512)
FFN_COL_CHUNKS = tuple((sum(FFN_PASS_WIDTHS[:i]), sum(FFN_PASS_WIDTHS[:i + 1]))
                       for i in range(len(FFN_PASS_WIDTHS)))
assert FFN_COL_CHUNKS[-1][1] == D_FF


def _dot(a, b):
    return jnp.dot(a, b, preferred_element_type=F32)


def _dot_nt(a, b):
    return lax.dot_general(a, b, (((1,), (1,)), ((), ())), preferred_element_type=F32)


def _dot_tn(a, b):
    return lax.dot_general(a, b, (((0,), (0,)), ((), ())), preferred_element_type=F32)


def _rms(x, g):
    ms = jnp.mean(x * x, axis=-1, keepdims=True)
    return x * lax.rsqrt(ms + EPS) * g


def _sigmoid(x):
    return 1.0 / (1.0 + jnp.exp(-x))


def _lower_bound(lb_ref):
    t = lb_ref[...]
    m = jnp.max(t, axis=0, keepdims=True)
    e = jnp.exp(t - m)
    return e[0:1, :] / jnp.sum(e, axis=0, keepdims=True)


def _gates(proj_q, proj_f, lb):
    qq = proj_q * _sigmoid(proj_q)
    sig = _sigmoid(proj_f)
    fgate = lb + (1.0 - lb) * sig
    kk = (1.0 - lb) * (1.0 - sig)
    return qq, fgate, kk


def _split2(x):
    hi = x.astype(BF16)
    return hi, (x - hi.astype(F32)).astype(BF16)


def _sample_gates(x, g_ref, winb_ref, lb_ref, o_ref, x2d_ref):
    x2d_ref[...] = x
    hb = _rms(x, g_ref[...]).astype(BF16)
    proj = _dot(hb, winb_ref[...])
    lb = _lower_bound(lb_ref)
    qq, fgate, kk = _gates(proj[:, 0:512], proj[:, 512:1024], lb)
    o_ref[:, 0:512] = qq
    o_ref[:, 512:1024] = fgate
    o_ref[:, 1024:1536] = kk
    o_ref[:, 1536:2048] = proj[:, 1024:1536]
    o_ref[:, 2048:2560] = _sigmoid(proj[:, 1536:2048])
    o_ref[:, 2560:3072] = proj[:, 2048:2560]


def _memkv_kernel(mem_ref, g_ref, w_ref, win_ref, wout_ref, wcq_ref, wco_ref, pmix_ref, ps_ref,
                  xs_ref, gmix_ref, lb_ref,
                  k_ref, v_ref, kt_ref, vb_ref, winb_ref, woutb_ref, wcqb_ref, wcob_ref,
                  gates_ref, xs2d_ref,
                  wb_ref, rows_ref):
    b = pl.program_id(0)
    nb = pl.num_programs(0) - 1
    first_pool_block = HGRN_WIDTH // POOL_GROUP

    @pl.when(b == 0)
    def _():
        wb_ref[...] = w_ref[...].astype(BF16)

    @pl.when(b < nb)
    def _():
        wrows = win_ref.shape[0]
        winb_ref[pl.ds(pl.multiple_of(b * wrows, wrows), wrows), :] = win_ref[...].astype(BF16)
        wcqb_ref[...] = wcq_ref[...].astype(BF16)
        wcob_ref[...] = wco_ref[...].astype(BF16)

        @pl.when(b < first_pool_block)
        def _():
            woutb_ref[...] = wout_ref[...].astype(BF16)

        @pl.when(b >= first_pool_block)
        def _():
            gi = b - first_pool_block
            a_hi, a_lo = _split2(pmix_ref[gi] * ps_ref[gi])
            w_hi, w_lo = _split2(wout_ref[...])
            woutb_ref[...] = (_dot(a_hi, w_hi) + _dot(a_hi, w_lo) + _dot(a_lo, w_hi)).astype(BF16)

        h = _rms(mem_ref[0], g_ref[...]).astype(BF16)
        kv = _dot(h, wb_ref[...])
        k = kv[:, :D_MODEL]
        v = kv[:, D_MODEL:]
        kt_ref[0] = k.T.astype(BF16)
        vb_ref[0] = v.astype(BF16)
        for val, out_ref in ((k, k_ref), (v, v_ref)):
            for r in range(2 * X_HEADS):
                half, head = divmod(r, X_HEADS)
                c0 = head * X_HEAD_DIM + half * 128
                rows_ref[r * SPLIT_PITCH:r * SPLIT_PITCH + MEM_LEN, :] = val[:, c0:c0 + 128]
            for m in range(MEM_LEN):
                out_ref[0, m] = rows_ref[pl.ds(m, 2 * X_HEADS, stride=SPLIT_PITCH), :]

    @pl.when(b == nb)
    def _():
        _sample_gates(xs_ref[:, 0, :], gmix_ref, winb_ref, lb_ref, gates_ref, xs2d_ref)


def _mem_kv_sample_gates(mem, g_mem, w_kv, w_in, w_out, w_cq, w_co, pool_mix, pool_scale,
                         x_sample, g_mix, hgrn_lb):
    nb = mem.shape[0]
    ns = x_sample.shape[0]
    full = lambda shape: pl.BlockSpec(shape, lambda b: (0,) * len(shape))
    blk = lambda b: jnp.minimum(b, nb - 1)
    split_spec = pl.BlockSpec((1, MEM_LEN, 2 * X_HEADS, 128), lambda b: (blk(b), 0, 0, 0))
    wrows = D_MODEL // nb
    assert wrows == POOL_GROUP
    win_spec = pl.BlockSpec((wrows, IN_PROJ), lambda b: (blk(b), 0))
    wout_spec = pl.BlockSpec((wrows, D_MODEL), lambda b: (blk(b), 0))
    return pl.pallas_call(
        _memkv_kernel,
        grid=(nb + 1,),
        in_specs=[pl.BlockSpec((1, MEM_LEN, D_MODEL), lambda b: (blk(b), 0, 0)),
                  full((1, D_MODEL)), full((D_MODEL, 2 * D_MODEL)), win_spec, wout_spec,
                  wout_spec, wout_spec,
                  full((4, POOL_GROUP, POOL_GROUP)), full((4, 1, POOL_GROUP)),
                  full((ns, 1, D_MODEL)), full((1, D_MODEL)), full((2, HGRN_WIDTH))],
        out_specs=[split_spec, split_spec,
                   pl.BlockSpec((1, D_MODEL, MEM_LEN), lambda b: (blk(b), 0, 0)),
                   pl.BlockSpec((1, MEM_LEN, D_MODEL), lambda b: (blk(b), 0, 0)),
                   full((D_MODEL, IN_PROJ)), wout_spec, wout_spec, wout_spec,
                   full((ns, 6 * 512)), full((ns, D_MODEL))],
        out_shape=[jax.ShapeDtypeStruct((nb, MEM_LEN, 2 * X_HEADS, 128), F32),
                   jax.ShapeDtypeStruct((nb, MEM_LEN, 2 * X_HEADS, 128), F32),
                   jax.ShapeDtypeStruct((nb, D_MODEL, MEM_LEN), BF16),
                   jax.ShapeDtypeStruct((nb, MEM_LEN, D_MODEL), BF16),
                   jax.ShapeDtypeStruct((D_MODEL, IN_PROJ), BF16),
                   jax.ShapeDtypeStruct((D_MODEL, D_MODEL), BF16),
                   jax.ShapeDtypeStruct((D_MODEL, D_MODEL), BF16),
                   jax.ShapeDtypeStruct((D_MODEL, D_MODEL), BF16),
                   jax.ShapeDtypeStruct((ns, 6 * 512), F32),
                   jax.ShapeDtypeStruct((ns, D_MODEL), F32)],
        scratch_shapes=[pltpu.VMEM((D_MODEL, 2 * D_MODEL), BF16),
                        pltpu.VMEM((2 * X_HEADS * SPLIT_PITCH, 128), F32)],
        compiler_params=pltpu.CompilerParams(
            dimension_semantics=("arbitrary",), vmem_limit_bytes=VMEM_LIMIT_BYTES),
        name="mem_kv_sample_gates",
    )(mem, g_mem, w_kv, w_in, w_out, w_cq, w_co, pool_mix, pool_scale.reshape(4, 1, POOL_GROUP),
      x_sample, g_mix, hgrn_lb)


def _mix_block(n, x, proj_ref, out_ref, fillers, lb_ref, hn_ref, wout_ref,
               st_ref, ext_ref, qq_ref, kk_ref, b_ref, mrg_ref):
    T = MIX_ROWS
    fillers = list(fillers)
    schedule = list(MIX_FILLER_SCHEDULE)
    assert len(schedule) == 5 + T // CHUNK and sum(schedule) == len(fillers)

    def fill():
        for _ in range(schedule.pop(0)):
            fillers.pop(0)()

    l = n % MIX_BLOCKS_PER_SEQ
    first = l == 0
    for h in range(HEADS):
        st_ref[h] = jnp.where(first, 0.0, st_ref[h])
    ext_ref[0:16, :] = jnp.where(first, 0.0, ext_ref[0:16, :])

    fill()
    lb = _lower_bound(lb_ref)
    qq, fgate, kk = _gates(proj_ref[:, 0:512], proj_ref[:, 512:1024], lb)
    qq_ref[...] = qq
    kk_ref[...] = kk
    fill()

    r = lax.broadcasted_iota(jnp.int32, (TRI_ROWS, TRI_ROWS), 0)
    c = lax.broadcasted_iota(jnp.int32, (TRI_ROWS, TRI_ROWS), 1)
    tri = jnp.where((c <= r) & (c >= (r & -CHUNK)), 1.0, 0.0).astype(BF16)
    logf = jnp.log2(fgate)
    for blk in range(T // TRI_ROWS):
        rows = slice(blk * TRI_ROWS, (blk + 1) * TRI_ROWS)
        hi, lo = _split2(logf[rows])
        b_ref[rows, :] = _dot(tri, hi) + _dot(tri, lo)
    fill()

    cr = lax.broadcasted_iota(jnp.int32, (CHUNK, 2 * CHUNK), 0)
    cc = lax.broadcasted_iota(jnp.int32, (CHUNK, 2 * CHUNK), 1)
    causal = (cc & (CHUNK - 1)) <= cr
    mid_row = (CHUNK - 1) // 2

    def block_diag(a, b):
        za = jnp.zeros((a.shape[0], b.shape[1]), a.dtype)
        zb = jnp.zeros((b.shape[0], a.shape[1]), a.dtype)
        return jnp.concatenate([jnp.concatenate([a, za], axis=1),
                                jnp.concatenate([zb, b], axis=1)], axis=0)

    for ci in range(T // CHUNK):
        r0 = ci * CHUNK
        rows = slice(r0, r0 + CHUNK)
        scores, inters, vals = [], [], []
        for p in range(HEADS // 2):
            cols = slice(2 * p * DK, (2 * p + 2) * DK)
            b = b_ref[rows, cols]
            m = b_ref[r0 + mid_row:r0 + mid_row + 1, cols]
            b_end = b_ref[r0 + CHUNK - 1:r0 + CHUNK, cols]
            e1 = jnp.exp2(b - m)
            e2 = jnp.exp2(m - b)
            q1 = qq_ref[rows, cols] * e1
            q0 = (q1 * jnp.exp2(m)).astype(BF16)
            ks = kk_ref[rows, cols] * e2
            k2 = (ks * jnp.exp2(b_end - m)).astype(BF16)
            ks = ks.astype(BF16)
            v = proj_ref[rows, 1024 + 2 * p * DK:1024 + (2 * p + 2) * DK].astype(BF16)
            decay = jnp.exp2(b_end)
            st0 = st_ref[2 * p]
            st1 = st_ref[2 * p + 1]
            scores.append(_dot_nt(q1.astype(BF16), block_diag(ks[:, :DK], ks[:, DK:])))
            inters.append(_dot_nt(q0, block_diag(st0.astype(BF16), st1.astype(BF16))))
            st_ref[2 * p] = st0 * decay[:, :DK] + _dot_tn(v[:, :DK], k2[:, :DK])
            st_ref[2 * p + 1] = st1 * decay[:, DK:] + _dot_tn(v[:, DK:], k2[:, DK:])
            vals.append(block_diag(v[:, :DK], v[:, DK:]))
        fill()
        for p in range(HEADS // 2):
            a = jnp.where(causal, scores[p], 0.0)
            o2 = inters[p] + _dot(a.astype(BF16), vals[p])
            for h in (2 * p, 2 * p + 1):
                cols = slice(h * DK, (h + 1) * DK)
                o = o2[:, (h - 2 * p) * DK:(h - 2 * p + 1) * DK]
                o = o * lax.rsqrt(jnp.mean(o * o, axis=-1, keepdims=True) + EPS)
                o = o * hn_ref[:, cols]
                g = proj_ref[rows, 1536 + h * DK:1536 + (h + 1) * DK]
                mrg_ref[rows, cols] = (o * _sigmoid(g)).astype(BF16)

    ext_ref[16:16 + T, :] = proj_ref[:, 2048:2560]
    pos = l * T + lax.broadcasted_iota(jnp.int32, (16, POOL_GROUP), 0)
    for gi, w in enumerate(POOL_WINDOWS):
        if gi % 2 == 0:
            fill()
        cols = slice(gi * POOL_GROUP, (gi + 1) * POOL_GROUP)
        s = ext_ref[:, cols]
        u = s[16:]
        shift = 1
        while shift < w:
            s = s + pltpu.roll(s, shift, axis=0)
            shift *= 2
        acc = s[16:]
        cnt = jnp.minimum(pos + 1, w).astype(F32)
        pooled = jnp.concatenate([acc[0:16] / cnt, acc[16:] * (1.0 / w)], axis=0) - u
        mrg_ref[:, HGRN_WIDTH + gi * POOL_GROUP:HGRN_WIDTH + (gi + 1) * POOL_GROUP] = pooled.astype(BF16)

    ext_ref[0:16, :] = ext_ref[T:T + 16, :]
    out_ref[...] = x + _dot(mrg_ref[...], wout_ref[...])


def _mixer_kernel(xa_ref, xn_ref, g_ref, win_ref, lb_ref, hn_ref, wout_ref,
                  w1f_ref,
                  x1_ref, hst_ref, pst_ref, w1b_ref,
                  p0_ref, p1_ref, hb_ref, st_ref, ext_ref, qq_ref, kk_ref, b_ref, mrg_ref):
    T = MIX_ROWS
    g = pl.program_id(0)
    w1b_ref[...] = w1f_ref[...].astype(BF16)
    rest = (lb_ref, hn_ref, wout_ref, st_ref, ext_ref, qq_ref, kk_ref, b_ref, mrg_ref)

    def in_proj_pieces(x_ref, rows, p_ref):
        def prep():
            hb_ref[...] = _rms(x_ref[rows, :], g_ref[...]).astype(BF16)

        def piece(k):
            cols = slice(k * IN_PROJ_PIECE, (k + 1) * IN_PROJ_PIECE)
            p_ref[:, cols] = _dot(hb_ref[...], win_ref[:, cols])

        return [prep] + [functools.partial(piece, k) for k in range(IN_PROJ // IN_PROJ_PIECE)]

    @pl.when(g == 0)
    def _():
        st_ref[...] = jnp.zeros_like(st_ref)
        ext_ref[0:16, :] = jnp.zeros((16, POOL_WIDTH), F32)
        for f in in_proj_pieces(xa_ref, slice(0, T), p0_ref):
            f()

    _mix_block(2 * g, xa_ref[0:T, :], p0_ref, x1_ref.at[0:T, :],
               in_proj_pieces(xa_ref, slice(T, 2 * T), p1_ref), *rest)
    _mix_block(2 * g + 1, xa_ref[T:2 * T, :], p1_ref, x1_ref.at[T:2 * T, :],
               in_proj_pieces(xn_ref, slice(0, T), p0_ref), *rest)

    steps_per_seq = MIX_BLOCKS_PER_SEQ // 2

    @pl.when(g % steps_per_seq == steps_per_seq - 1)
    def _():
        for h in range(HEADS):
            hst_ref[0, h] = st_ref[h].T

    for seq in range(pst_ref.shape[1]):
        @pl.when(g == seq * steps_per_seq + steps_per_seq - 1)
        def _():
            pst_ref[:, seq, :] = ext_ref[T + 1:T + 16, :]


def _mixer_prompt(x, g_mix, w_in, hgrn_lb, hgrn_norm, w_out, w_ffn_in):
    nb, L, _ = x.shape
    T = MIX_ROWS
    assert L // T == MIX_BLOCKS_PER_SEQ and MIX_BLOCKS_PER_SEQ % 2 == 0
    n_blocks = nb * MIX_BLOCKS_PER_SEQ
    steps = n_blocks // 2
    steps_per_seq = MIX_BLOCKS_PER_SEQ // 2
    x2d = x.reshape(nb * L, D_MODEL)
    full = lambda shape: pl.BlockSpec(shape, lambda g: (0,) * len(shape))
    rows = lambda w: pl.BlockSpec((w.shape[0] // steps, w.shape[1]), lambda g: (g, 0))
    side = (w_ffn_in,)
    x1, hst, pst, w1b = pl.pallas_call(
        _mixer_kernel,
        grid=(steps,),
        in_specs=[pl.BlockSpec((2 * T, D_MODEL), lambda g: (g, 0)),
                  pl.BlockSpec((T, D_MODEL), lambda g: (jnp.minimum(2 * g + 2, n_blocks - 1), 0)),
                  full((1, D_MODEL)), full((D_MODEL, IN_PROJ)), full((2, HGRN_WIDTH)),
                  full((1, HGRN_WIDTH)), full((D_MODEL, D_MODEL))] + [rows(w) for w in side],
        out_specs=[pl.BlockSpec((2 * T, D_MODEL), lambda g: (g, 0)),
                   pl.BlockSpec((1, HEADS, DK, DK), lambda g: (g // steps_per_seq, 0, 0, 0)),
                   full((POOL_STATE, nb, POOL_WIDTH))]
                  + [rows(w) for w in side],
        out_shape=[jax.ShapeDtypeStruct((nb * L, D_MODEL), F32),
                   jax.ShapeDtypeStruct((nb, HEADS, DK, DK), F32),
                   jax.ShapeDtypeStruct((POOL_STATE, nb, POOL_WIDTH), F32)]
                  + [jax.ShapeDtypeStruct(w.shape, BF16) for w in side],
        scratch_shapes=[pltpu.VMEM((T, IN_PROJ), F32),
                        pltpu.VMEM((T, IN_PROJ), F32),
                        pltpu.VMEM((T, D_MODEL), BF16),
                        pltpu.VMEM((HEADS, DK, DK), F32),
                        pltpu.VMEM((16 + T, POOL_WIDTH), F32),
                        pltpu.VMEM((T, HGRN_WIDTH), F32),
                        pltpu.VMEM((T, HGRN_WIDTH), F32),
                        pltpu.VMEM((T, HGRN_WIDTH), F32),
                        pltpu.VMEM((T, D_MODEL), BF16)],
        compiler_params=pltpu.CompilerParams(
            dimension_semantics=("arbitrary",), vmem_limit_bytes=VMEM_LIMIT_BYTES),
        name="mixer_prompt",
    )(x2d, x2d, g_mix, w_in, hgrn_lb, hgrn_norm, w_out, *side)
    return x1.reshape(nb, L, D_MODEL), hst, pst, w1b


def _state_step(base, gates_ref, s_ref, so_ref, o_ref):
    G = STATE_TOKENS
    pad = jnp.zeros((DK - G, DK), F32)
    for h in range(HEADS):
        cols = slice(h * DK, (h + 1) * DK)
        qt = jnp.concatenate([gates_ref[:, h * DK:(h + 1) * DK], pad], axis=0).T
        ft = jnp.concatenate([gates_ref[:, 512 + h * DK:512 + (h + 1) * DK], pad], axis=0).T
        kt = jnp.concatenate([gates_ref[:, 1024 + h * DK:1024 + (h + 1) * DK], pad], axis=0).T
        readouts = []
        for j in range(G):
            v = gates_ref[j:j + 1, 1536 + h * DK:1536 + (h + 1) * DK]
            s_new = ft[:, j:j + 1] * s_ref[j, h] + kt[:, j:j + 1] * v
            so_ref[j, h] = s_new
            readouts.append(jnp.sum(qt[:, j:j + 1] * s_new, axis=0, keepdims=True))
        o_ref[pl.ds(pl.multiple_of(base, G), G), cols] = jnp.concatenate(readouts, axis=0)


def _sample_mix(x_ref, gates_ref, o_ref, past_ref, hn_ref, wout_ref, gc_ref, wq_ref,
                x1_ref, pool_ref, q_ref, mrg_ref):
    n = x_ref.shape[0]
    for h in range(HEADS):
        cols = slice(h * DK, (h + 1) * DK)
        o = o_ref[:, cols]
        o = o * lax.rsqrt(jnp.mean(o * o, axis=-1, keepdims=True) + EPS) * hn_ref[:, cols]
        mrg_ref[0:n, cols] = (o * gates_ref[:, 2048 + h * DK:2048 + (h + 1) * DK]).astype(BF16)
    for gi, w in enumerate(POOL_WINDOWS):
        cols = slice(gi * POOL_GROUP, (gi + 1) * POOL_GROUP)
        u = gates_ref[:, 2560 + gi * POOL_GROUP:2560 + (gi + 1) * POOL_GROUP]
        acc = u
        for j in range(1, w):
            acc = acc + past_ref[POOL_STATE - j, :, cols]
        pooled = acc / float(w) - u
        mrg_ref[0:n, HGRN_WIDTH + gi * POOL_GROUP:HGRN_WIDTH + (gi + 1) * POOL_GROUP] = pooled.astype(BF16)
    pool_ref[0:POOL_STATE - 1] = past_ref[1:POOL_STATE]
    pool_ref[POOL_STATE - 1] = gates_ref[:, 2560:3072]
    x1 = x_ref[...] + _dot(mrg_ref[0:n, :], wout_ref[...])
    x1_ref[...] = x1
    hb = _rms(x1, gc_ref[...]).astype(BF16)
    q = _dot(hb, wq_ref[...]) * QUERY_SCALE
    for r in range(2 * X_HEADS):
        half, head = divmod(r, X_HEADS)
        c0 = head * X_HEAD_DIM + half * 128
        q_ref[:, r, :] = q[:, c0:c0 + 128]


def _attn_state_kernel(x_ref, g_ref, wq_ref, kt_ref, vb_ref, wo_ref, gates_ref, s_ref, w2f_ref,
                       xs_ref, gall_ref, past_ref, hn_ref, wout_ref,
                       o_ref, so_ref, w2b_ref, x1s_ref, pool_ref, qs_ref,
                       att_ref, oo_ref):
    i = pl.program_id(0)
    last = pl.num_programs(0) - 1

    @pl.when(i < last)
    def _():
        w2b_ref[...] = w2f_ref[...].astype(BF16)
        _attention_block(x_ref, g_ref, kt_ref, vb_ref, wq_ref, wo_ref, o_ref, att_ref)
        _state_step(i * STATE_TOKENS, gates_ref, s_ref, so_ref, oo_ref)

    @pl.when(i == last)
    def _():
        _sample_mix(xs_ref, gall_ref, oo_ref, past_ref, hn_ref, wout_ref, g_ref, wq_ref,
                    x1s_ref, pool_ref, qs_ref, att_ref)


def _attention_block(x_ref, g_ref, kt_ref, vb_ref, wq_ref, wo_ref, o_ref, att_ref):
    H = ATTN_ROWS // 2
    halves = (slice(0, H), slice(H, 2 * H))

    def query(rows):
        hb = _rms(x_ref[0, rows, :], g_ref[...]).astype(BF16)
        return (_dot(hb, wq_ref[...]) * QUERY_SCALE).astype(BF16)

    def scores(q):
        return [_dot(q[:, h * X_HEAD_DIM:(h + 1) * X_HEAD_DIM],
                     kt_ref[0, h * X_HEAD_DIM:(h + 1) * X_HEAD_DIM, :]) for h in range(X_HEADS)]

    def values(rows, ss):
        for h, s in enumerate(ss):
            cols = slice(h * X_HEAD_DIM, (h + 1) * X_HEAD_DIM)
            e = jnp.exp2(s - jnp.max(s, axis=-1, keepdims=True))
            den = jnp.sum(e, axis=-1, keepdims=True)
            o = _dot(e.astype(BF16), vb_ref[0, :, cols]) / den
            att_ref[rows, cols] = o.astype(BF16)

    def project(rows):
        o_ref[0, rows, :] = x_ref[0, rows, :] + _dot(att_ref[rows, :], wo_ref[...])

    s0 = scores(query(halves[0]))
    q1 = query(halves[1])
    values(halves[0], s0)
    s1 = scores(q1)
    project(halves[0])
    values(halves[1], s1)
    project(halves[1])


def _attn_prompt_state_sample(x, g_cross, w_cq, kt, vb, w_co, gates, state, w_ffn_out,
                              x_sample, past, hgrn_norm, w_out):
    nb, L, _ = x.shape
    T = ATTN_ROWS
    G = STATE_TOKENS
    steps_per_seq = L // T
    steps = nb * steps_per_seq
    ns = gates.shape[0]
    assert steps == ns // G
    full = lambda shape: pl.BlockSpec(shape, lambda i: (0,) * len(shape))
    blk = lambda i: jnp.minimum(i, steps - 1)
    st_spec = pl.BlockSpec((G, HEADS, DK, DK), lambda i: (blk(i), 0, 0, 0))
    x_spec = pl.BlockSpec((1, T, D_MODEL),
                          lambda i: (blk(i) // steps_per_seq, blk(i) % steps_per_seq, 0))
    kv_map = lambda i: (blk(i) // steps_per_seq, 0, 0)
    w2_spec = pl.BlockSpec((D_FF // steps, D_MODEL), lambda i: (blk(i), 0))
    return pl.pallas_call(
        _attn_state_kernel,
        grid=(steps + 1,),
        in_specs=[x_spec, full((1, D_MODEL)), full((D_MODEL, D_MODEL)),
                  pl.BlockSpec((1, D_MODEL, MEM_LEN), kv_map),
                  pl.BlockSpec((1, MEM_LEN, D_MODEL), kv_map),
                  full((D_MODEL, D_MODEL)),
                  pl.BlockSpec((G, 4 * 512), lambda i: (blk(i), 0)), st_spec, w2_spec,
                  full((ns, D_MODEL)), full((ns, 6 * 512)), full((POOL_STATE, ns, POOL_WIDTH)),
                  full((1, HGRN_WIDTH)), full((D_MODEL, D_MODEL))],
        out_specs=[x_spec, st_spec, w2_spec,
                   full((ns, D_MODEL)), full((POOL_STATE, ns, POOL_WIDTH)),
                   full((ns, 2 * X_HEADS, 128))],
        out_shape=[jax.ShapeDtypeStruct((nb, L, D_MODEL), F32),
                   jax.ShapeDtypeStruct((ns, HEADS, DK, DK), F32),
                   jax.ShapeDtypeStruct((D_FF, D_MODEL), BF16),
                   jax.ShapeDtypeStruct((ns, D_MODEL), F32),
                   jax.ShapeDtypeStruct((POOL_STATE, ns, POOL_WIDTH), F32),
                   jax.ShapeDtypeStruct((ns, 2 * X_HEADS, 128), F32)],
        scratch_shapes=[pltpu.VMEM((T, D_MODEL), BF16), pltpu.VMEM((ns, HGRN_WIDTH), F32)],
        compiler_params=pltpu.CompilerParams(
            dimension_semantics=("arbitrary",), vmem_limit_bytes=VMEM_LIMIT_BYTES),
        name="attn_prompt_state_sample",
    )(x, g_cross, w_cq, kt, vb, w_co, gates, state, w_ffn_out,
      x_sample, gates, past, hgrn_norm, w_out)


def _zero_after(x):
    u = lax.bitcast_convert_type(x, jnp.uint32)
    z = lax.shift_right_logical(lax.shift_right_logical(u, jnp.uint32(16)), jnp.uint32(16))
    return lax.bitcast_convert_type(z, F32)


def _ffn_body(x, g_ref, w1_ref, w2_ref, gf_ref, fillers=()):
    hb = _rms(x, g_ref[...]).astype(BF16)

    def up(c0, c1):
        return _dot(hb, w1_ref[:, c0:c1]), _dot(hb, w1_ref[:, D_FF + c0:D_FF + c1])

    def add_to_first_tile(m, z):
        top = jnp.concatenate([m[0:8, 0:128] + z, m[0:8, 128:]], axis=1)
        return jnp.concatenate([top, m[8:, :]], axis=0)

    fillers = list(fillers)
    acts = []
    nxt = up(*FFN_COL_CHUNKS[0])
    for i, (c0, c1) in enumerate(FFN_COL_CHUNKS):
        a, bg = nxt
        if i + 1 < len(FFN_COL_CHUNKS):
            nxt = up(*FFN_COL_CHUNKS[i + 1])
        for _ in range(-(-len(fillers) // (len(FFN_COL_CHUNKS) - i))):
            bg = add_to_first_tile(bg, _zero_after(fillers.pop(0)()))
        acts.append((a * _sigmoid(a) * bg).astype(BF16))
    split = FFN_COL_CHUNKS[-1][0]
    y = x + _dot(jnp.concatenate(acts[:-1], axis=1), w2_ref[0:split, :])
    y = y + _dot(acts[-1], w2_ref[split:, :])
    return _rms(y, gf_ref[...])


def _split_heads(x):
    lead = x.shape[:-2]
    x = x.reshape(lead + (X_HEADS, 2, 128))
    x = jnp.swapaxes(x, -3, -2)
    return x.reshape(lead + (2 * X_HEADS, 128))


def _merge_heads(x):
    lead = x.shape[:-2]
    x = x.reshape(lead + (2, X_HEADS, 128))
    x = jnp.swapaxes(x, -3, -2)
    return x.reshape(lead + (X_HEADS, X_HEAD_DIM))


def _memory_attention(j, base, q_ref, k_ref, v_ref, o_ref):
    prod = k_ref[j] * q_ref[base + j][None]
    s = jnp.sum(prod + pltpu.roll(prod, X_HEADS, axis=1), axis=-1, keepdims=True)
    e = jnp.exp2(s - jnp.max(s, axis=0, keepdims=True))
    den = jnp.sum(e, axis=0)
    o = jnp.sum(e * v_ref[j], axis=0) / den
    o_ref[base + j] = o
    return o


def _ffn_attn_kernel(x_ref, g_ref, w1_ref, w2_ref, gf_ref, q_ref, k_ref, v_ref, xs_ref, wo_ref,
                     o_ref, ys_ref, att_ref):
    i = pl.program_id(0)
    last = pl.num_programs(0) - 1

    @pl.when(i < last)
    def _():
        base = i * ATTN_TOKENS
        fillers = [functools.partial(_memory_attention, j, base, q_ref, k_ref, v_ref, att_ref)
                   for j in range(ATTN_TOKENS)]
        o_ref[...] = _ffn_body(x_ref[...], g_ref, w1_ref, w2_ref, gf_ref, fillers)

    @pl.when(i == last)
    def _():
        att = jnp.concatenate([att_ref[:, half * X_HEADS + head, :]
                               for head in range(X_HEADS) for half in range(2)], axis=1)
        x2 = xs_ref[...] + _dot(att.astype(BF16), wo_ref[...])
        ys_ref[:, 0, :] = _ffn_body(x2, g_ref, w1_ref, w2_ref, gf_ref)


def _ffn_prompt_attn_sample(x, g_ffn, w1, w2, g_final, q, cache_k, cache_v, x_sample, w_co):
    n = x.shape[0]
    ns = q.shape[0]
    T = FFN_ROWS
    G = ATTN_TOKENS
    steps = n // T
    assert steps == ns // G
    full = lambda shape: pl.BlockSpec(shape, lambda i: (0,) * len(shape))
    blk = lambda i: jnp.minimum(i, steps - 1)
    kv_spec = pl.BlockSpec((G, MEM_LEN, 2 * X_HEADS, 128), lambda i: (blk(i), 0, 0, 0))
    q_spec = full((ns, 2 * X_HEADS, 128))
    y, ys = pl.pallas_call(
        _ffn_attn_kernel,
        grid=(steps + 1,),
        in_specs=[pl.BlockSpec((T, D_MODEL), lambda i: (blk(i), 0)),
                  full((1, D_MODEL)), full((D_MODEL, 2 * D_FF)), full((D_FF, D_MODEL)),
                  full((1, D_MODEL)), q_spec, kv_spec, kv_spec,
                  full((ns, D_MODEL)), full((D_MODEL, D_MODEL))],
        out_specs=[pl.BlockSpec((T, D_MODEL), lambda i: (blk(i), 0)),
                   full((ns, 1, D_MODEL))],
        out_shape=[jax.ShapeDtypeStruct((n, D_MODEL), F32),
                   jax.ShapeDtypeStruct((ns, 1, D_MODEL), F32)],
        scratch_shapes=[pltpu.VMEM((ns, 2 * X_HEADS, 128), F32)],
        compiler_params=pltpu.CompilerParams(
            dimension_semantics=("arbitrary",), vmem_limit_bytes=VMEM_LIMIT_BYTES),
        name="ffn_prompt_attn_sample",
    )(x, g_ffn, w1, w2, g_final, q, cache_k, cache_v, x_sample, w_co)
    return y, ys


def kernel(x_prompt, x_sample, mem_prompt, state_hgrn, state_pool, cache_mem_k, cache_mem_v,
           g_mix, w_in, hgrn_lb, hgrn_norm, pool_mix, pool_scale, w_out, g_mem, w_mem_kv,
           g_cross, w_cq, w_co, g_ffn, w_ffn_in, w_ffn_out, g_final):
    nb, L, _ = x_prompt.shape
    ns = x_sample.shape[0]

    g_final2 = g_final.reshape(1, D_MODEL)

    mem_k, mem_v, kt, vb, w_in_b, w_out_b, w_cq_b, w_co_b, gates, xs = _mem_kv_sample_gates(
        mem_prompt, g_mem, w_mem_kv[0], w_in[0], w_out[0], w_cq[0], w_co[0], pool_mix[0], pool_scale,
        x_sample, g_mix, hgrn_lb)
    x1, hgrn_p, pool_p, w1_b = _mixer_prompt(x_prompt, g_mix, w_in_b, hgrn_lb, hgrn_norm, w_out_b,
                                             w_ffn_in[0])
    x2, hgrn_s, w2_b, x1s, pool_s, qs = _attn_prompt_state_sample(
        x1, g_cross, w_cq_b, kt, vb, w_co_b, gates, state_hgrn[0], w_ffn_out[0],
        xs, jnp.swapaxes(state_pool[0], 0, 1), hgrn_norm, w_out_b)
    y_prompt, y_sample = _ffn_prompt_attn_sample(
        x2.reshape(nb * L, D_MODEL), g_ffn, w1_b, w2_b, g_final2, qs,
        _split_heads(cache_mem_k[0]), _split_heads(cache_mem_v[0]), x1s, w_co_b)

    return (y_prompt.reshape(nb, L, D_MODEL),
            y_sample,
            hgrn_p[None],
            jnp.swapaxes(pool_p, 0, 1)[None],
            _merge_heads(mem_k)[None],
            _merge_heads(mem_v)[None],
            hgrn_s[None],
            jnp.swapaxes(pool_s, 0, 1)[None])
```

```python
import functools

import jax
import jax.numpy as jnp
from jax import lax
from jax.experimental import pallas as pl
from jax.experimental.pallas import tpu as pltpu

F32 = jnp.float32
BF16 = jnp.bfloat16

D_MODEL = 1024
HGRN_WIDTH = 512
HEADS = 4
DK = 128
CHUNK = 64
POOL_WIDTH = 512
POOL_WINDOWS = (2, 4, 8, 16)
POOL_GROUP = 128
POOL_STATE = 15
IN_PROJ = 4 * HGRN_WIDTH + POOL_WIDTH
MEM_LEN = 256
X_HEADS = 4
X_HEAD_DIM = 256
D_FF = 2816
EPS = 1e-6
ATTN_SCALE = X_HEAD_DIM ** -0.5
LOG2_E = 1.4426950408889634
QUERY_SCALE = ATTN_SCALE * LOG2_E

VMEM_LIMIT_BYTES = 56 * 1024 * 1024

MIX_ROWS = 512
MIX_BLOCKS_PER_SEQ = 4
IN_PROJ_PIECE = 256
MIX_FILLER_SCHEDULE = (3, 0, 0) + (1, 0, 1, 0, 1, 0, 1, 0) + (2, 2)
TRI_ROWS = 256
ATTN_ROWS = 1024
SPLIT_PITCH = MEM_LEN + 4
FFN_ROWS = 512
STATE_TOKENS = 8
ATTN_TOKENS = 4
FFN_PASS_WIDTHS = (768, 768, 768, 512)
FFN_COL_CHUNKS = tuple((sum(FFN_PASS_WIDTHS[:i]), sum(FFN_PASS_WIDTHS[:i + 1]))
                       for i in range(len(FFN_PASS_WIDTHS)))
assert FFN_COL_CHUNKS[-1][1] == D_FF


def _dot(a, b):
    return jnp.dot(a, b, preferred_element_type=F32)


def _dot_nt(a, b):
    return lax.dot_general(a, b, (((1,), (1,)), ((), ())), preferred_element_type=F32)


def _dot_tn(a, b):
    return lax.dot_general(a, b, (((0,), (0,)), ((), ())), preferred_element_type=F32)


def _rms(x, g):
    ms = jnp.mean(x * x, axis=-1, keepdims=True)
    return x * lax.rsqrt(ms + EPS) * g


def _sigmoid(x):
    return 1.0 / (1.0 + jnp.exp(-x))


def _lower_bound(lb_ref):
    t = lb_ref[...]
    m = jnp.max(t, axis=0, keepdims=True)
    e = jnp.exp(t - m)
    return e[0:1, :] / jnp.sum(e, axis=0, keepdims=True)


def _gates(proj_q, proj_f, lb):
    qq = proj_q * _sigmoid(proj_q)
    sig = _sigmoid(proj_f)
    fgate = lb + (1.0 - lb) * sig
    kk = (1.0 - lb) * (1.0 - sig)
    return qq, fgate, kk


def _split2(x):
    hi = x.astype(BF16)
    return hi, (x - hi.astype(F32)).astype(BF16)


def _sample_gates(x, g_ref, winb_ref, lb_ref, o_ref, x2d_ref):
    x2d_ref[...] = x
    hb = _rms(x, g_ref[...]).astype(BF16)
    proj = _dot(hb, winb_ref[...])
    lb = _lower_bound(lb_ref)
    qq, fgate, kk = _gates(proj[:, 0:512], proj[:, 512:1024], lb)
    o_ref[:, 0:512] = qq
    o_ref[:, 512:1024] = fgate
    o_ref[:, 1024:1536] = kk
    o_ref[:, 1536:2048] = proj[:, 1024:1536]
    o_ref[:, 2048:2560] = _sigmoid(proj[:, 1536:2048])
    o_ref[:, 2560:3072] = proj[:, 2048:2560]


def _memkv_kernel(mem_ref, g_ref, w_ref, win_ref, wout_ref, wcq_ref, wco_ref, pmix_ref, ps_ref,
                  xs_ref, gmix_ref, lb_ref,
                  k_ref, v_ref, kt_ref, vb_ref, winb_ref, woutb_ref, wcqb_ref, wcob_ref,
                  gates_ref, xs2d_ref,
                  wb_ref, rows_ref):
    b = pl.program_id(0)
    nb = pl.num_programs(0) - 1
    first_pool_block = HGRN_WIDTH // POOL_GROUP

    @pl.when(b == 0)
    def _():
        wb_ref[...] = w_ref[...].astype(BF16)

    @pl.when(b < nb)
    def _():
        wrows = win_ref.shape[0]
        winb_ref[pl.ds(pl.multiple_of(b * wrows, wrows), wrows), :] = win_ref[...].astype(BF16)
        wcqb_ref[...] = wcq_ref[...].astype(BF16)
        wcob_ref[...] = wco_ref[...].astype(BF16)

        @pl.when(b < first_pool_block)
        def _():
            woutb_ref[...] = wout_ref[...].astype(BF16)

        @pl.when(b >= first_pool_block)
        def _():
            gi = b - first_pool_block
            a_hi, a_lo = _split2(pmix_ref[gi] * ps_ref[gi])
            w_hi, w_lo = _split2(wout_ref[...])
            woutb_ref[...] = (_dot(a_hi, w_hi) + _dot(a_hi, w_lo) + _dot(a_lo, w_hi)).astype(BF16)

        h = _rms(mem_ref[0], g_ref[...]).astype(BF16)
        kv = _dot(h, wb_ref[...])
        k = kv[:, :D_MODEL]
        v = kv[:, D_MODEL:]
        kt_ref[0] = k.T.astype(BF16)
        vb_ref[0] = v.astype(BF16)
        for val, out_ref in ((k, k_ref), (v, v_ref)):
            for r in range(2 * X_HEADS):
                half, head = divmod(r, X_HEADS)
                c0 = head * X_HEAD_DIM + half * 128
                rows_ref[r * SPLIT_PITCH:r * SPLIT_PITCH + MEM_LEN, :] = val[:, c0:c0 + 128]
            for m in range(MEM_LEN):
                out_ref[0, m] = rows_ref[pl.ds(m, 2 * X_HEADS, stride=SPLIT_PITCH), :]

    @pl.when(b == nb)
    def _():
        _sample_gates(xs_ref[:, 0, :], gmix_ref, winb_ref, lb_ref, gates_ref, xs2d_ref)


def _mem_kv_sample_gates(mem, g_mem, w_kv, w_in, w_out, w_cq, w_co, pool_mix, pool_scale,
                         x_sample, g_mix, hgrn_lb):
    nb = mem.shape[0]
    ns = x_sample.shape[0]
    full = lambda shape: pl.BlockSpec(shape, lambda b: (0,) * len(shape))
    blk = lambda b: jnp.minimum(b, nb - 1)
    split_spec = pl.BlockSpec((1, MEM_LEN, 2 * X_HEADS, 128), lambda b: (blk(b), 0, 0, 0))
    wrows = D_MODEL // nb
    assert wrows == POOL_GROUP
    win_spec = pl.BlockSpec((wrows, IN_PROJ), lambda b: (blk(b), 0))
    wout_spec = pl.BlockSpec((wrows, D_MODEL), lambda b: (blk(b), 0))
    return pl.pallas_call(
        _memkv_kernel,
        grid=(nb + 1,),
        in_specs=[pl.BlockSpec((1, MEM_LEN, D_MODEL), lambda b: (blk(b), 0, 0)),
                  full((1, D_MODEL)), full((D_MODEL, 2 * D_MODEL)), win_spec, wout_spec,
                  wout_spec, wout_spec,
                  full((4, POOL_GROUP, POOL_GROUP)), full((4, 1, POOL_GROUP)),
                  full((ns, 1, D_MODEL)), full((1, D_MODEL)), full((2, HGRN_WIDTH))],
        out_specs=[split_spec, split_spec,
                   pl.BlockSpec((1, D_MODEL, MEM_LEN), lambda b: (blk(b), 0, 0)),
                   pl.BlockSpec((1, MEM_LEN, D_MODEL), lambda b: (blk(b), 0, 0)),
                   full((D_MODEL, IN_PROJ)), wout_spec, wout_spec, wout_spec,
                   full((ns, 6 * 512)), full((ns, D_MODEL))],
        out_shape=[jax.ShapeDtypeStruct((nb, MEM_LEN, 2 * X_HEADS, 128), F32),
                   jax.ShapeDtypeStruct((nb, MEM_LEN, 2 * X_HEADS, 128), F32),
                   jax.ShapeDtypeStruct((nb, D_MODEL, MEM_LEN), BF16),
                   jax.ShapeDtypeStruct((nb, MEM_LEN, D_MODEL), BF16),
                   jax.ShapeDtypeStruct((D_MODEL, IN_PROJ), BF16),
                   jax.ShapeDtypeStruct((D_MODEL, D_MODEL), BF16),
                   jax.ShapeDtypeStruct((D_MODEL, D_MODEL), BF16),
                   jax.ShapeDtypeStruct((D_MODEL, D_MODEL), BF16),
                   jax.ShapeDtypeStruct((ns, 6 * 512), F32),
                   jax.ShapeDtypeStruct((ns, D_MODEL), F32)],
        scratch_shapes=[pltpu.VMEM((D_MODEL, 2 * D_MODEL), BF16),
                        pltpu.VMEM((2 * X_HEADS * SPLIT_PITCH, 128), F32)],
        compiler_params=pltpu.CompilerParams(
            dimension_semantics=("arbitrary",), vmem_limit_bytes=VMEM_LIMIT_BYTES),
        name="mem_kv_sample_gates",
    )(mem, g_mem, w_kv, w_in, w_out, w_cq, w_co, pool_mix, pool_scale.reshape(4, 1, POOL_GROUP),
      x_sample, g_mix, hgrn_lb)


def _mix_block(n, x, proj_ref, out_ref, fillers, lb_ref, hn_ref, wout_ref,
               st_ref, ext_ref, qq_ref, kk_ref, b_ref, mrg_ref):
    T = MIX_ROWS
    fillers = list(fillers)
    schedule = list(MIX_FILLER_SCHEDULE)
    assert len(schedule) == 5 + T // CHUNK and sum(schedule) == len(fillers)

    def fill():
        for _ in range(schedule.pop(0)):
            fillers.pop(0)()

    l = n % MIX_BLOCKS_PER_SEQ
    first = l == 0
    for h in range(HEADS):
        st_ref[h] = jnp.where(first, 0.0, st_ref[h])
    ext_ref[0:16, :] = jnp.where(first, 0.0, ext_ref[0:16, :])

    fill()
    lb = _lower_bound(lb_ref)
    qq, fgate, kk = _gates(proj_ref[:, 0:512], proj_ref[:, 512:1024], lb)
    qq_ref[...] = qq
    kk_ref[...] = kk
    fill()

    r = lax.broadcasted_iota(jnp.int32, (TRI_ROWS, TRI_ROWS), 0)
    c = lax.broadcasted_iota(jnp.int32, (TRI_ROWS, TRI_ROWS), 1)
    tri = jnp.where((c <= r) & (c >= (r & -CHUNK)), 1.0, 0.0).astype(BF16)
    logf = jnp.log2(fgate)
    for blk in range(T // TRI_ROWS):
        rows = slice(blk * TRI_ROWS, (blk + 1) * TRI_ROWS)
        hi, lo = _split2(logf[rows])
        b_ref[rows, :] = _dot(tri, hi) + _dot(tri, lo)
    fill()

    cr = lax.broadcasted_iota(jnp.int32, (CHUNK, CHUNK), 0)
    cc = lax.broadcasted_iota(jnp.int32, (CHUNK, CHUNK), 1)
    causal = cc <= cr
    mid_row = (CHUNK - 1) // 2

    for ci in range(T // CHUNK):
        r0 = ci * CHUNK
        rows = slice(r0, r0 + CHUNK)
        scores, inters, vals = [], [], []
        for h in range(HEADS):
            cols = slice(h * DK, (h + 1) * DK)
            b = b_ref[rows, cols]
            m = b_ref[r0 + mid_row:r0 + mid_row + 1, cols]
            b_end = b_ref[r0 + CHUNK - 1:r0 + CHUNK, cols]
            e1 = jnp.exp2(b - m)
            e2 = jnp.exp2(m - b)
            q1 = qq_ref[rows, cols] * e1
            q0 = q1 * jnp.exp2(m)
            ks = kk_ref[rows, cols] * e2
            k2 = ks * jnp.exp2(b_end - m)
            v = proj_ref[rows, 1024 + h * DK:1024 + (h + 1) * DK].astype(BF16)
            st = st_ref[h]
            scores.append(_dot_nt(q1.astype(BF16), ks.astype(BF16)))
            inters.append(_dot_nt(q0.astype(BF16), st.astype(BF16)))
            st_ref[h] = st * jnp.exp2(b_end) + _dot_tn(v, k2.astype(BF16))
            vals.append(v)
        fill()
        for h in range(HEADS):
            cols = slice(h * DK, (h + 1) * DK)
            a = jnp.where(causal, scores[h], 0.0)
            o = inters[h] + _dot(a.astype(BF16), vals[h])
            o = o * lax.rsqrt(jnp.mean(o * o, axis=-1, keepdims=True) + EPS)
            o = o * hn_ref[:, cols]
            g = proj_ref[rows, 1536 + h * DK:1536 + (h + 1) * DK]
            mrg_ref[rows, cols] = (o * _sigmoid(g)).astype(BF16)

    ext_ref[16:16 + T, :] = proj_ref[:, 2048:2560]
    pos = l * T + lax.broadcasted_iota(jnp.int32, (16, POOL_GROUP), 0)
    for gi, w in enumerate(POOL_WINDOWS):
        if gi % 2 == 0:
            fill()
        cols = slice(gi * POOL_GROUP, (gi + 1) * POOL_GROUP)
        s = ext_ref[:, cols]
        u = s[16:]
        shift = 1
        while shift < w:
            s = s + pltpu.roll(s, shift, axis=0)
            shift *= 2
        acc = s[16:]
        cnt = jnp.minimum(pos + 1, w).astype(F32)
        pooled = jnp.concatenate([acc[0:16] / cnt, acc[16:] * (1.0 / w)], axis=0) - u
        mrg_ref[:, HGRN_WIDTH + gi * POOL_GROUP:HGRN_WIDTH + (gi + 1) * POOL_GROUP] = pooled.astype(BF16)

    ext_ref[0:16, :] = ext_ref[T:T + 16, :]
    out_ref[...] = x + _dot(mrg_ref[...], wout_ref[...])


def _mixer_kernel(xa_ref, xn_ref, g_ref, win_ref, lb_ref, hn_ref, wout_ref,
                  w1f_ref,
                  x1_ref, hst_ref, pst_ref, w1b_ref,
                  p0_ref, p1_ref, hb_ref, st_ref, ext_ref, qq_ref, kk_ref, b_ref, mrg_ref):
    T = MIX_ROWS
    g = pl.program_id(0)
    w1b_ref[...] = w1f_ref[...].astype(BF16)
    rest = (lb_ref, hn_ref, wout_ref, st_ref, ext_ref, qq_ref, kk_ref, b_ref, mrg_ref)

    def in_proj_pieces(x_ref, rows, p_ref):
        def prep():
            hb_ref[...] = _rms(x_ref[rows, :], g_ref[...]).astype(BF16)

        def piece(k):
            cols = slice(k * IN_PROJ_PIECE, (k + 1) * IN_PROJ_PIECE)
            p_ref[:, cols] = _dot(hb_ref[...], win_ref[:, cols])

        return [prep] + [functools.partial(piece, k) for k in range(IN_PROJ // IN_PROJ_PIECE)]

    @pl.when(g == 0)
    def _():
        st_ref[...] = jnp.zeros_like(st_ref)
        ext_ref[0:16, :] = jnp.zeros((16, POOL_WIDTH), F32)
        for f in in_proj_pieces(xa_ref, slice(0, T), p0_ref):
            f()

    _mix_block(2 * g, xa_ref[0:T, :], p0_ref, x1_ref.at[0:T, :],
               in_proj_pieces(xa_ref, slice(T, 2 * T), p1_ref), *rest)
    _mix_block(2 * g + 1, xa_ref[T:2 * T, :], p1_ref, x1_ref.at[T:2 * T, :],
               in_proj_pieces(xn_ref, slice(0, T), p0_ref), *rest)

    steps_per_seq = MIX_BLOCKS_PER_SEQ // 2

    @pl.when(g % steps_per_seq == steps_per_seq - 1)
    def _():
        for h in range(HEADS):
            hst_ref[0, h] = st_ref[h].T

    for seq in range(pst_ref.shape[1]):
        @pl.when(g == seq * steps_per_seq + steps_per_seq - 1)
        def _():
            pst_ref[:, seq, :] = ext_ref[T + 1:T + 16, :]


def _mixer_prompt(x, g_mix, w_in, hgrn_lb, hgrn_norm, w_out, w_ffn_in):
    nb, L, _ = x.shape
    T = MIX_ROWS
    assert L // T == MIX_BLOCKS_PER_SEQ and MIX_BLOCKS_PER_SEQ % 2 == 0
    n_blocks = nb * MIX_BLOCKS_PER_SEQ
    steps = n_blocks // 2
    steps_per_seq = MIX_BLOCKS_PER_SEQ // 2
    x2d = x.reshape(nb * L, D_MODEL)
    full = lambda shape: pl.BlockSpec(shape, lambda g: (0,) * len(shape))
    rows = lambda w: pl.BlockSpec((w.shape[0] // steps, w.shape[1]), lambda g: (g, 0))
    side = (w_ffn_in,)
    x1, hst, pst, w1b = pl.pallas_call(
        _mixer_kernel,
        grid=(steps,),
        in_specs=[pl.BlockSpec((2 * T, D_MODEL), lambda g: (g, 0)),
                  pl.BlockSpec((T, D_MODEL), lambda g: (jnp.minimum(2 * g + 2, n_blocks - 1), 0)),
                  full((1, D_MODEL)), full((D_MODEL, IN_PROJ)), full((2, HGRN_WIDTH)),
                  full((1, HGRN_WIDTH)), full((D_MODEL, D_MODEL))] + [rows(w) for w in side],
        out_specs=[pl.BlockSpec((2 * T, D_MODEL), lambda g: (g, 0)),
                   pl.BlockSpec((1, HEADS, DK, DK), lambda g: (g // steps_per_seq, 0, 0, 0)),
                   full((POOL_STATE, nb, POOL_WIDTH))]
                  + [rows(w) for w in side],
        out_shape=[jax.ShapeDtypeStruct((nb * L, D_MODEL), F32),
                   jax.ShapeDtypeStruct((nb, HEADS, DK, DK), F32),
                   jax.ShapeDtypeStruct((POOL_STATE, nb, POOL_WIDTH), F32)]
                  + [jax.ShapeDtypeStruct(w.shape, BF16) for w in side],
        scratch_shapes=[pltpu.VMEM((T, IN_PROJ), F32),
                        pltpu.VMEM((T, IN_PROJ), F32),
                        pltpu.VMEM((T, D_MODEL), BF16),
                        pltpu.VMEM((HEADS, DK, DK), F32),
                        pltpu.VMEM((16 + T, POOL_WIDTH), F32),
                        pltpu.VMEM((T, HGRN_WIDTH), F32),
                        pltpu.VMEM((T, HGRN_WIDTH), F32),
                        pltpu.VMEM((T, HGRN_WIDTH), F32),
                        pltpu.VMEM((T, D_MODEL), BF16)],
        compiler_params=pltpu.CompilerParams(
            dimension_semantics=("arbitrary",), vmem_limit_bytes=VMEM_LIMIT_BYTES),
        name="mixer_prompt",
    )(x2d, x2d, g_mix, w_in, hgrn_lb, hgrn_norm, w_out, *side)
    return x1.reshape(nb, L, D_MODEL), hst, pst, w1b


def _state_step(base, gates_ref, s_ref, so_ref, o_ref):
    G = STATE_TOKENS
    pad = jnp.zeros((DK - G, DK), F32)
    for h in range(HEADS):
        cols = slice(h * DK, (h + 1) * DK)
        qt = jnp.concatenate([gates_ref[:, h * DK:(h + 1) * DK], pad], axis=0).T
        ft = jnp.concatenate([gates_ref[:, 512 + h * DK:512 + (h + 1) * DK], pad], axis=0).T
        kt = jnp.concatenate([gates_ref[:, 1024 + h * DK:1024 + (h + 1) * DK], pad], axis=0).T
        readouts = []
        for j in range(G):
            v = gates_ref[j:j + 1, 1536 + h * DK:1536 + (h + 1) * DK]
            s_new = ft[:, j:j + 1] * s_ref[j, h] + kt[:, j:j + 1] * v
            so_ref[j, h] = s_new
            readouts.append(jnp.sum(qt[:, j:j + 1] * s_new, axis=0, keepdims=True))
        o_ref[pl.ds(pl.multiple_of(base, G), G), cols] = jnp.concatenate(readouts, axis=0)


def _sample_mix(x_ref, gates_ref, o_ref, past_ref, hn_ref, wout_ref, gc_ref, wq_ref,
                x1_ref, pool_ref, q_ref, mrg_ref):
    n = x_ref.shape[0]
    for h in range(HEADS):
        cols = slice(h * DK, (h + 1) * DK)
        o = o_ref[:, cols]
        o = o * lax.rsqrt(jnp.mean(o * o, axis=-1, keepdims=True) + EPS) * hn_ref[:, cols]
        mrg_ref[0:n, cols] = (o * gates_ref[:, 2048 + h * DK:2048 + (h + 1) * DK]).astype(BF16)
    for gi, w in enumerate(POOL_WINDOWS):
        cols = slice(gi * POOL_GROUP, (gi + 1) * POOL_GROUP)
        u = gates_ref[:, 2560 + gi * POOL_GROUP:2560 + (gi + 1) * POOL_GROUP]
        acc = u
        for j in range(1, w):
            acc = acc + past_ref[POOL_STATE - j, :, cols]
        pooled = acc / float(w) - u
        mrg_ref[0:n, HGRN_WIDTH + gi * POOL_GROUP:HGRN_WIDTH + (gi + 1) * POOL_GROUP] = pooled.astype(BF16)
    pool_ref[0:POOL_STATE - 1] = past_ref[1:POOL_STATE]
    pool_ref[POOL_STATE - 1] = gates_ref[:, 2560:3072]
    x1 = x_ref[...] + _dot(mrg_ref[0:n, :], wout_ref[...])
    x1_ref[...] = x1
    hb = _rms(x1, gc_ref[...]).astype(BF16)
    q = _dot(hb, wq_ref[...]) * QUERY_SCALE
    for r in range(2 * X_HEADS):
        half, head = divmod(r, X_HEADS)
        c0 = head * X_HEAD_DIM + half * 128
        q_ref[:, r, :] = q[:, c0:c0 + 128]


def _attn_state_kernel(x_ref, g_ref, wq_ref, kt_ref, vb_ref, wo_ref, gates_ref, s_ref, w2f_ref,
                       xs_hbm, gall_hbm, past_hbm, hn_ref, wout_hbm,
                       o_ref, so_ref, w2b_ref, x1s_ref, pool_ref, qs_ref,
                       att_ref, oo_ref, xs_ref, gall_ref, past_ref, wout_ref, late_sem):
    i = pl.program_id(0)
    last = pl.num_programs(0) - 1

    late_copies = [pltpu.make_async_copy(src, dst, late_sem.at[k]) for k, (src, dst) in enumerate(
        ((xs_hbm, xs_ref), (gall_hbm, gall_ref), (past_hbm, past_ref), (wout_hbm, wout_ref)))]

    @pl.when(i == 0)
    def _():
        for cp in late_copies:
            cp.start()

    @pl.when(i < last)
    def _():
        w2b_ref[...] = w2f_ref[...].astype(BF16)
        _attention_block(x_ref, g_ref, kt_ref, vb_ref, wq_ref, wo_ref, o_ref, att_ref)
        _state_step(i * STATE_TOKENS, gates_ref, s_ref, so_ref, oo_ref)

    @pl.when(i == last)
    def _():
        for cp in late_copies:
            cp.wait()
        _sample_mix(xs_ref, gall_ref, oo_ref, past_ref, hn_ref, wout_ref, g_ref, wq_ref,
                    x1s_ref, pool_ref, qs_ref, att_ref)


def _attention_block(x_ref, g_ref, kt_ref, vb_ref, wq_ref, wo_ref, o_ref, att_ref):
    H = ATTN_ROWS // 2
    halves = (slice(0, H), slice(H, 2 * H))

    def query(rows):
        hb = _rms(x_ref[0, rows, :], g_ref[...]).astype(BF16)
        return (_dot(hb, wq_ref[...]) * QUERY_SCALE).astype(BF16)

    def scores(q):
        return [_dot(q[:, h * X_HEAD_DIM:(h + 1) * X_HEAD_DIM],
                     kt_ref[0, h * X_HEAD_DIM:(h + 1) * X_HEAD_DIM, :]) for h in range(X_HEADS)]

    def values(rows, ss):
        for h, s in enumerate(ss):
            cols = slice(h * X_HEAD_DIM, (h + 1) * X_HEAD_DIM)
            e = jnp.exp2(s - jnp.max(s, axis=-1, keepdims=True))
            den = jnp.sum(e, axis=-1, keepdims=True)
            o = _dot(e.astype(BF16), vb_ref[0, :, cols]) / den
            att_ref[rows, cols] = o.astype(BF16)

    def project(rows):
        o_ref[0, rows, :] = x_ref[0, rows, :] + _dot(att_ref[rows, :], wo_ref[...])

    s0 = scores(query(halves[0]))
    q1 = query(halves[1])
    values(halves[0], s0)
    s1 = scores(q1)
    project(halves[0])
    values(halves[1], s1)
    project(halves[1])


def _attn_prompt_state_sample(x, g_cross, w_cq, kt, vb, w_co, gates, state, w_ffn_out,
                              x_sample, past, hgrn_norm, w_out):
    nb, L, _ = x.shape
    T = ATTN_ROWS
    G = STATE_TOKENS
    steps_per_seq = L // T
    steps = nb * steps_per_seq
    ns = gates.shape[0]
    assert steps == ns // G
    full = lambda shape: pl.BlockSpec(shape, lambda i: (0,) * len(shape))
    hbm = pl.BlockSpec(memory_space=pl.ANY)
    blk = lambda i: jnp.minimum(i, steps - 1)
    st_spec = pl.BlockSpec((G, HEADS, DK, DK), lambda i: (blk(i), 0, 0, 0))
    x_spec = pl.BlockSpec((1, T, D_MODEL),
                          lambda i: (blk(i) // steps_per_seq, blk(i) % steps_per_seq, 0))
    kv_map = lambda i: (blk(i) // steps_per_seq, 0, 0)
    w2_spec = pl.BlockSpec((D_FF // steps, D_MODEL), lambda i: (blk(i), 0))
    return pl.pallas_call(
        _attn_state_kernel,
        grid=(steps + 1,),
        in_specs=[x_spec, full((1, D_MODEL)), full((D_MODEL, D_MODEL)),
                  pl.BlockSpec((1, D_MODEL, MEM_LEN), kv_map),
                  pl.BlockSpec((1, MEM_LEN, D_MODEL), kv_map),
                  full((D_MODEL, D_MODEL)),
                  pl.BlockSpec((G, 4 * 512), lambda i: (blk(i), 0)), st_spec, w2_spec,
                  hbm, hbm, hbm, full((1, HGRN_WIDTH)), hbm],
        out_specs=[x_spec, st_spec, w2_spec,
                   full((ns, D_MODEL)), full((POOL_STATE, ns, POOL_WIDTH)),
                   full((ns, 2 * X_HEADS, 128))],
        out_shape=[jax.ShapeDtypeStruct((nb, L, D_MODEL), F32),
                   jax.ShapeDtypeStruct((ns, HEADS, DK, DK), F32),
                   jax.ShapeDtypeStruct((D_FF, D_MODEL), BF16),
                   jax.ShapeDtypeStruct((ns, D_MODEL), F32),
                   jax.ShapeDtypeStruct((POOL_STATE, ns, POOL_WIDTH), F32),
                   jax.ShapeDtypeStruct((ns, 2 * X_HEADS, 128), F32)],
        scratch_shapes=[pltpu.VMEM((T, D_MODEL), BF16), pltpu.VMEM((ns, HGRN_WIDTH), F32),
                        pltpu.VMEM((ns, D_MODEL), F32), pltpu.VMEM((ns, 6 * 512), F32),
                        pltpu.VMEM((POOL_STATE, ns, POOL_WIDTH), F32),
                        pltpu.VMEM((D_MODEL, D_MODEL), BF16), pltpu.SemaphoreType.DMA((4,))],
        compiler_params=pltpu.CompilerParams(
            dimension_semantics=("arbitrary",), vmem_limit_bytes=VMEM_LIMIT_BYTES),
        name="attn_prompt_state_sample",
    )(x, g_cross, w_cq, kt, vb, w_co, gates, state, w_ffn_out,
      x_sample, gates, past, hgrn_norm, w_out)


def _zero_after(x):
    u = lax.bitcast_convert_type(x, jnp.uint32)
    z = lax.shift_right_logical(lax.shift_right_logical(u, jnp.uint32(16)), jnp.uint32(16))
    return lax.bitcast_convert_type(z, F32)


def _ffn_body(x, g_ref, w1_ref, w2_ref, gf_ref, fillers=()):
    hb = _rms(x, g_ref[...]).astype(BF16)

    def up(c0, c1):
        return _dot(hb, w1_ref[:, c0:c1]), _dot(hb, w1_ref[:, D_FF + c0:D_FF + c1])

    def add_to_first_tile(m, z):
        top = jnp.concatenate([m[0:8, 0:128] + z, m[0:8, 128:]], axis=1)
        return jnp.concatenate([top, m[8:, :]], axis=0)

    fillers = list(fillers)
    acts = []
    nxt = up(*FFN_COL_CHUNKS[0])
    for i, (c0, c1) in enumerate(FFN_COL_CHUNKS):
        a, bg = nxt
        if i + 1 < len(FFN_COL_CHUNKS):
            nxt = up(*FFN_COL_CHUNKS[i + 1])
        for _ in range(-(-len(fillers) // (len(FFN_COL_CHUNKS) - i))):
            bg = add_to_first_tile(bg, _zero_after(fillers.pop(0)()))
        acts.append((a * _sigmoid(a) * bg).astype(BF16))
    split = FFN_COL_CHUNKS[-1][0]
    y = x + _dot(jnp.concatenate(acts[:-1], axis=1), w2_ref[0:split, :])
    y = y + _dot(acts[-1], w2_ref[split:, :])
    return _rms(y, gf_ref[...])


def _split_heads(x):
    lead = x.shape[:-2]
    x = x.reshape(lead + (X_HEADS, 2, 128))
    x = jnp.swapaxes(x, -3, -2)
    return x.reshape(lead + (2 * X_HEADS, 128))


def _merge_heads(x):
    lead = x.shape[:-2]
    x = x.reshape(lead + (2, X_HEADS, 128))
    x = jnp.swapaxes(x, -3, -2)
    return x.reshape(lead + (X_HEADS, X_HEAD_DIM))


def _memory_attention(j, base, q_ref, k_ref, v_ref, o_ref):
    prod = k_ref[j] * q_ref[base + j][None]
    s = jnp.sum(prod + pltpu.roll(prod, X_HEADS, axis=1), axis=-1, keepdims=True)
    e = jnp.exp2(s - jnp.max(s, axis=0, keepdims=True))
    den = jnp.sum(e, axis=0)
    o = jnp.sum(e * v_ref[j], axis=0) / den
    o_ref[base + j] = o
    return o


def _ffn_attn_kernel(x_ref, g_ref, w1_ref, w2_ref, gf_ref, q_ref, k_ref, v_ref, xs_ref, wo_ref,
                     o_ref, ys_ref, att_ref):
    i = pl.program_id(0)
    last = pl.num_programs(0) - 1

    @pl.when(i < last)
    def _():
        base = i * ATTN_TOKENS
        fillers = [functools.partial(_memory_attention, j, base, q_ref, k_ref, v_ref, att_ref)
                   for j in range(ATTN_TOKENS)]
        o_ref[...] = _ffn_body(x_ref[...], g_ref, w1_ref, w2_ref, gf_ref, fillers)

    @pl.when(i == last)
    def _():
        att = jnp.concatenate([att_ref[:, half * X_HEADS + head, :]
                               for head in range(X_HEADS) for half in range(2)], axis=1)
        x2 = xs_ref[...] + _dot(att.astype(BF16), wo_ref[...])
        ys_ref[:, 0, :] = _ffn_body(x2, g_ref, w1_ref, w2_ref, gf_ref)


def _ffn_prompt_attn_sample(x, g_ffn, w1, w2, g_final, q, cache_k, cache_v, x_sample, w_co):
    n = x.shape[0]
    ns = q.shape[0]
    T = FFN_ROWS
    G = ATTN_TOKENS
    steps = n // T
    assert steps == ns // G
    full = lambda shape: pl.BlockSpec(shape, lambda i: (0,) * len(shape))
    blk = lambda i: jnp.minimum(i, steps - 1)
    kv_spec = pl.BlockSpec((G, MEM_LEN, 2 * X_HEADS, 128), lambda i: (blk(i), 0, 0, 0))
    q_spec = full((ns, 2 * X_HEADS, 128))
    y, ys = pl.pallas_call(
        _ffn_attn_kernel,
        grid=(steps + 1,),
        in_specs=[pl.BlockSpec((T, D_MODEL), lambda i: (blk(i), 0)),
                  full((1, D_MODEL)), full((D_MODEL, 2 * D_FF)), full((D_FF, D_MODEL)),
                  full((1, D_MODEL)), q_spec, kv_spec, kv_spec,
                  full((ns, D_MODEL)), full((D_MODEL, D_MODEL))],
        out_specs=[pl.BlockSpec((T, D_MODEL), lambda i: (blk(i), 0)),
                   full((ns, 1, D_MODEL))],
        out_shape=[jax.ShapeDtypeStruct((n, D_MODEL), F32),
                   jax.ShapeDtypeStruct((ns, 1, D_MODEL), F32)],
        scratch_shapes=[pltpu.VMEM((ns, 2 * X_HEADS, 128), F32)],
        compiler_params=pltpu.CompilerParams(
            dimension_semantics=("arbitrary",), vmem_limit_bytes=VMEM_LIMIT_BYTES),
        name="ffn_prompt_attn_sample",
    )(x, g_ffn, w1, w2, g_final, q, cache_k, cache_v, x_sample, w_co)
    return y, ys


def kernel(x_prompt, x_sample, mem_prompt, state_hgrn, state_pool, cache_mem_k, cache_mem_v,
           g_mix, w_in, hgrn_lb, hgrn_norm, pool_mix, pool_scale, w_out, g_mem, w_mem_kv,
           g_cross, w_cq, w_co, g_ffn, w_ffn_in, w_ffn_out, g_final):
    nb, L, _ = x_prompt.shape
    ns = x_sample.shape[0]

    g_final2 = g_final.reshape(1, D_MODEL)

    mem_k, mem_v, kt, vb, w_in_b, w_out_b, w_cq_b, w_co_b, gates, xs = _mem_kv_sample_gates(
        mem_prompt, g_mem, w_mem_kv[0], w_in[0], w_out[0], w_cq[0], w_co[0], pool_mix[0], pool_scale,
        x_sample, g_mix, hgrn_lb)
    x1, hgrn_p, pool_p, w1_b = _mixer_prompt(x_prompt, g_mix, w_in_b, hgrn_lb, hgrn_norm, w_out_b,
                                             w_ffn_in[0])
    x2, hgrn_s, w2_b, x1s, pool_s, qs = _attn_prompt_state_sample(
        x1, g_cross, w_cq_b, kt, vb, w_co_b, gates, state_hgrn[0], w_ffn_out[0],
        xs, jnp.swapaxes(state_pool[0], 0, 1), hgrn_norm, w_out_b)
    y_prompt, y_sample = _ffn_prompt_attn_sample(
        x2.reshape(nb * L, D_MODEL), g_ffn, w1_b, w2_b, g_final2, qs,
        _split_heads(cache_mem_k[0]), _split_heads(cache_mem_v[0]), x1s, w_co_b)

    return (y_prompt.reshape(nb, L, D_MODEL),
            y_sample,
            hgrn_p[None],
            jnp.swapaxes(pool_p, 0, 1)[None],
            _merge_heads(mem_k)[None],
            _merge_heads(mem_v)[None],
            hgrn_s[None],
            jnp.swapaxes(pool_s, 0, 1)[None])
```

```python
import functools

import jax
import jax.numpy as jnp
from jax import lax
from jax.experimental import pallas as pl
from jax.experimental.pallas import tpu as pltpu

F32 = jnp.float32
BF16 = jnp.bfloat16

D_MODEL = 1024
HGRN_WIDTH = 512
HEADS = 4
DK = 128
CHUNK = 64
POOL_WIDTH = 512
POOL_WINDOWS = (2, 4, 8, 16)
POOL_GROUP = 128
POOL_STATE = 15
IN_PROJ = 4 * HGRN_WIDTH + POOL_WIDTH
MEM_LEN = 256
X_HEADS = 4
X_HEAD_DIM = 256
D_FF = 2816
EPS = 1e-6
ATTN_SCALE = X_HEAD_DIM ** -0.5
LOG2_E = 1.4426950408889634
QUERY_SCALE = ATTN_SCALE * LOG2_E

VMEM_LIMIT_BYTES = 56 * 1024 * 1024

MIX_ROWS = 512
MIX_BLOCKS_PER_SEQ = 4
IN_PROJ_PIECE = 256
MIX_FILLER_SCHEDULE = (3, 0, 0) + (1, 0, 1, 0, 1, 0, 1, 0) + (2, 2)
TRI_ROWS = 256
ATTN_ROWS = 1024
SPLIT_PITCH = MEM_LEN + 4
FFN_ROWS = 512
STATE_TOKENS = 8
ATTN_TOKENS = 4
FFN_PASS_WIDTHS = (768, 768, 768, 512)
FFN_COL_CHUNKS = tuple((sum(FFN_PASS_WIDTHS[:i]), sum(FFN_PASS_WIDTHS[:i + 1]))
                       for i in range(len(FFN_PASS_WIDTHS)))
assert FFN_COL_CHUNKS[-1][1] == D_FF


def _dot(a, b):
    return jnp.dot(a, b, preferred_element_type=F32)


def _dot_nt(a, b):
    return lax.dot_general(a, b, (((1,), (1,)), ((), ())), preferred_element_type=F32)


def _dot_tn(a, b):
    return lax.dot_general(a, b, (((0,), (0,)), ((), ())), preferred_element_type=F32)


def _rms(x, g):
    ms = jnp.mean(x * x, axis=-1, keepdims=True)
    return x * lax.rsqrt(ms + EPS) * g


def _sigmoid(x):
    return 1.0 / (1.0 + jnp.exp(-x))


def _lower_bound(lb_ref):
    t = lb_ref[...]
    m = jnp.max(t, axis=0, keepdims=True)
    e = jnp.exp(t - m)
    return e[0:1, :] / jnp.sum(e, axis=0, keepdims=True)


def _gates(proj_q, proj_f, lb):
    qq = proj_q * _sigmoid(proj_q)
    sig = _sigmoid(proj_f)
    fgate = lb + (1.0 - lb) * sig
    kk = (1.0 - lb) * (1.0 - sig)
    return qq, fgate, kk


def _split2(x):
    hi = x.astype(BF16)
    return hi, (x - hi.astype(F32)).astype(BF16)


def _sample_gates(x, g_ref, winb_ref, lb_ref, o_ref, x2d_ref):
    x2d_ref[...] = x
    hb = _rms(x, g_ref[...]).astype(BF16)
    proj = _dot(hb, winb_ref[...])
    lb = _lower_bound(lb_ref)
    qq, fgate, kk = _gates(proj[:, 0:512], proj[:, 512:1024], lb)
    o_ref[:, 0:512] = qq
    o_ref[:, 512:1024] = fgate
    o_ref[:, 1024:1536] = kk
    o_ref[:, 1536:2048] = proj[:, 1024:1536]
    o_ref[:, 2048:2560] = _sigmoid(proj[:, 1536:2048])
    o_ref[:, 2560:3072] = proj[:, 2048:2560]


def _memkv_kernel(mem_ref, g_ref, w_ref, win_ref, wout_ref, wcq_ref, wco_ref, pmix_ref, ps_ref,
                  xs_ref, gmix_ref, lb_ref,
                  k_ref, v_ref, kt_ref, vb_ref, winb_ref, woutb_ref, wcqb_ref, wcob_ref,
                  gates_ref, xs2d_ref,
                  wb_ref, rows_ref):
    b = pl.program_id(0)
    nb = pl.num_programs(0) - 1
    first_pool_block = HGRN_WIDTH // POOL_GROUP

    @pl.when(b == 0)
    def _():
        wb_ref[...] = w_ref[...].astype(BF16)

    @pl.when(b < nb)
    def _():
        wrows = win_ref.shape[0]
        winb_ref[pl.ds(pl.multiple_of(b * wrows, wrows), wrows), :] = win_ref[...].astype(BF16)
        wcqb_ref[...] = wcq_ref[...].astype(BF16)
        wcob_ref[...] = wco_ref[...].astype(BF16)

        @pl.when(b < first_pool_block)
        def _():
            woutb_ref[...] = wout_ref[...].astype(BF16)

        @pl.when(b >= first_pool_block)
        def _():
            gi = b - first_pool_block
            a_hi, a_lo = _split2(pmix_ref[gi] * ps_ref[gi])
            w_hi, w_lo = _split2(wout_ref[...])
            woutb_ref[...] = (_dot(a_hi, w_hi) + _dot(a_hi, w_lo) + _dot(a_lo, w_hi)).astype(BF16)

        h = _rms(mem_ref[0], g_ref[...]).astype(BF16)
        kv = _dot(h, wb_ref[...])
        k = kv[:, :D_MODEL]
        v = kv[:, D_MODEL:]
        kt_ref[0] = k.T.astype(BF16)
        vb_ref[0] = v.astype(BF16)
        for val, out_ref in ((k, k_ref), (v, v_ref)):
            for r in range(2 * X_HEADS):
                half, head = divmod(r, X_HEADS)
                c0 = head * X_HEAD_DIM + half * 128
                rows_ref[r * SPLIT_PITCH:r * SPLIT_PITCH + MEM_LEN, :] = val[:, c0:c0 + 128]
            for m in range(MEM_LEN):
                out_ref[0, m] = rows_ref[pl.ds(m, 2 * X_HEADS, stride=SPLIT_PITCH), :]

    @pl.when(b == nb)
    def _():
        _sample_gates(xs_ref[:, 0, :], gmix_ref, winb_ref, lb_ref, gates_ref, xs2d_ref)


def _mem_kv_sample_gates(mem, g_mem, w_kv, w_in, w_out, w_cq, w_co, pool_mix, pool_scale,
                         x_sample, g_mix, hgrn_lb):
    nb = mem.shape[0]
    ns = x_sample.shape[0]
    full = lambda shape: pl.BlockSpec(shape, lambda b: (0,) * len(shape))
    blk = lambda b: jnp.minimum(b, nb - 1)
    split_spec = pl.BlockSpec((1, MEM_LEN, 2 * X_HEADS, 128), lambda b: (blk(b), 0, 0, 0))
    wrows = D_MODEL // nb
    assert wrows == POOL_GROUP
    win_spec = pl.BlockSpec((wrows, IN_PROJ), lambda b: (blk(b), 0))
    wout_spec = pl.BlockSpec((wrows, D_MODEL), lambda b: (blk(b), 0))
    return pl.pallas_call(
        _memkv_kernel,
        grid=(nb + 1,),
        in_specs=[pl.BlockSpec((1, MEM_LEN, D_MODEL), lambda b: (blk(b), 0, 0)),
                  full((1, D_MODEL)), full((D_MODEL, 2 * D_MODEL)), win_spec, wout_spec,
                  wout_spec, wout_spec,
                  full((4, POOL_GROUP, POOL_GROUP)), full((4, 1, POOL_GROUP)),
                  full((ns, 1, D_MODEL)), full((1, D_MODEL)), full((2, HGRN_WIDTH))],
        out_specs=[split_spec, split_spec,
                   pl.BlockSpec((1, D_MODEL, MEM_LEN), lambda b: (blk(b), 0, 0)),
                   pl.BlockSpec((1, MEM_LEN, D_MODEL), lambda b: (blk(b), 0, 0)),
                   full((D_MODEL, IN_PROJ)), wout_spec, wout_spec, wout_spec,
                   full((ns, 6 * 512)), full((ns, D_MODEL))],
        out_shape=[jax.ShapeDtypeStruct((nb, MEM_LEN, 2 * X_HEADS, 128), F32),
                   jax.ShapeDtypeStruct((nb, MEM_LEN, 2 * X_HEADS, 128), F32),
                   jax.ShapeDtypeStruct((nb, D_MODEL, MEM_LEN), BF16),
                   jax.ShapeDtypeStruct((nb, MEM_LEN, D_MODEL), BF16),
                   jax.ShapeDtypeStruct((D_MODEL, IN_PROJ), BF16),
                   jax.ShapeDtypeStruct((D_MODEL, D_MODEL), BF16),
                   jax.ShapeDtypeStruct((D_MODEL, D_MODEL), BF16),
                   jax.ShapeDtypeStruct((D_MODEL, D_MODEL), BF16),
                   jax.ShapeDtypeStruct((ns, 6 * 512), F32),
                   jax.ShapeDtypeStruct((ns, D_MODEL), F32)],
        scratch_shapes=[pltpu.VMEM((D_MODEL, 2 * D_MODEL), BF16),
                        pltpu.VMEM((2 * X_HEADS * SPLIT_PITCH, 128), F32)],
        compiler_params=pltpu.CompilerParams(
            dimension_semantics=("arbitrary",), vmem_limit_bytes=VMEM_LIMIT_BYTES),
        name="mem_kv_sample_gates",
    )(mem, g_mem, w_kv, w_in, w_out, w_cq, w_co, pool_mix, pool_scale.reshape(4, 1, POOL_GROUP),
      x_sample, g_mix, hgrn_lb)


def _mix_block(n, x, proj_ref, out_ref, fillers, lb_ref, hn_ref, wout_ref,
               st_ref, ext_ref, qq_ref, kk_ref, b_ref, mrg_ref):
    T = MIX_ROWS
    fillers = list(fillers)
    schedule = list(MIX_FILLER_SCHEDULE)
    assert len(schedule) == 5 + T // CHUNK and sum(schedule) == len(fillers)

    def fill():
        for _ in range(schedule.pop(0)):
            fillers.pop(0)()

    l = n % MIX_BLOCKS_PER_SEQ
    first = l == 0
    for h in range(HEADS):
        st_ref[h] = jnp.where(first, 0.0, st_ref[h])
    ext_ref[0:16, :] = jnp.where(first, 0.0, ext_ref[0:16, :])

    fill()
    lb = _lower_bound(lb_ref)
    qq, fgate, kk = _gates(proj_ref[:, 0:512], proj_ref[:, 512:1024], lb)
    qq_ref[...] = qq
    kk_ref[...] = kk
    fill()

    r = lax.broadcasted_iota(jnp.int32, (TRI_ROWS, TRI_ROWS), 0)
    c = lax.broadcasted_iota(jnp.int32, (TRI_ROWS, TRI_ROWS), 1)
    tri = jnp.where((c <= r) & (c >= (r & -CHUNK)), 1.0, 0.0).astype(BF16)
    logf = jnp.log2(fgate)
    for blk in range(T // TRI_ROWS):
        rows = slice(blk * TRI_ROWS, (blk + 1) * TRI_ROWS)
        hi, lo = _split2(logf[rows])
        b_ref[rows, :] = _dot(tri, hi) + _dot(tri, lo)
    fill()

    cr = lax.broadcasted_iota(jnp.int32, (CHUNK, CHUNK), 0)
    cc = lax.broadcasted_iota(jnp.int32, (CHUNK, CHUNK), 1)
    causal = cc <= cr
    mid_row = (CHUNK - 1) // 2

    for ci in range(T // CHUNK):
        r0 = ci * CHUNK
        rows = slice(r0, r0 + CHUNK)
        scores, inters, vals = [], [], []
        for h in range(HEADS):
            cols = slice(h * DK, (h + 1) * DK)
            b = b_ref[rows, cols]
            m = b_ref[r0 + mid_row:r0 + mid_row + 1, cols]
            b_end = b_ref[r0 + CHUNK - 1:r0 + CHUNK, cols]
            e1 = jnp.exp2(b - m)
            e2 = jnp.exp2(m - b)
            q1 = qq_ref[rows, cols] * e1
            q0 = q1 * jnp.exp2(m)
            ks = kk_ref[rows, cols] * e2
            k2 = ks * jnp.exp2(b_end - m)
            v = proj_ref[rows, 1024 + h * DK:1024 + (h + 1) * DK].astype(BF16)
            st = st_ref[h]
            scores.append(_dot_nt(q1.astype(BF16), ks.astype(BF16)))
            inters.append(_dot_nt(q0.astype(BF16), st.astype(BF16)))
            st_ref[h] = st * jnp.exp2(b_end) + _dot_tn(v, k2.astype(BF16))
            vals.append(v)
        fill()
        for h in range(HEADS):
            cols = slice(h * DK, (h + 1) * DK)
            a = jnp.where(causal, scores[h], 0.0)
            o = inters[h] + _dot(a.astype(BF16), vals[h])
            o = o * lax.rsqrt(jnp.mean(o * o, axis=-1, keepdims=True) + EPS)
            o = o * hn_ref[:, cols]
            g = proj_ref[rows, 1536 + h * DK:1536 + (h + 1) * DK]
            mrg_ref[rows, cols] = (o * _sigmoid(g)).astype(BF16)

    ext_ref[16:16 + T, :] = proj_ref[:, 2048:2560]
    pos = l * T + lax.broadcasted_iota(jnp.int32, (16, POOL_GROUP), 0)
    for gi, w in enumerate(POOL_WINDOWS):
        if gi % 2 == 0:
            fill()
        cols = slice(gi * POOL_GROUP, (gi + 1) * POOL_GROUP)
        s = ext_ref[:, cols]
        u = s[16:]
        shift = 1
        while shift < w:
            s = s + pltpu.roll(s, shift, axis=0)
            shift *= 2
        acc = s[16:]
        cnt = jnp.minimum(pos + 1, w).astype(F32)
        pooled = jnp.concatenate([acc[0:16] / cnt, acc[16:] * (1.0 / w)], axis=0) - u
        mrg_ref[:, HGRN_WIDTH + gi * POOL_GROUP:HGRN_WIDTH + (gi + 1) * POOL_GROUP] = pooled.astype(BF16)

    ext_ref[0:16, :] = ext_ref[T:T + 16, :]
    out_ref[...] = x + _dot(mrg_ref[...], wout_ref[...])


def _mixer_kernel(xa_ref, xn_ref, g_ref, win_ref, lb_ref, hn_ref, wout_ref,
                  w1f_ref,
                  x1_ref, hst_ref, pst_ref, w1b_ref,
                  p0_ref, p1_ref, hb_ref, st_ref, ext_ref, qq_ref, kk_ref, b_ref, mrg_ref):
    T = MIX_ROWS
    g = pl.program_id(0)
    w1b_ref[...] = w1f_ref[...].astype(BF16)
    rest = (lb_ref, hn_ref, wout_ref, st_ref, ext_ref, qq_ref, kk_ref, b_ref, mrg_ref)

    def in_proj_pieces(x_ref, rows, p_ref):
        def prep():
            hb_ref[...] = _rms(x_ref[rows, :], g_ref[...]).astype(BF16)

        def piece(k):
            cols = slice(k * IN_PROJ_PIECE, (k + 1) * IN_PROJ_PIECE)
            p_ref[:, cols] = _dot(hb_ref[...], win_ref[:, cols])

        return [prep] + [functools.partial(piece, k) for k in range(IN_PROJ // IN_PROJ_PIECE)]

    @pl.when(g == 0)
    def _():
        st_ref[...] = jnp.zeros_like(st_ref)
        ext_ref[0:16, :] = jnp.zeros((16, POOL_WIDTH), F32)
        for f in in_proj_pieces(xa_ref, slice(0, T), p0_ref):
            f()

    _mix_block(2 * g, xa_ref[0:T, :], p0_ref, x1_ref.at[0:T, :],
               in_proj_pieces(xa_ref, slice(T, 2 * T), p1_ref), *rest)
    _mix_block(2 * g + 1, xa_ref[T:2 * T, :], p1_ref, x1_ref.at[T:2 * T, :],
               in_proj_pieces(xn_ref, slice(0, T), p0_ref), *rest)

    steps_per_seq = MIX_BLOCKS_PER_SEQ // 2

    @pl.when(g % steps_per_seq == steps_per_seq - 1)
    def _():
        for h in range(HEADS):
            hst_ref[0, h] = st_ref[h].T

    for seq in range(pst_ref.shape[1]):
        @pl.when(g == seq * steps_per_seq + steps_per_seq - 1)
        def _():
            pst_ref[:, seq, :] = ext_ref[T + 1:T + 16, :]


def _mixer_prompt(x, g_mix, w_in, hgrn_lb, hgrn_norm, w_out, w_ffn_in):
    nb, L, _ = x.shape
    T = MIX_ROWS
    assert L // T == MIX_BLOCKS_PER_SEQ and MIX_BLOCKS_PER_SEQ % 2 == 0
    n_blocks = nb * MIX_BLOCKS_PER_SEQ
    steps = n_blocks // 2
    steps_per_seq = MIX_BLOCKS_PER_SEQ // 2
    x2d = x.reshape(nb * L, D_MODEL)
    full = lambda shape: pl.BlockSpec(shape, lambda g: (0,) * len(shape))
    rows = lambda w: pl.BlockSpec((w.shape[0] // steps, w.shape[1]), lambda g: (g, 0))
    side = (w_ffn_in,)
    x1, hst, pst, w1b = pl.pallas_call(
        _mixer_kernel,
        grid=(steps,),
        in_specs=[pl.BlockSpec((2 * T, D_MODEL), lambda g: (g, 0)),
                  pl.BlockSpec((T, D_MODEL), lambda g: (jnp.minimum(2 * g + 2, n_blocks - 1), 0)),
                  full((1, D_MODEL)), full((D_MODEL, IN_PROJ)), full((2, HGRN_WIDTH)),
                  full((1, HGRN_WIDTH)), full((D_MODEL, D_MODEL))] + [rows(w) for w in side],
        out_specs=[pl.BlockSpec((2 * T, D_MODEL), lambda g: (g, 0)),
                   pl.BlockSpec((1, HEADS, DK, DK), lambda g: (g // steps_per_seq, 0, 0, 0)),
                   full((POOL_STATE, nb, POOL_WIDTH))]
                  + [rows(w) for w in side],
        out_shape=[jax.ShapeDtypeStruct((nb * L, D_MODEL), F32),
                   jax.ShapeDtypeStruct((nb, HEADS, DK, DK), F32),
                   jax.ShapeDtypeStruct((POOL_STATE, nb, POOL_WIDTH), F32)]
                  + [jax.ShapeDtypeStruct(w.shape, BF16) for w in side],
        scratch_shapes=[pltpu.VMEM((T, IN_PROJ), F32),
                        pltpu.VMEM((T, IN_PROJ), F32),
                        pltpu.VMEM((T, D_MODEL), BF16),
                        pltpu.VMEM((HEADS, DK, DK), F32),
                        pltpu.VMEM((16 + T, POOL_WIDTH), F32),
                        pltpu.VMEM((T, HGRN_WIDTH), F32),
                        pltpu.VMEM((T, HGRN_WIDTH), F32),
                        pltpu.VMEM((T, HGRN_WIDTH), F32),
                        pltpu.VMEM((T, D_MODEL), BF16)],
        compiler_params=pltpu.CompilerParams(
            dimension_semantics=("arbitrary",), vmem_limit_bytes=VMEM_LIMIT_BYTES),
        name="mixer_prompt",
    )(x2d, x2d, g_mix, w_in, hgrn_lb, hgrn_norm, w_out, *side)
    return x1.reshape(nb, L, D_MODEL), hst, pst, w1b


def _state_step(base, gates_ref, s_ref, so_ref, o_ref):
    G = STATE_TOKENS
    pad = jnp.zeros((DK - G, DK), F32)
    for h in range(HEADS):
        cols = slice(h * DK, (h + 1) * DK)
        qt = jnp.concatenate([gates_ref[:, h * DK:(h + 1) * DK], pad], axis=0).T
        ft = jnp.concatenate([gates_ref[:, 512 + h * DK:512 + (h + 1) * DK], pad], axis=0).T
        kt = jnp.concatenate([gates_ref[:, 1024 + h * DK:1024 + (h + 1) * DK], pad], axis=0).T
        readouts = []
        for j in range(G):
            v = gates_ref[j:j + 1, 1536 + h * DK:1536 + (h + 1) * DK]
            s_new = ft[:, j:j + 1] * s_ref[j, h] + kt[:, j:j + 1] * v
            so_ref[j, h] = s_new
            readouts.append(jnp.sum(qt[:, j:j + 1] * s_new, axis=0, keepdims=True))
        o_ref[pl.ds(pl.multiple_of(base, G), G), cols] = jnp.concatenate(readouts, axis=0)


def _sample_mix(x_ref, gates_ref, o_ref, past_ref, hn_ref, wout_ref, gc_ref, wq_ref,
                x1_ref, pool_ref, q_ref, mrg_ref):
    n = x_ref.shape[0]
    for h in range(HEADS):
        cols = slice(h * DK, (h + 1) * DK)
        o = o_ref[:, cols]
        o = o * lax.rsqrt(jnp.mean(o * o, axis=-1, keepdims=True) + EPS) * hn_ref[:, cols]
        mrg_ref[0:n, cols] = (o * gates_ref[:, 2048 + h * DK:2048 + (h + 1) * DK]).astype(BF16)
    for gi, w in enumerate(POOL_WINDOWS):
        cols = slice(gi * POOL_GROUP, (gi + 1) * POOL_GROUP)
        u = gates_ref[:, 2560 + gi * POOL_GROUP:2560 + (gi + 1) * POOL_GROUP]
        acc = u
        for j in range(1, w):
            acc = acc + past_ref[POOL_STATE - j, :, cols]
        pooled = acc / float(w) - u
        mrg_ref[0:n, HGRN_WIDTH + gi * POOL_GROUP:HGRN_WIDTH + (gi + 1) * POOL_GROUP] = pooled.astype(BF16)
    pool_ref[0:POOL_STATE - 1] = past_ref[1:POOL_STATE]
    pool_ref[POOL_STATE - 1] = gates_ref[:, 2560:3072]
    x1 = x_ref[...] + _dot(mrg_ref[0:n, :], wout_ref[...])
    x1_ref[...] = x1
    hb = _rms(x1, gc_ref[...]).astype(BF16)
    q = _dot(hb, wq_ref[...]) * QUERY_SCALE
    for r in range(2 * X_HEADS):
        half, head = divmod(r, X_HEADS)
        c0 = head * X_HEAD_DIM + half * 128
        q_ref[:, r, :] = q[:, c0:c0 + 128]


def _attn_state_kernel(x_ref, g_ref, wq_ref, kt_ref, vb_ref, wo_ref, gates_ref, s_ref, w2f_ref,
                       xs_hbm, gall_hbm, past_hbm, hn_ref, wout_hbm,
                       o_ref, so_ref, w2b_ref, x1s_ref, pool_ref, qs_ref,
                       att_ref, oo_ref, xs_ref, gall_ref, past_ref, wout_ref, late_sem):
    i = pl.program_id(0)
    last = pl.num_programs(0) - 1

    late_copies = [pltpu.make_async_copy(src, dst, late_sem.at[k]) for k, (src, dst) in enumerate(
        ((xs_hbm, xs_ref), (gall_hbm, gall_ref), (past_hbm, past_ref), (wout_hbm, wout_ref)))]

    @pl.when(i == 0)
    def _():
        for cp in late_copies:
            cp.start()

    @pl.when(i < last)
    def _():
        w2b_ref[...] = w2f_ref[...].astype(BF16)
        _attention_block(x_ref, g_ref, kt_ref, vb_ref, wq_ref, wo_ref, o_ref, att_ref)
        _state_step(i * STATE_TOKENS, gates_ref, s_ref, so_ref, oo_ref)

    @pl.when(i == last)
    def _():
        for cp in late_copies:
            cp.wait()
        _sample_mix(xs_ref, gall_ref, oo_ref, past_ref, hn_ref, wout_ref, g_ref, wq_ref,
                    x1s_ref, pool_ref, qs_ref, att_ref)


def _attention_block(x_ref, g_ref, kt_ref, vb_ref, wq_ref, wo_ref, o_ref, att_ref):
    H = ATTN_ROWS // 2
    halves = (slice(0, H), slice(H, 2 * H))

    def query(rows):
        hb = _rms(x_ref[0, rows, :], g_ref[...]).astype(BF16)
        return (_dot(hb, wq_ref[...]) * QUERY_SCALE).astype(BF16)

    def scores(q):
        return [_dot(q[:, h * X_HEAD_DIM:(h + 1) * X_HEAD_DIM],
                     kt_ref[0, h * X_HEAD_DIM:(h + 1) * X_HEAD_DIM, :]) for h in range(X_HEADS)]

    def values(rows, ss):
        for h, s in enumerate(ss):
            cols = slice(h * X_HEAD_DIM, (h + 1) * X_HEAD_DIM)
            e = jnp.exp2(s - jnp.max(s, axis=-1, keepdims=True))
            den = jnp.sum(e, axis=-1, keepdims=True)
            o = _dot(e.astype(BF16), vb_ref[0, :, cols]) / den
            att_ref[rows, cols] = o.astype(BF16)

    def project(rows):
        o_ref[0, rows, :] = x_ref[0, rows, :] + _dot(att_ref[rows, :], wo_ref[...])

    s0 = scores(query(halves[0]))
    q1 = query(halves[1])
    values(halves[0], s0)
    s1 = scores(q1)
    project(halves[0])
    values(halves[1], s1)
    project(halves[1])


def _attn_prompt_state_sample(x, g_cross, w_cq, kt, vb, w_co, gates, state, w_ffn_out,
                              x_sample, past, hgrn_norm, w_out):
    nb, L, _ = x.shape
    T = ATTN_ROWS
    G = STATE_TOKENS
    steps_per_seq = L // T
    steps = nb * steps_per_seq
    ns = gates.shape[0]
    assert steps == ns // G
    full = lambda shape: pl.BlockSpec(shape, lambda i: (0,) * len(shape))
    hbm = pl.BlockSpec(memory_space=pl.ANY)
    blk = lambda i: jnp.minimum(i, steps - 1)
    st_spec = pl.BlockSpec((G, HEADS, DK, DK), lambda i: (blk(i), 0, 0, 0))
    x_spec = pl.BlockSpec((1, T, D_MODEL),
                          lambda i: (blk(i) // steps_per_seq, blk(i) % steps_per_seq, 0))
    kv_map = lambda i: (blk(i) // steps_per_seq, 0, 0)
    w2_spec = pl.BlockSpec((D_FF // steps, D_MODEL), lambda i: (blk(i), 0))
    return pl.pallas_call(
        _attn_state_kernel,
        grid=(steps + 1,),
        in_specs=[x_spec, full((1, D_MODEL)), full((D_MODEL, D_MODEL)),
                  pl.BlockSpec((1, D_MODEL, MEM_LEN), kv_map),
                  pl.BlockSpec((1, MEM_LEN, D_MODEL), kv_map),
                  full((D_MODEL, D_MODEL)),
                  pl.BlockSpec((G, 4 * 512), lambda i: (blk(i), 0)), st_spec, w2_spec,
                  hbm, hbm, hbm, full((1, HGRN_WIDTH)), hbm],
        out_specs=[x_spec, st_spec, w2_spec,
                   full((ns, D_MODEL)), full((POOL_STATE, ns, POOL_WIDTH)),
                   full((ns, 2 * X_HEADS, 128))],
        out_shape=[jax.ShapeDtypeStruct((nb, L, D_MODEL), F32),
                   jax.ShapeDtypeStruct((ns, HEADS, DK, DK), F32),
                   jax.ShapeDtypeStruct((D_FF, D_MODEL), BF16),
                   jax.ShapeDtypeStruct((ns, D_MODEL), F32),
                   jax.ShapeDtypeStruct((POOL_STATE, ns, POOL_WIDTH), F32),
                   jax.ShapeDtypeStruct((ns, 2 * X_HEADS, 128), F32)],
        scratch_shapes=[pltpu.VMEM((T, D_MODEL), BF16), pltpu.VMEM((ns, HGRN_WIDTH), F32),
                        pltpu.VMEM((ns, D_MODEL), F32), pltpu.VMEM((ns, 6 * 512), F32),
                        pltpu.VMEM((POOL_STATE, ns, POOL_WIDTH), F32),
                        pltpu.VMEM((D_MODEL, D_MODEL), BF16), pltpu.SemaphoreType.DMA((4,))],
        compiler_params=pltpu.CompilerParams(
            dimension_semantics=("arbitrary",), vmem_limit_bytes=VMEM_LIMIT_BYTES),
        name="attn_prompt_state_sample",
    )(x, g_cross, w_cq, kt, vb, w_co, gates, state, w_ffn_out,
      x_sample, gates, past, hgrn_norm, w_out)


def _zero_after(x):
    u = lax.bitcast_convert_type(x, jnp.uint32)
    z = lax.shift_right_logical(lax.shift_right_logical(u, jnp.uint32(16)), jnp.uint32(16))
    return lax.bitcast_convert_type(z, F32)


def _ffn_body(x, g_ref, w1_ref, w2_ref, gf_ref, fillers=()):
    hb = _rms(x, g_ref[...]).astype(BF16)

    def up(c0, c1):
        return _dot(hb, w1_ref[:, c0:c1]), _dot(hb, w1_ref[:, D_FF + c0:D_FF + c1])

    def add_to_first_tile(m, z):
        top = jnp.concatenate([m[0:8, 0:128] + z, m[0:8, 128:]], axis=1)
        return jnp.concatenate([top, m[8:, :]], axis=0)

    fillers = list(fillers)
    acts = []
    nxt = up(*FFN_COL_CHUNKS[0])
    for i, (c0, c1) in enumerate(FFN_COL_CHUNKS):
        a, bg = nxt
        if i + 1 < len(FFN_COL_CHUNKS):
            nxt = up(*FFN_COL_CHUNKS[i + 1])
        for _ in range(-(-len(fillers) // (len(FFN_COL_CHUNKS) - i))):
            bg = add_to_first_tile(bg, _zero_after(fillers.pop(0)()))
        acts.append((a * _sigmoid(a) * bg).astype(BF16))
    split = FFN_COL_CHUNKS[-1][0]
    y = x + _dot(jnp.concatenate(acts[:-1], axis=1), w2_ref[0:split, :])
    y = y + _dot(acts[-1], w2_ref[split:, :])
    return _rms(y, gf_ref[...])


def _split_heads(x):
    lead = x.shape[:-2]
    x = x.reshape(lead + (X_HEADS, 2, 128))
    x = jnp.swapaxes(x, -3, -2)
    return x.reshape(lead + (2 * X_HEADS, 128))


def _merge_heads(x):
    lead = x.shape[:-2]
    x = x.reshape(lead + (2, X_HEADS, 128))
    x = jnp.swapaxes(x, -3, -2)
    return x.reshape(lead + (X_HEADS, X_HEAD_DIM))


def _memory_attention(j, base, q_ref, k_ref, v_ref, o_ref):
    prod = k_ref[j] * q_ref[base + j][None]
    s = jnp.sum(prod + pltpu.roll(prod, X_HEADS, axis=1), axis=-1, keepdims=True)
    e = jnp.exp2(s - jnp.max(s, axis=0, keepdims=True))
    den = jnp.sum(e, axis=0)
    o = jnp.sum(e * v_ref[j], axis=0) / den
    o_ref[base + j] = o
    return o


def _ffn_attn_kernel(x_ref, g_ref, w1_ref, w2_ref, gf_ref, q_ref, k_ref, v_ref, xs_hbm, wo_hbm,
                     o_ref, ys_ref, att_ref, xs_ref, wo_ref, late_sem):
    i = pl.program_id(0)
    last = pl.num_programs(0) - 1

    late_copies = [pltpu.make_async_copy(xs_hbm, xs_ref, late_sem.at[0]),
                   pltpu.make_async_copy(wo_hbm, wo_ref, late_sem.at[1])]

    @pl.when(i == 0)
    def _():
        for cp in late_copies:
            cp.start()

    @pl.when(i < last)
    def _():
        base = i * ATTN_TOKENS
        fillers = [functools.partial(_memory_attention, j, base, q_ref, k_ref, v_ref, att_ref)
                   for j in range(ATTN_TOKENS)]
        o_ref[...] = _ffn_body(x_ref[...], g_ref, w1_ref, w2_ref, gf_ref, fillers)

    @pl.when(i == last)
    def _():
        for cp in late_copies:
            cp.wait()
        att = jnp.concatenate([att_ref[:, half * X_HEADS + head, :]
                               for head in range(X_HEADS) for half in range(2)], axis=1)
        x2 = xs_ref[...] + _dot(att.astype(BF16), wo_ref[...])
        ys_ref[:, 0, :] = _ffn_body(x2, g_ref, w1_ref, w2_ref, gf_ref)


def _ffn_prompt_attn_sample(x, g_ffn, w1, w2, g_final, q, cache_k, cache_v, x_sample, w_co):
    n = x.shape[0]
    ns = q.shape[0]
    T = FFN_ROWS
    G = ATTN_TOKENS
    steps = n // T
    assert steps == ns // G
    full = lambda shape: pl.BlockSpec(shape, lambda i: (0,) * len(shape))
    blk = lambda i: jnp.minimum(i, steps - 1)
    kv_spec = pl.BlockSpec((G, MEM_LEN, 2 * X_HEADS, 128), lambda i: (blk(i), 0, 0, 0))
    q_spec = full((ns, 2 * X_HEADS, 128))
    y, ys = pl.pallas_call(
        _ffn_attn_kernel,
        grid=(steps + 1,),
        in_specs=[pl.BlockSpec((T, D_MODEL), lambda i: (blk(i), 0)),
                  full((1, D_MODEL)), full((D_MODEL, 2 * D_FF)), full((D_FF, D_MODEL)),
                  full((1, D_MODEL)), q_spec, kv_spec, kv_spec,
                  pl.BlockSpec(memory_space=pl.ANY), pl.BlockSpec(memory_space=pl.ANY)],
        out_specs=[pl.BlockSpec((T, D_MODEL), lambda i: (blk(i), 0)),
                   full((ns, 1, D_MODEL))],
        out_shape=[jax.ShapeDtypeStruct((n, D_MODEL), F32),
                   jax.ShapeDtypeStruct((ns, 1, D_MODEL), F32)],
        scratch_shapes=[pltpu.VMEM((ns, 2 * X_HEADS, 128), F32), pltpu.VMEM((ns, D_MODEL), F32),
                        pltpu.VMEM((D_MODEL, D_MODEL), BF16), pltpu.SemaphoreType.DMA((2,))],
        compiler_params=pltpu.CompilerParams(
            dimension_semantics=("arbitrary",), vmem_limit_bytes=VMEM_LIMIT_BYTES),
        name="ffn_prompt_attn_sample",
    )(x, g_ffn, w1, w2, g_final, q, cache_k, cache_v, x_sample, w_co)
    return y, ys


def kernel(x_prompt, x_sample, mem_prompt, state_hgrn, state_pool, cache_mem_k, cache_mem_v,
           g_mix, w_in, hgrn_lb, hgrn_norm, pool_mix, pool_scale, w_out, g_mem, w_mem_kv,
           g_cross, w_cq, w_co, g_ffn, w_ffn_in, w_ffn_out, g_final):
    nb, L, _ = x_prompt.shape
    ns = x_sample.shape[0]

    g_final2 = g_final.reshape(1, D_MODEL)

    mem_k, mem_v, kt, vb, w_in_b, w_out_b, w_cq_b, w_co_b, gates, xs = _mem_kv_sample_gates(
        mem_prompt, g_mem, w_mem_kv[0], w_in[0], w_out[0], w_cq[0], w_co[0], pool_mix[0], pool_scale,
        x_sample, g_mix, hgrn_lb)
    x1, hgrn_p, pool_p, w1_b = _mixer_prompt(x_prompt, g_mix, w_in_b, hgrn_lb, hgrn_norm, w_out_b,
                                             w_ffn_in[0])
    x2, hgrn_s, w2_b, x1s, pool_s, qs = _attn_prompt_state_sample(
        x1, g_cross, w_cq_b, kt, vb, w_co_b, gates, state_hgrn[0], w_ffn_out[0],
        xs, jnp.swapaxes(state_pool[0], 0, 1), hgrn_norm, w_out_b)
    y_prompt, y_sample = _ffn_prompt_attn_sample(
        x2.reshape(nb * L, D_MODEL), g_ffn, w1_b, w2_b, g_final2, qs,
        _split_heads(cache_mem_k[0]), _split_heads(cache_mem_v[0]), x1s, w_co_b)

    return (y_prompt.reshape(nb, L, D_MODEL),
            y_sample,
            hgrn_p[None],
            jnp.swapaxes(pool_p, 0, 1)[None],
            _merge_heads(mem_k)[None],
            _merge_heads(mem_v)[None],
            hgrn_s[None],
            jnp.swapaxes(pool_s, 0, 1)[None])
```
